```python
import math
import jax, jax.numpy as jnp
from jax import lax
import numpy as np

D_MODEL = 1024
BATCH = 1
SEQ = 16384
DEPTH = 2
DEC_BATCH = 32
DEC_SEQ = 1
PAST_LEN = 16384
PAGE_SIZE = 128

N_EVEN = (DEPTH + 1) // 2
N_ODD = DEPTH // 2
D_FF = 4 * D_MODEL
PLE_DIM = 256
LRU_WIDTH = D_MODEL // 2
LRU_BLOCKS = 8
LRU_BW = LRU_WIDTH // LRU_BLOCKS
CONV_W = 4
RG_C = 8.0
NSA_HEADS = 8
HEAD_DIM = 64
KV_GROUPS = 2
HPG = NSA_HEADS // KV_GROUPS
CMP_BLOCK = 32
SEL_BLOCK = 64
TOP_N = 16
WINDOW = 512
Q_BLOCK = 128
FORCE_SCORE = 1.0e4
N_BUCKETS = 32
MAX_DIST = 128
CHUNK = 128
SGU_WIDTH = D_MODEL
SGU_GROUPS = 8
SGU_GW = SGU_WIDTH // SGU_GROUPS
Q_W = NSA_HEADS * HEAD_DIM
KV_W = KV_GROUPS * HEAD_DIM
EVEN_SPLITS = (LRU_WIDTH, LRU_WIDTH, Q_W, 6 * KV_W, 3 * NSA_HEADS)
EVEN_IN = sum(EVEN_SPLITS)
EPS = 1e-6
NEG = -1e30

kernel_name = 'hybrid_rglru_nsa_gmlp_decode_step'


def rmsnorm(x, g):
    xf = x.astype(jnp.float32)
    return xf * lax.rsqrt(jnp.mean(xf * xf, axis=-1, keepdims=True) + EPS) * g


def layernorm(x, g, b):
    xf = x.astype(jnp.float32)
    mu = jnp.mean(xf, axis=-1, keepdims=True)
    xc = xf - mu
    return xc * lax.rsqrt(jnp.mean(xc * xc, axis=-1, keepdims=True) + EPS) * g + b


def masked_softmax(s, mask):
    s = jnp.where(mask, s.astype(jnp.float32), NEG)
    return jnp.where(mask, jax.nn.softmax(s, axis=-1), 0.0)


def t5_bucket(dist):
    n = jnp.maximum(dist, 0)
    exact = N_BUCKETS // 2
    nf = jnp.maximum(n, 1).astype(jnp.float32)
    large = exact + (jnp.log(nf / exact) / math.log(MAX_DIST / exact) * (N_BUCKETS - exact)).astype(jnp.int32)
    return jnp.where(n < exact, n, jnp.minimum(large, N_BUCKETS - 1))


def pad_rows(x, mult):
    extra = (-x.shape[1]) % mult
    pad = [(0, 0)] * x.ndim
    pad[1] = (0, extra)
    return jnp.pad(x, pad)


def tail_rows(x, n):
    t = x.shape[1]
    if t >= n:
        return x[:, t - n:]
    pad = [(0, 0)] * x.ndim
    pad[1] = (n - t, 0)
    return jnp.pad(x, pad)


def even_project(hn, w_in, q_gain, k_gain):
    B, T, _ = hn.shape
    z = hn @ w_in
    cuts = [int(c) for c in np.cumsum(EVEN_SPLITS)[:-1]]
    xa, ga, q, kv, gl = jnp.split(z, cuts, axis=-1)
    q = rmsnorm(q.reshape(B, T, NSA_HEADS, HEAD_DIM), q_gain)
    kv = kv.reshape(B, T, 6, KV_GROUPS, HEAD_DIM)
    k_cmp, v_cmp = kv[:, :, 0], kv[:, :, 1]
    k_sel, v_sel = rmsnorm(kv[:, :, 2], k_gain[1]), kv[:, :, 3]
    k_win, v_win = rmsnorm(kv[:, :, 4], k_gain[2]), kv[:, :, 5]
    gates = jax.nn.sigmoid(gl.reshape(B, T, NSA_HEADS, 3).astype(jnp.float32))
    kv_rows = jnp.stack([k_cmp.astype(jnp.float32), v_cmp.astype(jnp.float32), k_sel, v_sel.astype(jnp.float32)], axis=2)
    win_rows = jnp.stack([k_win, v_win.astype(jnp.float32)], axis=2)
    return xa, ga, q, gates, kv_rows, win_rows


def rglru_mixer(xa, conv_state, h0, conv_w, conv_b, w_a, b_a, w_x, b_x, lam):
    B, T, C = xa.shape
    xf = jnp.concatenate([conv_state.astype(jnp.float32), xa.astype(jnp.float32)], axis=1)
    conv = conv_b.astype(jnp.float32)
    for k in range(CONV_W):
        conv = conv + conv_w[k] * xf[:, k:k + T]
    new_conv = xf[:, T:]
    xb = conv.reshape(B, T, LRU_BLOCKS, LRU_BW)
    r = jax.nn.sigmoid(jnp.einsum('btnc,ncd->btnd', xb, w_a).reshape(B, T, C) + b_a)
    i = jax.nn.sigmoid(jnp.einsum('btnc,ncd->btnd', xb, w_x).reshape(B, T, C) + b_x)
    log_a = -RG_C * r * jax.nn.softplus(-lam.astype(jnp.float32))
    a = jnp.exp(log_a)
    b = jnp.sqrt(-jnp.expm1(2.0 * log_a)) * (i * conv)

    def step(h, ab):
        h = ab[0] * h + ab[1]
        return h, h

    h_last, hs = lax.scan(step, h0.astype(jnp.float32), (a.swapaxes(0, 1), b.swapaxes(0, 1)))
    return hs.swapaxes(0, 1), new_conv, h_last


def compress(k_raw, pe, w):
    B, Tp, G, D = k_raw.shape
    nc = Tp // CMP_BLOCK
    blk = k_raw.reshape(B, nc, CMP_BLOCK, G, D) + pe[:, None, :]
    blk = blk.transpose(0, 1, 3, 2, 4).reshape(B, nc, G, CMP_BLOCK * D)
    return blk @ w


def nsa_keys(kv_rows, cmp_w_k, cmp_w_v, pe_k, pe_v, k_gain_cmp):
    kv = pad_rows(kv_rows, SEL_BLOCK)
    B, Tp = kv.shape[:2]
    kc = rmsnorm(compress(kv[:, :, 0], pe_k, cmp_w_k), k_gain_cmp)
    vc = compress(kv[:, :, 1], pe_v, cmp_w_v)
    nsb = Tp // SEL_BLOCK
    ksb = kv[:, :, 2].reshape(B, nsb, SEL_BLOCK, KV_GROUPS, HEAD_DIM).transpose(0, 3, 1, 2, 4)
    vsb = kv[:, :, 3].reshape(B, nsb, SEL_BLOCK, KV_GROUPS, HEAD_DIM).transpose(0, 3, 1, 2, 4)
    return kc, vc, ksb, vsb


def nsa_core(q, qpos, kc, vc, ksb, vsb, kw, vw, wpos, gates, rel_bias):
    B, Tq = q.shape[:2]
    scale = HEAD_DIM ** -0.5
    qg = q.reshape(B, Tq, KV_GROUPS, HPG, HEAD_DIM)
    table = rel_bias.astype(jnp.float32)
    ncb = kc.shape[1]
    cend = jnp.arange(ncb, dtype=jnp.int32) * CMP_BLOCK + (CMP_BLOCK - 1)
    dist_c = qpos[:, None] - cend[None, :]
    bias_c = table[t5_bucket(dist_c)].reshape(Tq, ncb, KV_GROUPS, HPG).transpose(0, 2, 3, 1)
    s_c = jnp.einsum('btgrd,bjgd->btgrj', qg, kc) * scale + bias_c
    p_c = masked_softmax(s_c, (dist_c >= 0)[:, None, None, :])
    o_c = jnp.einsum('btgrj,bjgd->btgrd', p_c, vc)
    nsb = ksb.shape[2]
    ratio = SEL_BLOCK // CMP_BLOCK
    imp = p_c.sum(axis=3).reshape(B, Tq, KV_GROUPS, nsb, ratio).sum(-1)
    jb = jnp.arange(nsb, dtype=jnp.int32)[None, :]
    cur = (qpos // SEL_BLOCK)[:, None]
    forced = (jb == 0) | (jb == cur) | (jb == cur - 1)
    imp = jnp.where(forced[None, :, None, :], FORCE_SCORE, imp)
    imp = jnp.where((jb <= cur)[None, :, None, :], imp, -1.0)
    n_sel = min(TOP_N, nsb)
    _, idx = lax.top_k(imp, n_sel)
    bi = jnp.arange(B)[:, None, None, None]
    gi = jnp.arange(KV_GROUPS)[None, None, :, None]
    kg = ksb[bi, gi, idx]
    vg = vsb[bi, gi, idx]
    kpos = idx[..., None] * SEL_BLOCK + jnp.arange(SEL_BLOCK, dtype=jnp.int32)
    dist_s = qpos[None, :, None, None, None] - kpos
    table_g = table.reshape(N_BUCKETS, KV_GROUPS, HPG).transpose(1, 0, 2)
    bias_s = jnp.moveaxis(table_g[gi[..., None], t5_bucket(dist_s)], -1, 3)
    s_s = jnp.einsum('btgrd,btgnsd->btgrns', qg, kg) * scale + bias_s
    nk = n_sel * SEL_BLOCK
    p_s = masked_softmax(s_s.reshape(B, Tq, KV_GROUPS, HPG, nk),
                         (dist_s >= 0).reshape(B, Tq, KV_GROUPS, 1, nk))
    o_s = jnp.einsum('btgrk,btgkd->btgrd', p_s, vg.reshape(B, Tq, KV_GROUPS, nk, HEAD_DIM))
    lw = kw.shape[1]
    dist_w = qpos[:, None] - wpos[None, :]
    bias_w = table[t5_bucket(dist_w)].reshape(Tq, lw, KV_GROUPS, HPG).transpose(0, 2, 3, 1)
    mask_w = (wpos[None, :] >= 0) & (dist_w >= 0) & (dist_w < WINDOW)
    s_w = jnp.einsum('btgrd,blgd->btgrl', qg, kw) * scale + bias_w
    p_w = masked_softmax(s_w, mask_w[:, None, None, :])
    o_w = jnp.einsum('btgrl,blgd->btgrd', p_w, vw)
    g = gates.reshape(B, Tq, KV_GROUPS, HPG, 3)
    o = g[..., 0:1] * o_c + g[..., 1:2] * o_s + g[..., 2:3] * o_w
    return o.reshape(B, Tq, Q_W)


def even_mixer(hn, lp, rel_bias, past_kv, past_win, conv_state, h0, blocked):
    B, T, _ = hn.shape
    xa, ga, q, gates, kv_rows, win_rows = even_project(hn, lp['w_in'], lp['q_gain'], lp['k_gain'])
    y_a, new_conv, h_last = rglru_mixer(xa, conv_state, h0, lp['conv_w'], lp['conv_b'], lp['w_a'],
                                        lp['b_a'], lp['w_x'], lp['b_x'], lp['lam'])
    y_a = y_a * jax.nn.gelu(ga.astype(jnp.float32))
    w_buf = min(WINDOW, PAST_LEN)
    if past_kv is None:
        full_kv, full_w, pos0, wpos0 = kv_rows, win_rows, 0, 0
    else:
        full_kv = jnp.concatenate([past_kv.astype(jnp.float32), kv_rows], axis=1)
        full_w = jnp.concatenate([past_win.astype(jnp.float32), win_rows], axis=1)
        pos0, wpos0 = PAST_LEN, PAST_LEN - w_buf
    kc, vc, ksb, vsb = nsa_keys(full_kv, lp['cmp_w_k'], lp['cmp_w_v'], lp['pe_k'], lp['pe_v'], lp['k_gain'][0])
    if blocked:
        wpad = jnp.pad(full_w, ((0, 0), (WINDOW, 0), (0, 0), (0, 0), (0, 0)))
        nqb = T // Q_BLOCK

        def block(args):
            qb, gb, qs = args
            wr = lax.dynamic_slice_in_dim(wpad, qs, WINDOW + Q_BLOCK, axis=1)
            wpos = wpos0 - WINDOW + qs + jnp.arange(WINDOW + Q_BLOCK, dtype=jnp.int32)
            qpos = pos0 + qs + jnp.arange(Q_BLOCK, dtype=jnp.int32)
            return nsa_core(qb, qpos, kc, vc, ksb, vsb, wr[:, :, 0], wr[:, :, 1], wpos, gb, rel_bias)

        qbs = q.reshape(B, nqb, Q_BLOCK, NSA_HEADS, HEAD_DIM).swapaxes(0, 1)
        gbs = gates.reshape(B, nqb, Q_BLOCK, NSA_HEADS, 3).swapaxes(0, 1)
        starts = jnp.arange(nqb, dtype=jnp.int32) * Q_BLOCK
        y_b = lax.map(block, (qbs, gbs, starts)).swapaxes(0, 1).reshape(B, T, Q_W)
    else:
        wpos = wpos0 + jnp.arange(full_w.shape[1], dtype=jnp.int32)
        qpos = pos0 + jnp.arange(T, dtype=jnp.int32)
        y_b = nsa_core(q, qpos, kc, vc, ksb, vsb, full_w[:, :, 0], full_w[:, :, 1], wpos, gates, rel_bias)
    out = jnp.concatenate([y_a, y_b.astype(jnp.float32)], axis=-1) @ lp['w_out']
    return out, (kv_rows, tail_rows(full_w, w_buf), new_conv, h_last)


def chunk_gmlp(hn, w_in, ln_g, ln_b, sgu_w, sgu_b, w_out):
    B, T, _ = hn.shape
    z = jax.nn.gelu(hn @ w_in)
    u, v = jnp.split(z, 2, axis=-1)
    vn = layernorm(v, ln_g, ln_b)
    vp = pad_rows(vn, CHUNK)
    nch = vp.shape[1] // CHUNK
    vc = vp.reshape(B, nch, CHUNK, SGU_GROUPS, SGU_GW)
    ws = sgu_w * jnp.tril(jnp.ones((CHUNK, CHUNK), sgu_w.dtype))
    s = jnp.einsum('gts,bcsgd->bctgd', ws, vc) + sgu_b.T[:, :, None]
    s = s.reshape(B, nch * CHUNK, SGU_WIDTH)[:, :T]
    return (u * s) @ w_out, vn


def channel_and_ple(h, p, g_mlp, w_up, w_down, g_ple, w_gate, w_proj):
    h = h + jnp.square(jax.nn.relu(rmsnorm(h, g_mlp) @ w_up)) @ w_down
    gate = jax.nn.sigmoid(rmsnorm(h, g_ple) @ w_gate)
    return h + gate * (p @ w_proj)


def setup_inputs(seed: int = 0) -> dict:
    key = jax.random.key(seed)
    keys = jax.random.split(key, 64)
    ctr = iter(range(64))

    def nrm(shape, scale):
        return scale * jax.random.normal(keys[next(ctr)], shape, jnp.float32)

    n_pages = PAST_LEN // PAGE_SIZE
    used = DEC_BATCH * n_pages
    n_phys = used + max(1, used // 4)
    w_buf = min(WINDOW, PAST_LEN)
    perm = jax.random.permutation(keys[next(ctr)], n_phys)
    page_table = perm[:used].reshape(DEC_BATCH, n_pages).astype(jnp.int32)
    u = jax.random.uniform(keys[next(ctr)], (N_EVEN, LRU_WIDTH), jnp.float32, 0.9, 0.999)
    rg_lambda = jnp.log(u) - jnp.log1p(-u)
    d = D_MODEL
    return {
        'x_prompt': nrm((BATCH, SEQ, d), 1.0),
        'x_sample': nrm((DEC_BATCH, DEC_SEQ, d), 1.0),
        'cache_kv': nrm((N_EVEN, n_phys, PAGE_SIZE, 4, KV_GROUPS, HEAD_DIM), 1.0),
        'cache_win': nrm((N_EVEN, DEC_BATCH, w_buf, 2, KV_GROUPS, HEAD_DIM), 1.0),
        'state_conv': nrm((N_EVEN, DEC_BATCH, CONV_W - 1, LRU_WIDTH), 1.0),
        'state_h': nrm((N_EVEN, DEC_BATCH, LRU_WIDTH), 0.5),
        'page_table': page_table,
        'p_prompt': nrm((DEPTH, BATCH, SEQ, PLE_DIM), 1.0),
        'p_sample': nrm((DEPTH, DEC_BATCH, DEC_SEQ, PLE_DIM), 1.0),
        'rel_bias': nrm((N_BUCKETS, NSA_HEADS), 0.5),
        'g_mix': 1.0 + nrm((DEPTH, d), 0.02),
        'g_mlp': 1.0 + nrm((DEPTH, d), 0.02),
        'w_up': nrm((DEPTH, d, D_FF), d ** -0.5),
        'w_down': nrm((DEPTH, D_FF, d), D_FF ** -0.5),
        'g_ple': 1.0 + nrm((DEPTH, d), 0.02),
        'w_ple_gate': nrm((DEPTH, d, d), d ** -0.5),
        'w_ple_proj': nrm((DEPTH, PLE_DIM, d), PLE_DIM ** -0.5),
        'w_in_even': nrm((N_EVEN, d, EVEN_IN), d ** -0.5),
        'w_out_even': nrm((N_EVEN, LRU_WIDTH + Q_W, d), (LRU_WIDTH + Q_W) ** -0.5),
        'conv_w': nrm((N_EVEN, CONV_W, LRU_WIDTH), CONV_W ** -0.5),
        'conv_b': nrm((N_EVEN, LRU_WIDTH), 0.01),
        'rg_w_a': nrm((N_EVEN, LRU_BLOCKS, LRU_BW, LRU_BW), LRU_BW ** -0.5),
        'rg_b_a': nrm((N_EVEN, LRU_WIDTH), 0.01),
        'rg_w_x': nrm((N_EVEN, LRU_BLOCKS, LRU_BW, LRU_BW), LRU_BW ** -0.5),
        'rg_b_x': nrm((N_EVEN, LRU_WIDTH), 0.01),
        'rg_lambda': rg_lambda,
        'q_gain': 1.0 + nrm((N_EVEN, HEAD_DIM), 0.02),
        'k_gain': 1.0 + nrm((N_EVEN, 3, HEAD_DIM), 0.02),
        'cmp_w_k': nrm((N_EVEN, CMP_BLOCK * HEAD_DIM, HEAD_DIM), (CMP_BLOCK * HEAD_DIM) ** -0.5),
        'cmp_w_v': nrm((N_EVEN, CMP_BLOCK * HEAD_DIM, HEAD_DIM), (CMP_BLOCK * HEAD_DIM) ** -0.5),
        'cmp_pe_k': nrm((N_EVEN, CMP_BLOCK, HEAD_DIM), 0.1),
        'cmp_pe_v': nrm((N_EVEN, CMP_BLOCK, HEAD_DIM), 0.1),
        'w_in_odd': nrm((N_ODD, d, 2 * SGU_WIDTH), d ** -0.5),
        'ln_v_g': 1.0 + nrm((N_ODD, SGU_WIDTH), 0.02),
        'ln_v_b': nrm((N_ODD, SGU_WIDTH), 0.01),
        'sgu_w': nrm((N_ODD, SGU_GROUPS, CHUNK, CHUNK), CHUNK ** -0.5),
        'sgu_b': 1.0 + nrm((N_ODD, SGU_GROUPS, CHUNK), 0.1),
        'w_out_odd': nrm((N_ODD, SGU_WIDTH, d), SGU_WIDTH ** -0.5),
    }


def reference(x_prompt, x_sample, cache_kv, cache_win, state_conv, state_h, page_table, p_prompt, p_sample,
              rel_bias, g_mix, g_mlp, w_up, w_down, g_ple, w_ple_gate, w_ple_proj,
              w_in_even, w_out_even, conv_w, conv_b, rg_w_a, rg_b_a, rg_w_x, rg_b_x, rg_lambda,
              q_gain, k_gain, cmp_w_k, cmp_w_v, cmp_pe_k, cmp_pe_v,
              w_in_odd, ln_v_g, ln_v_b, sgu_w, sgu_b, w_out_odd):
    n_pages = PAST_LEN // PAGE_SIZE
    hp, hs = x_prompt, x_sample
    kvp, kvs, wnp, wns, cvp, cvs, hhp, hhs, vvs = [], [], [], [], [], [], [], [], []
    for i in range(DEPTH):
        if i % 2 == 0:
            e = i // 2
            lp = {'w_in': w_in_even[e], 'w_out': w_out_even[e], 'conv_w': conv_w[e], 'conv_b': conv_b[e],
                  'w_a': rg_w_a[e], 'b_a': rg_b_a[e], 'w_x': rg_w_x[e], 'b_x': rg_b_x[e], 'lam': rg_lambda[e],
                  'q_gain': q_gain[e], 'k_gain': k_gain[e], 'cmp_w_k': cmp_w_k[e], 'cmp_w_v': cmp_w_v[e],
                  'pe_k': cmp_pe_k[e], 'pe_v': cmp_pe_v[e]}
            bp = hp.shape[0]
            out_p, st_p = even_mixer(rmsnorm(hp, g_mix[i]), lp, rel_bias, None, None,
                                     jnp.zeros((bp, CONV_W - 1, LRU_WIDTH), jnp.float32),
                                     jnp.zeros((bp, LRU_WIDTH), jnp.float32), True)
            past = cache_kv[e, page_table].reshape(DEC_BATCH, n_pages * PAGE_SIZE, 4, KV_GROUPS, HEAD_DIM)
            out_s, st_s = even_mixer(rmsnorm(hs, g_mix[i]), lp, rel_bias, past, cache_win[e],
                                     state_conv[e], state_h[e], False)
            kvp.append(st_p[0]); wnp.append(st_p[1]); cvp.append(st_p[2]); hhp.append(st_p[3])
            kvs.append(st_s[0]); wns.append(st_s[1]); cvs.append(st_s[2]); hhs.append(st_s[3])
        else:
            o = i // 2
            out_p, _ = chunk_gmlp(rmsnorm(hp, g_mix[i]), w_in_odd[o], ln_v_g[o], ln_v_b[o],
                                  sgu_w[o], sgu_b[o], w_out_odd[o])
            out_s, v_new = chunk_gmlp(rmsnorm(hs, g_mix[i]), w_in_odd[o], ln_v_g[o], ln_v_b[o],
                                      sgu_w[o], sgu_b[o], w_out_odd[o])
            vvs.append(v_new)
        hp = channel_and_ple(hp + out_p, p_prompt[i], g_mlp[i], w_up[i], w_down[i], g_ple[i],
                             w_ple_gate[i], w_ple_proj[i])
        hs = channel_and_ple(hs + out_s, p_sample[i], g_mlp[i], w_up[i], w_down[i], g_ple[i],
                             w_ple_gate[i], w_ple_proj[i])
    y_prompt = hp
    y_sample = hs
    kv_rows_prompt = jnp.stack(kvp)
    kv_rows_sample = jnp.stack(kvs)
    win_prompt = jnp.stack(wnp)
    win_sample = jnp.stack(wns)
    conv_prompt = jnp.stack(cvp)
    conv_sample = jnp.stack(cvs)
    h_prompt = jnp.stack(hhp)
    h_sample = jnp.stack(hhs)
    v_sample = jnp.stack(vvs) if vvs else jnp.zeros((0, DEC_BATCH, DEC_SEQ, SGU_WIDTH), jnp.float32)
    return (y_prompt, y_sample, kv_rows_prompt, kv_rows_sample, win_prompt, win_sample,
            conv_prompt, conv_sample, h_prompt, h_sample, v_sample)
```

```python
import functools
import math

import numpy as np
import jax
import jax.numpy as jnp
from jax import lax
from jax.experimental import pallas as pl
from jax.experimental.pallas import tpu as pltpu

F32 = jnp.float32
BF16 = jnp.bfloat16

EPS = 1e-6
NEG = -1e30
RG_C = 8.0
CMP_BLOCK = 32
SEL_BLOCK = 64
TOP_N = 16
WINDOW = 512
Q_BLOCK = 128
FORCE_SCORE = 1.0e4
MAX_DIST = 128
CHUNK = 128

LANES = 128
VMEM_LIMIT = 56 * 1024 * 1024

NK_SEL = 256
NKS_SAMPLE = 2048


def _cparams(n_axes=1):
    return pltpu.CompilerParams(dimension_semantics=("arbitrary",) * n_axes,
                                vmem_limit_bytes=VMEM_LIMIT)


def _const_spec(shape):
    nd = len(shape)
    return pl.BlockSpec(shape, lambda *_: (0,) * nd, pipeline_mode=pl.Buffered(1))


def _dot(a, b):
    return jnp.dot(a, b, preferred_element_type=F32)


def _dot_nt(a, b):
    return lax.dot_general(a, b, (((1,), (1,)), ((), ())), preferred_element_type=F32)


def _gelu(x):
    return 0.5 * x * (1.0 + jnp.tanh(math.sqrt(2.0 / math.pi) * (x + 0.044715 * (x * x * x))))


def _softplus(x):
    return jnp.maximum(x, 0.0) + jnp.log1p(jnp.exp(-jnp.abs(x)))


def _rms_rows(x, g):
    return x * lax.rsqrt(jnp.mean(x * x, axis=-1, keepdims=True) + EPS) * g


def _group_rmsnorm(x, gain, width):
    n = x.shape[-1] // width
    lane = lax.broadcasted_iota(jnp.int32, x.shape, 1)
    x2 = x * x
    ms = jnp.zeros_like(x)
    for i in range(n):
        inside = (lane >= i * width) & (lane < (i + 1) * width)
        s = jnp.sum(jnp.where(inside, x2, 0.0), axis=-1, keepdims=True)
        ms = jnp.where(inside, s, ms)
    return x * lax.rsqrt(ms * (1.0 / width) + EPS) * gain


def _bucket_np(dist, n_buckets):
    n = np.maximum(dist, 0)
    exact = n_buckets // 2
    nf = np.maximum(n, 1).astype(np.float32)
    large = exact + (np.log(nf / np.float32(exact)) / np.float32(math.log(MAX_DIST / exact))
                     * np.float32(n_buckets - exact)).astype(np.int32)
    return np.where(n < exact, n, np.minimum(large, n_buckets - 1)).astype(np.int32)


def _bias_tile(rel_bias, dist, valid, lane_heads, base=None, invalid=NEG):
    nb = rel_bias.shape[0]
    bucket = _bucket_np(dist, nb)
    tbl = rel_bias.astype(F32)
    if base is not None:
        tbl = tbl - base[None, :]
    vals = tbl[jnp.asarray(bucket)]
    vals = jnp.where(jnp.asarray(valid)[:, :, None], vals, invalid)
    vals = vals[:, :, jnp.asarray(lane_heads)]
    r, c = dist.shape
    return jnp.transpose(vals, (0, 2, 1)).reshape(r, len(lane_heads) * c)


def _even_in_prompt_kernel(x_ref, g_ref, wn_ref, wt_ref, qg_ref, kg_ref,
                           xa_ref, ga_ref, kv_ref, win_ref, kc_ref, vc_ref,
                           ks_ref, kw_ref, qT_ref, vsT_ref, vwT_ref, gT_ref, *, lw, qw, kvw, hd):
    tm = x_ref.shape[0]
    hn = _rms_rows(x_ref[...], g_ref[...]).astype(BF16)
    zn = _dot(hn, wn_ref[...])
    xa_ref[...] = zn[:, :lw]
    ga_ref[...] = zn[:, lw:2 * lw]
    kv = [zn[:, 2 * lw + i * kvw: 2 * lw + (i + 1) * kvw] for i in range(6)]
    k_sel = _group_rmsnorm(kv[2], kg_ref[1:2, :], hd)
    k_win = _group_rmsnorm(kv[4], kg_ref[2:3, :], hd)
    kv_ref[:, 0 * kvw:1 * kvw] = kv[0]
    kv_ref[:, 1 * kvw:2 * kvw] = kv[1]
    kv_ref[:, 2 * kvw:3 * kvw] = k_sel
    kv_ref[:, 3 * kvw:4 * kvw] = kv[3]
    win_ref[:, :kvw] = k_win
    win_ref[:, kvw:] = kv[5]
    kc_ref[...] = kv[0]
    vc_ref[...] = kv[1]
    ks_ref[...] = k_sel.astype(BF16)
    kw_ref[...] = k_win.astype(BF16)

    zt = _dot_nt(wt_ref[...], hn)
    for h in range(qw // hd):
        blk = zt[h * hd:(h + 1) * hd, :]
        r = lax.rsqrt(jnp.mean(blk * blk, axis=0, keepdims=True) + EPS)
        qn = blk * r
        for j in range(0, tm, LANES):
            qT_ref[h * hd:(h + 1) * hd, j:j + LANES] = (
                qn[:, j:j + LANES] * qg_ref[h * hd:(h + 1) * hd, :]).astype(BF16)
    vsT_ref[...] = zt[qw:qw + kvw, :].astype(BF16)
    vwT_ref[...] = zt[qw + kvw:qw + 2 * kvw, :].astype(BF16)
    gT_ref[...] = jax.nn.sigmoid(zt[qw + 2 * kvw:, :])


def _even_in_prompt(x, g_mix, w_in, q_gain, k_gain, dims, tm=512):
    t, d = x.shape
    lw, qw, kvw, hd, heads = dims
    tm = min(tm, t)
    c0, c1, c2 = 2 * lw, 2 * lw + qw, 2 * lw + qw + 6 * kvw
    wn = jnp.concatenate([w_in[:, :c0], w_in[:, c1:c2]], axis=1).astype(BF16)
    wg = w_in[:, c2:c2 + 3 * heads].reshape(d, heads, 3).transpose(0, 2, 1).reshape(d, 3 * heads)
    wt = jnp.concatenate([w_in[:, c0:c1], w_in[:, c1 + 3 * kvw:c1 + 4 * kvw],
                          w_in[:, c1 + 5 * kvw:c1 + 6 * kvw], wg,
                          jnp.zeros((d, 32 - 3 * heads), F32)], axis=1).T.astype(BF16)
    qg = jnp.broadcast_to((jnp.tile(q_gain, heads) * hd ** -0.5)[:, None], (qw, LANES)).astype(F32)
    kg = jnp.tile(k_gain, (1, kvw // hd)).astype(F32)
    kg = jnp.concatenate([kg, jnp.zeros((5, kvw), F32)], axis=0)
    nt = wt.shape[0]
    row = lambda w: pl.BlockSpec((tm, w), lambda i: (i, 0))
    col = lambda r: pl.BlockSpec((r, tm), lambda i: (0, i))
    outs = pl.pallas_call(
        functools.partial(_even_in_prompt_kernel, lw=lw, qw=qw, kvw=kvw, hd=hd),
        grid=(t // tm,),
        in_specs=[row(d), _const_spec((1, d)), _const_spec(wn.shape), _const_spec(wt.shape),
                  _const_spec(qg.shape), _const_spec(kg.shape)],
        out_specs=[row(lw), row(lw), row(4 * kvw), row(2 * kvw), row(kvw), row(kvw),
                   row(kvw), row(kvw), col(qw), col(kvw), col(kvw), col(32)],
        out_shape=[jax.ShapeDtypeStruct((t, lw), F32), jax.ShapeDtypeStruct((t, lw), F32),
                   jax.ShapeDtypeStruct((t, 4 * kvw), F32), jax.ShapeDtypeStruct((t, 2 * kvw), F32),
                   jax.ShapeDtypeStruct((t, kvw), F32), jax.ShapeDtypeStruct((t, kvw), F32),
                   jax.ShapeDtypeStruct((t, kvw), BF16), jax.ShapeDtypeStruct((t, kvw), BF16),
                   jax.ShapeDtypeStruct((qw, t), BF16), jax.ShapeDtypeStruct((kvw, t), BF16),
                   jax.ShapeDtypeStruct((kvw, t), BF16), jax.ShapeDtypeStruct((32, t), F32)],
        compiler_params=_cparams(), name="even_in_prompt",
    )(x, g_mix[None, :], wn, wt, qg, kg)
    return outs


def _even_in_sample_kernel(x_ref, g_ref, w_ref, qg_ref, kg_ref,
                           xa_ref, ga_ref, q_ref, kv_ref, win_ref, gate_ref, *, lw, qw, kvw, hd):
    hn = _rms_rows(x_ref[...], g_ref[...]).astype(BF16)
    z = _dot(hn, w_ref[...])
    xa_ref[...] = z[:, :lw]
    ga_ref[...] = z[:, lw:2 * lw]
    q_ref[...] = _group_rmsnorm(z[:, 2 * lw:2 * lw + qw], qg_ref[...], hd)
    o = 2 * lw + qw
    kv = [z[:, o + i * kvw: o + (i + 1) * kvw] for i in range(6)]
    kv_ref[:, 0 * kvw:1 * kvw] = kv[0]
    kv_ref[:, 1 * kvw:2 * kvw] = kv[1]
    kv_ref[:, 2 * kvw:3 * kvw] = _group_rmsnorm(kv[2], kg_ref[1:2, :], hd)
    kv_ref[:, 3 * kvw:4 * kvw] = kv[3]
    win_ref[:, :kvw] = _group_rmsnorm(kv[4], kg_ref[2:3, :], hd)
    win_ref[:, kvw:] = kv[5]
    gate_ref[...] = jax.nn.sigmoid(z[:, o + 6 * kvw:])


def _even_in_sample(x, g_mix, w_in, q_gain, k_gain, dims):
    r, d = x.shape
    lw, qw, kvw, hd, heads = dims
    n_in = w_in.shape[1]
    pad = (-n_in) % LANES
    w = jnp.pad(w_in, ((0, 0), (0, pad))).astype(BF16)
    gw = n_in + pad - (2 * lw + qw + 6 * kvw)
    qg = (jnp.tile(q_gain, heads) * hd ** -0.5)[None, :].astype(F32)
    kg = jnp.tile(k_gain, (1, kvw // hd)).astype(F32)
    kg = jnp.concatenate([kg, jnp.zeros((5, kvw), F32)], axis=0)
    full = lambda s: pl.BlockSpec(s, lambda i: (0,) * len(s))
    return pl.pallas_call(
        functools.partial(_even_in_sample_kernel, lw=lw, qw=qw, kvw=kvw, hd=hd),
        grid=(1,),
        in_specs=[full((r, d)), full((1, d)), full(w.shape), full(qg.shape), full(kg.shape)],
        out_specs=[full((r, lw)), full((r, lw)), full((r, qw)), full((r, 4 * kvw)),
                   full((r, 2 * kvw)), full((r, gw))],
        out_shape=[jax.ShapeDtypeStruct((r, lw), F32), jax.ShapeDtypeStruct((r, lw), F32),
                   jax.ShapeDtypeStruct((r, qw), F32), jax.ShapeDtypeStruct((r, 4 * kvw), F32),
                   jax.ShapeDtypeStruct((r, 2 * kvw), F32), jax.ShapeDtypeStruct((r, gw), F32)],
        compiler_params=_cparams(), name="even_in_sample",
    )(x, g_mix[None, :], w, qg, kg)


def _rglru_gates(conv, wa_ref, ba_ref, wx_ref, bx_ref, lam_ref):
    cb = conv.astype(BF16)
    r = jax.nn.sigmoid(_dot(cb, wa_ref[...]) + ba_ref[...])
    i = jax.nn.sigmoid(_dot(cb, wx_ref[...]) + bx_ref[...])
    log_a = -RG_C * r * _softplus(-lam_ref[...])
    a = jnp.exp(log_a)
    th = jnp.tanh(log_a)
    b = jnp.sqrt(-2.0 * th / (1.0 - th)) * (i * conv)
    return a, b


def _rglru_prompt_kernel(xa_ref, ga_ref, cw_ref, cb_ref, wa_ref, ba_ref, wx_ref, bx_ref, lam_ref,
                         ya_ref, hl_ref, xext_ref, h_ref):
    tr = xa_ref.shape[0]

    @pl.when(pl.program_id(0) == 0)
    def _():
        xext_ref[0:8, :] = jnp.zeros((8, xext_ref.shape[1]), F32)
        h_ref[...] = jnp.zeros(h_ref.shape, F32)

    x = xa_ref[...]
    xext_ref[8:8 + tr, :] = x
    conv = (cb_ref[...] + cw_ref[3:4, :] * x + cw_ref[2:3, :] * xext_ref[7:7 + tr, :]
            + cw_ref[1:2, :] * xext_ref[6:6 + tr, :] + cw_ref[0:1, :] * xext_ref[5:5 + tr, :])
    xext_ref[0:8, :] = xext_ref[tr:tr + 8, :]
    a, b = _rglru_gates(conv, wa_ref, ba_ref, wx_ref, bx_ref, lam_ref)
    row = lax.broadcasted_iota(jnp.int32, a.shape, 0)
    s = 1
    while s < tr:
        keep = row >= s
        a_sh = jnp.where(keep, pltpu.roll(a, s, 0), 1.0)
        b_sh = jnp.where(keep, pltpu.roll(b, s, 0), 0.0)
        b = a * b_sh + b
        a = a * a_sh
        s *= 2
    hs = a * h_ref[0:1, :] + b
    h_last = hs[tr - 1:tr, :]
    h_ref[...] = jnp.broadcast_to(h_last, h_ref.shape)
    hl_ref[...] = h_last
    ya_ref[...] = (hs * _gelu(ga_ref[...])).astype(BF16)


def _block_diag(w):
    n, a, b = w.shape
    eye = jnp.eye(n, dtype=w.dtype)
    return (eye[:, None, :, None] * w[:, :, None, :]).reshape(n * a, n * b)


def _rglru_weights(conv_w, conv_b, w_a, b_a, w_x, b_x, lam):
    return (conv_w.astype(F32), conv_b[None, :], _block_diag(w_a).astype(BF16), b_a[None, :],
            _block_diag(w_x).astype(BF16), b_x[None, :], lam[None, :])


def _rglru_prompt(xa, ga, weights, tr=256):
    t, lw = xa.shape
    tr = min(tr, t)
    row = pl.BlockSpec((tr, lw), lambda i: (i, 0))
    return pl.pallas_call(
        _rglru_prompt_kernel, grid=(t // tr,),
        in_specs=[row, row] + [_const_spec(w.shape) for w in weights],
        out_specs=[row, pl.BlockSpec((1, lw), lambda i: (0, 0))],
        out_shape=[jax.ShapeDtypeStruct((t, lw), BF16), jax.ShapeDtypeStruct((1, lw), F32)],
        scratch_shapes=[pltpu.VMEM((tr + 8, lw), F32), pltpu.VMEM((8, lw), F32)],
        compiler_params=_cparams(), name="rglru_prompt",
    )(xa, ga, *weights)


def _rglru_sample_kernel(xa_ref, ga_ref, sc_ref, h0_ref, cw_ref, cb_ref, wa_ref, ba_ref, wx_ref,
                         bx_ref, lam_ref, ya_ref, h_ref):
    conv = (cb_ref[...] + cw_ref[3:4, :] * xa_ref[...] + cw_ref[2:3, :] * sc_ref[2]
            + cw_ref[1:2, :] * sc_ref[1] + cw_ref[0:1, :] * sc_ref[0])
    a, b = _rglru_gates(conv, wa_ref, ba_ref, wx_ref, bx_ref, lam_ref)
    h = a * h0_ref[...] + b
    h_ref[...] = h
    ya_ref[...] = (h * _gelu(ga_ref[...])).astype(BF16)


def _rglru_sample(xa, ga, state_conv, state_h, weights):
    r, lw = xa.shape
    sc = jnp.transpose(state_conv, (1, 0, 2))
    args = (xa, ga, sc, state_h) + tuple(weights)
    full = lambda s: pl.BlockSpec(s, lambda i: (0,) * len(s))
    return pl.pallas_call(
        _rglru_sample_kernel, grid=(1,),
        in_specs=[full(a.shape) for a in args],
        out_specs=[full((r, lw)), full((r, lw))],
        out_shape=[jax.ShapeDtypeStruct((r, lw), BF16), jax.ShapeDtypeStruct((r, lw), F32)],
        compiler_params=_cparams(), name="rglru_sample",
    )(*args)


def _compress_prompt_kernel(xk_ref, xv_ref, pek_ref, pev_ref, wk_ref, wvT_ref, kg_ref,
                            kc_ref, vcT_ref, *, hd):
    xk = (xk_ref[...] + pek_ref[...]).astype(BF16)
    kc = _dot(xk, wk_ref[...])
    kc_ref[...] = _group_rmsnorm(kc, kg_ref[...], hd).astype(BF16)
    xv = (xv_ref[...] + pev_ref[...]).astype(BF16)
    vcT_ref[...] = _dot_nt(wvT_ref[...], xv).astype(BF16)


def _compress_weight(w, groups, hd):
    w3 = w.reshape(CMP_BLOCK, hd, hd)
    eye = jnp.eye(groups, dtype=w.dtype)
    full = eye[None, :, None, :, None] * w3[:, None, :, None, :]
    return full.reshape(CMP_BLOCK * groups * hd, groups * hd)


def _compress_prompt(kcmp, vcmp, w_k, w_v, pe_k, pe_v, k_gain_cmp, groups, hd):
    t, kvw = kcmp.shape
    ncb = t // CMP_BLOCK
    kdim = CMP_BLOCK * kvw
    xk = kcmp.reshape(ncb, kdim)
    xv = vcmp.reshape(ncb, kdim)
    wk = _compress_weight(w_k, groups, hd).astype(BF16)
    wvT = _compress_weight(w_v, groups, hd).T.astype(BF16)
    pek = jnp.tile(pe_k, (1, groups)).reshape(1, kdim)
    pev = jnp.tile(pe_v, (1, groups)).reshape(1, kdim)
    kg = jnp.tile(k_gain_cmp, groups)[None, :]
    nb = min(LANES, ncb)
    return pl.pallas_call(
        functools.partial(_compress_prompt_kernel, hd=hd), grid=(ncb // nb,),
        in_specs=[pl.BlockSpec((nb, kdim), lambda i: (i, 0)), pl.BlockSpec((nb, kdim), lambda i: (i, 0)),
                  _const_spec((1, kdim)), _const_spec((1, kdim)), _const_spec(wk.shape),
                  _const_spec(wvT.shape), _const_spec((1, kvw))],
        out_specs=[pl.BlockSpec((nb, kvw), lambda i: (i, 0)), pl.BlockSpec((kvw, nb), lambda i: (0, i))],
        out_shape=[jax.ShapeDtypeStruct((ncb, kvw), BF16), jax.ShapeDtypeStruct((kvw, ncb), BF16)],
        compiler_params=_cparams(), name="compress_prompt",
    )(xk, xv, pek, pev, wk, wvT, kg)


def _tile_lanes(x, n):
    return jnp.concatenate([x] * n, axis=1)


def _rows_repeat(x, rep):
    return jnp.concatenate([jnp.broadcast_to(x[i:i + 1, :], (rep, x.shape[1]))
                            for i in range(x.shape[0])], axis=0)


def _mask_rows(selb_ref, g, c, bpc):
    if bpc % 8 == 0:
        return selb_ref[g, pl.ds(pl.multiple_of(c * bpc, bpc), bpc), :]
    per = 8 // bpc
    tile = selb_ref[g, pl.ds(pl.multiple_of((c // per) * 8, 8), 8), :]
    out = tile[0:bpc, :]
    for k in range(1, per):
        out = jnp.where(c % per == k, tile[k * bpc:(k + 1) * bpc, :], out)
    return out


def _online_update(s, vT, g, slot, m_ref, l_ref, acc_ref):
    m_old = m_ref[slot, g]
    m_new = jnp.maximum(m_old, jnp.max(s, axis=0, keepdims=True))
    alpha = jnp.exp(m_old - m_new)
    p = jnp.exp(s - m_new)
    l_ref[slot, g] = alpha * l_ref[slot, g] + jnp.sum(p, axis=0, keepdims=True)
    acc_ref[slot, g] = alpha * acc_ref[slot, g] + _dot(vT, p.astype(BF16))
    m_ref[slot, g] = m_new


def _nsa_prompt_kernel(qT_ref, gT_ref, kc_ref, vcT_ref, ks_ref, vsT_ref, kw_ref, vwT_ref,
                       b31_ref, ctile_ref, stile_ref, wtile_ref,
                       y_ref,
                       sc_ref, p4_ref, selb_ref, m_ref, l_ref, acc_ref, *, groups, hpg, hd, nk):
    qb = pl.program_id(0)
    t0 = qb * Q_BLOCK
    ncb = kc_ref.shape[0]
    nsb = ncb * CMP_BLOCK // SEL_BLOCK
    gl = hpg * Q_BLOCK
    n_top = min(TOP_N, nsb)

    rows = []
    for g in range(groups):
        top = jnp.concatenate([qT_ref[(g * hpg + hh) * hd:(g * hpg + hh + 1) * hd, :]
                               for hh in range(hpg)], axis=1)
        z = jnp.zeros_like(top)
        rows.append(jnp.concatenate([top if gg == g else z for gg in range(groups)], axis=1))
    qbd = jnp.concatenate(rows, axis=0)
    b31 = b31_ref[...]

    jrow = lax.broadcasted_iota(jnp.int32, (ncb, Q_BLOCK), 0)
    qcol = lax.broadcasted_iota(jnp.int32, (ncb, Q_BLOCK), 1)
    vis = _tile_lanes(jrow * CMP_BLOCK + (CMP_BLOCK - 1) <= t0 + qcol, groups * hpg)
    s_c = _dot(kc_ref[...], qbd) + b31
    sc_ref[...] = jnp.where(vis, s_c, NEG)
    odd = (qb % 2) == 1
    v_a = jnp.where(qb == 0, 0, jnp.where(odd, 1, 2))
    w_a = pl.multiple_of(jnp.where(qb == 0, 0, jnp.where(odd, 4 * qb - 4, 4 * qb - 8)), 8)
    v_b = jnp.where((qb > 0) & jnp.logical_not(odd), 0, 3)
    w_b = pl.multiple_of(jnp.where((qb > 0) & jnp.logical_not(odd), 4 * qb, 0), 8)
    sc_ref[pl.ds(w_a, 8), :] = sc_ref[pl.ds(w_a, 8), :] + ctile_ref[v_a]
    sc_ref[pl.ds(w_b, 8), :] = sc_ref[pl.ds(w_b, 8), :] + ctile_ref[v_b]
    s_c = sc_ref[...]
    m_c = jnp.max(s_c, axis=0, keepdims=True)
    p_c = jnp.where(vis, jnp.exp(s_c - m_c), 0.0)
    l_c = jnp.sum(p_c, axis=0, keepdims=True)
    p_c = p_c * jnp.where(l_c > 0.0, 1.0 / l_c, 0.0)
    o_cmp = []
    for g in range(groups):
        pg = p_c[:, g * gl:(g + 1) * gl]
        o_cmp.append(_dot(vcT_ref[g * hd:(g + 1) * hd, :], pg.astype(BF16)))
        p4 = pg[:, 0:Q_BLOCK]
        for hh in range(1, hpg):
            p4 = p4 + pg[:, hh * Q_BLOCK:(hh + 1) * Q_BLOCK]
        p4_ref[g] = p4

    ratio = SEL_BLOCK // CMP_BLOCK
    jb = lax.broadcasted_iota(jnp.int32, (nsb, Q_BLOCK), 0)
    qc = lax.broadcasted_iota(jnp.int32, (nsb, Q_BLOCK), 1)
    cur = (t0 + qc) // SEL_BLOCK
    forced = (jb == 0) | (jb == cur) | (jb == cur - 1)
    for g in range(groups):
        imp = p4_ref[g, pl.ds(0, nsb, stride=ratio), :]
        for k in range(1, ratio):
            imp = imp + p4_ref[g, pl.ds(k, nsb, stride=ratio), :]
        imp = jnp.where(forced, FORCE_SCORE, imp)
        imp = jnp.where(jb <= cur, imp, -1.0)
        selb = jnp.full((nsb, Q_BLOCK), NEG, F32)
        for _ in range(n_top):
            mx = jnp.max(imp, axis=0, keepdims=True)
            idx = jnp.min(jnp.where(imp == mx, jb, nsb), axis=0, keepdims=True)
            hit = jb == idx
            selb = jnp.where(hit, 0.0, selb)
            imp = jnp.where(hit, -3.0e38, imp)
        selb_ref[g] = selb

    m_ref[...] = jnp.full(m_ref.shape, NEG, F32)
    l_ref[...] = jnp.zeros(l_ref.shape, F32)
    acc_ref[...] = jnp.zeros(acc_ref.shape, F32)
    bpc = nk // SEL_BLOCK

    def sel_chunk(c, tile_idx):
        k0 = pl.multiple_of(c * nk, nk)
        s = _dot(ks_ref[pl.ds(k0, nk), :], qbd)
        for g in range(groups):
            mrow = _mask_rows(selb_ref, g, c, bpc)
            sg = s[:, g * gl:(g + 1) * gl] + _tile_lanes(_rows_repeat(mrow, SEL_BLOCK), hpg)
            if tile_idx is None:
                sg = sg + b31[:, g * gl:(g + 1) * gl]
            else:
                sg = sg + stile_ref[tile_idx, :, g * gl:(g + 1) * gl]
            _online_update(sg, vsT_ref[g * hd:(g + 1) * hd, pl.ds(k0, nk)], g, 0, m_ref, l_ref, acc_ref)

    c_d = t0 // nk
    off = (t0 - c_d * nk) // Q_BLOCK

    def far_body(c, carry):
        sel_chunk(c, None)
        return carry

    lax.fori_loop(0, jnp.maximum(c_d - 1, 0), far_body, 0)

    @pl.when(c_d >= 1)
    def _():
        sel_chunk(c_d - 1, (nk // Q_BLOCK) + off)

    sel_chunk(c_d, off)

    n_wc = WINDOW // Q_BLOCK + 1
    for i in reversed(range(n_wc)):
        k0 = t0 - WINDOW + Q_BLOCK * i
        penalty = jnp.where(k0 >= 0, 0.0, NEG)
        k0c = pl.multiple_of(jnp.maximum(k0, 0), Q_BLOCK)
        s = _dot(kw_ref[pl.ds(k0c, Q_BLOCK), :], qbd)
        for g in range(groups):
            sg = s[:, g * gl:(g + 1) * gl] + wtile_ref[i, :, g * gl:(g + 1) * gl] + penalty
            _online_update(sg, vwT_ref[g * hd:(g + 1) * hd, pl.ds(k0c, Q_BLOCK)], g, 1,
                           m_ref, l_ref, acc_ref)

    heads = groups * hpg
    pieces = []
    for g in range(groups):
        gate = [jnp.concatenate([gT_ref[j * heads + g * hpg + hh: j * heads + g * hpg + hh + 1, :]
                                 for hh in range(hpg)], axis=1) for j in range(3)]
        o = (gate[0] * o_cmp[g]
             + gate[1] * (acc_ref[0, g] * (1.0 / l_ref[0, g]))
             + gate[2] * (acc_ref[1, g] * (1.0 / l_ref[1, g])))
        for hh in range(hpg):
            pieces.append(o[:, hh * Q_BLOCK:(hh + 1) * Q_BLOCK])
    y_ref[...] = jnp.concatenate(pieces, axis=0).T.astype(BF16)


def _nsa_prompt_tables(rel_bias, groups, hpg, nk):
    heads = groups * hpg
    lane_heads = np.arange(heads)
    b31v = rel_bias[-1].astype(F32)
    c = np.arange(Q_BLOCK)[None, :]
    i8 = np.arange(8)[:, None]
    cts = []
    for base in (-(CMP_BLOCK - 1), 3 * CMP_BLOCK + 1, 7 * CMP_BLOCK + 1):
        d = base + c - CMP_BLOCK * i8
        cts.append(_bias_tile(rel_bias, d, d >= 0, lane_heads, base=b31v, invalid=0.0))
    cts.append(jnp.zeros_like(cts[0]))
    ctile = jnp.stack(cts)
    r = np.arange(nk)[:, None]
    sts = []
    for extra in (0, nk):
        for o in range(nk // Q_BLOCK):
            d = o * Q_BLOCK + extra + c - r
            sts.append(_bias_tile(rel_bias, d, d >= 0, lane_heads))
    stile = jnp.stack(sts)
    rq = np.arange(Q_BLOCK)[:, None]
    wts = []
    for i in range(WINDOW // Q_BLOCK + 1):
        d = WINDOW - Q_BLOCK * i + c - rq
        wts.append(_bias_tile(rel_bias, d, (d >= 0) & (d < WINDOW), lane_heads))
    wtile = jnp.stack(wts)
    b31 = jnp.repeat(b31v, Q_BLOCK)[None, :]
    return b31, ctile, stile, wtile


def _nsa_prompt(qT, gT, kc, vcT, ks, vsT, kw, vwT, rel_bias, groups, hpg, hd):
    qw, t = qT.shape
    ncb = kc.shape[0]
    nsb = t // SEL_BLOCK
    nk = min(NK_SEL, t)
    gl = hpg * Q_BLOCK
    tables = _nsa_prompt_tables(rel_bias, groups, hpg, nk)
    resident = (kc, vcT, ks, vsT, kw, vwT) + tuple(tables)
    return pl.pallas_call(
        functools.partial(_nsa_prompt_kernel, groups=groups, hpg=hpg, hd=hd, nk=nk),
        grid=(t // Q_BLOCK,),
        in_specs=[pl.BlockSpec((qw, Q_BLOCK), lambda i: (0, i)),
                  pl.BlockSpec((gT.shape[0], Q_BLOCK), lambda i: (0, i))]
                 + [_const_spec(a.shape) for a in resident],
        out_specs=pl.BlockSpec((Q_BLOCK, qw), lambda i: (i, 0)),
        out_shape=jax.ShapeDtypeStruct((t, qw), BF16),
        scratch_shapes=[pltpu.VMEM((ncb, groups * gl), F32),
                        pltpu.VMEM((groups, ncb, Q_BLOCK), F32),
                        pltpu.VMEM((groups, nsb, Q_BLOCK), F32),
                        pltpu.VMEM((2, groups, 1, gl), F32),
                        pltpu.VMEM((2, groups, 1, gl), F32),
                        pltpu.VMEM((2, groups, hd, gl), F32)],
        compiler_params=_cparams(), name="nsa_prompt",
    )(qT, gT, *resident)


def _split3(x):
    hi = x.astype(BF16)
    r1 = x - hi.astype(F32)
    mid = r1.astype(BF16)
    lo = (r1 - mid.astype(F32)).astype(BF16)
    return hi, mid, lo


def _softmax_pv(s, v_bf16):
    m = jnp.max(s, axis=0, keepdims=True)
    p = jnp.exp(s - m)
    l = jnp.sum(p, axis=0, keepdims=True)
    return _dot((p * (1.0 / l)).T.astype(BF16), v_bf16)


def _nsa_sample_kernel(pt_ref, cache_ref, qbd_ref, gates_ref, kvn_ref, winn_ref, cwin_ref,
                       wk_ref, wv_ref, pek_ref, pev_ref, kg_ref, gsum_ref,
                       cb_ref, sbn_ref, wb_ref, b31_ref, b0_ref,
                       o_ref,
                       bufk_ref, bufv_ref, bufb_ref, sema, semb, p_ref, selb_ref, s_ref, *, hd, n_pages, page):
    b = pl.program_id(0)
    nb = pl.num_programs(0)
    kvw = kg_ref.shape[1]
    past = n_pages * page
    ncb = past // CMP_BLOCK
    nsb = past // SEL_BLOCK
    nks = min(NKS_SAMPLE, past)

    def page_copies(bb, p, half):
        src = cache_ref.at[pt_ref[bb, p]]
        rows = pl.ds(p * page, page)
        if half == 0:
            return (pltpu.make_async_copy(src.at[:, 0:kvw], bufk_ref.at[rows, :], sema),
                    pltpu.make_async_copy(src.at[:, kvw:2 * kvw], bufv_ref.at[rows, :], sema))
        return (pltpu.make_async_copy(src.at[:, 2 * kvw:4 * kvw], bufb_ref.at[rows, :], semb),)

    def start_all(bb, half):
        def body(p, c):
            for cp in page_copies(bb, p, half):
                cp.start()
            return c
        lax.fori_loop(0, n_pages, body, 0)

    def wait_all(bb, half):
        def body(p, c):
            for cp in page_copies(bb, p, half):
                cp.wait()
            return c
        lax.fori_loop(0, n_pages, body, 0)

    @pl.when(b == 0)
    def _():
        start_all(0, 0)
        start_all(0, 1)

    qbd = qbd_ref[0]

    wait_all(b, 0)
    kc = jnp.zeros((ncb, kvw), F32)
    vc = jnp.zeros((ncb, kvw), F32)
    for r in range(CMP_BLOCK):
        xk = bufk_ref[pl.ds(r, ncb, stride=CMP_BLOCK), :] + pek_ref[r]
        kc = kc + _dot(xk.astype(BF16), wk_ref[r])
        xv = bufv_ref[pl.ds(r, ncb, stride=CMP_BLOCK), :] + pev_ref[r]
        vc = vc + _dot(xv.astype(BF16), wv_ref[r])

    @pl.when(b + 1 < nb)
    def _():
        start_all(b + 1, 0)

    kc = _group_rmsnorm(kc, kg_ref[...], hd).astype(BF16)
    s_c = _dot(kc, qbd) + cb_ref[...]
    m_c = jnp.max(s_c, axis=0, keepdims=True)
    p_c = jnp.exp(s_c - m_c)
    p_c = p_c * (1.0 / jnp.sum(p_c, axis=0, keepdims=True))
    o_cmp = _dot(p_c.T.astype(BF16), vc.astype(BF16))
    p_ref[...] = p_c
    ratio = SEL_BLOCK // CMP_BLOCK
    imp = p_ref[pl.ds(0, nsb, stride=ratio), :]
    for k in range(1, ratio):
        imp = imp + p_ref[pl.ds(k, nsb, stride=ratio), :]
    hi, mid, lo = _split3(imp)
    gs = gsum_ref[...]
    imp = _dot(hi, gs) + _dot(mid, gs) + _dot(lo, gs)
    jb = lax.broadcasted_iota(jnp.int32, (nsb, LANES), 0)
    imp = jnp.where((jb == 0) | (jb == nsb - 1), FORCE_SCORE, imp)
    selb = jnp.full((nsb, LANES), NEG, F32)
    for _ in range(min(TOP_N, nsb + 1) - 1):
        mx = jnp.max(imp, axis=0, keepdims=True)
        idx = jnp.min(jnp.where(imp == mx, jb, nsb), axis=0, keepdims=True)
        hit = jb == idx
        selb = jnp.where(hit, 0.0, selb)
        imp = jnp.where(hit, -3.0e38, imp)
    selb_ref[...] = selb

    wait_all(b, 1)
    row0 = lax.broadcasted_iota(jnp.int32, (LANES, kvw), 0) == 0
    kvn = kvn_ref[0]
    k_new = jnp.where(row0, jnp.broadcast_to(kvn[:, 2 * kvw:3 * kvw], (LANES, kvw)), 0.0)
    v_new = jnp.where(row0, jnp.broadcast_to(kvn[:, 3 * kvw:4 * kvw], (LANES, kvw)), 0.0)
    row0l = lax.broadcasted_iota(jnp.int32, (LANES, LANES), 0) == 0
    s_new = jnp.where(row0l, _dot(k_new.astype(BF16), qbd) + b0_ref[...], NEG)
    s_ref[past:past + LANES, :] = s_new
    m_s = jnp.max(s_new, axis=0, keepdims=True)
    bps = nks // SEL_BLOCK
    for c in range(past // nks):
        s = _dot(bufb_ref[c * nks:(c + 1) * nks, 0:kvw].astype(BF16), qbd)
        s = s + _rows_repeat(selb_ref[c * bps:(c + 1) * bps, :], SEL_BLOCK) + b31_ref[...]
        if c == past // nks - 1:
            s = jnp.concatenate([s[:nks - LANES], s[nks - LANES:] + sbn_ref[...]], axis=0)
        s_ref[c * nks:(c + 1) * nks, :] = s
        m_s = jnp.maximum(m_s, jnp.max(s, axis=0, keepdims=True))
    l_s = jnp.zeros((1, LANES), F32)
    acc = jnp.zeros((LANES, kvw), F32)
    for c in range(past // nks):
        p = jnp.exp(s_ref[c * nks:(c + 1) * nks, :] - m_s)
        l_s = l_s + jnp.sum(p, axis=0, keepdims=True)
        acc = acc + _dot(p.T.astype(BF16), bufb_ref[c * nks:(c + 1) * nks, kvw:2 * kvw].astype(BF16))
    p = jnp.exp(s_ref[past:past + LANES, :] - m_s)
    l_s = l_s + jnp.sum(p, axis=0, keepdims=True)
    acc = acc + _dot(p.T.astype(BF16), v_new.astype(BF16))

    @pl.when(b + 1 < nb)
    def _():
        start_all(b + 1, 1)

    inv_col = jnp.broadcast_to(1.0 / l_s, (LANES, LANES)).T[:, 0:1]
    o_sel = acc * inv_col

    wbuf = cwin_ref.shape[1]
    winn = winn_ref[0]
    kw_new = jnp.where(row0, jnp.broadcast_to(winn[:, 0:kvw], (LANES, kvw)), 0.0)
    vw_new = jnp.where(row0, jnp.broadcast_to(winn[:, kvw:2 * kvw], (LANES, kvw)), 0.0)
    s_w = jnp.concatenate([
        _dot(cwin_ref[0, :, 0:kvw].astype(BF16), qbd) + wb_ref[...],
        jnp.where(row0l, _dot(kw_new.astype(BF16), qbd) + b0_ref[...], NEG)], axis=0)
    v_w = jnp.concatenate([cwin_ref[0, :, kvw:2 * kvw], vw_new], axis=0).astype(BF16)
    o_win = _softmax_pv(s_w, v_w)

    g = gates_ref[0]
    out = g[0] * o_cmp + g[1] * o_sel + g[2] * o_win
    o_ref[0] = out[0:8, :]


def _nsa_sample(q, gates, kv_new, win_new, cache_kv, cache_win, page_table, lp, rel_bias,
                groups, hpg, hd):
    r = q.shape[0]
    heads = groups * hpg
    kvw = groups * hd
    n_phys, page = cache_kv.shape[0], cache_kv.shape[1]
    n_pages = page_table.shape[1]
    past = n_pages * page
    wbuf = cache_win.shape[1]
    ncb, nsb = past // CMP_BLOCK, past // SEL_BLOCK
    lane_heads = np.concatenate([np.arange(heads), np.zeros(LANES - heads, np.int64)])
    q3 = q.reshape(r, heads, hd)
    gmask = (jnp.arange(groups)[:, None] == (jnp.arange(heads) // hpg)[None, :]).astype(F32)
    qbd = jnp.einsum('rhd,gh->rgdh', q3, gmask).reshape(r, kvw, heads)
    qbd = jnp.pad(qbd, ((0, 0), (0, 0), (0, LANES - heads))).astype(BF16)
    gts = gates[:, :3 * heads].reshape(r, heads, 3).transpose(0, 2, 1)
    gts = jnp.broadcast_to(gts[:, :, :, None], (r, 3, heads, kvw))
    gts = jnp.pad(gts, ((0, 0), (0, 0), (0, LANES - heads), (0, 0)))
    wk = _compress_weight(lp['cmp_w_k'], groups, hd).reshape(CMP_BLOCK, kvw, kvw).astype(BF16)
    wv = _compress_weight(lp['cmp_w_v'], groups, hd).reshape(CMP_BLOCK, kvw, kvw).astype(BF16)
    pek = jnp.tile(lp['pe_k'], (1, groups))[:, None, :]
    pev = jnp.tile(lp['pe_v'], (1, groups))[:, None, :]
    kg = jnp.tile(lp['k_gain'][0], groups)[None, :]
    lane_g = np.where(np.arange(LANES) < heads, np.arange(LANES) // hpg, -1)
    gsum = jnp.asarray((lane_g[:, None] == lane_g[None, :]) & (lane_g[:, None] >= 0), BF16)
    one = np.ones((1, 1), bool)
    b31v = rel_bias[-1].astype(F32)
    d_c = (past - (np.arange(ncb) * CMP_BLOCK + CMP_BLOCK - 1))[:, None]
    cb = _bias_tile(rel_bias, d_c, d_c >= 0, lane_heads)
    d_s = (past - (past - LANES + np.arange(LANES)))[:, None]
    sbn = _bias_tile(rel_bias, d_s, d_s >= 0, lane_heads, base=b31v)
    d_w = (wbuf - np.arange(wbuf))[:, None]
    wb = _bias_tile(rel_bias, d_w, d_w < WINDOW, lane_heads)
    b31 = _bias_tile(rel_bias, np.full((1, 1), 10 * MAX_DIST), one, lane_heads)
    b0 = _bias_tile(rel_bias, np.zeros((1, 1), np.int64), one, lane_heads)

    cache2 = cache_kv.reshape(n_phys, page, 4 * kvw)
    cwin = cache_win.reshape(r, wbuf, 2 * kvw)
    kvn = kv_new.reshape(r, 1, 4 * kvw)
    winn = win_new.reshape(r, 1, 2 * kvw)
    per_b = lambda s: pl.BlockSpec((1,) + s, lambda b, pt: (b,) + (0,) * len(s))
    const = lambda a: pl.BlockSpec(a.shape, lambda b, pt: (0,) * a.ndim, pipeline_mode=pl.Buffered(1))
    consts = (wk, wv, pek, pev, kg, gsum, cb, sbn, wb, b31, b0)
    grid_spec = pltpu.PrefetchScalarGridSpec(
        num_scalar_prefetch=1, grid=(r,),
        in_specs=[pl.BlockSpec(memory_space=pl.ANY), per_b((kvw, LANES)), per_b((3, LANES, kvw)),
                  per_b((1, 4 * kvw)), per_b((1, 2 * kvw)), per_b((wbuf, 2 * kvw))]
                 + [const(a) for a in consts],
        out_specs=per_b((8, kvw)),
        scratch_shapes=[pltpu.VMEM((past, kvw), F32), pltpu.VMEM((past, kvw), F32),
                        pltpu.VMEM((past, 2 * kvw), F32),
                        pltpu.SemaphoreType.DMA(()), pltpu.SemaphoreType.DMA(()),
                        pltpu.VMEM((ncb, LANES), F32), pltpu.VMEM((nsb, LANES), F32),
                        pltpu.VMEM((past + LANES, LANES), F32)])
    o8 = pl.pallas_call(
        functools.partial(_nsa_sample_kernel, hd=hd, n_pages=n_pages, page=page),
        grid_spec=grid_spec,
        out_shape=jax.ShapeDtypeStruct((r, 8, kvw), F32),
        compiler_params=_cparams(), name="nsa_sample",
    )(page_table, cache2, qbd, gts, kvn, winn, cwin, *consts)
    o4 = o8[:, :heads, :].reshape(r, heads, groups, hd)
    y = jnp.take_along_axis(o4, (jnp.arange(heads) // hpg)[None, :, None, None], axis=2)
    return y.reshape(r, heads * hd).astype(BF16)


def _mlp_ple(h, p_ref, gm_ref, wu_ref, wd_ref, gp_ref, wg_ref, wp_ref):
    up = _dot(_rms_rows(h, gm_ref[...]).astype(BF16), wu_ref[...])
    act = jnp.square(jnp.maximum(up, 0.0)).astype(BF16)
    h = h + _dot(act, wd_ref[...])
    gate = jax.nn.sigmoid(_dot(_rms_rows(h, gp_ref[...]).astype(BF16), wg_ref[...]))
    return h + gate * _dot(p_ref[...].astype(BF16), wp_ref[...])


def _even_tail_kernel(h_ref, ya_ref, yb_ref, p_ref, woa_ref, wob_ref,
                      gm_ref, wu_ref, wd_ref, gp_ref, wg_ref, wp_ref, o_ref):
    h = h_ref[...] + _dot(ya_ref[...], woa_ref[...]) + _dot(yb_ref[...], wob_ref[...])
    o_ref[...] = _mlp_ple(h, p_ref, gm_ref, wu_ref, wd_ref, gp_ref, wg_ref, wp_ref)


def _tail_weights(g_mlp, w_up, w_down, g_ple, w_gate, w_proj):
    return (g_mlp[None, :], w_up.astype(BF16), w_down.astype(BF16), g_ple[None, :],
            w_gate.astype(BF16), w_proj.astype(BF16))


def _even_tail(h, ya, yb, p, w_out, tail_w, tm=256):
    t, d = h.shape
    tm = min(tm, t)
    lw = ya.shape[1]
    woa, wob = w_out[:lw].astype(BF16), w_out[lw:].astype(BF16)
    row = lambda w: pl.BlockSpec((tm, w), lambda i: (i, 0))
    weights = (woa, wob) + tuple(tail_w)
    return pl.pallas_call(
        _even_tail_kernel, grid=(t // tm,),
        in_specs=[row(d), row(lw), row(yb.shape[1]), row(p.shape[1])]
                 + [_const_spec(w.shape) for w in weights],
        out_specs=row(d), out_shape=jax.ShapeDtypeStruct((t, d), F32),
        compiler_params=_cparams(), name="even_tail",
    )(h, ya, yb, p, *weights)


def _odd_kernel(h_ref, p_ref, gx_ref, wi_ref, lg_ref, lb_ref, sw_ref, sb_ref, wo_ref,
                gm_ref, wu_ref, wd_ref, gp_ref, wg_ref, wp_ref, o_ref, v_ref, *, sw_groups, single):
    h = h_ref[...]
    tm = h.shape[0]
    z = _gelu(_dot(_rms_rows(h, gx_ref[...]).astype(BF16), wi_ref[...]))
    width = z.shape[1] // 2
    u, v = z[:, :width], z[:, width:]
    mu = jnp.mean(v, axis=-1, keepdims=True)
    vc = v - mu
    vn = vc * lax.rsqrt(jnp.mean(vc * vc, axis=-1, keepdims=True) + EPS) * lg_ref[...] + lb_ref[...]
    v_ref[...] = vn
    gw = width // sw_groups
    if single:
        s = vn * sw_ref[...] + sb_ref[...]
    else:
        vb = vn.astype(BF16)
        parts = []
        for c in range(tm // CHUNK):
            cols = [_dot(sw_ref[g], vb[c * CHUNK:(c + 1) * CHUNK, g * gw:(g + 1) * gw])
                    for g in range(sw_groups)]
            parts.append(jnp.concatenate(cols, axis=1) + sb_ref[...])
        s = jnp.concatenate(parts, axis=0)
    h = h + _dot((u * s).astype(BF16), wo_ref[...])
    o_ref[...] = _mlp_ple(h, p_ref, gm_ref, wu_ref, wd_ref, gp_ref, wg_ref, wp_ref)


def _odd_layer(h, p, g_mix, w_in, ln_g, ln_b, sgu_w, sgu_b, w_out, tail_w, single, tm=256):
    t, d = h.shape
    tm = min(tm, t)
    ng, ch, _ = sgu_w.shape
    width = w_in.shape[1] // 2
    gw = width // ng
    if single:
        sw = jnp.repeat(sgu_w[:, 0, 0], gw)[None, :]
        sb = jnp.repeat(sgu_b[:, 0], gw)[None, :]
    else:
        sw = (sgu_w * jnp.tril(jnp.ones((ch, ch), sgu_w.dtype))).astype(BF16)
        sb = jnp.repeat(sgu_b.T, gw, axis=1)
    weights = (g_mix[None, :], w_in.astype(BF16), ln_g[None, :], ln_b[None, :], sw, sb,
               w_out.astype(BF16)) + tuple(tail_w)
    row = lambda w: pl.BlockSpec((tm, w), lambda i: (i, 0))
    return pl.pallas_call(
        functools.partial(_odd_kernel, sw_groups=ng, single=single), grid=(t // tm,),
        in_specs=[row(d), row(p.shape[1])] + [_const_spec(w.shape) for w in weights],
        out_specs=[row(d), row(width)],
        out_shape=[jax.ShapeDtypeStruct((t, d), F32), jax.ShapeDtypeStruct((t, width), F32)],
        compiler_params=_cparams(), name="odd_single" if single else "odd_prompt",
    )(h, p, *weights)


def kernel(x_prompt, x_sample, cache_kv, cache_win, state_conv, state_h, page_table, p_prompt, p_sample,
           rel_bias, g_mix, g_mlp, w_up, w_down, g_ple, w_ple_gate, w_ple_proj,
           w_in_even, w_out_even, conv_w, conv_b, rg_w_a, rg_b_a, rg_w_x, rg_b_x, rg_lambda,
           q_gain, k_gain, cmp_w_k, cmp_w_v, cmp_pe_k, cmp_pe_v,
           w_in_odd, ln_v_g, ln_v_b, sgu_w, sgu_b, w_out_odd):
    bsz, t, d = x_prompt.shape
    r = x_sample.shape[0]
    assert bsz == 1 and x_sample.shape[1] == 1
    depth = g_mix.shape[0]
    lw = conv_w.shape[-1]
    hd = q_gain.shape[-1]
    groups = cache_kv.shape[4]
    heads = rel_bias.shape[1]
    hpg = heads // groups
    qw, kvw = heads * hd, groups * hd
    dims = (lw, qw, kvw, hd, heads)
    wbuf = cache_win.shape[2]

    hp = x_prompt.reshape(t, d)
    hs = x_sample.reshape(r, d)
    kvp, kvs, wnp, wns, cvp, cvs, hhp, hhs, vvs = [], [], [], [], [], [], [], [], []
    for i in range(depth):
        tail_w = _tail_weights(g_mlp[i], w_up[i], w_down[i], g_ple[i], w_ple_gate[i], w_ple_proj[i])
        if i % 2 == 0:
            e = i // 2
            lp = {'cmp_w_k': cmp_w_k[e], 'cmp_w_v': cmp_w_v[e], 'pe_k': cmp_pe_k[e], 'pe_v': cmp_pe_v[e],
                  'k_gain': k_gain[e]}
            rg_w = _rglru_weights(conv_w[e], conv_b[e], rg_w_a[e], rg_b_a[e], rg_w_x[e], rg_b_x[e],
                                  rg_lambda[e])
            (xa, ga, kv_rows, win_rows, kcmp, vcmp, ks, kw, qT, vsT, vwT, gT) = _even_in_prompt(
                hp, g_mix[i], w_in_even[e], q_gain[e], k_gain[e], dims)
            ya, h_last = _rglru_prompt(xa, ga, rg_w)
            kc, vcT = _compress_prompt(kcmp, vcmp, cmp_w_k[e], cmp_w_v[e], cmp_pe_k[e], cmp_pe_v[e],
                                       k_gain[e, 0], groups, hd)
            yb = _nsa_prompt(qT, gT, kc, vcT, ks, vsT, kw, vwT, rel_bias, groups, hpg, hd)
            hp = _even_tail(hp, ya, yb, p_prompt[i, 0], w_out_even[e], tail_w)
            kvp.append(kv_rows.reshape(1, t, 4, groups, hd))
            wnp.append(win_rows[t - wbuf:].reshape(1, wbuf, 2, groups, hd))
            cvp.append(xa[t - (conv_w.shape[1] - 1):][None])
            hhp.append(h_last)
            xa_s, ga_s, q_s, kv_s, win_s, gate_s = _even_in_sample(
                hs, g_mix[i], w_in_even[e], q_gain[e], k_gain[e], dims)
            ya_s, h_s = _rglru_sample(xa_s, ga_s, state_conv[e], state_h[e], rg_w)
            yb_s = _nsa_sample(q_s, gate_s, kv_s, win_s, cache_kv[e], cache_win[e], page_table, lp,
                               rel_bias, groups, hpg, hd)
            hs = _even_tail(hs, ya_s, yb_s, p_sample[i, :, 0], w_out_even[e], tail_w)
            kvs.append(kv_s.reshape(r, 1, 4, groups, hd))
            wns.append(jnp.concatenate([cache_win[e][:, 1:], win_s.reshape(r, 1, 2, groups, hd)], axis=1))
            cvs.append(jnp.concatenate([state_conv[e][:, 1:], xa_s[:, None, :]], axis=1))
            hhs.append(h_s)
        else:
            o = i // 2
            hp, _ = _odd_layer(hp, p_prompt[i, 0], g_mix[i], w_in_odd[o], ln_v_g[o], ln_v_b[o],
                               sgu_w[o], sgu_b[o], w_out_odd[o], tail_w, single=False)
            hs, v_new = _odd_layer(hs, p_sample[i, :, 0], g_mix[i], w_in_odd[o], ln_v_g[o], ln_v_b[o],
                                   sgu_w[o], sgu_b[o], w_out_odd[o], tail_w, single=True)
            vvs.append(v_new[:, None, :])
    v_sample = jnp.stack(vvs) if vvs else jnp.zeros((0, r, 1, w_in_odd.shape[-1] // 2), F32)
    return (hp[None], hs[:, None, :], jnp.stack(kvp), jnp.stack(kvs), jnp.stack(wnp), jnp.stack(wns),
            jnp.stack(cvp), jnp.stack(cvs), jnp.stack(hhp), jnp.stack(hhs), v_sample)
```

```python
import functools
import math

import numpy as np
import jax
import jax.numpy as jnp
from jax import lax
from jax.experimental import pallas as pl
from jax.experimental.pallas import tpu as pltpu

F32 = jnp.float32
BF16 = jnp.bfloat16

EPS = 1e-6
NEG = -1e30
RG_C = 8.0
CMP_BLOCK = 32
SEL_BLOCK = 64
TOP_N = 16
WINDOW = 512
Q_BLOCK = 128
FORCE_SCORE = 1.0e4
MAX_DIST = 128
CHUNK = 128

LANES = 128
VMEM_LIMIT = 56 * 1024 * 1024

LOG2E = math.log2(math.e)
KEY_GROUP = 512
ACC_ROWS = 80
NKS_SAMPLE = 2048


def _cparams(n_axes=1):
    return pltpu.CompilerParams(dimension_semantics=("arbitrary",) * n_axes,
                                vmem_limit_bytes=VMEM_LIMIT)


def _const_spec(shape):
    nd = len(shape)
    return pl.BlockSpec(shape, lambda *_: (0,) * nd, pipeline_mode=pl.Buffered(1))


def _dot(a, b):
    return jnp.dot(a, b, preferred_element_type=F32)


def _dot_nt(a, b):
    return lax.dot_general(a, b, (((1,), (1,)), ((), ())), preferred_element_type=F32)


def _gelu(x):
    return 0.5 * x * (1.0 + jnp.tanh(math.sqrt(2.0 / math.pi) * (x + 0.044715 * (x * x * x))))


def _softplus(x):
    return jnp.maximum(x, 0.0) + jnp.log1p(jnp.exp(-jnp.abs(x)))


def _rms_rows(x, g):
    return x * lax.rsqrt(jnp.mean(x * x, axis=-1, keepdims=True) + EPS) * g


def _group_rmsnorm(x, gain, width):
    n = x.shape[-1] // width
    lane = lax.broadcasted_iota(jnp.int32, x.shape, 1)
    x2 = x * x
    ms = jnp.zeros_like(x)
    for i in range(n):
        inside = (lane >= i * width) & (lane < (i + 1) * width)
        s = jnp.sum(jnp.where(inside, x2, 0.0), axis=-1, keepdims=True)
        ms = jnp.where(inside, s, ms)
    return x * lax.rsqrt(ms * (1.0 / width) + EPS) * gain


def _bucket_np(dist, n_buckets):
    n = np.maximum(dist, 0)
    exact = n_buckets // 2
    nf = np.maximum(n, 1).astype(np.float32)
    large = exact + (np.log(nf / np.float32(exact)) / np.float32(math.log(MAX_DIST / exact))
                     * np.float32(n_buckets - exact)).astype(np.int32)
    return np.where(n < exact, n, np.minimum(large, n_buckets - 1)).astype(np.int32)


def _bias_tile(rel_bias, dist, valid, lane_heads, base=None, invalid=NEG):
    nb = rel_bias.shape[0]
    bucket = _bucket_np(dist, nb)
    tbl = rel_bias.astype(F32)
    if base is not None:
        tbl = tbl - base[None, :]
    vals = tbl[jnp.asarray(bucket)]
    vals = jnp.where(jnp.asarray(valid)[:, :, None], vals, invalid)
    vals = vals[:, :, jnp.asarray(lane_heads)]
    r, c = dist.shape
    return jnp.transpose(vals, (0, 2, 1)).reshape(r, len(lane_heads) * c)


def _even_in_prompt_kernel(x_ref, g_ref, wn_ref, wt_ref, qg_ref, kg_ref,
                           xa_ref, ga_ref, kv_ref, win_ref, kc_ref, vc_ref,
                           ks_ref, kw_ref, qT_ref, vsT_ref, vwT_ref, gT_ref, *, lw, qw, kvw, hd):
    tm = x_ref.shape[0]
    hn = _rms_rows(x_ref[...], g_ref[...]).astype(BF16)
    zn = _dot(hn, wn_ref[...])
    xa_ref[...] = zn[:, :lw]
    ga_ref[...] = zn[:, lw:2 * lw]
    kv = [zn[:, 2 * lw + i * kvw: 2 * lw + (i + 1) * kvw] for i in range(6)]
    k_sel = _group_rmsnorm(kv[2], kg_ref[1:2, :], hd)
    k_win = _group_rmsnorm(kv[4], kg_ref[2:3, :], hd)
    kv_ref[:, 0 * kvw:1 * kvw] = kv[0]
    kv_ref[:, 1 * kvw:2 * kvw] = kv[1]
    kv_ref[:, 2 * kvw:3 * kvw] = k_sel
    kv_ref[:, 3 * kvw:4 * kvw] = kv[3]
    win_ref[:, :kvw] = k_win
    win_ref[:, kvw:] = kv[5]
    kc_ref[...] = kv[0]
    vc_ref[...] = kv[1]
    blk = lax.broadcasted_iota(jnp.int32, (tm, kvw), 0) // SEL_BLOCK
    col = lax.broadcasted_iota(jnp.int32, (tm, kvw), 1)
    ks_ref[:, :kvw] = k_sel.astype(BF16)
    ks_ref[:, kvw:] = jnp.where(col == blk, 1.0, 0.0).astype(BF16)
    kw_ref[...] = k_win.astype(BF16)

    zt = _dot_nt(wt_ref[...], hn)
    for h in range(qw // hd):
        blk = zt[h * hd:(h + 1) * hd, :]
        r = lax.rsqrt(jnp.mean(blk * blk, axis=0, keepdims=True) + EPS)
        qn = blk * r
        for j in range(0, tm, LANES):
            qT_ref[h * hd:(h + 1) * hd, j:j + LANES] = (
                qn[:, j:j + LANES] * qg_ref[h * hd:(h + 1) * hd, :]).astype(BF16)
    pad = ACC_ROWS - hd
    ones_rows = jnp.where(lax.broadcasted_iota(jnp.int32, (pad, tm), 0) == 0, 1.0, 0.0).astype(BF16)
    for out_ref, base in ((vsT_ref, qw), (vwT_ref, qw + kvw)):
        for g in range(kvw // hd):
            out_ref[g * ACC_ROWS:g * ACC_ROWS + hd, :] = zt[base + g * hd:base + (g + 1) * hd, :].astype(BF16)
            out_ref[g * ACC_ROWS + hd:(g + 1) * ACC_ROWS, :] = ones_rows
    gT_ref[...] = jax.nn.sigmoid(zt[qw + 2 * kvw:, :])


def _even_in_prompt(x, g_mix, w_in, q_gain, k_gain, dims, tm=512):
    t, d = x.shape
    lw, qw, kvw, hd, heads = dims
    tm = min(tm, t)
    c0, c1, c2 = 2 * lw, 2 * lw + qw, 2 * lw + qw + 6 * kvw
    wn = jnp.concatenate([w_in[:, :c0], w_in[:, c1:c2]], axis=1).astype(BF16)
    wg = w_in[:, c2:c2 + 3 * heads].reshape(d, heads, 3).transpose(0, 2, 1).reshape(d, 3 * heads)
    wt = jnp.concatenate([w_in[:, c0:c1], w_in[:, c1 + 3 * kvw:c1 + 4 * kvw],
                          w_in[:, c1 + 5 * kvw:c1 + 6 * kvw], wg,
                          jnp.zeros((d, 32 - 3 * heads), F32)], axis=1).T.astype(BF16)
    assert tm % KEY_GROUP == 0
    qg = jnp.broadcast_to((jnp.tile(q_gain, heads) * (hd ** -0.5 * LOG2E))[:, None], (qw, LANES)).astype(F32)
    vrows = (kvw // hd) * ACC_ROWS
    kg = jnp.tile(k_gain, (1, kvw // hd)).astype(F32)
    kg = jnp.concatenate([kg, jnp.zeros((5, kvw), F32)], axis=0)
    nt = wt.shape[0]
    row = lambda w: pl.BlockSpec((tm, w), lambda i: (i, 0))
    col = lambda r: pl.BlockSpec((r, tm), lambda i: (0, i))
    outs = pl.pallas_call(
        functools.partial(_even_in_prompt_kernel, lw=lw, qw=qw, kvw=kvw, hd=hd),
        grid=(t // tm,),
        in_specs=[row(d), _const_spec((1, d)), _const_spec(wn.shape), _const_spec(wt.shape),
                  _const_spec(qg.shape), _const_spec(kg.shape)],
        out_specs=[row(lw), row(lw), row(4 * kvw), row(2 * kvw), row(kvw), row(kvw),
                   row(2 * kvw), row(kvw), col(qw), col(vrows), col(vrows), col(32)],
        out_shape=[jax.ShapeDtypeStruct((t, lw), F32), jax.ShapeDtypeStruct((t, lw), F32),
                   jax.ShapeDtypeStruct((t, 4 * kvw), F32), jax.ShapeDtypeStruct((t, 2 * kvw), F32),
                   jax.ShapeDtypeStruct((t, kvw), F32), jax.ShapeDtypeStruct((t, kvw), F32),
                   jax.ShapeDtypeStruct((t, 2 * kvw), BF16), jax.ShapeDtypeStruct((t, kvw), BF16),
                   jax.ShapeDtypeStruct((qw, t), BF16), jax.ShapeDtypeStruct((vrows, t), BF16),
                   jax.ShapeDtypeStruct((vrows, t), BF16), jax.ShapeDtypeStruct((32, t), F32)],
        compiler_params=_cparams(), name="even_in_prompt",
    )(x, g_mix[None, :], wn, wt, qg, kg)
    return outs


def _even_in_sample_kernel(x_ref, g_ref, w_ref, qg_ref, kg_ref,
                           xa_ref, ga_ref, q_ref, kv_ref, win_ref, gate_ref, *, lw, qw, kvw, hd):
    hn = _rms_rows(x_ref[...], g_ref[...]).astype(BF16)
    z = _dot(hn, w_ref[...])
    xa_ref[...] = z[:, :lw]
    ga_ref[...] = z[:, lw:2 * lw]
    q_ref[...] = _group_rmsnorm(z[:, 2 * lw:2 * lw + qw], qg_ref[...], hd)
    o = 2 * lw + qw
    kv = [z[:, o + i * kvw: o + (i + 1) * kvw] for i in range(6)]
    kv_ref[:, 0 * kvw:1 * kvw] = kv[0]
    kv_ref[:, 1 * kvw:2 * kvw] = kv[1]
    kv_ref[:, 2 * kvw:3 * kvw] = _group_rmsnorm(kv[2], kg_ref[1:2, :], hd)
    kv_ref[:, 3 * kvw:4 * kvw] = kv[3]
    win_ref[:, :kvw] = _group_rmsnorm(kv[4], kg_ref[2:3, :], hd)
    win_ref[:, kvw:] = kv[5]
    gate_ref[...] = jax.nn.sigmoid(z[:, o + 6 * kvw:])


def _even_in_sample(x, g_mix, w_in, q_gain, k_gain, dims):
    r, d = x.shape
    lw, qw, kvw, hd, heads = dims
    n_in = w_in.shape[1]
    pad = (-n_in) % LANES
    w = jnp.pad(w_in, ((0, 0), (0, pad))).astype(BF16)
    gw = n_in + pad - (2 * lw + qw + 6 * kvw)
    qg = (jnp.tile(q_gain, heads) * hd ** -0.5)[None, :].astype(F32)
    kg = jnp.tile(k_gain, (1, kvw // hd)).astype(F32)
    kg = jnp.concatenate([kg, jnp.zeros((5, kvw), F32)], axis=0)
    full = lambda s: pl.BlockSpec(s, lambda i: (0,) * len(s))
    return pl.pallas_call(
        functools.partial(_even_in_sample_kernel, lw=lw, qw=qw, kvw=kvw, hd=hd),
        grid=(1,),
        in_specs=[full((r, d)), full((1, d)), full(w.shape), full(qg.shape), full(kg.shape)],
        out_specs=[full((r, lw)), full((r, lw)), full((r, qw)), full((r, 4 * kvw)),
                   full((r, 2 * kvw)), full((r, gw))],
        out_shape=[jax.ShapeDtypeStruct((r, lw), F32), jax.ShapeDtypeStruct((r, lw), F32),
                   jax.ShapeDtypeStruct((r, qw), F32), jax.ShapeDtypeStruct((r, 4 * kvw), F32),
                   jax.ShapeDtypeStruct((r, 2 * kvw), F32), jax.ShapeDtypeStruct((r, gw), F32)],
        compiler_params=_cparams(), name="even_in_sample",
    )(x, g_mix[None, :], w, qg, kg)


def _rglru_gates(conv, wa_ref, ba_ref, wx_ref, bx_ref, lam_ref):
    cb = conv.astype(BF16)
    r = jax.nn.sigmoid(_dot(cb, wa_ref[...]) + ba_ref[...])
    i = jax.nn.sigmoid(_dot(cb, wx_ref[...]) + bx_ref[...])
    log_a = -RG_C * r * _softplus(-lam_ref[...])
    a = jnp.exp(log_a)
    th = jnp.tanh(log_a)
    b = jnp.sqrt(-2.0 * th / (1.0 - th)) * (i * conv)
    return a, b


def _rglru_prompt_kernel(xa_ref, ga_ref, cw_ref, cb_ref, wa_ref, ba_ref, wx_ref, bx_ref, lam_ref,
                         ya_ref, hl_ref, xext_ref, h_ref):
    tr = xa_ref.shape[0]

    @pl.when(pl.program_id(0) == 0)
    def _():
        xext_ref[0:8, :] = jnp.zeros((8, xext_ref.shape[1]), F32)
        h_ref[...] = jnp.zeros(h_ref.shape, F32)

    x = xa_ref[...]
    xext_ref[8:8 + tr, :] = x
    conv = (cb_ref[...] + cw_ref[3:4, :] * x + cw_ref[2:3, :] * xext_ref[7:7 + tr, :]
            + cw_ref[1:2, :] * xext_ref[6:6 + tr, :] + cw_ref[0:1, :] * xext_ref[5:5 + tr, :])
    xext_ref[0:8, :] = xext_ref[tr:tr + 8, :]
    a, b = _rglru_gates(conv, wa_ref, ba_ref, wx_ref, bx_ref, lam_ref)
    row = lax.broadcasted_iota(jnp.int32, a.shape, 0)
    s = 1
    while s < tr:
        keep = row >= s
        a_sh = jnp.where(keep, pltpu.roll(a, s, 0), 1.0)
        b_sh = jnp.where(keep, pltpu.roll(b, s, 0), 0.0)
        b = a * b_sh + b
        a = a * a_sh
        s *= 2
    hs = a * h_ref[0:1, :] + b
    h_last = hs[tr - 1:tr, :]
    h_ref[...] = jnp.broadcast_to(h_last, h_ref.shape)
    hl_ref[...] = h_last
    ya_ref[...] = (hs * _gelu(ga_ref[...])).astype(BF16)


def _block_diag(w):
    n, a, b = w.shape
    eye = jnp.eye(n, dtype=w.dtype)
    return (eye[:, None, :, None] * w[:, :, None, :]).reshape(n * a, n * b)


def _rglru_weights(conv_w, conv_b, w_a, b_a, w_x, b_x, lam):
    return (conv_w.astype(F32), conv_b[None, :], _block_diag(w_a).astype(BF16), b_a[None, :],
            _block_diag(w_x).astype(BF16), b_x[None, :], lam[None, :])


def _rglru_prompt(xa, ga, weights, tr=256):
    t, lw = xa.shape
    tr = min(tr, t)
    row = pl.BlockSpec((tr, lw), lambda i: (i, 0))
    return pl.pallas_call(
        _rglru_prompt_kernel, grid=(t // tr,),
        in_specs=[row, row] + [_const_spec(w.shape) for w in weights],
        out_specs=[row, pl.BlockSpec((1, lw), lambda i: (0, 0))],
        out_shape=[jax.ShapeDtypeStruct((t, lw), BF16), jax.ShapeDtypeStruct((1, lw), F32)],
        scratch_shapes=[pltpu.VMEM((tr + 8, lw), F32), pltpu.VMEM((8, lw), F32)],
        compiler_params=_cparams(), name="rglru_prompt",
    )(xa, ga, *weights)


def _rglru_sample_kernel(xa_ref, ga_ref, sc_ref, h0_ref, cw_ref, cb_ref, wa_ref, ba_ref, wx_ref,
                         bx_ref, lam_ref, ya_ref, h_ref):
    conv = (cb_ref[...] + cw_ref[3:4, :] * xa_ref[...] + cw_ref[2:3, :] * sc_ref[2]
            + cw_ref[1:2, :] * sc_ref[1] + cw_ref[0:1, :] * sc_ref[0])
    a, b = _rglru_gates(conv, wa_ref, ba_ref, wx_ref, bx_ref, lam_ref)
    h = a * h0_ref[...] + b
    h_ref[...] = h
    ya_ref[...] = (h * _gelu(ga_ref[...])).astype(BF16)


def _rglru_sample(xa, ga, state_conv, state_h, weights):
    r, lw = xa.shape
    sc = jnp.transpose(state_conv, (1, 0, 2))
    args = (xa, ga, sc, state_h) + tuple(weights)
    full = lambda s: pl.BlockSpec(s, lambda i: (0,) * len(s))
    return pl.pallas_call(
        _rglru_sample_kernel, grid=(1,),
        in_specs=[full(a.shape) for a in args],
        out_specs=[full((r, lw)), full((r, lw))],
        out_shape=[jax.ShapeDtypeStruct((r, lw), BF16), jax.ShapeDtypeStruct((r, lw), F32)],
        compiler_params=_cparams(), name="rglru_sample",
    )(*args)


def _compress_prompt_kernel(xk_ref, xv_ref, pek_ref, pev_ref, wk_ref, wvT_ref, kg_ref,
                            kc_ref, vcT_ref, *, hd):
    xk = (xk_ref[...] + pek_ref[...]).astype(BF16)
    kc = _dot(xk, wk_ref[...])
    kc_ref[...] = _group_rmsnorm(kc, kg_ref[...], hd).astype(BF16)
    xv = (xv_ref[...] + pev_ref[...]).astype(BF16)
    vcT_ref[...] = _dot_nt(wvT_ref[...], xv).astype(BF16)


def _compress_weight(w, groups, hd):
    w3 = w.reshape(CMP_BLOCK, hd, hd)
    eye = jnp.eye(groups, dtype=w.dtype)
    full = eye[None, :, None, :, None] * w3[:, None, :, None, :]
    return full.reshape(CMP_BLOCK * groups * hd, groups * hd)


def _compress_prompt(kcmp, vcmp, w_k, w_v, pe_k, pe_v, k_gain_cmp, groups, hd):
    t, kvw = kcmp.shape
    ncb = t // CMP_BLOCK
    kdim = CMP_BLOCK * kvw
    xk = kcmp.reshape(ncb, kdim)
    xv = vcmp.reshape(ncb, kdim)
    wk = _compress_weight(w_k, groups, hd).astype(BF16)
    wvT = _compress_weight(w_v, groups, hd).T.astype(BF16)
    pek = jnp.tile(pe_k, (1, groups)).reshape(1, kdim)
    pev = jnp.tile(pe_v, (1, groups)).reshape(1, kdim)
    kg = jnp.tile(k_gain_cmp, groups)[None, :]
    nb = min(LANES, ncb)
    return pl.pallas_call(
        functools.partial(_compress_prompt_kernel, hd=hd), grid=(ncb // nb,),
        in_specs=[pl.BlockSpec((nb, kdim), lambda i: (i, 0)), pl.BlockSpec((nb, kdim), lambda i: (i, 0)),
                  _const_spec((1, kdim)), _const_spec((1, kdim)), _const_spec(wk.shape),
                  _const_spec(wvT.shape), _const_spec((1, kvw))],
        out_specs=[pl.BlockSpec((nb, kvw), lambda i: (i, 0)), pl.BlockSpec((kvw, nb), lambda i: (0, i))],
        out_shape=[jax.ShapeDtypeStruct((ncb, kvw), BF16), jax.ShapeDtypeStruct((kvw, ncb), BF16)],
        compiler_params=_cparams(), name="compress_prompt",
    )(xk, xv, pek, pev, wk, wvT, kg)


def _tile_lanes(x, n):
    return jnp.concatenate([x] * n, axis=1)


def _rows_repeat(x, rep):
    return jnp.concatenate([jnp.broadcast_to(x[i:i + 1, :], (rep, x.shape[1]))
                            for i in range(x.shape[0])], axis=0)


def _attend(s, shift, vT, slot, g, m_ref, acc_ref):
    m_old = m_ref[slot, g]
    m_blk = jnp.max(s, axis=0, keepdims=True)
    if shift is not None:
        m_blk = m_blk + shift
    m_new = jnp.maximum(m_old, m_blk)
    p = jnp.exp2(s - (m_new if shift is None else m_new - shift))
    acc_ref[slot, g] = jnp.exp2(m_old - m_new) * acc_ref[slot, g] + _dot(vT, p.astype(BF16))
    m_ref[slot, g] = m_new


def _bias_table_kernel(id_ref, val_ref, o_ref, *, reps):
    ids = _tile_lanes(id_ref[...], reps)
    out = jnp.zeros(ids.shape, F32)
    for k in range(val_ref.shape[0]):
        out = jnp.where(ids == k, val_ref[k:k + 1, :], out)
    o_ref[...] = out


def _bias_table(ids, vals):
    r = ids.shape[0]
    reps = vals.shape[1] // LANES
    full = lambda s: pl.BlockSpec(s, lambda i: (0,) * len(s))
    return pl.pallas_call(
        functools.partial(_bias_table_kernel, reps=reps), grid=(1,),
        in_specs=[full(ids.shape), full(vals.shape)], out_specs=full((r, vals.shape[1])),
        out_shape=jax.ShapeDtypeStruct((r, vals.shape[1]), F32),
        compiler_params=_cparams(), name="bias_table",
    )(jnp.asarray(ids, jnp.int32), vals)


def _nsa_prompt_kernel(qT_ref, gT_ref, kc_ref, vcT_ref, ks_ref, vsT_ref, kw_ref, vwT_ref,
                       b31_ref, tab_ref,
                       y_ref,
                       qaug_ref, sc_ref, p4_ref, selb_ref, m_ref, acc_ref, *, groups, hpg, hd):
    qb = pl.program_id(0)
    t0 = qb * Q_BLOCK
    ncb = kc_ref.shape[0]
    nsb = ncb * CMP_BLOCK // SEL_BLOCK
    gl = hpg * Q_BLOCK
    n_top = min(TOP_N, nsb)
    kvw = groups * hd
    bpg = KEY_GROUP // SEL_BLOCK

    rows = []
    for g in range(groups):
        top = jnp.concatenate([qT_ref[(g * hpg + hh) * hd:(g * hpg + hh + 1) * hd, :]
                               for hh in range(hpg)], axis=1)
        z = jnp.zeros_like(top)
        rows.append(jnp.concatenate([top if gg == g else z for gg in range(groups)], axis=1))
    qbd = jnp.concatenate(rows, axis=0)
    qaug_ref[0:kvw, :] = qbd
    qaug_ref[kvw:2 * kvw, :] = jnp.zeros((kvw, groups * gl), BF16)
    b31 = b31_ref[...]

    jrow = lax.broadcasted_iota(jnp.int32, (ncb, Q_BLOCK), 0)
    qcol = lax.broadcasted_iota(jnp.int32, (ncb, Q_BLOCK), 1)
    vis = _tile_lanes(jrow * CMP_BLOCK + (CMP_BLOCK - 1) <= t0 + qcol, groups * hpg)
    s_c = _dot(kc_ref[...], qbd) + b31
    sc_ref[...] = jnp.where(vis, s_c, NEG)
    odd = (qb % 2) == 1
    v_a = jnp.where(qb == 0, 0, jnp.where(odd, 1, 2))
    w_a = pl.multiple_of(jnp.where(qb == 0, 0, jnp.where(odd, 4 * qb - 4, 4 * qb - 8)), 8)
    v_b = jnp.where((qb > 0) & jnp.logical_not(odd), 0, 3)
    w_b = pl.multiple_of(jnp.where((qb > 0) & jnp.logical_not(odd), 4 * qb, 0), 8)
    sc_ref[pl.ds(w_a, 8), :] = sc_ref[pl.ds(w_a, 8), :] + tab_ref[pl.ds(pl.multiple_of(v_a * 8, 8), 8), :]
    sc_ref[pl.ds(w_b, 8), :] = sc_ref[pl.ds(w_b, 8), :] + tab_ref[pl.ds(pl.multiple_of(v_b * 8, 8), 8), :]
    s_c = sc_ref[...]
    m_c = jnp.max(s_c, axis=0, keepdims=True)
    p_c = jnp.where(vis, jnp.exp2(s_c - m_c), 0.0)
    l_c = jnp.sum(p_c, axis=0, keepdims=True)
    p_c = p_c * jnp.where(l_c > 0.0, 1.0 / l_c, 0.0)
    o_cmp = []
    for g in range(groups):
        pg = p_c[:, g * gl:(g + 1) * gl]
        o_cmp.append(_dot(vcT_ref[g * hd:(g + 1) * hd, :], pg.astype(BF16)))
        p4 = pg[:, 0:Q_BLOCK]
        for hh in range(1, hpg):
            p4 = p4 + pg[:, hh * Q_BLOCK:(hh + 1) * Q_BLOCK]
        p4_ref[g] = p4

    ratio = SEL_BLOCK // CMP_BLOCK
    jb = lax.broadcasted_iota(jnp.int32, (nsb, Q_BLOCK), 0)
    qc = lax.broadcasted_iota(jnp.int32, (nsb, Q_BLOCK), 1)
    cur = (t0 + qc) // SEL_BLOCK
    forced = (jb == 0) | (jb == cur) | (jb == cur - 1)
    for g in range(groups):
        imp = p4_ref[g, pl.ds(0, nsb, stride=ratio), :]
        for k in range(1, ratio):
            imp = imp + p4_ref[g, pl.ds(k, nsb, stride=ratio), :]
        imp = jnp.where(forced, FORCE_SCORE, imp)
        imp = jnp.where(jb <= cur, imp, -1.0)
        selb = jnp.full((nsb, Q_BLOCK), NEG, F32)
        for _ in range(n_top):
            mx = jnp.max(imp, axis=0, keepdims=True)
            idx = jnp.min(jnp.where(imp == mx, jb, nsb), axis=0, keepdims=True)
            hit = jb == idx
            selb = jnp.where(hit, 0.0, selb)
            imp = jnp.where(hit, -3.0e38, imp)
        selb_ref[g] = selb

    m_ref[...] = jnp.full(m_ref.shape, NEG, F32)
    acc_ref[...] = jnp.zeros(acc_ref.shape, F32)
    near_a = tab_ref[_TAB_NEAR_A:_TAB_NEAR_A + Q_BLOCK, :]
    near_b = tab_ref[_TAB_NEAR_B:_TAB_NEAR_B + Q_BLOCK, :]

    def set_mask_rows(grp):
        for g in range(groups):
            rows8 = selb_ref[g, pl.ds(pl.multiple_of(grp * bpg, bpg), bpg), :]
            tile = jnp.concatenate([rows8, jnp.zeros_like(rows8)], axis=0).astype(BF16)
            for hh in range(hpg):
                qaug_ref[kvw:kvw + 2 * bpg, g * gl + hh * Q_BLOCK:g * gl + (hh + 1) * Q_BLOCK] = tile

    def sel_chunk(k0, n, tile):
        s = _dot(ks_ref[pl.ds(k0, n), :], qaug_ref[...])
        for g in range(groups):
            sg = s[:, g * gl:(g + 1) * gl]
            vT = vsT_ref[g * ACC_ROWS:(g + 1) * ACC_ROWS, pl.ds(k0, n)]
            if tile is None:
                _attend(sg, b31[:, g * gl:(g + 1) * gl], vT, 0, g, m_ref, acc_ref)
            else:
                _attend(sg + tile[:, g * gl:(g + 1) * gl], None, vT, 0, g, m_ref, acc_ref)

    far_len = jnp.maximum(t0 - Q_BLOCK, 0)
    n_far = far_len // KEY_GROUP

    def far_body(c, carry):
        set_mask_rows(c)
        sel_chunk(pl.multiple_of(c * KEY_GROUP, KEY_GROUP), KEY_GROUP, None)
        return carry

    lax.fori_loop(0, n_far, far_body, 0)
    set_mask_rows(n_far)

    def rem_body(j, carry):
        sel_chunk(pl.multiple_of(n_far * KEY_GROUP + j * Q_BLOCK, Q_BLOCK), Q_BLOCK, None)
        return carry

    lax.fori_loop(0, (far_len - n_far * KEY_GROUP) // Q_BLOCK, rem_body, 0)

    @pl.when(qb >= 1)
    def _():
        set_mask_rows((t0 - Q_BLOCK) // KEY_GROUP)
        sel_chunk(pl.multiple_of(t0 - Q_BLOCK, Q_BLOCK), Q_BLOCK, near_a)

    set_mask_rows(t0 // KEY_GROUP)
    sel_chunk(pl.multiple_of(t0, Q_BLOCK), Q_BLOCK, near_b)

    def win_chunk(k0, tile):
        s = _dot(kw_ref[pl.ds(k0, Q_BLOCK), :], qaug_ref[0:kvw, :])
        for g in range(groups):
            sg = s[:, g * gl:(g + 1) * gl]
            vT = vwT_ref[g * ACC_ROWS:(g + 1) * ACC_ROWS, pl.ds(k0, Q_BLOCK)]
            if tile is None:
                _attend(sg, b31[:, g * gl:(g + 1) * gl], vT, 1, g, m_ref, acc_ref)
            else:
                _attend(sg + tile[:, g * gl:(g + 1) * gl], None, vT, 1, g, m_ref, acc_ref)

    win_chunk(pl.multiple_of(t0, Q_BLOCK), near_b)
    n_wc = WINDOW // Q_BLOCK
    for i in range(1, n_wc + 1):
        k0 = t0 - i * Q_BLOCK

        @pl.when(k0 >= 0)
        def _(i=i, k0=k0):
            k0a = pl.multiple_of(k0, Q_BLOCK)
            if i == 1:
                win_chunk(k0a, near_a)
            elif i == n_wc:
                win_chunk(k0a, tab_ref[_TAB_WIN_EDGE:_TAB_WIN_EDGE + Q_BLOCK, :])
            else:
                win_chunk(k0a, None)

    heads = groups * hpg
    pieces = []
    for g in range(groups):
        gate = [jnp.concatenate([gT_ref[j * heads + g * hpg + hh: j * heads + g * hpg + hh + 1, :]
                                 for hh in range(hpg)], axis=1) for j in range(3)]
        o = gate[0] * o_cmp[g]
        for slot in range(2):
            acc = acc_ref[slot, g]
            o = o + gate[1 + slot] * (acc[0:hd, :] * (1.0 / acc[hd:hd + 1, :]))
        for hh in range(hpg):
            pieces.append(o[:, hh * Q_BLOCK:(hh + 1) * Q_BLOCK])
    y_ref[...] = jnp.concatenate(pieces, axis=0).T.astype(BF16)


_TAB_NEAR_A = 32
_TAB_NEAR_B = _TAB_NEAR_A + Q_BLOCK
_TAB_WIN_EDGE = _TAB_NEAR_B + Q_BLOCK


def _nsa_prompt_tables(rel_bias, n_lanes):
    nb, heads = rel_bias.shape
    tbl = rel_bias.astype(F32) * LOG2E
    lanes = lambda x: jnp.repeat(x, n_lanes // heads, axis=1)
    vals = jnp.concatenate([lanes(tbl), jnp.full((1, n_lanes), NEG, F32), jnp.zeros((1, n_lanes), F32),
                            lanes(tbl - tbl[nb - 1:nb])], axis=0)
    c = np.arange(Q_BLOCK)[None, :]
    i8 = np.arange(8)[:, None]
    ids = []
    for base in (-(CMP_BLOCK - 1), 3 * CMP_BLOCK + 1, 7 * CMP_BLOCK + 1):
        d = base + c - CMP_BLOCK * i8
        ids.append(np.where(d >= 0, nb + 2 + _bucket_np(d, nb), nb + 1))
    ids.append(np.full((8, Q_BLOCK), nb + 1))
    r = np.arange(Q_BLOCK)[:, None]
    d = Q_BLOCK + c - r
    ids.append(_bucket_np(d, nb))
    d = c - r
    ids.append(np.where(d >= 0, _bucket_np(d, nb), nb))
    d = WINDOW + c - r
    ids.append(np.where(d < WINDOW, _bucket_np(d, nb), nb))
    return vals[nb - 1:nb], _bias_table(np.concatenate(ids, axis=0), vals)


def _nsa_prompt(qT, gT, kc, vcT, ks, vsT, kw, vwT, rel_bias, groups, hpg, hd):
    qw, t = qT.shape
    ncb = kc.shape[0]
    nsb = t // SEL_BLOCK
    gl = hpg * Q_BLOCK
    kvw = groups * hd
    assert t % KEY_GROUP == 0 and Q_BLOCK == 4 * CMP_BLOCK and MAX_DIST <= Q_BLOCK
    tables = _nsa_prompt_tables(rel_bias, groups * gl)
    resident = (kc, vcT, ks, vsT, kw, vwT) + tuple(tables)
    return pl.pallas_call(
        functools.partial(_nsa_prompt_kernel, groups=groups, hpg=hpg, hd=hd),
        grid=(t // Q_BLOCK,),
        in_specs=[pl.BlockSpec((qw, Q_BLOCK), lambda i: (0, i)),
                  pl.BlockSpec((gT.shape[0], Q_BLOCK), lambda i: (0, i))]
                 + [_const_spec(a.shape) for a in resident],
        out_specs=pl.BlockSpec((Q_BLOCK, qw), lambda i: (i, 0)),
        out_shape=jax.ShapeDtypeStruct((t, qw), BF16),
        scratch_shapes=[pltpu.VMEM((2 * kvw, groups * gl), BF16),
                        pltpu.VMEM((ncb, groups * gl), F32),
                        pltpu.VMEM((groups, ncb, Q_BLOCK), F32),
                        pltpu.VMEM((groups, nsb, Q_BLOCK), F32),
                        pltpu.VMEM((2, groups, 1, gl), F32),
                        pltpu.VMEM((2, groups, ACC_ROWS, gl), F32)],
        compiler_params=_cparams(), name="nsa_prompt",
    )(qT, gT, *resident)


def _split3(x):
    hi = x.astype(BF16)
    r1 = x - hi.astype(F32)
    mid = r1.astype(BF16)
    lo = (r1 - mid.astype(F32)).astype(BF16)
    return hi, mid, lo


def _softmax_pv(s, v_bf16):
    m = jnp.max(s, axis=0, keepdims=True)
    p = jnp.exp(s - m)
    l = jnp.sum(p, axis=0, keepdims=True)
    return _dot((p * (1.0 / l)).T.astype(BF16), v_bf16)


def _nsa_sample_kernel(pt_ref, cache_ref, qbd_ref, gates_ref, kvn_ref, winn_ref, cwin_ref,
                       wk_ref, wv_ref, pek_ref, pev_ref, kg_ref, gsum_ref,
                       cb_ref, sbn_ref, wb_ref, b31_ref, b0_ref,
                       o_ref,
                       bufk_ref, bufv_ref, bufb_ref, sema, semb, p_ref, selb_ref, s_ref, *, hd, n_pages, page):
    b = pl.program_id(0)
    nb = pl.num_programs(0)
    kvw = kg_ref.shape[1]
    past = n_pages * page
    ncb = past // CMP_BLOCK
    nsb = past // SEL_BLOCK
    nks = min(NKS_SAMPLE, past)

    def page_copies(bb, p, half):
        src = cache_ref.at[pt_ref[bb, p]]
        rows = pl.ds(p * page, page)
        if half == 0:
            return (pltpu.make_async_copy(src.at[:, 0:kvw], bufk_ref.at[rows, :], sema),
                    pltpu.make_async_copy(src.at[:, kvw:2 * kvw], bufv_ref.at[rows, :], sema))
        return (pltpu.make_async_copy(src.at[:, 2 * kvw:4 * kvw], bufb_ref.at[rows, :], semb),)

    def start_all(bb, half):
        def body(p, c):
            for cp in page_copies(bb, p, half):
                cp.start()
            return c
        lax.fori_loop(0, n_pages, body, 0)

    def wait_all(bb, half):
        def body(p, c):
            for cp in page_copies(bb, p, half):
                cp.wait()
            return c
        lax.fori_loop(0, n_pages, body, 0)

    @pl.when(b == 0)
    def _():
        start_all(0, 0)
        start_all(0, 1)

    qbd = qbd_ref[0]

    wait_all(b, 0)
    kc = jnp.zeros((ncb, kvw), F32)
    vc = jnp.zeros((ncb, kvw), F32)
    for r in range(CMP_BLOCK):
        xk = bufk_ref[pl.ds(r, ncb, stride=CMP_BLOCK), :] + pek_ref[r]
        kc = kc + _dot(xk.astype(BF16), wk_ref[r])
        xv = bufv_ref[pl.ds(r, ncb, stride=CMP_BLOCK), :] + pev_ref[r]
        vc = vc + _dot(xv.astype(BF16), wv_ref[r])

    @pl.when(b + 1 < nb)
    def _():
        start_all(b + 1, 0)

    kc = _group_rmsnorm(kc, kg_ref[...], hd).astype(BF16)
    s_c = _dot(kc, qbd) + cb_ref[...]
    m_c = jnp.max(s_c, axis=0, keepdims=True)
    p_c = jnp.exp(s_c - m_c)
    p_c = p_c * (1.0 / jnp.sum(p_c, axis=0, keepdims=True))
    o_cmp = _dot(p_c.T.astype(BF16), vc.astype(BF16))
    p_ref[...] = p_c
    ratio = SEL_BLOCK // CMP_BLOCK
    imp = p_ref[pl.ds(0, nsb, stride=ratio), :]
    for k in range(1, ratio):
        imp = imp + p_ref[pl.ds(k, nsb, stride=ratio), :]
    hi, mid, lo = _split3(imp)
    gs = gsum_ref[...]
    imp = _dot(hi, gs) + _dot(mid, gs) + _dot(lo, gs)
    jb = lax.broadcasted_iota(jnp.int32, (nsb, LANES), 0)
    imp = jnp.where((jb == 0) | (jb == nsb - 1), FORCE_SCORE, imp)
    selb = jnp.full((nsb, LANES), NEG, F32)
    for _ in range(min(TOP_N, nsb + 1) - 1):
        mx = jnp.max(imp, axis=0, keepdims=True)
        idx = jnp.min(jnp.where(imp == mx, jb, nsb), axis=0, keepdims=True)
        hit = jb == idx
        selb = jnp.where(hit, 0.0, selb)
        imp = jnp.where(hit, -3.0e38, imp)
    selb_ref[...] = selb

    wait_all(b, 1)
    row0 = lax.broadcasted_iota(jnp.int32, (LANES, kvw), 0) == 0
    kvn = kvn_ref[0]
    k_new = jnp.where(row0, jnp.broadcast_to(kvn[:, 2 * kvw:3 * kvw], (LANES, kvw)), 0.0)
    v_new = jnp.where(row0, jnp.broadcast_to(kvn[:, 3 * kvw:4 * kvw], (LANES, kvw)), 0.0)
    row0l = lax.broadcasted_iota(jnp.int32, (LANES, LANES), 0) == 0
    s_new = jnp.where(row0l, _dot(k_new.astype(BF16), qbd) + b0_ref[...], NEG)
    s_ref[past:past + LANES, :] = s_new
    m_s = jnp.max(s_new, axis=0, keepdims=True)
    bps = nks // SEL_BLOCK
    for c in range(past // nks):
        s = _dot(bufb_ref[c * nks:(c + 1) * nks, 0:kvw].astype(BF16), qbd)
        s = s + _rows_repeat(selb_ref[c * bps:(c + 1) * bps, :], SEL_BLOCK) + b31_ref[...]
        if c == past // nks - 1:
            s = jnp.concatenate([s[:nks - LANES], s[nks - LANES:] + sbn_ref[...]], axis=0)
        s_ref[c * nks:(c + 1) * nks, :] = s
        m_s = jnp.maximum(m_s, jnp.max(s, axis=0, keepdims=True))
    l_s = jnp.zeros((1, LANES), F32)
    acc = jnp.zeros((LANES, kvw), F32)
    for c in range(past // nks):
        p = jnp.exp(s_ref[c * nks:(c + 1) * nks, :] - m_s)
        l_s = l_s + jnp.sum(p, axis=0, keepdims=True)
        acc = acc + _dot(p.T.astype(BF16), bufb_ref[c * nks:(c + 1) * nks, kvw:2 * kvw].astype(BF16))
    p = jnp.exp(s_ref[past:past + LANES, :] - m_s)
    l_s = l_s + jnp.sum(p, axis=0, keepdims=True)
    acc = acc + _dot(p.T.astype(BF16), v_new.astype(BF16))

    @pl.when(b + 1 < nb)
    def _():
        start_all(b + 1, 1)

    inv_col = jnp.broadcast_to(1.0 / l_s, (LANES, LANES)).T[:, 0:1]
    o_sel = acc * inv_col

    wbuf = cwin_ref.shape[1]
    winn = winn_ref[0]
    kw_new = jnp.where(row0, jnp.broadcast_to(winn[:, 0:kvw], (LANES, kvw)), 0.0)
    vw_new = jnp.where(row0, jnp.broadcast_to(winn[:, kvw:2 * kvw], (LANES, kvw)), 0.0)
    s_w = jnp.concatenate([
        _dot(cwin_ref[0, :, 0:kvw].astype(BF16), qbd) + wb_ref[...],
        jnp.where(row0l, _dot(kw_new.astype(BF16), qbd) + b0_ref[...], NEG)], axis=0)
    v_w = jnp.concatenate([cwin_ref[0, :, kvw:2 * kvw], vw_new], axis=0).astype(BF16)
    o_win = _softmax_pv(s_w, v_w)

    g = gates_ref[0]
    out = g[0] * o_cmp + g[1] * o_sel + g[2] * o_win
    o_ref[0] = out[0:8, :]


def _nsa_sample(q, gates, kv_new, win_new, cache_kv, cache_win, page_table, lp, rel_bias,
                groups, hpg, hd):
    r = q.shape[0]
    heads = groups * hpg
    kvw = groups * hd
    n_phys, page = cache_kv.shape[0], cache_kv.shape[1]
    n_pages = page_table.shape[1]
    past = n_pages * page
    wbuf = cache_win.shape[1]
    ncb, nsb = past // CMP_BLOCK, past // SEL_BLOCK
    lane_heads = np.concatenate([np.arange(heads), np.zeros(LANES - heads, np.int64)])
    q3 = q.reshape(r, heads, hd)
    gmask = (jnp.arange(groups)[:, None] == (jnp.arange(heads) // hpg)[None, :]).astype(F32)
    qbd = jnp.einsum('rhd,gh->rgdh', q3, gmask).reshape(r, kvw, heads)
    qbd = jnp.pad(qbd, ((0, 0), (0, 0), (0, LANES - heads))).astype(BF16)
    gts = gates[:, :3 * heads].reshape(r, heads, 3).transpose(0, 2, 1)
    gts = jnp.broadcast_to(gts[:, :, :, None], (r, 3, heads, kvw))
    gts = jnp.pad(gts, ((0, 0), (0, 0), (0, LANES - heads), (0, 0)))
    wk = _compress_weight(lp['cmp_w_k'], groups, hd).reshape(CMP_BLOCK, kvw, kvw).astype(BF16)
    wv = _compress_weight(lp['cmp_w_v'], groups, hd).reshape(CMP_BLOCK, kvw, kvw).astype(BF16)
    pek = jnp.tile(lp['pe_k'], (1, groups))[:, None, :]
    pev = jnp.tile(lp['pe_v'], (1, groups))[:, None, :]
    kg = jnp.tile(lp['k_gain'][0], groups)[None, :]
    lane_g = np.where(np.arange(LANES) < heads, np.arange(LANES) // hpg, -1)
    gsum = jnp.asarray((lane_g[:, None] == lane_g[None, :]) & (lane_g[:, None] >= 0), BF16)
    one = np.ones((1, 1), bool)
    b31v = rel_bias[-1].astype(F32)
    d_c = (past - (np.arange(ncb) * CMP_BLOCK + CMP_BLOCK - 1))[:, None]
    cb = _bias_tile(rel_bias, d_c, d_c >= 0, lane_heads)
    d_s = (past - (past - LANES + np.arange(LANES)))[:, None]
    sbn = _bias_tile(rel_bias, d_s, d_s >= 0, lane_heads, base=b31v)
    d_w = (wbuf - np.arange(wbuf))[:, None]
    wb = _bias_tile(rel_bias, d_w, d_w < WINDOW, lane_heads)
    b31 = _bias_tile(rel_bias, np.full((1, 1), 10 * MAX_DIST), one, lane_heads)
    b0 = _bias_tile(rel_bias, np.zeros((1, 1), np.int64), one, lane_heads)

    cache2 = cache_kv.reshape(n_phys, page, 4 * kvw)
    cwin = cache_win.reshape(r, wbuf, 2 * kvw)
    kvn = kv_new.reshape(r, 1, 4 * kvw)
    winn = win_new.reshape(r, 1, 2 * kvw)
    per_b = lambda s: pl.BlockSpec((1,) + s, lambda b, pt: (b,) + (0,) * len(s))
    const = lambda a: pl.BlockSpec(a.shape, lambda b, pt: (0,) * a.ndim, pipeline_mode=pl.Buffered(1))
    consts = (wk, wv, pek, pev, kg, gsum, cb, sbn, wb, b31, b0)
    grid_spec = pltpu.PrefetchScalarGridSpec(
        num_scalar_prefetch=1, grid=(r,),
        in_specs=[pl.BlockSpec(memory_space=pl.ANY), per_b((kvw, LANES)), per_b((3, LANES, kvw)),
                  per_b((1, 4 * kvw)), per_b((1, 2 * kvw)), per_b((wbuf, 2 * kvw))]
                 + [const(a) for a in consts],
        out_specs=per_b((8, kvw)),
        scratch_shapes=[pltpu.VMEM((past, kvw), F32), pltpu.VMEM((past, kvw), F32),
                        pltpu.VMEM((past, 2 * kvw), F32),
                        pltpu.SemaphoreType.DMA(()), pltpu.SemaphoreType.DMA(()),
                        pltpu.VMEM((ncb, LANES), F32), pltpu.VMEM((nsb, LANES), F32),
                        pltpu.VMEM((past + LANES, LANES), F32)])
    o8 = pl.pallas_call(
        functools.partial(_nsa_sample_kernel, hd=hd, n_pages=n_pages, page=page),
        grid_spec=grid_spec,
        out_shape=jax.ShapeDtypeStruct((r, 8, kvw), F32),
        compiler_params=_cparams(), name="nsa_sample",
    )(page_table, cache2, qbd, gts, kvn, winn, cwin, *consts)
    o4 = o8[:, :heads, :].reshape(r, heads, groups, hd)
    y = jnp.take_along_axis(o4, (jnp.arange(heads) // hpg)[None, :, None, None], axis=2)
    return y.reshape(r, heads * hd).astype(BF16)


def _mlp_ple(h, p_ref, gm_ref, wu_ref, wd_ref, gp_ref, wg_ref, wp_ref):
    up = _dot(_rms_rows(h, gm_ref[...]).astype(BF16), wu_ref[...])
    act = jnp.square(jnp.maximum(up, 0.0)).astype(BF16)
    h = h + _dot(act, wd_ref[...])
    gate = jax.nn.sigmoid(_dot(_rms_rows(h, gp_ref[...]).astype(BF16), wg_ref[...]))
    return h + gate * _dot(p_ref[...].astype(BF16), wp_ref[...])


def _even_tail_kernel(h_ref, ya_ref, yb_ref, p_ref, woa_ref, wob_ref,
                      gm_ref, wu_ref, wd_ref, gp_ref, wg_ref, wp_ref, o_ref):
    h = h_ref[...] + _dot(ya_ref[...], woa_ref[...]) + _dot(yb_ref[...], wob_ref[...])
    o_ref[...] = _mlp_ple(h, p_ref, gm_ref, wu_ref, wd_ref, gp_ref, wg_ref, wp_ref)


def _tail_weights(g_mlp, w_up, w_down, g_ple, w_gate, w_proj):
    return (g_mlp[None, :], w_up.astype(BF16), w_down.astype(BF16), g_ple[None, :],
            w_gate.astype(BF16), w_proj.astype(BF16))


def _even_tail(h, ya, yb, p, w_out, tail_w, tm=256):
    t, d = h.shape
    tm = min(tm, t)
    lw = ya.shape[1]
    woa, wob = w_out[:lw].astype(BF16), w_out[lw:].astype(BF16)
    row = lambda w: pl.BlockSpec((tm, w), lambda i: (i, 0))
    weights = (woa, wob) + tuple(tail_w)
    return pl.pallas_call(
        _even_tail_kernel, grid=(t // tm,),
        in_specs=[row(d), row(lw), row(yb.shape[1]), row(p.shape[1])]
                 + [_const_spec(w.shape) for w in weights],
        out_specs=row(d), out_shape=jax.ShapeDtypeStruct((t, d), F32),
        compiler_params=_cparams(), name="even_tail",
    )(h, ya, yb, p, *weights)


def _odd_kernel(h_ref, p_ref, gx_ref, wi_ref, lg_ref, lb_ref, sw_ref, sb_ref, wo_ref,
                gm_ref, wu_ref, wd_ref, gp_ref, wg_ref, wp_ref, o_ref, v_ref, *, sw_groups, single):
    h = h_ref[...]
    tm = h.shape[0]
    z = _gelu(_dot(_rms_rows(h, gx_ref[...]).astype(BF16), wi_ref[...]))
    width = z.shape[1] // 2
    u, v = z[:, :width], z[:, width:]
    mu = jnp.mean(v, axis=-1, keepdims=True)
    vc = v - mu
    vn = vc * lax.rsqrt(jnp.mean(vc * vc, axis=-1, keepdims=True) + EPS) * lg_ref[...] + lb_ref[...]
    v_ref[...] = vn
    gw = width // sw_groups
    if single:
        s = vn * sw_ref[...] + sb_ref[...]
    else:
        vb = vn.astype(BF16)
        parts = []
        for c in range(tm // CHUNK):
            cols = [_dot(sw_ref[g], vb[c * CHUNK:(c + 1) * CHUNK, g * gw:(g + 1) * gw])
                    for g in range(sw_groups)]
            parts.append(jnp.concatenate(cols, axis=1) + sb_ref[...])
        s = jnp.concatenate(parts, axis=0)
    h = h + _dot((u * s).astype(BF16), wo_ref[...])
    o_ref[...] = _mlp_ple(h, p_ref, gm_ref, wu_ref, wd_ref, gp_ref, wg_ref, wp_ref)


def _odd_layer(h, p, g_mix, w_in, ln_g, ln_b, sgu_w, sgu_b, w_out, tail_w, single, tm=256):
    t, d = h.shape
    tm = min(tm, t)
    ng, ch, _ = sgu_w.shape
    width = w_in.shape[1] // 2
    gw = width // ng
    if single:
        sw = jnp.repeat(sgu_w[:, 0, 0], gw)[None, :]
        sb = jnp.repeat(sgu_b[:, 0], gw)[None, :]
    else:
        sw = (sgu_w * jnp.tril(jnp.ones((ch, ch), sgu_w.dtype))).astype(BF16)
        sb = jnp.repeat(sgu_b.T, gw, axis=1)
    weights = (g_mix[None, :], w_in.astype(BF16), ln_g[None, :], ln_b[None, :], sw, sb,
               w_out.astype(BF16)) + tuple(tail_w)
    row = lambda w: pl.BlockSpec((tm, w), lambda i: (i, 0))
    return pl.pallas_call(
        functools.partial(_odd_kernel, sw_groups=ng, single=single), grid=(t // tm,),
        in_specs=[row(d), row(p.shape[1])] + [_const_spec(w.shape) for w in weights],
        out_specs=[row(d), row(width)],
        out_shape=[jax.ShapeDtypeStruct((t, d), F32), jax.ShapeDtypeStruct((t, width), F32)],
        compiler_params=_cparams(), name="odd_single" if single else "odd_prompt",
    )(h, p, *weights)


def kernel(x_prompt, x_sample, cache_kv, cache_win, state_conv, state_h, page_table, p_prompt, p_sample,
           rel_bias, g_mix, g_mlp, w_up, w_down, g_ple, w_ple_gate, w_ple_proj,
           w_in_even, w_out_even, conv_w, conv_b, rg_w_a, rg_b_a, rg_w_x, rg_b_x, rg_lambda,
           q_gain, k_gain, cmp_w_k, cmp_w_v, cmp_pe_k, cmp_pe_v,
           w_in_odd, ln_v_g, ln_v_b, sgu_w, sgu_b, w_out_odd):
    bsz, t, d = x_prompt.shape
    r = x_sample.shape[0]
    assert bsz == 1 and x_sample.shape[1] == 1
    depth = g_mix.shape[0]
    lw = conv_w.shape[-1]
    hd = q_gain.shape[-1]
    groups = cache_kv.shape[4]
    heads = rel_bias.shape[1]
    hpg = heads // groups
    qw, kvw = heads * hd, groups * hd
    dims = (lw, qw, kvw, hd, heads)
    wbuf = cache_win.shape[2]

    hp = x_prompt.reshape(t, d)
    hs = x_sample.reshape(r, d)
    kvp, kvs, wnp, wns, cvp, cvs, hhp, hhs, vvs = [], [], [], [], [], [], [], [], []
    for i in range(depth):
        tail_w = _tail_weights(g_mlp[i], w_up[i], w_down[i], g_ple[i], w_ple_gate[i], w_ple_proj[i])
        if i % 2 == 0:
            e = i // 2
            lp = {'cmp_w_k': cmp_w_k[e], 'cmp_w_v': cmp_w_v[e], 'pe_k': cmp_pe_k[e], 'pe_v': cmp_pe_v[e],
                  'k_gain': k_gain[e]}
            rg_w = _rglru_weights(conv_w[e], conv_b[e], rg_w_a[e], rg_b_a[e], rg_w_x[e], rg_b_x[e],
                                  rg_lambda[e])
            (xa, ga, kv_rows, win_rows, kcmp, vcmp, ks, kw, qT, vsT, vwT, gT) = _even_in_prompt(
                hp, g_mix[i], w_in_even[e], q_gain[e], k_gain[e], dims)
            ya, h_last = _rglru_prompt(xa, ga, rg_w)
            kc, vcT = _compress_prompt(kcmp, vcmp, cmp_w_k[e], cmp_w_v[e], cmp_pe_k[e], cmp_pe_v[e],
                                       k_gain[e, 0], groups, hd)
            yb = _nsa_prompt(qT, gT, kc, vcT, ks, vsT, kw, vwT, rel_bias, groups, hpg, hd)
            hp = _even_tail(hp, ya, yb, p_prompt[i, 0], w_out_even[e], tail_w)
            kvp.append(kv_rows.reshape(1, t, 4, groups, hd))
            wnp.append(win_rows[t - wbuf:].reshape(1, wbuf, 2, groups, hd))
            cvp.append(xa[t - (conv_w.shape[1] - 1):][None])
            hhp.append(h_last)
            xa_s, ga_s, q_s, kv_s, win_s, gate_s = _even_in_sample(
                hs, g_mix[i], w_in_even[e], q_gain[e], k_gain[e], dims)
            ya_s, h_s = _rglru_sample(xa_s, ga_s, state_conv[e], state_h[e], rg_w)
            yb_s = _nsa_sample(q_s, gate_s, kv_s, win_s, cache_kv[e], cache_win[e], page_table, lp,
                               rel_bias, groups, hpg, hd)
            hs = _even_tail(hs, ya_s, yb_s, p_sample[i, :, 0], w_out_even[e], tail_w)
            kvs.append(kv_s.reshape(r, 1, 4, groups, hd))
            wns.append(jnp.concatenate([cache_win[e][:, 1:], win_s.reshape(r, 1, 2, groups, hd)], axis=1))
            cvs.append(jnp.concatenate([state_conv[e][:, 1:], xa_s[:, None, :]], axis=1))
            hhs.append(h_s)
        else:
            o = i // 2
            hp, _ = _odd_layer(hp, p_prompt[i, 0], g_mix[i], w_in_odd[o], ln_v_g[o], ln_v_b[o],
                               sgu_w[o], sgu_b[o], w_out_odd[o], tail_w, single=False)
            hs, v_new = _odd_layer(hs, p_sample[i, :, 0], g_mix[i], w_in_odd[o], ln_v_g[o], ln_v_b[o],
                                   sgu_w[o], sgu_b[o], w_out_odd[o], tail_w, single=True)
            vvs.append(v_new[:, None, :])
    v_sample = jnp.stack(vvs) if vvs else jnp.zeros((0, r, 1, w_in_odd.shape[-1] // 2), F32)
    return (hp[None], hs[:, None, :], jnp.stack(kvp), jnp.stack(kvs), jnp.stack(wnp), jnp.stack(wns),
            jnp.stack(cvp), jnp.stack(cvs), jnp.stack(hhp), jnp.stack(hhs), v_sample)
```

```python
import functools
import math

import numpy as np
import jax
import jax.numpy as jnp
from jax import lax
from jax.experimental import pallas as pl
from jax.experimental.pallas import tpu as pltpu

F32 = jnp.float32
BF16 = jnp.bfloat16

EPS = 1e-6
NEG = -1e30
RG_C = 8.0
CMP_BLOCK = 32
SEL_BLOCK = 64
TOP_N = 16
WINDOW = 512
Q_BLOCK = 128
FORCE_SCORE = 1.0e4
MAX_DIST = 128
CHUNK = 128

LANES = 128
VMEM_LIMIT = 56 * 1024 * 1024

LOG2E = math.log2(math.e)
KEY_GROUP = 512
ACC_ROWS = 80
NKS_SAMPLE = 2048


def _cparams(n_axes=1):
    return pltpu.CompilerParams(dimension_semantics=("arbitrary",) * n_axes,
                                vmem_limit_bytes=VMEM_LIMIT)


def _const_spec(shape):
    nd = len(shape)
    return pl.BlockSpec(shape, lambda *_: (0,) * nd, pipeline_mode=pl.Buffered(1))


def _dot(a, b):
    return jnp.dot(a, b, preferred_element_type=F32)


def _dot_nt(a, b):
    return lax.dot_general(a, b, (((1,), (1,)), ((), ())), preferred_element_type=F32)


def _gelu(x):
    return 0.5 * x * (1.0 + jnp.tanh(math.sqrt(2.0 / math.pi) * (x + 0.044715 * (x * x * x))))


def _softplus(x):
    return jnp.maximum(x, 0.0) + jnp.log1p(jnp.exp(-jnp.abs(x)))


def _rms_rows(x, g):
    return x * lax.rsqrt(jnp.mean(x * x, axis=-1, keepdims=True) + EPS) * g


def _group_rmsnorm(x, gain, width):
    n = x.shape[-1] // width
    lane = lax.broadcasted_iota(jnp.int32, x.shape, 1)
    x2 = x * x
    ms = jnp.zeros_like(x)
    for i in range(n):
        inside = (lane >= i * width) & (lane < (i + 1) * width)
        s = jnp.sum(jnp.where(inside, x2, 0.0), axis=-1, keepdims=True)
        ms = jnp.where(inside, s, ms)
    return x * lax.rsqrt(ms * (1.0 / width) + EPS) * gain


def _bucket_np(dist, n_buckets):
    n = np.maximum(dist, 0)
    exact = n_buckets // 2
    nf = np.maximum(n, 1).astype(np.float32)
    large = exact + (np.log(nf / np.float32(exact)) / np.float32(math.log(MAX_DIST / exact))
                     * np.float32(n_buckets - exact)).astype(np.int32)
    return np.where(n < exact, n, np.minimum(large, n_buckets - 1)).astype(np.int32)


def _even_in_prompt_kernel(x_ref, g_ref, wn_ref, wt_ref, qg_ref, kg_ref,
                           xa_ref, ga_ref, kv_ref, win_ref, kc_ref, vc_ref,
                           ks_ref, kw_ref, qT_ref, vsT_ref, vwT_ref, gT_ref, *, lw, qw, kvw, hd):
    tm = x_ref.shape[0]
    hn = _rms_rows(x_ref[...], g_ref[...]).astype(BF16)
    zn = _dot(hn, wn_ref[...])
    xa_ref[...] = zn[:, :lw]
    ga_ref[...] = zn[:, lw:2 * lw]
    kv = [zn[:, 2 * lw + i * kvw: 2 * lw + (i + 1) * kvw] for i in range(6)]
    k_sel = _group_rmsnorm(kv[2], kg_ref[1:2, :], hd)
    k_win = _group_rmsnorm(kv[4], kg_ref[2:3, :], hd)
    kv_ref[:, 0 * kvw:1 * kvw] = kv[0]
    kv_ref[:, 1 * kvw:2 * kvw] = kv[1]
    kv_ref[:, 2 * kvw:3 * kvw] = k_sel
    kv_ref[:, 3 * kvw:4 * kvw] = kv[3]
    win_ref[:, :kvw] = k_win
    win_ref[:, kvw:] = kv[5]
    kc_ref[...] = kv[0]
    vc_ref[...] = kv[1]
    blk = lax.broadcasted_iota(jnp.int32, (tm, kvw), 0) // SEL_BLOCK
    col = lax.broadcasted_iota(jnp.int32, (tm, kvw), 1)
    ks_ref[:, :kvw] = k_sel.astype(BF16)
    ks_ref[:, kvw:] = jnp.where(col == blk, 1.0, 0.0).astype(BF16)
    kw_ref[...] = k_win.astype(BF16)

    zt = _dot_nt(wt_ref[...], hn)
    for h in range(qw // hd):
        blk = zt[h * hd:(h + 1) * hd, :]
        r = lax.rsqrt(jnp.mean(blk * blk, axis=0, keepdims=True) + EPS)
        qn = blk * r
        for j in range(0, tm, LANES):
            qT_ref[h * hd:(h + 1) * hd, j:j + LANES] = (
                qn[:, j:j + LANES] * qg_ref[h * hd:(h + 1) * hd, :]).astype(BF16)
    pad = ACC_ROWS - hd
    ones_rows = jnp.where(lax.broadcasted_iota(jnp.int32, (pad, tm), 0) == 0, 1.0, 0.0).astype(BF16)
    for out_ref, base in ((vsT_ref, qw), (vwT_ref, qw + kvw)):
        for g in range(kvw // hd):
            out_ref[g * ACC_ROWS:g * ACC_ROWS + hd, :] = zt[base + g * hd:base + (g + 1) * hd, :].astype(BF16)
            out_ref[g * ACC_ROWS + hd:(g + 1) * ACC_ROWS, :] = ones_rows
    gT_ref[...] = jax.nn.sigmoid(zt[qw + 2 * kvw:, :])


def _even_in_prompt(x, g_mix, w_in, q_gain, k_gain, dims, tm=512):
    t, d = x.shape
    lw, qw, kvw, hd, heads = dims
    tm = min(tm, t)
    c0, c1, c2 = 2 * lw, 2 * lw + qw, 2 * lw + qw + 6 * kvw
    wn = jnp.concatenate([w_in[:, :c0], w_in[:, c1:c2]], axis=1).astype(BF16)
    wg = w_in[:, c2:c2 + 3 * heads].reshape(d, heads, 3).transpose(0, 2, 1).reshape(d, 3 * heads)
    wt = jnp.concatenate([w_in[:, c0:c1], w_in[:, c1 + 3 * kvw:c1 + 4 * kvw],
                          w_in[:, c1 + 5 * kvw:c1 + 6 * kvw], wg,
                          jnp.zeros((d, 32 - 3 * heads), F32)], axis=1).T.astype(BF16)
    assert tm % KEY_GROUP == 0
    qg = jnp.broadcast_to((jnp.tile(q_gain, heads) * (hd ** -0.5 * LOG2E))[:, None], (qw, LANES)).astype(F32)
    vrows = (kvw // hd) * ACC_ROWS
    kg = jnp.tile(k_gain, (1, kvw // hd)).astype(F32)
    kg = jnp.concatenate([kg, jnp.zeros((5, kvw), F32)], axis=0)
    nt = wt.shape[0]
    row = lambda w: pl.BlockSpec((tm, w), lambda i: (i, 0))
    col = lambda r: pl.BlockSpec((r, tm), lambda i: (0, i))
    outs = pl.pallas_call(
        functools.partial(_even_in_prompt_kernel, lw=lw, qw=qw, kvw=kvw, hd=hd),
        grid=(t // tm,),
        in_specs=[row(d), _const_spec((1, d)), _const_spec(wn.shape), _const_spec(wt.shape),
                  _const_spec(qg.shape), _const_spec(kg.shape)],
        out_specs=[row(lw), row(lw), row(4 * kvw), row(2 * kvw), row(kvw), row(kvw),
                   row(2 * kvw), row(kvw), col(qw), col(vrows), col(vrows), col(32)],
        out_shape=[jax.ShapeDtypeStruct((t, lw), F32), jax.ShapeDtypeStruct((t, lw), F32),
                   jax.ShapeDtypeStruct((t, 4 * kvw), F32), jax.ShapeDtypeStruct((t, 2 * kvw), F32),
                   jax.ShapeDtypeStruct((t, kvw), F32), jax.ShapeDtypeStruct((t, kvw), F32),
                   jax.ShapeDtypeStruct((t, 2 * kvw), BF16), jax.ShapeDtypeStruct((t, kvw), BF16),
                   jax.ShapeDtypeStruct((qw, t), BF16), jax.ShapeDtypeStruct((vrows, t), BF16),
                   jax.ShapeDtypeStruct((vrows, t), BF16), jax.ShapeDtypeStruct((32, t), F32)],
        compiler_params=_cparams(), name="even_in_prompt",
    )(x, g_mix[None, :], wn, wt, qg, kg)
    return outs


def _even_in_sample_kernel(x_ref, g_ref, w_ref, qg_ref, kg_ref,
                           xa_ref, ga_ref, q_ref, kv_ref, win_ref, gate_ref, *, lw, qw, kvw, hd):
    hn = _rms_rows(x_ref[...], g_ref[...]).astype(BF16)
    z = _dot(hn, w_ref[...])
    xa_ref[...] = z[:, :lw]
    ga_ref[...] = z[:, lw:2 * lw]
    q_ref[...] = _group_rmsnorm(z[:, 2 * lw:2 * lw + qw], qg_ref[...], hd)
    o = 2 * lw + qw
    kv = [z[:, o + i * kvw: o + (i + 1) * kvw] for i in range(6)]
    kv_ref[:, 0 * kvw:1 * kvw] = kv[0]
    kv_ref[:, 1 * kvw:2 * kvw] = kv[1]
    kv_ref[:, 2 * kvw:3 * kvw] = _group_rmsnorm(kv[2], kg_ref[1:2, :], hd)
    kv_ref[:, 3 * kvw:4 * kvw] = kv[3]
    win_ref[:, :kvw] = _group_rmsnorm(kv[4], kg_ref[2:3, :], hd)
    win_ref[:, kvw:] = kv[5]
    gate_ref[...] = jax.nn.sigmoid(z[:, o + 6 * kvw:])


def _even_in_sample(x, g_mix, w_in, q_gain, k_gain, dims):
    r, d = x.shape
    lw, qw, kvw, hd, heads = dims
    n_in = w_in.shape[1]
    pad = (-n_in) % LANES
    w = jnp.pad(w_in, ((0, 0), (0, pad))).astype(BF16)
    gw = n_in + pad - (2 * lw + qw + 6 * kvw)
    qg = (jnp.tile(q_gain, heads) * hd ** -0.5)[None, :].astype(F32)
    kg = jnp.tile(k_gain, (1, kvw // hd)).astype(F32)
    kg = jnp.concatenate([kg, jnp.zeros((5, kvw), F32)], axis=0)
    full = lambda s: pl.BlockSpec(s, lambda i: (0,) * len(s))
    return pl.pallas_call(
        functools.partial(_even_in_sample_kernel, lw=lw, qw=qw, kvw=kvw, hd=hd),
        grid=(1,),
        in_specs=[full((r, d)), full((1, d)), full(w.shape), full(qg.shape), full(kg.shape)],
        out_specs=[full((r, lw)), full((r, lw)), full((r, qw)), full((r, 4 * kvw)),
                   full((r, 2 * kvw)), full((r, gw))],
        out_shape=[jax.ShapeDtypeStruct((r, lw), F32), jax.ShapeDtypeStruct((r, lw), F32),
                   jax.ShapeDtypeStruct((r, qw), F32), jax.ShapeDtypeStruct((r, 4 * kvw), F32),
                   jax.ShapeDtypeStruct((r, 2 * kvw), F32), jax.ShapeDtypeStruct((r, gw), F32)],
        compiler_params=_cparams(), name="even_in_sample",
    )(x, g_mix[None, :], w, qg, kg)


def _rglru_gates(conv, wa_ref, ba_ref, wx_ref, bx_ref, lam_ref):
    cb = conv.astype(BF16)
    r = jax.nn.sigmoid(_dot(cb, wa_ref[...]) + ba_ref[...])
    i = jax.nn.sigmoid(_dot(cb, wx_ref[...]) + bx_ref[...])
    log_a = -RG_C * r * _softplus(-lam_ref[...])
    a = jnp.exp(log_a)
    th = jnp.tanh(log_a)
    b = jnp.sqrt(-2.0 * th / (1.0 - th)) * (i * conv)
    return a, b


def _rglru_prompt_kernel(xa_ref, ga_ref, cw_ref, cb_ref, wa_ref, ba_ref, wx_ref, bx_ref, lam_ref,
                         ya_ref, hl_ref, xext_ref, h_ref):
    tr = xa_ref.shape[0]

    @pl.when(pl.program_id(0) == 0)
    def _():
        xext_ref[0:8, :] = jnp.zeros((8, xext_ref.shape[1]), F32)
        h_ref[...] = jnp.zeros(h_ref.shape, F32)

    x = xa_ref[...]
    xext_ref[8:8 + tr, :] = x
    conv = (cb_ref[...] + cw_ref[3:4, :] * x + cw_ref[2:3, :] * xext_ref[7:7 + tr, :]
            + cw_ref[1:2, :] * xext_ref[6:6 + tr, :] + cw_ref[0:1, :] * xext_ref[5:5 + tr, :])
    xext_ref[0:8, :] = xext_ref[tr:tr + 8, :]
    a, b = _rglru_gates(conv, wa_ref, ba_ref, wx_ref, bx_ref, lam_ref)
    row = lax.broadcasted_iota(jnp.int32, a.shape, 0)
    s = 1
    while s < tr:
        keep = row >= s
        a_sh = jnp.where(keep, pltpu.roll(a, s, 0), 1.0)
        b_sh = jnp.where(keep, pltpu.roll(b, s, 0), 0.0)
        b = a * b_sh + b
        a = a * a_sh
        s *= 2
    hs = a * h_ref[0:1, :] + b
    h_last = hs[tr - 1:tr, :]
    h_ref[...] = jnp.broadcast_to(h_last, h_ref.shape)
    hl_ref[...] = h_last
    ya_ref[...] = (hs * _gelu(ga_ref[...])).astype(BF16)


def _block_diag(w):
    n, a, b = w.shape
    eye = jnp.eye(n, dtype=w.dtype)
    return (eye[:, None, :, None] * w[:, :, None, :]).reshape(n * a, n * b)


def _rglru_weights(conv_w, conv_b, w_a, b_a, w_x, b_x, lam):
    return (conv_w.astype(F32), conv_b[None, :], _block_diag(w_a).astype(BF16), b_a[None, :],
            _block_diag(w_x).astype(BF16), b_x[None, :], lam[None, :])


def _rglru_prompt(xa, ga, weights, tr=256):
    t, lw = xa.shape
    tr = min(tr, t)
    row = pl.BlockSpec((tr, lw), lambda i: (i, 0))
    return pl.pallas_call(
        _rglru_prompt_kernel, grid=(t // tr,),
        in_specs=[row, row] + [_const_spec(w.shape) for w in weights],
        out_specs=[row, pl.BlockSpec((1, lw), lambda i: (0, 0))],
        out_shape=[jax.ShapeDtypeStruct((t, lw), BF16), jax.ShapeDtypeStruct((1, lw), F32)],
        scratch_shapes=[pltpu.VMEM((tr + 8, lw), F32), pltpu.VMEM((8, lw), F32)],
        compiler_params=_cparams(), name="rglru_prompt",
    )(xa, ga, *weights)


def _rglru_sample_kernel(xa_ref, ga_ref, sc_ref, h0_ref, cw_ref, cb_ref, wa_ref, ba_ref, wx_ref,
                         bx_ref, lam_ref, ya_ref, h_ref):
    conv = (cb_ref[...] + cw_ref[3:4, :] * xa_ref[...] + cw_ref[2:3, :] * sc_ref[2]
            + cw_ref[1:2, :] * sc_ref[1] + cw_ref[0:1, :] * sc_ref[0])
    a, b = _rglru_gates(conv, wa_ref, ba_ref, wx_ref, bx_ref, lam_ref)
    h = a * h0_ref[...] + b
    h_ref[...] = h
    ya_ref[...] = (h * _gelu(ga_ref[...])).astype(BF16)


def _rglru_sample(xa, ga, state_conv, state_h, weights):
    r, lw = xa.shape
    sc = jnp.transpose(state_conv, (1, 0, 2))
    args = (xa, ga, sc, state_h) + tuple(weights)
    full = lambda s: pl.BlockSpec(s, lambda i: (0,) * len(s))
    return pl.pallas_call(
        _rglru_sample_kernel, grid=(1,),
        in_specs=[full(a.shape) for a in args],
        out_specs=[full((r, lw)), full((r, lw))],
        out_shape=[jax.ShapeDtypeStruct((r, lw), BF16), jax.ShapeDtypeStruct((r, lw), F32)],
        compiler_params=_cparams(), name="rglru_sample",
    )(*args)


def _compress_prompt_kernel(xk_ref, xv_ref, pek_ref, pev_ref, wk_ref, wvT_ref, kg_ref,
                            kc_ref, vcT_ref, *, hd):
    xk = (xk_ref[...] + pek_ref[...]).astype(BF16)
    kc = _dot(xk, wk_ref[...])
    kc_ref[...] = _group_rmsnorm(kc, kg_ref[...], hd).astype(BF16)
    xv = (xv_ref[...] + pev_ref[...]).astype(BF16)
    vcT_ref[...] = _dot_nt(wvT_ref[...], xv).astype(BF16)


def _compress_weight(w, groups, hd):
    w3 = w.reshape(CMP_BLOCK, hd, hd)
    eye = jnp.eye(groups, dtype=w.dtype)
    full = eye[None, :, None, :, None] * w3[:, None, :, None, :]
    return full.reshape(CMP_BLOCK * groups * hd, groups * hd)


def _compress_prompt(kcmp, vcmp, w_k, w_v, pe_k, pe_v, k_gain_cmp, groups, hd):
    t, kvw = kcmp.shape
    ncb = t // CMP_BLOCK
    kdim = CMP_BLOCK * kvw
    xk = kcmp.reshape(ncb, kdim)
    xv = vcmp.reshape(ncb, kdim)
    wk = _compress_weight(w_k, groups, hd).astype(BF16)
    wvT = _compress_weight(w_v, groups, hd).T.astype(BF16)
    pek = jnp.tile(pe_k, (1, groups)).reshape(1, kdim)
    pev = jnp.tile(pe_v, (1, groups)).reshape(1, kdim)
    kg = jnp.tile(k_gain_cmp, groups)[None, :]
    nb = min(LANES, ncb)
    return pl.pallas_call(
        functools.partial(_compress_prompt_kernel, hd=hd), grid=(ncb // nb,),
        in_specs=[pl.BlockSpec((nb, kdim), lambda i: (i, 0)), pl.BlockSpec((nb, kdim), lambda i: (i, 0)),
                  _const_spec((1, kdim)), _const_spec((1, kdim)), _const_spec(wk.shape),
                  _const_spec(wvT.shape), _const_spec((1, kvw))],
        out_specs=[pl.BlockSpec((nb, kvw), lambda i: (i, 0)), pl.BlockSpec((kvw, nb), lambda i: (0, i))],
        out_shape=[jax.ShapeDtypeStruct((ncb, kvw), BF16), jax.ShapeDtypeStruct((kvw, ncb), BF16)],
        compiler_params=_cparams(), name="compress_prompt",
    )(xk, xv, pek, pev, wk, wvT, kg)


def _tile_lanes(x, n):
    return jnp.concatenate([x] * n, axis=1)


def _attend(s, shift, vT, slot, g, m_ref, acc_ref):
    m_old = m_ref[slot, g]
    m_blk = jnp.max(s, axis=0, keepdims=True)
    if shift is not None:
        m_blk = m_blk + shift
    m_new = jnp.maximum(m_old, m_blk)
    p = jnp.exp2(s - (m_new if shift is None else m_new - shift))
    acc_ref[slot, g] = jnp.exp2(m_old - m_new) * acc_ref[slot, g] + _dot(vT, p.astype(BF16))
    m_ref[slot, g] = m_new


def _bias_table_kernel(id_ref, val_ref, o_ref, *, reps):
    ids = _tile_lanes(id_ref[...], reps)
    out = jnp.zeros(ids.shape, F32)
    for k in range(val_ref.shape[0]):
        out = jnp.where(ids == k, val_ref[k:k + 1, :], out)
    o_ref[...] = out


def _bias_table(ids, vals):
    r = ids.shape[0]
    reps = vals.shape[1] // LANES
    full = lambda s: pl.BlockSpec(s, lambda i: (0,) * len(s))
    return pl.pallas_call(
        functools.partial(_bias_table_kernel, reps=reps), grid=(1,),
        in_specs=[full(ids.shape), full(vals.shape)], out_specs=full((r, vals.shape[1])),
        out_shape=jax.ShapeDtypeStruct((r, vals.shape[1]), F32),
        compiler_params=_cparams(), name="bias_table",
    )(jnp.asarray(ids, jnp.int32), vals)


def _nsa_prompt_kernel(qT_ref, gT_ref, kc_ref, vcT_ref, ks_ref, vsT_ref, kw_ref, vwT_ref,
                       b31_ref, tab_ref,
                       y_ref,
                       qaug_ref, sc_ref, p4_ref, selb_ref, m_ref, acc_ref, *, groups, hpg, hd):
    qb = pl.program_id(0)
    t0 = qb * Q_BLOCK
    ncb = kc_ref.shape[0]
    nsb = ncb * CMP_BLOCK // SEL_BLOCK
    gl = hpg * Q_BLOCK
    n_top = min(TOP_N, nsb)
    kvw = groups * hd
    bpg = KEY_GROUP // SEL_BLOCK

    rows = []
    for g in range(groups):
        top = jnp.concatenate([qT_ref[(g * hpg + hh) * hd:(g * hpg + hh + 1) * hd, :]
                               for hh in range(hpg)], axis=1)
        z = jnp.zeros_like(top)
        rows.append(jnp.concatenate([top if gg == g else z for gg in range(groups)], axis=1))
    qbd = jnp.concatenate(rows, axis=0)
    qaug_ref[0:kvw, :] = qbd
    qaug_ref[kvw:2 * kvw, :] = jnp.zeros((kvw, groups * gl), BF16)
    b31 = b31_ref[...]

    jrow = lax.broadcasted_iota(jnp.int32, (ncb, Q_BLOCK), 0)
    qcol = lax.broadcasted_iota(jnp.int32, (ncb, Q_BLOCK), 1)
    vis = _tile_lanes(jrow * CMP_BLOCK + (CMP_BLOCK - 1) <= t0 + qcol, groups * hpg)
    s_c = _dot(kc_ref[...], qbd) + b31
    sc_ref[...] = jnp.where(vis, s_c, NEG)
    odd = (qb % 2) == 1
    v_a = jnp.where(qb == 0, 0, jnp.where(odd, 1, 2))
    w_a = pl.multiple_of(jnp.where(qb == 0, 0, jnp.where(odd, 4 * qb - 4, 4 * qb - 8)), 8)
    v_b = jnp.where((qb > 0) & jnp.logical_not(odd), 0, 3)
    w_b = pl.multiple_of(jnp.where((qb > 0) & jnp.logical_not(odd), 4 * qb, 0), 8)
    sc_ref[pl.ds(w_a, 8), :] = sc_ref[pl.ds(w_a, 8), :] + tab_ref[pl.ds(pl.multiple_of(v_a * 8, 8), 8), :]
    sc_ref[pl.ds(w_b, 8), :] = sc_ref[pl.ds(w_b, 8), :] + tab_ref[pl.ds(pl.multiple_of(v_b * 8, 8), 8), :]
    s_c = sc_ref[...]
    m_c = jnp.max(s_c, axis=0, keepdims=True)
    p_c = jnp.where(vis, jnp.exp2(s_c - m_c), 0.0)
    l_c = jnp.sum(p_c, axis=0, keepdims=True)
    p_c = p_c * jnp.where(l_c > 0.0, 1.0 / l_c, 0.0)
    o_cmp = []
    for g in range(groups):
        pg = p_c[:, g * gl:(g + 1) * gl]
        o_cmp.append(_dot(vcT_ref[g * hd:(g + 1) * hd, :], pg.astype(BF16)))
        p4 = pg[:, 0:Q_BLOCK]
        for hh in range(1, hpg):
            p4 = p4 + pg[:, hh * Q_BLOCK:(hh + 1) * Q_BLOCK]
        p4_ref[g] = p4

    ratio = SEL_BLOCK // CMP_BLOCK
    jb = lax.broadcasted_iota(jnp.int32, (nsb, Q_BLOCK), 0)
    qc = lax.broadcasted_iota(jnp.int32, (nsb, Q_BLOCK), 1)
    cur = (t0 + qc) // SEL_BLOCK
    forced = (jb == 0) | (jb == cur) | (jb == cur - 1)
    for g in range(groups):
        imp = p4_ref[g, pl.ds(0, nsb, stride=ratio), :]
        for k in range(1, ratio):
            imp = imp + p4_ref[g, pl.ds(k, nsb, stride=ratio), :]
        imp = jnp.where(forced, FORCE_SCORE, imp)
        imp = jnp.where(jb <= cur, imp, -1.0)
        selb = jnp.full((nsb, Q_BLOCK), NEG, F32)
        for _ in range(n_top):
            mx = jnp.max(imp, axis=0, keepdims=True)
            idx = jnp.min(jnp.where(imp == mx, jb, nsb), axis=0, keepdims=True)
            hit = jb == idx
            selb = jnp.where(hit, 0.0, selb)
            imp = jnp.where(hit, -3.0e38, imp)
        selb_ref[g] = selb

    m_ref[...] = jnp.full(m_ref.shape, NEG, F32)
    acc_ref[...] = jnp.zeros(acc_ref.shape, F32)
    near_a = tab_ref[_TAB_NEAR_A:_TAB_NEAR_A + Q_BLOCK, :]
    near_b = tab_ref[_TAB_NEAR_B:_TAB_NEAR_B + Q_BLOCK, :]

    def set_mask_rows(grp):
        for g in range(groups):
            rows8 = selb_ref[g, pl.ds(pl.multiple_of(grp * bpg, bpg), bpg), :]
            tile = jnp.concatenate([rows8, jnp.zeros_like(rows8)], axis=0).astype(BF16)
            for hh in range(hpg):
                qaug_ref[kvw:kvw + 2 * bpg, g * gl + hh * Q_BLOCK:g * gl + (hh + 1) * Q_BLOCK] = tile

    def sel_chunk(k0, n, tile):
        s = _dot(ks_ref[pl.ds(k0, n), :], qaug_ref[...])
        for g in range(groups):
            sg = s[:, g * gl:(g + 1) * gl]
            vT = vsT_ref[g * ACC_ROWS:(g + 1) * ACC_ROWS, pl.ds(k0, n)]
            if tile is None:
                _attend(sg, b31[:, g * gl:(g + 1) * gl], vT, 0, g, m_ref, acc_ref)
            else:
                _attend(sg + tile[:, g * gl:(g + 1) * gl], None, vT, 0, g, m_ref, acc_ref)

    far_len = jnp.maximum(t0 - Q_BLOCK, 0)
    n_far = far_len // KEY_GROUP

    def far_body(c, carry):
        set_mask_rows(c)
        sel_chunk(pl.multiple_of(c * KEY_GROUP, KEY_GROUP), KEY_GROUP, None)
        return carry

    lax.fori_loop(0, n_far, far_body, 0)
    set_mask_rows(n_far)

    def rem_body(j, carry):
        sel_chunk(pl.multiple_of(n_far * KEY_GROUP + j * Q_BLOCK, Q_BLOCK), Q_BLOCK, None)
        return carry

    lax.fori_loop(0, (far_len - n_far * KEY_GROUP) // Q_BLOCK, rem_body, 0)

    @pl.when(qb >= 1)
    def _():
        set_mask_rows((t0 - Q_BLOCK) // KEY_GROUP)
        sel_chunk(pl.multiple_of(t0 - Q_BLOCK, Q_BLOCK), Q_BLOCK, near_a)

    set_mask_rows(t0 // KEY_GROUP)
    sel_chunk(pl.multiple_of(t0, Q_BLOCK), Q_BLOCK, near_b)

    def win_chunk(k0, tile):
        s = _dot(kw_ref[pl.ds(k0, Q_BLOCK), :], qaug_ref[0:kvw, :])
        for g in range(groups):
            sg = s[:, g * gl:(g + 1) * gl]
            vT = vwT_ref[g * ACC_ROWS:(g + 1) * ACC_ROWS, pl.ds(k0, Q_BLOCK)]
            if tile is None:
                _attend(sg, b31[:, g * gl:(g + 1) * gl], vT, 1, g, m_ref, acc_ref)
            else:
                _attend(sg + tile[:, g * gl:(g + 1) * gl], None, vT, 1, g, m_ref, acc_ref)

    win_chunk(pl.multiple_of(t0, Q_BLOCK), near_b)
    n_wc = WINDOW // Q_BLOCK
    for i in range(1, n_wc + 1):
        k0 = t0 - i * Q_BLOCK

        @pl.when(k0 >= 0)
        def _(i=i, k0=k0):
            k0a = pl.multiple_of(k0, Q_BLOCK)
            if i == 1:
                win_chunk(k0a, near_a)
            elif i == n_wc:
                win_chunk(k0a, tab_ref[_TAB_WIN_EDGE:_TAB_WIN_EDGE + Q_BLOCK, :])
            else:
                win_chunk(k0a, None)

    heads = groups * hpg
    pieces = []
    for g in range(groups):
        gate = [jnp.concatenate([gT_ref[j * heads + g * hpg + hh: j * heads + g * hpg + hh + 1, :]
                                 for hh in range(hpg)], axis=1) for j in range(3)]
        o = gate[0] * o_cmp[g]
        for slot in range(2):
            acc = acc_ref[slot, g]
            o = o + gate[1 + slot] * (acc[0:hd, :] * (1.0 / acc[hd:hd + 1, :]))
        for hh in range(hpg):
            pieces.append(o[:, hh * Q_BLOCK:(hh + 1) * Q_BLOCK])
    y_ref[...] = jnp.concatenate(pieces, axis=0).T.astype(BF16)


_TAB_NEAR_A = 32
_TAB_NEAR_B = _TAB_NEAR_A + Q_BLOCK
_TAB_WIN_EDGE = _TAB_NEAR_B + Q_BLOCK


def _nsa_prompt_tables(rel_bias, n_lanes):
    nb, heads = rel_bias.shape
    tbl = rel_bias.astype(F32) * LOG2E
    lanes = lambda x: jnp.repeat(x, n_lanes // heads, axis=1)
    vals = jnp.concatenate([lanes(tbl), jnp.full((1, n_lanes), NEG, F32), jnp.zeros((1, n_lanes), F32),
                            lanes(tbl - tbl[nb - 1:nb])], axis=0)
    c = np.arange(Q_BLOCK)[None, :]
    i8 = np.arange(8)[:, None]
    ids = []
    for base in (-(CMP_BLOCK - 1), 3 * CMP_BLOCK + 1, 7 * CMP_BLOCK + 1):
        d = base + c - CMP_BLOCK * i8
        ids.append(np.where(d >= 0, nb + 2 + _bucket_np(d, nb), nb + 1))
    ids.append(np.full((8, Q_BLOCK), nb + 1))
    r = np.arange(Q_BLOCK)[:, None]
    d = Q_BLOCK + c - r
    ids.append(_bucket_np(d, nb))
    d = c - r
    ids.append(np.where(d >= 0, _bucket_np(d, nb), nb))
    d = WINDOW + c - r
    ids.append(np.where(d < WINDOW, _bucket_np(d, nb), nb))
    return vals[nb - 1:nb], _bias_table(np.concatenate(ids, axis=0), vals)


def _nsa_prompt(qT, gT, kc, vcT, ks, vsT, kw, vwT, rel_bias, groups, hpg, hd):
    qw, t = qT.shape
    ncb = kc.shape[0]
    nsb = t // SEL_BLOCK
    gl = hpg * Q_BLOCK
    kvw = groups * hd
    assert t % KEY_GROUP == 0 and Q_BLOCK == 4 * CMP_BLOCK and MAX_DIST <= Q_BLOCK
    tables = _nsa_prompt_tables(rel_bias, groups * gl)
    resident = (kc, vcT, ks, vsT, kw, vwT) + tuple(tables)
    return pl.pallas_call(
        functools.partial(_nsa_prompt_kernel, groups=groups, hpg=hpg, hd=hd),
        grid=(t // Q_BLOCK,),
        in_specs=[pl.BlockSpec((qw, Q_BLOCK), lambda i: (0, i)),
                  pl.BlockSpec((gT.shape[0], Q_BLOCK), lambda i: (0, i))]
                 + [_const_spec(a.shape) for a in resident],
        out_specs=pl.BlockSpec((Q_BLOCK, qw), lambda i: (i, 0)),
        out_shape=jax.ShapeDtypeStruct((t, qw), BF16),
        scratch_shapes=[pltpu.VMEM((2 * kvw, groups * gl), BF16),
                        pltpu.VMEM((ncb, groups * gl), F32),
                        pltpu.VMEM((groups, ncb, Q_BLOCK), F32),
                        pltpu.VMEM((groups, nsb, Q_BLOCK), F32),
                        pltpu.VMEM((2, groups, 1, gl), F32),
                        pltpu.VMEM((2, groups, ACC_ROWS, gl), F32)],
        compiler_params=_cparams(), name="nsa_prompt",
    )(qT, gT, *resident)


def _split3(x):
    hi = x.astype(BF16)
    r1 = x - hi.astype(F32)
    mid = r1.astype(BF16)
    lo = (r1 - mid.astype(F32)).astype(BF16)
    return hi, mid, lo


def _nsa_sample_kernel(pt_ref, cache_ref, qbd_ref, qn_ref, gates_ref, kvn_ref, winn_ref, cwin_ref,
                       wkv_ref, pe_ref, perm_ref, kg_ref, gsum_ref,
                       cb_ref, sb_ref, wb_ref, b0_ref, e_ref,
                       o_ref,
                       ring_ref, bufkv_ref, bufb_ref, sema, semb, p_ref, s_ref,
                       *, hd, n_pages, page, chunk):
    b = pl.program_id(0)
    nb = pl.num_programs(0)
    kvw = kg_ref.shape[1]
    past = n_pages * page
    ncb = past // CMP_BLOCK
    nsb = past // SEL_BLOCK
    n_chunks = n_pages // chunk
    bpp = 2 * page // CMP_BLOCK

    def raw_copy(bb, ch, p):
        return pltpu.make_async_copy(cache_ref.at[pt_ref[bb, ch * chunk + p], 0:2 * kvw, :],
                                     ring_ref.at[ch % 2, p], sema.at[ch % 2])

    def sel_copy(bb, p):
        return pltpu.make_async_copy(cache_ref.at[pt_ref[bb, p], 2 * kvw:4 * kvw, :],
                                     bufb_ref.at[:, pl.ds(p * page, page)], semb)

    def for_pages(n, fn, unroll=1):
        def body(p, c):
            fn(p)
            return c
        lax.fori_loop(0, n, body, 0, unroll=unroll)

    @pl.when(b == 0)
    def _():
        for_pages(chunk, lambda p: raw_copy(0, 0, p).start())
        for_pages(n_pages, lambda p: sel_copy(0, p).start())

    qbd = qbd_ref[0]

    for ch in range(n_chunks):
        if ch + 1 < n_chunks:
            for_pages(chunk, lambda p, ch=ch: raw_copy(b, ch + 1, p).start())
        else:
            @pl.when(b + 1 < nb)
            def _():
                for_pages(chunk, lambda p: raw_copy(b + 1, 0, p).start())
        for_pages(chunk, lambda p, ch=ch: raw_copy(b, ch, p).wait())

        def regroup(q, ch=ch):
            xt = jnp.concatenate([ring_ref[ch % 2, 2 * q], ring_ref[ch % 2, 2 * q + 1]], axis=1)
            y = _dot_nt(perm_ref[...], (xt + pe_ref[...]).astype(BF16))
            blocks = pl.ds(pl.multiple_of((ch * (chunk // 2) + q) * bpp, bpp), bpp)
            for r in range(CMP_BLOCK):
                bufkv_ref[r, blocks, :] = y[r * bpp:(r + 1) * bpp, :]
        for_pages(chunk // 2, regroup, unroll=4)

    zs = [jnp.zeros((ncb, 2 * kvw), F32), jnp.zeros((ncb, 2 * kvw), F32)]
    for r in range(CMP_BLOCK):
        zs[r % 2] = zs[r % 2] + _dot(bufkv_ref[r].astype(BF16), wkv_ref[r])
    z = zs[0] + zs[1]
    vc = z[:, kvw:2 * kvw]
    kc = _group_rmsnorm(z[:, 0:kvw], kg_ref[...], hd).astype(BF16)
    s_c = _dot(kc, qbd) + cb_ref[...]
    m_c = jnp.max(s_c, axis=0, keepdims=True)
    p_c = jnp.exp(s_c - m_c)
    p_c = p_c * (1.0 / jnp.sum(p_c, axis=0, keepdims=True))
    o_cmp = _dot(p_c.T.astype(BF16), vc.astype(BF16))
    p_ref[...] = p_c
    ratio = SEL_BLOCK // CMP_BLOCK
    imp = p_ref[pl.ds(0, nsb, stride=ratio), :]
    for k in range(1, ratio):
        imp = imp + p_ref[pl.ds(k, nsb, stride=ratio), :]
    hi, mid, lo = _split3(imp)
    gs = gsum_ref[...]
    imp = _dot(hi, gs) + _dot(mid, gs) + _dot(lo, gs)
    jb = lax.broadcasted_iota(jnp.int32, (nsb, LANES), 0)
    imp = jnp.where((jb == 0) | (jb == nsb - 1), FORCE_SCORE, imp)
    selb = jnp.full((nsb, LANES), NEG, F32)
    for _ in range(min(TOP_N, nsb + 1) - 1):
        mx = jnp.max(imp, axis=0, keepdims=True)
        idx = jnp.min(jnp.where(imp == mx, jb, nsb), axis=0, keepdims=True)
        hit = jb == idx
        selb = jnp.where(hit, 0.0, selb)
        imp = jnp.where(hit, -3.0e38, imp)
    rows = qn_ref.shape[1]
    sel_nat = selb.T[0:rows, :]

    for_pages(n_pages, lambda p: sel_copy(b, p).wait())
    qn = qn_ref[0]
    qnb = qn.astype(BF16)
    kvn = kvn_ref[0]
    b0 = b0_ref[:, 0:1]
    s_new = jnp.sum(qn * kvn[:, 2 * kvw:3 * kvw], axis=1, keepdims=True) + b0
    nks = e_ref.shape[1]
    bps = nks // SEL_BLOCK
    for c in range(past // nks):
        rolled = sel_nat if c == 0 else pltpu.roll(sel_nat, nsb - c * bps, 1)
        mask = _dot(rolled.astype(BF16), e_ref[...])
        k_t = bufb_ref[0:kvw, c * nks:(c + 1) * nks].astype(BF16)
        s_ref[:, c * nks:(c + 1) * nks] = _dot(qnb, k_t) + mask + sb_ref[:, c * nks:(c + 1) * nks]
    s_all = s_ref[...]
    m_s = jnp.maximum(jnp.max(s_all, axis=1, keepdims=True), s_new)
    p_new = jnp.exp(s_new - m_s)
    l_s = p_new
    acc = p_new * kvn[:, 3 * kvw:4 * kvw]
    for c in range(past // nks):
        p = jnp.exp(s_all[:, c * nks:(c + 1) * nks] - m_s)
        l_s = l_s + jnp.sum(p, axis=1, keepdims=True)
        acc = acc + _dot_nt(p.astype(BF16), bufb_ref[kvw:2 * kvw, c * nks:(c + 1) * nks].astype(BF16))

    @pl.when(b + 1 < nb)
    def _():
        for_pages(n_pages, lambda p: sel_copy(b + 1, p).start())

    o_sel = acc * (1.0 / l_s)

    winn = winn_ref[0]
    s_w = _dot(qnb, cwin_ref[0, 0:kvw, :].astype(BF16)) + wb_ref[...]
    s_wn = jnp.sum(qn * winn[:, 0:kvw], axis=1, keepdims=True) + b0
    m_w = jnp.maximum(jnp.max(s_w, axis=1, keepdims=True), s_wn)
    p_w = jnp.exp(s_w - m_w)
    p_wn = jnp.exp(s_wn - m_w)
    l_w = jnp.sum(p_w, axis=1, keepdims=True) + p_wn
    o_win = (_dot_nt(p_w.astype(BF16), cwin_ref[0, kvw:2 * kvw, :].astype(BF16))
             + p_wn * winn[:, kvw:2 * kvw]) * (1.0 / l_w)

    g = gates_ref[0]
    out = g[0] * o_cmp[0:rows, :] + g[1] * o_sel + g[2] * o_win
    o_ref[0] = out[0:8, :]


def _nsa_sample(q, gates, kv_new, win_new, cache_kv, cache_win, page_table, lp, rel_bias,
                groups, hpg, hd):
    r = q.shape[0]
    heads = groups * hpg
    kvw = groups * hd
    n_phys, page = cache_kv.shape[0], cache_kv.shape[1]
    n_pages = page_table.shape[1]
    past = n_pages * page
    wbuf = cache_win.shape[1]
    ncb, nsb = past // CMP_BLOCK, past // SEL_BLOCK
    rows = 16
    nks = min(NKS_SAMPLE, past)
    chunk = min(32, n_pages // 2)
    assert n_pages % (2 * chunk) == 0 and chunk % 8 == 0 and past % nks == 0 and heads <= rows
    q3 = q.reshape(r, heads, hd)
    gmask = (jnp.arange(groups)[:, None] == (jnp.arange(heads) // hpg)[None, :]).astype(F32)
    qbd = jnp.einsum('rhd,gh->rgdh', q3, gmask).reshape(r, kvw, heads)
    qbd = jnp.pad(qbd, ((0, 0), (0, 0), (0, LANES - heads))).astype(BF16)
    qn = jnp.einsum('rhd,gh->rhgd', q3, gmask).reshape(r, heads, kvw)
    qn = jnp.pad(qn, ((0, 0), (0, rows - heads), (0, 0)))
    gts = gates[:, :3 * heads].reshape(r, heads, 3).transpose(0, 2, 1)
    gts = jnp.broadcast_to(gts[:, :, :, None], (r, 3, heads, kvw))
    gts = jnp.pad(gts, ((0, 0), (0, 0), (0, rows - heads), (0, 0)))
    wk = _compress_weight(lp['cmp_w_k'], groups, hd).reshape(CMP_BLOCK, kvw, kvw)
    wv = _compress_weight(lp['cmp_w_v'], groups, hd).reshape(CMP_BLOCK, kvw, kvw)
    zero = jnp.zeros_like(wk)
    wkv = jnp.concatenate([jnp.concatenate([wk, zero], axis=2),
                           jnp.concatenate([zero, wv], axis=2)], axis=1).astype(BF16)
    pe_t = jnp.concatenate([jnp.tile(lp['pe_k'].T, (groups, 1)), jnp.tile(lp['pe_v'].T, (groups, 1))], axis=0)
    pe_t = jnp.tile(pe_t, (1, 2 * page // CMP_BLOCK))
    bpp = 2 * page // CMP_BLOCK
    dst = np.arange(2 * page)
    src = (dst % bpp) * CMP_BLOCK + dst // bpp
    perm = jnp.asarray(src[:, None] == np.arange(2 * page)[None, :], BF16)
    kg = jnp.tile(lp['k_gain'][0], groups)[None, :]
    lane_g = np.where(np.arange(LANES) < heads, np.arange(LANES) // hpg, -1)
    gsum = jnp.asarray((lane_g[:, None] == lane_g[None, :]) & (lane_g[:, None] >= 0), BF16)
    nbk = rel_bias.shape[0]
    tbl = rel_bias.astype(F32)
    d_c = (past - (np.arange(ncb) * CMP_BLOCK + CMP_BLOCK - 1))[:, None]
    vals_t = jnp.concatenate([tbl, jnp.broadcast_to(tbl[:, :1], (nbk, LANES - heads))], axis=1)
    cb = _bias_table(np.broadcast_to(_bucket_np(d_c, nbk), (ncb, LANES)), vals_t)
    vals_n = jnp.concatenate([jnp.repeat(tbl, LANES, axis=1), jnp.full((1, heads * LANES), NEG, F32)], axis=0)

    def head_rows(ids):
        out = _bias_table(ids, vals_n).reshape(ids.shape[0], heads, LANES)
        out = jnp.transpose(out, (1, 0, 2)).reshape(heads, ids.shape[0] * LANES)
        return jnp.pad(out, ((0, rows - heads), (0, 0)))

    sb = head_rows(_bucket_np(past - np.arange(past), nbk).reshape(past // LANES, LANES))
    d_w = wbuf - np.arange(wbuf)
    wb = head_rows(np.where(d_w < WINDOW, _bucket_np(d_w, nbk), nbk).reshape(wbuf // LANES, LANES))
    b0 = jnp.pad(jnp.broadcast_to(tbl[0][:, None], (heads, LANES)), ((0, rows - heads), (0, 0)))
    blk = np.arange(nsb)[:, None]
    spread = jnp.asarray((blk == np.arange(nks)[None, :] // SEL_BLOCK), BF16)

    cache_t = jnp.transpose(cache_kv, (0, 2, 3, 4, 1)).reshape(n_phys, 4 * kvw, page)
    cwin_t = jnp.transpose(cache_win, (0, 2, 3, 4, 1)).reshape(r, 2 * kvw, wbuf)
    kvn = kv_new.reshape(r, 1, 4 * kvw)
    winn = win_new.reshape(r, 1, 2 * kvw)
    per_b = lambda s: pl.BlockSpec((1,) + s, lambda b, pt: (b,) + (0,) * len(s))
    const = lambda a: pl.BlockSpec(a.shape, lambda b, pt: (0,) * a.ndim, pipeline_mode=pl.Buffered(1))
    consts = (wkv, pe_t, perm, kg, gsum, cb, sb, wb, b0, spread)
    grid_spec = pltpu.PrefetchScalarGridSpec(
        num_scalar_prefetch=1, grid=(r,),
        in_specs=[pl.BlockSpec(memory_space=pl.ANY), per_b((kvw, LANES)), per_b((rows, kvw)),
                  per_b((3, rows, kvw)), per_b((1, 4 * kvw)), per_b((1, 2 * kvw)),
                  per_b((2 * kvw, wbuf))]
                 + [const(a) for a in consts],
        out_specs=per_b((8, kvw)),
        scratch_shapes=[pltpu.VMEM((2, chunk, 2 * kvw, page), F32),
                        pltpu.VMEM((CMP_BLOCK, ncb, 2 * kvw), F32),
                        pltpu.VMEM((2 * kvw, past), F32),
                        pltpu.SemaphoreType.DMA((2,)), pltpu.SemaphoreType.DMA(()),
                        pltpu.VMEM((ncb, LANES), F32), pltpu.VMEM((rows, past), F32)])
    o8 = pl.pallas_call(
        functools.partial(_nsa_sample_kernel, hd=hd, n_pages=n_pages, page=page, chunk=chunk),
        grid_spec=grid_spec,
        out_shape=jax.ShapeDtypeStruct((r, 8, kvw), F32),
        compiler_params=_cparams(), name="nsa_sample",
    )(page_table, cache_t, qbd, qn, gts, kvn, winn, cwin_t, *consts)
    o4 = o8[:, :heads, :].reshape(r, heads, groups, hd)
    y = jnp.take_along_axis(o4, (jnp.arange(heads) // hpg)[None, :, None, None], axis=2)
    return y.reshape(r, heads * hd).astype(BF16)


def _mlp_ple(h, p_ref, gm_ref, wu_ref, wd_ref, gp_ref, wg_ref, wp_ref):
    up = _dot(_rms_rows(h, gm_ref[...]).astype(BF16), wu_ref[...])
    act = jnp.square(jnp.maximum(up, 0.0)).astype(BF16)
    h = h + _dot(act, wd_ref[...])
    gate = jax.nn.sigmoid(_dot(_rms_rows(h, gp_ref[...]).astype(BF16), wg_ref[...]))
    return h + gate * _dot(p_ref[...].astype(BF16), wp_ref[...])


def _even_tail_kernel(h_ref, ya_ref, yb_ref, p_ref, woa_ref, wob_ref,
                      gm_ref, wu_ref, wd_ref, gp_ref, wg_ref, wp_ref, o_ref):
    h = h_ref[...] + _dot(ya_ref[...], woa_ref[...]) + _dot(yb_ref[...], wob_ref[...])
    o_ref[...] = _mlp_ple(h, p_ref, gm_ref, wu_ref, wd_ref, gp_ref, wg_ref, wp_ref)


def _tail_weights(g_mlp, w_up, w_down, g_ple, w_gate, w_proj):
    return (g_mlp[None, :], w_up.astype(BF16), w_down.astype(BF16), g_ple[None, :],
            w_gate.astype(BF16), w_proj.astype(BF16))


def _even_tail(h, ya, yb, p, w_out, tail_w, tm=256):
    t, d = h.shape
    tm = min(tm, t)
    lw = ya.shape[1]
    woa, wob = w_out[:lw].astype(BF16), w_out[lw:].astype(BF16)
    row = lambda w: pl.BlockSpec((tm, w), lambda i: (i, 0))
    weights = (woa, wob) + tuple(tail_w)
    return pl.pallas_call(
        _even_tail_kernel, grid=(t // tm,),
        in_specs=[row(d), row(lw), row(yb.shape[1]), row(p.shape[1])]
                 + [_const_spec(w.shape) for w in weights],
        out_specs=row(d), out_shape=jax.ShapeDtypeStruct((t, d), F32),
        compiler_params=_cparams(), name="even_tail",
    )(h, ya, yb, p, *weights)


def _odd_kernel(h_ref, p_ref, gx_ref, wi_ref, lg_ref, lb_ref, sw_ref, sb_ref, wo_ref,
                gm_ref, wu_ref, wd_ref, gp_ref, wg_ref, wp_ref, o_ref, v_ref, *, sw_groups, single):
    h = h_ref[...]
    tm = h.shape[0]
    z = _gelu(_dot(_rms_rows(h, gx_ref[...]).astype(BF16), wi_ref[...]))
    width = z.shape[1] // 2
    u, v = z[:, :width], z[:, width:]
    mu = jnp.mean(v, axis=-1, keepdims=True)
    vc = v - mu
    vn = vc * lax.rsqrt(jnp.mean(vc * vc, axis=-1, keepdims=True) + EPS) * lg_ref[...] + lb_ref[...]
    v_ref[...] = vn
    gw = width // sw_groups
    if single:
        s = vn * sw_ref[...] + sb_ref[...]
    else:
        vb = vn.astype(BF16)
        parts = []
        for c in range(tm // CHUNK):
            cols = [_dot(sw_ref[g], vb[c * CHUNK:(c + 1) * CHUNK, g * gw:(g + 1) * gw])
                    for g in range(sw_groups)]
            parts.append(jnp.concatenate(cols, axis=1) + sb_ref[...])
        s = jnp.concatenate(parts, axis=0)
    h = h + _dot((u * s).astype(BF16), wo_ref[...])
    o_ref[...] = _mlp_ple(h, p_ref, gm_ref, wu_ref, wd_ref, gp_ref, wg_ref, wp_ref)


def _odd_layer(h, p, g_mix, w_in, ln_g, ln_b, sgu_w, sgu_b, w_out, tail_w, single, tm=256):
    t, d = h.shape
    tm = min(tm, t)
    ng, ch, _ = sgu_w.shape
    width = w_in.shape[1] // 2
    gw = width // ng
    if single:
        sw = jnp.repeat(sgu_w[:, 0, 0], gw)[None, :]
        sb = jnp.repeat(sgu_b[:, 0], gw)[None, :]
    else:
        sw = (sgu_w * jnp.tril(jnp.ones((ch, ch), sgu_w.dtype))).astype(BF16)
        sb = jnp.repeat(sgu_b.T, gw, axis=1)
    weights = (g_mix[None, :], w_in.astype(BF16), ln_g[None, :], ln_b[None, :], sw, sb,
               w_out.astype(BF16)) + tuple(tail_w)
    row = lambda w: pl.BlockSpec((tm, w), lambda i: (i, 0))
    return pl.pallas_call(
        functools.partial(_odd_kernel, sw_groups=ng, single=single), grid=(t // tm,),
        in_specs=[row(d), row(p.shape[1])] + [_const_spec(w.shape) for w in weights],
        out_specs=[row(d), row(width)],
        out_shape=[jax.ShapeDtypeStruct((t, d), F32), jax.ShapeDtypeStruct((t, width), F32)],
        compiler_params=_cparams(), name="odd_single" if single else "odd_prompt",
    )(h, p, *weights)


def kernel(x_prompt, x_sample, cache_kv, cache_win, state_conv, state_h, page_table, p_prompt, p_sample,
           rel_bias, g_mix, g_mlp, w_up, w_down, g_ple, w_ple_gate, w_ple_proj,
           w_in_even, w_out_even, conv_w, conv_b, rg_w_a, rg_b_a, rg_w_x, rg_b_x, rg_lambda,
           q_gain, k_gain, cmp_w_k, cmp_w_v, cmp_pe_k, cmp_pe_v,
           w_in_odd, ln_v_g, ln_v_b, sgu_w, sgu_b, w_out_odd):
    bsz, t, d = x_prompt.shape
    r = x_sample.shape[0]
    assert bsz == 1 and x_sample.shape[1] == 1
    depth = g_mix.shape[0]
    lw = conv_w.shape[-1]
    hd = q_gain.shape[-1]
    groups = cache_kv.shape[4]
    heads = rel_bias.shape[1]
    hpg = heads // groups
    qw, kvw = heads * hd, groups * hd
    dims = (lw, qw, kvw, hd, heads)
    wbuf = cache_win.shape[2]

    hp = x_prompt.reshape(t, d)
    hs = x_sample.reshape(r, d)
    kvp, kvs, wnp, wns, cvp, cvs, hhp, hhs, vvs = [], [], [], [], [], [], [], [], []
    for i in range(depth):
        tail_w = _tail_weights(g_mlp[i], w_up[i], w_down[i], g_ple[i], w_ple_gate[i], w_ple_proj[i])
        if i % 2 == 0:
            e = i // 2
            lp = {'cmp_w_k': cmp_w_k[e], 'cmp_w_v': cmp_w_v[e], 'pe_k': cmp_pe_k[e], 'pe_v': cmp_pe_v[e],
                  'k_gain': k_gain[e]}
            rg_w = _rglru_weights(conv_w[e], conv_b[e], rg_w_a[e], rg_b_a[e], rg_w_x[e], rg_b_x[e],
                                  rg_lambda[e])
            (xa, ga, kv_rows, win_rows, kcmp, vcmp, ks, kw, qT, vsT, vwT, gT) = _even_in_prompt(
                hp, g_mix[i], w_in_even[e], q_gain[e], k_gain[e], dims)
            ya, h_last = _rglru_prompt(xa, ga, rg_w)
            kc, vcT = _compress_prompt(kcmp, vcmp, cmp_w_k[e], cmp_w_v[e], cmp_pe_k[e], cmp_pe_v[e],
                                       k_gain[e, 0], groups, hd)
            yb = _nsa_prompt(qT, gT, kc, vcT, ks, vsT, kw, vwT, rel_bias, groups, hpg, hd)
            hp = _even_tail(hp, ya, yb, p_prompt[i, 0], w_out_even[e], tail_w)
            kvp.append(kv_rows.reshape(1, t, 4, groups, hd))
            wnp.append(win_rows[t - wbuf:].reshape(1, wbuf, 2, groups, hd))
            cvp.append(xa[t - (conv_w.shape[1] - 1):][None])
            hhp.append(h_last)
            xa_s, ga_s, q_s, kv_s, win_s, gate_s = _even_in_sample(
                hs, g_mix[i], w_in_even[e], q_gain[e], k_gain[e], dims)
            ya_s, h_s = _rglru_sample(xa_s, ga_s, state_conv[e], state_h[e], rg_w)
            yb_s = _nsa_sample(q_s, gate_s, kv_s, win_s, cache_kv[e], cache_win[e], page_table, lp,
                               rel_bias, groups, hpg, hd)
            hs = _even_tail(hs, ya_s, yb_s, p_sample[i, :, 0], w_out_even[e], tail_w)
            kvs.append(kv_s.reshape(r, 1, 4, groups, hd))
            wns.append(jnp.concatenate([cache_win[e][:, 1:], win_s.reshape(r, 1, 2, groups, hd)], axis=1))
            cvs.append(jnp.concatenate([state_conv[e][:, 1:], xa_s[:, None, :]], axis=1))
            hhs.append(h_s)
        else:
            o = i // 2
            hp, _ = _odd_layer(hp, p_prompt[i, 0], g_mix[i], w_in_odd[o], ln_v_g[o], ln_v_b[o],
                               sgu_w[o], sgu_b[o], w_out_odd[o], tail_w, single=False)
            hs, v_new = _odd_layer(hs, p_sample[i, :, 0], g_mix[i], w_in_odd[o], ln_v_g[o], ln_v_b[o],
                                   sgu_w[o], sgu_b[o], w_out_odd[o], tail_w, single=True)
            vvs.append(v_new[:, None, :])
    v_sample = jnp.stack(vvs) if vvs else jnp.zeros((0, r, 1, w_in_odd.shape[-1] // 2), F32)
    return (hp[None], hs[:, None, :], jnp.stack(kvp), jnp.stack(kvs), jnp.stack(wnp), jnp.stack(wns),
            jnp.stack(cvp), jnp.stack(cvs), jnp.stack(hhp), jnp.stack(hhs), v_sample)
```

```python
import functools
import math

import numpy as np
import jax
import jax.numpy as jnp
from jax import lax
from jax.experimental import pallas as pl
from jax.experimental.pallas import tpu as pltpu

F32 = jnp.float32
BF16 = jnp.bfloat16

EPS = 1e-6
NEG = -1e30
RG_C = 8.0
CMP_BLOCK = 32
SEL_BLOCK = 64
TOP_N = 16
WINDOW = 512
Q_BLOCK = 128
FORCE_SCORE = 1.0e4
MAX_DIST = 128
CHUNK = 128

LANES = 128
VMEM_LIMIT = 56 * 1024 * 1024

LOG2E = math.log2(math.e)
KEY_GROUP = 512
ACC_ROWS = 80
NKS_SAMPLE = 2048


def _cparams(n_axes=1):
    return pltpu.CompilerParams(dimension_semantics=("arbitrary",) * n_axes,
                                vmem_limit_bytes=VMEM_LIMIT)


def _const_spec(shape):
    nd = len(shape)
    return pl.BlockSpec(shape, lambda *_: (0,) * nd, pipeline_mode=pl.Buffered(1))


def _dot(a, b):
    return jnp.dot(a, b, preferred_element_type=F32)


def _dot_nt(a, b):
    return lax.dot_general(a, b, (((1,), (1,)), ((), ())), preferred_element_type=F32)


def _gelu(x):
    return 0.5 * x * (1.0 + jnp.tanh(math.sqrt(2.0 / math.pi) * (x + 0.044715 * (x * x * x))))


def _softplus(x):
    return jnp.maximum(x, 0.0) + jnp.log1p(jnp.exp(-jnp.abs(x)))


def _rms_rows(x, g):
    return x * lax.rsqrt(jnp.mean(x * x, axis=-1, keepdims=True) + EPS) * g


def _group_rmsnorm(x, gain, width):
    n = x.shape[-1] // width
    lane = lax.broadcasted_iota(jnp.int32, x.shape, 1)
    x2 = x * x
    ms = jnp.zeros_like(x)
    for i in range(n):
        inside = (lane >= i * width) & (lane < (i + 1) * width)
        s = jnp.sum(jnp.where(inside, x2, 0.0), axis=-1, keepdims=True)
        ms = jnp.where(inside, s, ms)
    return x * lax.rsqrt(ms * (1.0 / width) + EPS) * gain


def _bucket_np(dist, n_buckets):
    n = np.maximum(dist, 0)
    exact = n_buckets // 2
    nf = np.maximum(n, 1).astype(np.float32)
    large = exact + (np.log(nf / np.float32(exact)) / np.float32(math.log(MAX_DIST / exact))
                     * np.float32(n_buckets - exact)).astype(np.int32)
    return np.where(n < exact, n, np.minimum(large, n_buckets - 1)).astype(np.int32)


def _even_in_prompt_kernel(x_ref, g_ref, wn_ref, wt_ref, qg_ref, kg_ref,
                           xa_ref, ga_ref, kv_ref, win_ref, kc_ref, vc_ref,
                           ks_ref, kw_ref, qT_ref, vsT_ref, vwT_ref, gT_ref, *, lw, qw, kvw, hd):
    tm = x_ref.shape[0]
    hn = _rms_rows(x_ref[...], g_ref[...]).astype(BF16)
    zn = _dot(hn, wn_ref[...])
    xa_ref[...] = zn[:, :lw]
    ga_ref[...] = zn[:, lw:2 * lw]
    kv = [zn[:, 2 * lw + i * kvw: 2 * lw + (i + 1) * kvw] for i in range(6)]
    k_sel = _group_rmsnorm(kv[2], kg_ref[1:2, :], hd)
    k_win = _group_rmsnorm(kv[4], kg_ref[2:3, :], hd)
    kv_ref[:, 0 * kvw:1 * kvw] = kv[0]
    kv_ref[:, 1 * kvw:2 * kvw] = kv[1]
    kv_ref[:, 2 * kvw:3 * kvw] = k_sel
    kv_ref[:, 3 * kvw:4 * kvw] = kv[3]
    win_ref[:, :kvw] = k_win
    win_ref[:, kvw:] = kv[5]
    kc_ref[...] = kv[0]
    vc_ref[...] = kv[1]
    blk = lax.broadcasted_iota(jnp.int32, (tm, kvw), 0) // SEL_BLOCK
    col = lax.broadcasted_iota(jnp.int32, (tm, kvw), 1)
    ks_ref[:, :kvw] = k_sel.astype(BF16)
    ks_ref[:, kvw:] = jnp.where(col == blk, 1.0, 0.0).astype(BF16)
    kw_ref[...] = k_win.astype(BF16)

    zt = _dot_nt(wt_ref[...], hn)
    for h in range(qw // hd):
        blk = zt[h * hd:(h + 1) * hd, :]
        r = lax.rsqrt(jnp.mean(blk * blk, axis=0, keepdims=True) + EPS)
        qn = blk * r
        for j in range(0, tm, LANES):
            qT_ref[h * hd:(h + 1) * hd, j:j + LANES] = (
                qn[:, j:j + LANES] * qg_ref[h * hd:(h + 1) * hd, :]).astype(BF16)
    pad = ACC_ROWS - hd
    ones_rows = jnp.where(lax.broadcasted_iota(jnp.int32, (pad, tm), 0) == 0, 1.0, 0.0).astype(BF16)
    for out_ref, base in ((vsT_ref, qw), (vwT_ref, qw + kvw)):
        for g in range(kvw // hd):
            out_ref[g * ACC_ROWS:g * ACC_ROWS + hd, :] = zt[base + g * hd:base + (g + 1) * hd, :].astype(BF16)
            out_ref[g * ACC_ROWS + hd:(g + 1) * ACC_ROWS, :] = ones_rows
    gT_ref[...] = jax.nn.sigmoid(zt[qw + 2 * kvw:, :])


def _even_in_prompt(x, g_mix, w_in, q_gain, k_gain, dims, tm=512):
    t, d = x.shape
    lw, qw, kvw, hd, heads = dims
    tm = min(tm, t)
    c0, c1, c2 = 2 * lw, 2 * lw + qw, 2 * lw + qw + 6 * kvw
    wn = jnp.concatenate([w_in[:, :c0], w_in[:, c1:c2]], axis=1).astype(BF16)
    wg = w_in[:, c2:c2 + 3 * heads].reshape(d, heads, 3).transpose(0, 2, 1).reshape(d, 3 * heads)
    wt = jnp.concatenate([w_in[:, c0:c1], w_in[:, c1 + 3 * kvw:c1 + 4 * kvw],
                          w_in[:, c1 + 5 * kvw:c1 + 6 * kvw], wg,
                          jnp.zeros((d, 32 - 3 * heads), F32)], axis=1).T.astype(BF16)
    assert tm % KEY_GROUP == 0
    qg = jnp.broadcast_to((jnp.tile(q_gain, heads) * (hd ** -0.5 * LOG2E))[:, None], (qw, LANES)).astype(F32)
    vrows = (kvw // hd) * ACC_ROWS
    kg = jnp.tile(k_gain, (1, kvw // hd)).astype(F32)
    kg = jnp.concatenate([kg, jnp.zeros((5, kvw), F32)], axis=0)
    nt = wt.shape[0]
    row = lambda w: pl.BlockSpec((tm, w), lambda i: (i, 0))
    col = lambda r: pl.BlockSpec((r, tm), lambda i: (0, i))
    outs = pl.pallas_call(
        functools.partial(_even_in_prompt_kernel, lw=lw, qw=qw, kvw=kvw, hd=hd),
        grid=(t // tm,),
        in_specs=[row(d), _const_spec((1, d)), _const_spec(wn.shape), _const_spec(wt.shape),
                  _const_spec(qg.shape), _const_spec(kg.shape)],
        out_specs=[row(lw), row(lw), row(4 * kvw), row(2 * kvw), row(kvw), row(kvw),
                   row(2 * kvw), row(kvw), col(qw), col(vrows), col(vrows), col(32)],
        out_shape=[jax.ShapeDtypeStruct((t, lw), F32), jax.ShapeDtypeStruct((t, lw), F32),
                   jax.ShapeDtypeStruct((t, 4 * kvw), F32), jax.ShapeDtypeStruct((t, 2 * kvw), F32),
                   jax.ShapeDtypeStruct((t, kvw), F32), jax.ShapeDtypeStruct((t, kvw), F32),
                   jax.ShapeDtypeStruct((t, 2 * kvw), BF16), jax.ShapeDtypeStruct((t, kvw), BF16),
                   jax.ShapeDtypeStruct((qw, t), BF16), jax.ShapeDtypeStruct((vrows, t), BF16),
                   jax.ShapeDtypeStruct((vrows, t), BF16), jax.ShapeDtypeStruct((32, t), F32)],
        compiler_params=_cparams(), name="even_in_prompt",
    )(x, g_mix[None, :], wn, wt, qg, kg)
    return outs


def _even_in_sample_kernel(x_ref, g_ref, w_ref, qg_ref, kg_ref,
                           xa_ref, ga_ref, q_ref, kv_ref, win_ref, gate_ref, *, lw, qw, kvw, hd):
    hn = _rms_rows(x_ref[...], g_ref[...]).astype(BF16)
    z = _dot(hn, w_ref[...])
    xa_ref[...] = z[:, :lw]
    ga_ref[...] = z[:, lw:2 * lw]
    q_ref[...] = _group_rmsnorm(z[:, 2 * lw:2 * lw + qw], qg_ref[...], hd)
    o = 2 * lw + qw
    kv = [z[:, o + i * kvw: o + (i + 1) * kvw] for i in range(6)]
    kv_ref[:, 0 * kvw:1 * kvw] = kv[0]
    kv_ref[:, 1 * kvw:2 * kvw] = kv[1]
    kv_ref[:, 2 * kvw:3 * kvw] = _group_rmsnorm(kv[2], kg_ref[1:2, :], hd)
    kv_ref[:, 3 * kvw:4 * kvw] = kv[3]
    win_ref[:, :kvw] = _group_rmsnorm(kv[4], kg_ref[2:3, :], hd)
    win_ref[:, kvw:] = kv[5]
    gate_ref[...] = jax.nn.sigmoid(z[:, o + 6 * kvw:])


def _even_in_sample(x, g_mix, w_in, q_gain, k_gain, dims):
    r, d = x.shape
    lw, qw, kvw, hd, heads = dims
    n_in = w_in.shape[1]
    pad = (-n_in) % LANES
    w = jnp.pad(w_in, ((0, 0), (0, pad))).astype(BF16)
    gw = n_in + pad - (2 * lw + qw + 6 * kvw)
    qg = (jnp.tile(q_gain, heads) * hd ** -0.5)[None, :].astype(F32)
    kg = jnp.tile(k_gain, (1, kvw // hd)).astype(F32)
    kg = jnp.concatenate([kg, jnp.zeros((5, kvw), F32)], axis=0)
    full = lambda s: pl.BlockSpec(s, lambda i: (0,) * len(s))
    return pl.pallas_call(
        functools.partial(_even_in_sample_kernel, lw=lw, qw=qw, kvw=kvw, hd=hd),
        grid=(1,),
        in_specs=[full((r, d)), full((1, d)), full(w.shape), full(qg.shape), full(kg.shape)],
        out_specs=[full((r, lw)), full((r, lw)), full((r, qw)), full((r, 4 * kvw)),
                   full((r, 2 * kvw)), full((r, gw))],
        out_shape=[jax.ShapeDtypeStruct((r, lw), F32), jax.ShapeDtypeStruct((r, lw), F32),
                   jax.ShapeDtypeStruct((r, qw), F32), jax.ShapeDtypeStruct((r, 4 * kvw), F32),
                   jax.ShapeDtypeStruct((r, 2 * kvw), F32), jax.ShapeDtypeStruct((r, gw), F32)],
        compiler_params=_cparams(), name="even_in_sample",
    )(x, g_mix[None, :], w, qg, kg)


def _rglru_gates(conv, wa_ref, ba_ref, wx_ref, bx_ref, lam_ref):
    cb = conv.astype(BF16)
    r = jax.nn.sigmoid(_dot(cb, wa_ref[...]) + ba_ref[...])
    i = jax.nn.sigmoid(_dot(cb, wx_ref[...]) + bx_ref[...])
    log_a = -RG_C * r * _softplus(-lam_ref[...])
    a = jnp.exp(log_a)
    th = jnp.tanh(log_a)
    b = jnp.sqrt(-2.0 * th / (1.0 - th)) * (i * conv)
    return a, b


def _rglru_prompt_kernel(xa_ref, ga_ref, cw_ref, cb_ref, wa_ref, ba_ref, wx_ref, bx_ref, lam_ref,
                         ya_ref, hl_ref, xext_ref, h_ref):
    tr = xa_ref.shape[0]

    @pl.when(pl.program_id(0) == 0)
    def _():
        xext_ref[0:8, :] = jnp.zeros((8, xext_ref.shape[1]), F32)
        h_ref[...] = jnp.zeros(h_ref.shape, F32)

    x = xa_ref[...]
    xext_ref[8:8 + tr, :] = x
    conv = (cb_ref[...] + cw_ref[3:4, :] * x + cw_ref[2:3, :] * xext_ref[7:7 + tr, :]
            + cw_ref[1:2, :] * xext_ref[6:6 + tr, :] + cw_ref[0:1, :] * xext_ref[5:5 + tr, :])
    xext_ref[0:8, :] = xext_ref[tr:tr + 8, :]
    a, b = _rglru_gates(conv, wa_ref, ba_ref, wx_ref, bx_ref, lam_ref)
    row = lax.broadcasted_iota(jnp.int32, a.shape, 0)
    s = 1
    while s < tr:
        keep = row >= s
        a_sh = jnp.where(keep, pltpu.roll(a, s, 0), 1.0)
        b_sh = jnp.where(keep, pltpu.roll(b, s, 0), 0.0)
        b = a * b_sh + b
        a = a * a_sh
        s *= 2
    hs = a * h_ref[0:1, :] + b
    h_last = hs[tr - 1:tr, :]
    h_ref[...] = jnp.broadcast_to(h_last, h_ref.shape)
    hl_ref[...] = h_last
    ya_ref[...] = (hs * _gelu(ga_ref[...])).astype(BF16)


def _block_diag(w):
    n, a, b = w.shape
    eye = jnp.eye(n, dtype=w.dtype)
    return (eye[:, None, :, None] * w[:, :, None, :]).reshape(n * a, n * b)


def _rglru_weights(conv_w, conv_b, w_a, b_a, w_x, b_x, lam):
    return (conv_w.astype(F32), conv_b[None, :], _block_diag(w_a).astype(BF16), b_a[None, :],
            _block_diag(w_x).astype(BF16), b_x[None, :], lam[None, :])


def _rglru_prompt(xa, ga, weights, tr=256):
    t, lw = xa.shape
    tr = min(tr, t)
    row = pl.BlockSpec((tr, lw), lambda i: (i, 0))
    return pl.pallas_call(
        _rglru_prompt_kernel, grid=(t // tr,),
        in_specs=[row, row] + [_const_spec(w.shape) for w in weights],
        out_specs=[row, pl.BlockSpec((1, lw), lambda i: (0, 0))],
        out_shape=[jax.ShapeDtypeStruct((t, lw), BF16), jax.ShapeDtypeStruct((1, lw), F32)],
        scratch_shapes=[pltpu.VMEM((tr + 8, lw), F32), pltpu.VMEM((8, lw), F32)],
        compiler_params=_cparams(), name="rglru_prompt",
    )(xa, ga, *weights)


def _rglru_sample_kernel(xa_ref, ga_ref, sc_ref, h0_ref, cw_ref, cb_ref, wa_ref, ba_ref, wx_ref,
                         bx_ref, lam_ref, ya_ref, h_ref):
    conv = (cb_ref[...] + cw_ref[3:4, :] * xa_ref[...] + cw_ref[2:3, :] * sc_ref[2]
            + cw_ref[1:2, :] * sc_ref[1] + cw_ref[0:1, :] * sc_ref[0])
    a, b = _rglru_gates(conv, wa_ref, ba_ref, wx_ref, bx_ref, lam_ref)
    h = a * h0_ref[...] + b
    h_ref[...] = h
    ya_ref[...] = (h * _gelu(ga_ref[...])).astype(BF16)


def _rglru_sample(xa, ga, state_conv, state_h, weights):
    r, lw = xa.shape
    sc = jnp.transpose(state_conv, (1, 0, 2))
    args = (xa, ga, sc, state_h) + tuple(weights)
    full = lambda s: pl.BlockSpec(s, lambda i: (0,) * len(s))
    return pl.pallas_call(
        _rglru_sample_kernel, grid=(1,),
        in_specs=[full(a.shape) for a in args],
        out_specs=[full((r, lw)), full((r, lw))],
        out_shape=[jax.ShapeDtypeStruct((r, lw), BF16), jax.ShapeDtypeStruct((r, lw), F32)],
        compiler_params=_cparams(), name="rglru_sample",
    )(*args)


def _compress_prompt_kernel(xk_ref, xv_ref, pek_ref, pev_ref, wk_ref, wvT_ref, kg_ref,
                            kc_ref, vcT_ref, *, hd):
    xk = (xk_ref[...] + pek_ref[...]).astype(BF16)
    kc = _dot(xk, wk_ref[...])
    kc_ref[...] = _group_rmsnorm(kc, kg_ref[...], hd).astype(BF16)
    xv = (xv_ref[...] + pev_ref[...]).astype(BF16)
    vcT_ref[...] = _dot_nt(wvT_ref[...], xv).astype(BF16)


def _compress_weight(w, groups, hd):
    w3 = w.reshape(CMP_BLOCK, hd, hd)
    eye = jnp.eye(groups, dtype=w.dtype)
    full = eye[None, :, None, :, None] * w3[:, None, :, None, :]
    return full.reshape(CMP_BLOCK * groups * hd, groups * hd)


def _compress_prompt(kcmp, vcmp, w_k, w_v, pe_k, pe_v, k_gain_cmp, groups, hd):
    t, kvw = kcmp.shape
    ncb = t // CMP_BLOCK
    kdim = CMP_BLOCK * kvw
    xk = kcmp.reshape(ncb, kdim)
    xv = vcmp.reshape(ncb, kdim)
    wk = _compress_weight(w_k, groups, hd).astype(BF16)
    wvT = _compress_weight(w_v, groups, hd).T.astype(BF16)
    pek = jnp.tile(pe_k, (1, groups)).reshape(1, kdim)
    pev = jnp.tile(pe_v, (1, groups)).reshape(1, kdim)
    kg = jnp.tile(k_gain_cmp, groups)[None, :]
    nb = min(LANES, ncb)
    return pl.pallas_call(
        functools.partial(_compress_prompt_kernel, hd=hd), grid=(ncb // nb,),
        in_specs=[pl.BlockSpec((nb, kdim), lambda i: (i, 0)), pl.BlockSpec((nb, kdim), lambda i: (i, 0)),
                  _const_spec((1, kdim)), _const_spec((1, kdim)), _const_spec(wk.shape),
                  _const_spec(wvT.shape), _const_spec((1, kvw))],
        out_specs=[pl.BlockSpec((nb, kvw), lambda i: (i, 0)), pl.BlockSpec((kvw, nb), lambda i: (0, i))],
        out_shape=[jax.ShapeDtypeStruct((ncb, kvw), BF16), jax.ShapeDtypeStruct((kvw, ncb), BF16)],
        compiler_params=_cparams(), name="compress_prompt",
    )(xk, xv, pek, pev, wk, wvT, kg)


def _tile_lanes(x, n):
    return jnp.concatenate([x] * n, axis=1)


def _attend(s, shift, vT, slot, g, m_ref, acc_ref):
    m_old = m_ref[slot, g]
    m_blk = jnp.max(s, axis=0, keepdims=True)
    if shift is not None:
        m_blk = m_blk + shift
    m_new = jnp.maximum(m_old, m_blk)
    p = jnp.exp2(s - (m_new if shift is None else m_new - shift))
    acc_ref[slot, g] = jnp.exp2(m_old - m_new) * acc_ref[slot, g] + _dot(vT, p.astype(BF16))
    m_ref[slot, g] = m_new


def _bias_table_kernel(id_ref, val_ref, o_ref, *, reps):
    ids = _tile_lanes(id_ref[...], reps)
    out = jnp.zeros(ids.shape, F32)
    for k in range(val_ref.shape[0]):
        out = jnp.where(ids == k, val_ref[k:k + 1, :], out)
    o_ref[...] = out


def _bias_table(ids, vals):
    r = ids.shape[0]
    reps = vals.shape[1] // LANES
    full = lambda s: pl.BlockSpec(s, lambda i: (0,) * len(s))
    return pl.pallas_call(
        functools.partial(_bias_table_kernel, reps=reps), grid=(1,),
        in_specs=[full(ids.shape), full(vals.shape)], out_specs=full((r, vals.shape[1])),
        out_shape=jax.ShapeDtypeStruct((r, vals.shape[1]), F32),
        compiler_params=_cparams(), name="bias_table",
    )(jnp.asarray(ids, jnp.int32), vals)


def _nsa_prompt_kernel(qT_ref, gT_ref, kc_ref, vcT_ref, ks_ref, vsT_ref, kw_ref, vwT_ref,
                       b31_ref, tab_ref,
                       y_ref,
                       qaug_ref, sa_ref, sb_ref, sc_ref, p4_ref, selb_ref, ocmp_ref, m_ref, acc_ref,
                       *, groups, hpg, hd):
    qb = pl.program_id(0)
    t0 = qb * Q_BLOCK
    ncb = kc_ref.shape[0]
    nsb = ncb * CMP_BLOCK // SEL_BLOCK
    gl = hpg * Q_BLOCK
    n_top = min(TOP_N, nsb)
    kvw = groups * hd
    bpg = KEY_GROUP // SEL_BLOCK

    rows = []
    for g in range(groups):
        top = jnp.concatenate([qT_ref[(g * hpg + hh) * hd:(g * hpg + hh + 1) * hd, :]
                               for hh in range(hpg)], axis=1)
        z = jnp.zeros_like(top)
        rows.append(jnp.concatenate([top if gg == g else z for gg in range(groups)], axis=1))
    qbd = jnp.concatenate(rows, axis=0)
    qaug_ref[0:kvw, :] = qbd
    qaug_ref[kvw:2 * kvw, :] = jnp.zeros((kvw, groups * gl), BF16)
    b31 = b31_ref[...]

    def compress_and_select(rc):
        nsv = rc * CMP_BLOCK // SEL_BLOCK
        jrow = lax.broadcasted_iota(jnp.int32, (rc, Q_BLOCK), 0)
        qcol = lax.broadcasted_iota(jnp.int32, (rc, Q_BLOCK), 1)
        vis = _tile_lanes(jrow * CMP_BLOCK + (CMP_BLOCK - 1) <= t0 + qcol, groups * hpg)
        s_c = _dot(kc_ref[0:rc, :], qbd) + b31
        sc_ref[0:rc, :] = jnp.where(vis, s_c, NEG)
        odd = (qb % 2) == 1
        v_a = jnp.where(qb == 0, 0, jnp.where(odd, 1, 2))
        w_a = pl.multiple_of(jnp.where(qb == 0, 0, jnp.where(odd, 4 * qb - 4, 4 * qb - 8)), 8)
        v_b = jnp.where((qb > 0) & jnp.logical_not(odd), 0, 3)
        w_b = pl.multiple_of(jnp.where((qb > 0) & jnp.logical_not(odd), 4 * qb, 0), 8)
        sc_ref[pl.ds(w_a, 8), :] = sc_ref[pl.ds(w_a, 8), :] + tab_ref[pl.ds(pl.multiple_of(v_a * 8, 8), 8), :]
        sc_ref[pl.ds(w_b, 8), :] = sc_ref[pl.ds(w_b, 8), :] + tab_ref[pl.ds(pl.multiple_of(v_b * 8, 8), 8), :]
        s_c = sc_ref[0:rc, :]
        m_c = jnp.max(s_c, axis=0, keepdims=True)
        p_c = jnp.where(vis, jnp.exp2(s_c - m_c), 0.0)
        l_c = jnp.sum(p_c, axis=0, keepdims=True)
        p_c = p_c * jnp.where(l_c > 0.0, 1.0 / l_c, 0.0)
        for g in range(groups):
            pg = p_c[:, g * gl:(g + 1) * gl]
            ocmp_ref[g] = _dot(vcT_ref[g * hd:(g + 1) * hd, 0:rc], pg.astype(BF16))
            p4 = pg[:, 0:Q_BLOCK]
            for hh in range(1, hpg):
                p4 = p4 + pg[:, hh * Q_BLOCK:(hh + 1) * Q_BLOCK]
            p4_ref[g, 0:rc, :] = p4

        ratio = SEL_BLOCK // CMP_BLOCK
        jb = lax.broadcasted_iota(jnp.int32, (nsv, Q_BLOCK), 0)
        qc = lax.broadcasted_iota(jnp.int32, (nsv, Q_BLOCK), 1)
        cur = (t0 + qc) // SEL_BLOCK
        forced = (jb == 0) | (jb == cur) | (jb == cur - 1)
        for g in range(groups):
            imp = p4_ref[g, pl.ds(0, nsv, stride=ratio), :]
            for k in range(1, ratio):
                imp = imp + p4_ref[g, pl.ds(k, nsv, stride=ratio), :]
            imp = jnp.where(forced, FORCE_SCORE, imp)
            imp = jnp.where(jb <= cur, imp, -1.0)
            selb = jnp.full((nsv, Q_BLOCK), NEG, F32)
            for _ in range(n_top):
                mx = jnp.max(imp, axis=0, keepdims=True)
                idx = jnp.min(jnp.where(imp == mx, jb, nsv), axis=0, keepdims=True)
                hit = jb == idx
                selb = jnp.where(hit, 0.0, selb)
                imp = jnp.where(hit, -3.0e38, imp)
            selb_ref[g, 0:nsv, :] = selb
            if nsv < nsb:
                selb_ref[g, nsv:nsb, :] = jnp.full((nsb - nsv, Q_BLOCK), NEG, F32)

    tiers = list(range(LANES, ncb + 1, LANES)) if ncb % LANES == 0 else [ncb]
    n_vis = (t0 + Q_BLOCK) // CMP_BLOCK
    for i, rc in enumerate(tiers):
        lo = tiers[i - 1] if i else 0

        @pl.when((n_vis > lo) & (n_vis <= rc))
        def _(rc=rc):
            compress_and_select(rc)

    m_ref[...] = jnp.full(m_ref.shape, NEG, F32)
    acc_ref[...] = jnp.zeros(acc_ref.shape, F32)

    def put_mask_rows(g, rows8):
        tile = jnp.concatenate([rows8, jnp.zeros_like(rows8)], axis=0).astype(BF16)
        for hh in range(hpg):
            qaug_ref[kvw:kvw + 2 * bpg, g * gl + hh * Q_BLOCK:g * gl + (hh + 1) * Q_BLOCK] = tile

    def far_scores(c, dst_ref):
        for g in range(groups):
            put_mask_rows(g, selb_ref[g, pl.ds(pl.multiple_of(c * bpg, bpg), bpg), :])
        k0 = pl.multiple_of(c * KEY_GROUP, KEY_GROUP)
        dst_ref[...] = _dot(ks_ref[pl.ds(k0, KEY_GROUP), :], qaug_ref[...])

    def far_attend(c, src_ref, live_rows=None):
        k0 = pl.multiple_of(c * KEY_GROUP, KEY_GROUP)
        drop = None
        if live_rows is not None:
            drop = lax.broadcasted_iota(jnp.int32, (KEY_GROUP, Q_BLOCK), 0) >= live_rows
        for g in range(groups):
            sg = src_ref[:, g * gl:(g + 1) * gl]
            if drop is not None:
                sg = jnp.where(_tile_lanes(drop, hpg), NEG, sg)
            _attend(sg, b31[:, g * gl:(g + 1) * gl],
                    vsT_ref[g * ACC_ROWS:(g + 1) * ACC_ROWS, pl.ds(k0, KEY_GROUP)], 0, g, m_ref, acc_ref)

    far_len = jnp.maximum(t0 - Q_BLOCK, 0)
    n_far = far_len // KEY_GROUP
    live_tail = far_len - n_far * KEY_GROUP
    far_scores(0, sa_ref)

    def far_pair(j, carry):
        far_scores(2 * j + 1, sb_ref)
        far_attend(2 * j, sa_ref)
        far_scores(2 * j + 2, sa_ref)
        far_attend(2 * j + 1, sb_ref)
        return carry

    lax.fori_loop(0, n_far // 2, far_pair, 0)
    odd_far = n_far % 2 == 1

    @pl.when(odd_far)
    def _():
        far_scores(n_far, sb_ref)
        far_attend(n_far - 1, sa_ref)

    @pl.when(odd_far & (live_tail > 0))
    def _():
        far_attend(n_far, sb_ref, live_tail)

    @pl.when(jnp.logical_not(odd_far) & (live_tail > 0))
    def _():
        far_attend(n_far, sa_ref, live_tail)

    near_tab = tab_ref[_TAB_NEAR:_TAB_NEAR + 2 * Q_BLOCK, :]

    def near_piece():
        b0 = (t0 - Q_BLOCK) // SEL_BLOCK
        base = (b0 // bpg) * bpg
        nxt = jnp.minimum(base + bpg, nsb - bpg)
        jrow8 = lax.broadcasted_iota(jnp.int32, (bpg, Q_BLOCK), 0)
        for g in range(groups):
            lo = selb_ref[g, pl.ds(pl.multiple_of(base, bpg), bpg), :]
            hi = selb_ref[g, pl.ds(pl.multiple_of(nxt, bpg), bpg), :]
            put_mask_rows(g, jnp.where(jrow8 >= b0 - base, lo, hi))
        k0 = pl.multiple_of(t0 - Q_BLOCK, Q_BLOCK)
        s = _dot(ks_ref[pl.ds(k0, 2 * Q_BLOCK), :], qaug_ref[...]) + near_tab
        for g in range(groups):
            _attend(s[:, g * gl:(g + 1) * gl], None,
                    vsT_ref[g * ACC_ROWS:(g + 1) * ACC_ROWS, pl.ds(k0, 2 * Q_BLOCK)], 0, g, m_ref, acc_ref)

    @pl.when(qb == 0)
    def _():
        for g in range(groups):
            put_mask_rows(g, selb_ref[g, 0:bpg, :])
        s = _dot(ks_ref[0:Q_BLOCK, :], qaug_ref[...]) + near_tab[Q_BLOCK:2 * Q_BLOCK, :]
        for g in range(groups):
            _attend(s[:, g * gl:(g + 1) * gl], None,
                    vsT_ref[g * ACC_ROWS:(g + 1) * ACC_ROWS, 0:Q_BLOCK], 0, g, m_ref, acc_ref)

    n_wk = WINDOW + Q_BLOCK

    def window_block():
        k0 = pl.multiple_of(t0 - WINDOW, Q_BLOCK)
        s = _dot(kw_ref[pl.ds(k0, n_wk), :], qaug_ref[0:kvw, :]) + tab_ref[_TAB_WIN:_TAB_WIN + n_wk, :]
        for g in range(groups):
            _attend(s[:, g * gl:(g + 1) * gl], None,
                    vwT_ref[g * ACC_ROWS:(g + 1) * ACC_ROWS, pl.ds(k0, n_wk)], 1, g, m_ref, acc_ref)

    @pl.when(t0 >= WINDOW)
    def _():
        near_piece()
        window_block()

    @pl.when((qb >= 1) & (t0 < WINDOW))
    def _():
        near_piece()

    for i in range(WINDOW // Q_BLOCK):
        k0 = t0 - i * Q_BLOCK

        @pl.when((t0 < WINDOW) & (k0 >= 0))
        def _(i=i, k0=k0):
            k0a = pl.multiple_of(k0, Q_BLOCK)
            r0 = _TAB_WIN + WINDOW - i * Q_BLOCK
            s = _dot(kw_ref[pl.ds(k0a, Q_BLOCK), :], qaug_ref[0:kvw, :]) + tab_ref[r0:r0 + Q_BLOCK, :]
            for g in range(groups):
                _attend(s[:, g * gl:(g + 1) * gl], None,
                        vwT_ref[g * ACC_ROWS:(g + 1) * ACC_ROWS, pl.ds(k0a, Q_BLOCK)], 1, g, m_ref, acc_ref)

    heads = groups * hpg
    pieces = []
    for g in range(groups):
        gate = [jnp.concatenate([gT_ref[j * heads + g * hpg + hh: j * heads + g * hpg + hh + 1, :]
                                 for hh in range(hpg)], axis=1) for j in range(3)]
        o = gate[0] * ocmp_ref[g]
        for slot in range(2):
            acc = acc_ref[slot, g]
            o = o + gate[1 + slot] * (acc[0:hd, :] * (1.0 / acc[hd:hd + 1, :]))
        for hh in range(hpg):
            pieces.append(o[:, hh * Q_BLOCK:(hh + 1) * Q_BLOCK])
    y_ref[...] = jnp.concatenate(pieces, axis=0).T.astype(BF16)


_TAB_NEAR = 32
_TAB_WIN = _TAB_NEAR + 2 * Q_BLOCK


def _nsa_prompt_tables(rel_bias, n_lanes):
    nb, heads = rel_bias.shape
    tbl = rel_bias.astype(F32) * LOG2E
    lanes = lambda x: jnp.repeat(x, n_lanes // heads, axis=1)
    vals = jnp.concatenate([lanes(tbl), jnp.full((1, n_lanes), NEG, F32), jnp.zeros((1, n_lanes), F32),
                            lanes(tbl - tbl[nb - 1:nb])], axis=0)
    c = np.arange(Q_BLOCK)[None, :]
    i8 = np.arange(8)[:, None]
    ids = []
    for base in (-(CMP_BLOCK - 1), 3 * CMP_BLOCK + 1, 7 * CMP_BLOCK + 1):
        d = base + c - CMP_BLOCK * i8
        ids.append(np.where(d >= 0, nb + 2 + _bucket_np(d, nb), nb + 1))
    ids.append(np.full((8, Q_BLOCK), nb + 1))
    d = Q_BLOCK + c - np.arange(2 * Q_BLOCK)[:, None]
    ids.append(np.where(d >= 0, _bucket_np(d, nb), nb))
    d = WINDOW + c - np.arange(WINDOW + Q_BLOCK)[:, None]
    ids.append(np.where((d >= 0) & (d < WINDOW), _bucket_np(d, nb), nb))
    return vals[nb - 1:nb], _bias_table(np.concatenate(ids, axis=0), vals)


def _nsa_prompt(qT, gT, kc, vcT, ks, vsT, kw, vwT, rel_bias, groups, hpg, hd):
    qw, t = qT.shape
    ncb = kc.shape[0]
    nsb = t // SEL_BLOCK
    gl = hpg * Q_BLOCK
    kvw = groups * hd
    assert t % KEY_GROUP == 0 and Q_BLOCK == 4 * CMP_BLOCK and MAX_DIST <= Q_BLOCK
    tables = _nsa_prompt_tables(rel_bias, groups * gl)
    resident = (kc, vcT, ks, vsT, kw, vwT) + tuple(tables)
    return pl.pallas_call(
        functools.partial(_nsa_prompt_kernel, groups=groups, hpg=hpg, hd=hd),
        grid=(t // Q_BLOCK,),
        in_specs=[pl.BlockSpec((qw, Q_BLOCK), lambda i: (0, i)),
                  pl.BlockSpec((gT.shape[0], Q_BLOCK), lambda i: (0, i))]
                 + [_const_spec(a.shape) for a in resident],
        out_specs=pl.BlockSpec((Q_BLOCK, qw), lambda i: (i, 0)),
        out_shape=jax.ShapeDtypeStruct((t, qw), BF16),
        scratch_shapes=[pltpu.VMEM((2 * kvw, groups * gl), BF16),
                        pltpu.VMEM((KEY_GROUP, groups * gl), F32),
                        pltpu.VMEM((KEY_GROUP, groups * gl), F32),
                        pltpu.VMEM((ncb, groups * gl), F32),
                        pltpu.VMEM((groups, ncb, Q_BLOCK), F32),
                        pltpu.VMEM((groups, nsb, Q_BLOCK), F32),
                        pltpu.VMEM((groups, hd, gl), F32),
                        pltpu.VMEM((2, groups, 1, gl), F32),
                        pltpu.VMEM((2, groups, ACC_ROWS, gl), F32)],
        compiler_params=_cparams(), name="nsa_prompt",
    )(qT, gT, *resident)


def _split3(x):
    hi = x.astype(BF16)
    r1 = x - hi.astype(F32)
    mid = r1.astype(BF16)
    lo = (r1 - mid.astype(F32)).astype(BF16)
    return hi, mid, lo


def _nsa_sample_kernel(pt_ref, cache_ref, qbd_ref, qn_ref, gates_ref, kvn_ref, winn_ref, cwin_ref,
                       wkv_ref, pe_ref, perm_ref, kg_ref, gsum_ref,
                       cb_ref, sb_ref, wb_ref, b0_ref, e_ref,
                       o_ref,
                       ring_ref, bufkv_ref, bufb_ref, sema, semb, p_ref, s_ref,
                       *, hd, n_pages, page, chunk):
    b = pl.program_id(0)
    nb = pl.num_programs(0)
    kvw = kg_ref.shape[1]
    past = n_pages * page
    ncb = past // CMP_BLOCK
    nsb = past // SEL_BLOCK
    n_chunks = n_pages // chunk
    bpp = 2 * page // CMP_BLOCK

    def raw_copy(bb, ch, p):
        return pltpu.make_async_copy(cache_ref.at[pt_ref[bb, ch * chunk + p], 0:2 * kvw, :],
                                     ring_ref.at[ch % 2, p], sema.at[ch % 2])

    def sel_copy(bb, p):
        return pltpu.make_async_copy(cache_ref.at[pt_ref[bb, p], 2 * kvw:4 * kvw, :],
                                     bufb_ref.at[:, pl.ds(p * page, page)], semb)

    def for_pages(n, fn, unroll=1):
        def body(p, c):
            fn(p)
            return c
        lax.fori_loop(0, n, body, 0, unroll=unroll)

    @pl.when(b == 0)
    def _():
        for_pages(chunk, lambda p: raw_copy(0, 0, p).start())
        for_pages(n_pages, lambda p: sel_copy(0, p).start())

    qbd = qbd_ref[0]

    for ch in range(n_chunks):
        if ch + 1 < n_chunks:
            for_pages(chunk, lambda p, ch=ch: raw_copy(b, ch + 1, p).start())
        else:
            @pl.when(b + 1 < nb)
            def _():
                for_pages(chunk, lambda p: raw_copy(b + 1, 0, p).start())
        for_pages(chunk, lambda p, ch=ch: raw_copy(b, ch, p).wait())

        def regroup(q, ch=ch):
            xt = jnp.concatenate([ring_ref[ch % 2, 2 * q], ring_ref[ch % 2, 2 * q + 1]], axis=1)
            y = _dot_nt(perm_ref[...], (xt + pe_ref[...]).astype(BF16))
            blocks = pl.ds(pl.multiple_of((ch * (chunk // 2) + q) * bpp, bpp), bpp)
            for r in range(CMP_BLOCK):
                bufkv_ref[r, blocks, :] = y[r * bpp:(r + 1) * bpp, :]
        for_pages(chunk // 2, regroup, unroll=4)

    zs = [jnp.zeros((ncb, 2 * kvw), F32), jnp.zeros((ncb, 2 * kvw), F32)]
    for r in range(CMP_BLOCK):
        zs[r % 2] = zs[r % 2] + _dot(bufkv_ref[r].astype(BF16), wkv_ref[r])
    z = zs[0] + zs[1]
    vc = z[:, kvw:2 * kvw]
    kc = _group_rmsnorm(z[:, 0:kvw], kg_ref[...], hd).astype(BF16)
    s_c = _dot(kc, qbd) + cb_ref[...]
    m_c = jnp.max(s_c, axis=0, keepdims=True)
    p_c = jnp.exp(s_c - m_c)
    p_c = p_c * (1.0 / jnp.sum(p_c, axis=0, keepdims=True))
    o_cmp = _dot(p_c.T.astype(BF16), vc.astype(BF16))
    p_ref[...] = p_c
    ratio = SEL_BLOCK // CMP_BLOCK
    imp = p_ref[pl.ds(0, nsb, stride=ratio), :]
    for k in range(1, ratio):
        imp = imp + p_ref[pl.ds(k, nsb, stride=ratio), :]
    hi, mid, lo = _split3(imp)
    gs = gsum_ref[...]
    imp = _dot(hi, gs) + _dot(mid, gs) + _dot(lo, gs)
    jb = lax.broadcasted_iota(jnp.int32, (nsb, LANES), 0)
    imp = jnp.where((jb == 0) | (jb == nsb - 1), FORCE_SCORE, imp)
    selb = jnp.full((nsb, LANES), NEG, F32)
    for _ in range(min(TOP_N, nsb + 1) - 1):
        mx = jnp.max(imp, axis=0, keepdims=True)
        idx = jnp.min(jnp.where(imp == mx, jb, nsb), axis=0, keepdims=True)
        hit = jb == idx
        selb = jnp.where(hit, 0.0, selb)
        imp = jnp.where(hit, -3.0e38, imp)
    rows = qn_ref.shape[1]
    sel_nat = selb.T[0:rows, :]

    for_pages(n_pages, lambda p: sel_copy(b, p).wait())
    qn = qn_ref[0]
    qnb = qn.astype(BF16)
    kvn = kvn_ref[0]
    b0 = b0_ref[:, 0:1]
    s_new = jnp.sum(qn * kvn[:, 2 * kvw:3 * kvw], axis=1, keepdims=True) + b0
    nks = e_ref.shape[1]
    bps = nks // SEL_BLOCK
    for c in range(past // nks):
        rolled = sel_nat if c == 0 else pltpu.roll(sel_nat, nsb - c * bps, 1)
        mask = _dot(rolled.astype(BF16), e_ref[...])
        k_t = bufb_ref[0:kvw, c * nks:(c + 1) * nks].astype(BF16)
        s_ref[:, c * nks:(c + 1) * nks] = _dot(qnb, k_t) + mask + sb_ref[:, c * nks:(c + 1) * nks]
    s_all = s_ref[...]
    m_s = jnp.maximum(jnp.max(s_all, axis=1, keepdims=True), s_new)
    p_new = jnp.exp(s_new - m_s)
    l_s = p_new
    acc = p_new * kvn[:, 3 * kvw:4 * kvw]
    for c in range(past // nks):
        p = jnp.exp(s_all[:, c * nks:(c + 1) * nks] - m_s)
        l_s = l_s + jnp.sum(p, axis=1, keepdims=True)
        acc = acc + _dot_nt(p.astype(BF16), bufb_ref[kvw:2 * kvw, c * nks:(c + 1) * nks].astype(BF16))

    @pl.when(b + 1 < nb)
    def _():
        for_pages(n_pages, lambda p: sel_copy(b + 1, p).start())

    o_sel = acc * (1.0 / l_s)

    winn = winn_ref[0]
    s_w = _dot(qnb, cwin_ref[0, 0:kvw, :].astype(BF16)) + wb_ref[...]
    s_wn = jnp.sum(qn * winn[:, 0:kvw], axis=1, keepdims=True) + b0
    m_w = jnp.maximum(jnp.max(s_w, axis=1, keepdims=True), s_wn)
    p_w = jnp.exp(s_w - m_w)
    p_wn = jnp.exp(s_wn - m_w)
    l_w = jnp.sum(p_w, axis=1, keepdims=True) + p_wn
    o_win = (_dot_nt(p_w.astype(BF16), cwin_ref[0, kvw:2 * kvw, :].astype(BF16))
             + p_wn * winn[:, kvw:2 * kvw]) * (1.0 / l_w)

    g = gates_ref[0]
    out = g[0] * o_cmp[0:rows, :] + g[1] * o_sel + g[2] * o_win
    o_ref[0] = out[0:8, :]


def _nsa_sample(q, gates, kv_new, win_new, cache_kv, cache_win, page_table, lp, rel_bias,
                groups, hpg, hd):
    r = q.shape[0]
    heads = groups * hpg
    kvw = groups * hd
    n_phys, page = cache_kv.shape[0], cache_kv.shape[1]
    n_pages = page_table.shape[1]
    past = n_pages * page
    wbuf = cache_win.shape[1]
    ncb, nsb = past // CMP_BLOCK, past // SEL_BLOCK
    rows = 16
    nks = min(NKS_SAMPLE, past)
    chunk = min(32, n_pages // 2)
    assert n_pages % (2 * chunk) == 0 and chunk % 8 == 0 and past % nks == 0 and heads <= rows
    q3 = q.reshape(r, heads, hd)
    gmask = (jnp.arange(groups)[:, None] == (jnp.arange(heads) // hpg)[None, :]).astype(F32)
    qbd = jnp.einsum('rhd,gh->rgdh', q3, gmask).reshape(r, kvw, heads)
    qbd = jnp.pad(qbd, ((0, 0), (0, 0), (0, LANES - heads))).astype(BF16)
    qn = jnp.einsum('rhd,gh->rhgd', q3, gmask).reshape(r, heads, kvw)
    qn = jnp.pad(qn, ((0, 0), (0, rows - heads), (0, 0)))
    gts = gates[:, :3 * heads].reshape(r, heads, 3).transpose(0, 2, 1)
    gts = jnp.broadcast_to(gts[:, :, :, None], (r, 3, heads, kvw))
    gts = jnp.pad(gts, ((0, 0), (0, 0), (0, rows - heads), (0, 0)))
    wk = _compress_weight(lp['cmp_w_k'], groups, hd).reshape(CMP_BLOCK, kvw, kvw)
    wv = _compress_weight(lp['cmp_w_v'], groups, hd).reshape(CMP_BLOCK, kvw, kvw)
    zero = jnp.zeros_like(wk)
    wkv = jnp.concatenate([jnp.concatenate([wk, zero], axis=2),
                           jnp.concatenate([zero, wv], axis=2)], axis=1).astype(BF16)
    pe_t = jnp.concatenate([jnp.tile(lp['pe_k'].T, (groups, 1)), jnp.tile(lp['pe_v'].T, (groups, 1))], axis=0)
    pe_t = jnp.tile(pe_t, (1, 2 * page // CMP_BLOCK))
    bpp = 2 * page // CMP_BLOCK
    dst = np.arange(2 * page)
    src = (dst % bpp) * CMP_BLOCK + dst // bpp
    perm = jnp.asarray(src[:, None] == np.arange(2 * page)[None, :], BF16)
    kg = jnp.tile(lp['k_gain'][0], groups)[None, :]
    lane_g = np.where(np.arange(LANES) < heads, np.arange(LANES) // hpg, -1)
    gsum = jnp.asarray((lane_g[:, None] == lane_g[None, :]) & (lane_g[:, None] >= 0), BF16)
    nbk = rel_bias.shape[0]
    tbl = rel_bias.astype(F32)
    d_c = (past - (np.arange(ncb) * CMP_BLOCK + CMP_BLOCK - 1))[:, None]
    vals_t = jnp.concatenate([tbl, jnp.broadcast_to(tbl[:, :1], (nbk, LANES - heads))], axis=1)
    cb = _bias_table(np.broadcast_to(_bucket_np(d_c, nbk), (ncb, LANES)), vals_t)
    vals_n = jnp.concatenate([jnp.repeat(tbl, LANES, axis=1), jnp.full((1, heads * LANES), NEG, F32)], axis=0)

    def head_rows(ids):
        out = _bias_table(ids, vals_n).reshape(ids.shape[0], heads, LANES)
        out = jnp.transpose(out, (1, 0, 2)).reshape(heads, ids.shape[0] * LANES)
        return jnp.pad(out, ((0, rows - heads), (0, 0)))

    sb = head_rows(_bucket_np(past - np.arange(past), nbk).reshape(past // LANES, LANES))
    d_w = wbuf - np.arange(wbuf)
    wb = head_rows(np.where(d_w < WINDOW, _bucket_np(d_w, nbk), nbk).reshape(wbuf // LANES, LANES))
    b0 = jnp.pad(jnp.broadcast_to(tbl[0][:, None], (heads, LANES)), ((0, rows - heads), (0, 0)))
    blk = np.arange(nsb)[:, None]
    spread = jnp.asarray((blk == np.arange(nks)[None, :] // SEL_BLOCK), BF16)

    cache_t = jnp.transpose(cache_kv, (0, 2, 3, 4, 1)).reshape(n_phys, 4 * kvw, page)
    cwin_t = jnp.transpose(cache_win, (0, 2, 3, 4, 1)).reshape(r, 2 * kvw, wbuf)
    kvn = kv_new.reshape(r, 1, 4 * kvw)
    winn = win_new.reshape(r, 1, 2 * kvw)
    per_b = lambda s: pl.BlockSpec((1,) + s, lambda b, pt: (b,) + (0,) * len(s))
    const = lambda a: pl.BlockSpec(a.shape, lambda b, pt: (0,) * a.ndim, pipeline_mode=pl.Buffered(1))
    consts = (wkv, pe_t, perm, kg, gsum, cb, sb, wb, b0, spread)
    grid_spec = pltpu.PrefetchScalarGridSpec(
        num_scalar_prefetch=1, grid=(r,),
        in_specs=[pl.BlockSpec(memory_space=pl.ANY), per_b((kvw, LANES)), per_b((rows, kvw)),
                  per_b((3, rows, kvw)), per_b((1, 4 * kvw)), per_b((1, 2 * kvw)),
                  per_b((2 * kvw, wbuf))]
                 + [const(a) for a in consts],
        out_specs=per_b((8, kvw)),
        scratch_shapes=[pltpu.VMEM((2, chunk, 2 * kvw, page), F32),
                        pltpu.VMEM((CMP_BLOCK, ncb, 2 * kvw), F32),
                        pltpu.VMEM((2 * kvw, past), F32),
                        pltpu.SemaphoreType.DMA((2,)), pltpu.SemaphoreType.DMA(()),
                        pltpu.VMEM((ncb, LANES), F32), pltpu.VMEM((rows, past), F32)])
    o8 = pl.pallas_call(
        functools.partial(_nsa_sample_kernel, hd=hd, n_pages=n_pages, page=page, chunk=chunk),
        grid_spec=grid_spec,
        out_shape=jax.ShapeDtypeStruct((r, 8, kvw), F32),
        compiler_params=_cparams(), name="nsa_sample",
    )(page_table, cache_t, qbd, qn, gts, kvn, winn, cwin_t, *consts)
    o4 = o8[:, :heads, :].reshape(r, heads, groups, hd)
    y = jnp.take_along_axis(o4, (jnp.arange(heads) // hpg)[None, :, None, None], axis=2)
    return y.reshape(r, heads * hd).astype(BF16)


def _mlp_ple(h, p_ref, gm_ref, wu_ref, wd_ref, gp_ref, wg_ref, wp_ref):
    up = _dot(_rms_rows(h, gm_ref[...]).astype(BF16), wu_ref[...])
    act = jnp.square(jnp.maximum(up, 0.0)).astype(BF16)
    h = h + _dot(act, wd_ref[...])
    gate = jax.nn.sigmoid(_dot(_rms_rows(h, gp_ref[...]).astype(BF16), wg_ref[...]))
    return h + gate * _dot(p_ref[...].astype(BF16), wp_ref[...])


def _even_tail_kernel(h_ref, ya_ref, yb_ref, p_ref, woa_ref, wob_ref,
                      gm_ref, wu_ref, wd_ref, gp_ref, wg_ref, wp_ref, o_ref):
    h = h_ref[...] + _dot(ya_ref[...], woa_ref[...]) + _dot(yb_ref[...], wob_ref[...])
    o_ref[...] = _mlp_ple(h, p_ref, gm_ref, wu_ref, wd_ref, gp_ref, wg_ref, wp_ref)


def _tail_weights(g_mlp, w_up, w_down, g_ple, w_gate, w_proj):
    return (g_mlp[None, :], w_up.astype(BF16), w_down.astype(BF16), g_ple[None, :],
            w_gate.astype(BF16), w_proj.astype(BF16))


def _even_tail(h, ya, yb, p, w_out, tail_w, tm=256):
    t, d = h.shape
    tm = min(tm, t)
    lw = ya.shape[1]
    woa, wob = w_out[:lw].astype(BF16), w_out[lw:].astype(BF16)
    row = lambda w: pl.BlockSpec((tm, w), lambda i: (i, 0))
    weights = (woa, wob) + tuple(tail_w)
    return pl.pallas_call(
        _even_tail_kernel, grid=(t // tm,),
        in_specs=[row(d), row(lw), row(yb.shape[1]), row(p.shape[1])]
                 + [_const_spec(w.shape) for w in weights],
        out_specs=row(d), out_shape=jax.ShapeDtypeStruct((t, d), F32),
        compiler_params=_cparams(), name="even_tail",
    )(h, ya, yb, p, *weights)


def _odd_kernel(h_ref, p_ref, gx_ref, wi_ref, lg_ref, lb_ref, sw_ref, sb_ref, wo_ref,
                gm_ref, wu_ref, wd_ref, gp_ref, wg_ref, wp_ref, o_ref, v_ref, *, sw_groups, single):
    h = h_ref[...]
    tm = h.shape[0]
    z = _gelu(_dot(_rms_rows(h, gx_ref[...]).astype(BF16), wi_ref[...]))
    width = z.shape[1] // 2
    u, v = z[:, :width], z[:, width:]
    mu = jnp.mean(v, axis=-1, keepdims=True)
    vc = v - mu
    vn = vc * lax.rsqrt(jnp.mean(vc * vc, axis=-1, keepdims=True) + EPS) * lg_ref[...] + lb_ref[...]
    v_ref[...] = vn
    gw = width // sw_groups
    if single:
        s = vn * sw_ref[...] + sb_ref[...]
    else:
        vb = vn.astype(BF16)
        parts = []
        for c in range(tm // CHUNK):
            cols = [_dot(sw_ref[g], vb[c * CHUNK:(c + 1) * CHUNK, g * gw:(g + 1) * gw])
                    for g in range(sw_groups)]
            parts.append(jnp.concatenate(cols, axis=1) + sb_ref[...])
        s = jnp.concatenate(parts, axis=0)
    h = h + _dot((u * s).astype(BF16), wo_ref[...])
    o_ref[...] = _mlp_ple(h, p_ref, gm_ref, wu_ref, wd_ref, gp_ref, wg_ref, wp_ref)


def _odd_layer(h, p, g_mix, w_in, ln_g, ln_b, sgu_w, sgu_b, w_out, tail_w, single, tm=256):
    t, d = h.shape
    tm = min(tm, t)
    ng, ch, _ = sgu_w.shape
    width = w_in.shape[1] // 2
    gw = width // ng
    if single:
        sw = jnp.repeat(sgu_w[:, 0, 0], gw)[None, :]
        sb = jnp.repeat(sgu_b[:, 0], gw)[None, :]
    else:
        sw = (sgu_w * jnp.tril(jnp.ones((ch, ch), sgu_w.dtype))).astype(BF16)
        sb = jnp.repeat(sgu_b.T, gw, axis=1)
    weights = (g_mix[None, :], w_in.astype(BF16), ln_g[None, :], ln_b[None, :], sw, sb,
               w_out.astype(BF16)) + tuple(tail_w)
    row = lambda w: pl.BlockSpec((tm, w), lambda i: (i, 0))
    return pl.pallas_call(
        functools.partial(_odd_kernel, sw_groups=ng, single=single), grid=(t // tm,),
        in_specs=[row(d), row(p.shape[1])] + [_const_spec(w.shape) for w in weights],
        out_specs=[row(d), row(width)],
        out_shape=[jax.ShapeDtypeStruct((t, d), F32), jax.ShapeDtypeStruct((t, width), F32)],
        compiler_params=_cparams(), name="odd_single" if single else "odd_prompt",
    )(h, p, *weights)


def kernel(x_prompt, x_sample, cache_kv, cache_win, state_conv, state_h, page_table, p_prompt, p_sample,
           rel_bias, g_mix, g_mlp, w_up, w_down, g_ple, w_ple_gate, w_ple_proj,
           w_in_even, w_out_even, conv_w, conv_b, rg_w_a, rg_b_a, rg_w_x, rg_b_x, rg_lambda,
           q_gain, k_gain, cmp_w_k, cmp_w_v, cmp_pe_k, cmp_pe_v,
           w_in_odd, ln_v_g, ln_v_b, sgu_w, sgu_b, w_out_odd):
    bsz, t, d = x_prompt.shape
    r = x_sample.shape[0]
    assert bsz == 1 and x_sample.shape[1] == 1
    depth = g_mix.shape[0]
    lw = conv_w.shape[-1]
    hd = q_gain.shape[-1]
    groups = cache_kv.shape[4]
    heads = rel_bias.shape[1]
    hpg = heads // groups
    qw, kvw = heads * hd, groups * hd
    dims = (lw, qw, kvw, hd, heads)
    wbuf = cache_win.shape[2]

    hp = x_prompt.reshape(t, d)
    hs = x_sample.reshape(r, d)
    kvp, kvs, wnp, wns, cvp, cvs, hhp, hhs, vvs = [], [], [], [], [], [], [], [], []
    for i in range(depth):
        tail_w = _tail_weights(g_mlp[i], w_up[i], w_down[i], g_ple[i], w_ple_gate[i], w_ple_proj[i])
        if i % 2 == 0:
            e = i // 2
            lp = {'cmp_w_k': cmp_w_k[e], 'cmp_w_v': cmp_w_v[e], 'pe_k': cmp_pe_k[e], 'pe_v': cmp_pe_v[e],
                  'k_gain': k_gain[e]}
            rg_w = _rglru_weights(conv_w[e], conv_b[e], rg_w_a[e], rg_b_a[e], rg_w_x[e], rg_b_x[e],
                                  rg_lambda[e])
            (xa, ga, kv_rows, win_rows, kcmp, vcmp, ks, kw, qT, vsT, vwT, gT) = _even_in_prompt(
                hp, g_mix[i], w_in_even[e], q_gain[e], k_gain[e], dims)
            ya, h_last = _rglru_prompt(xa, ga, rg_w)
            kc, vcT = _compress_prompt(kcmp, vcmp, cmp_w_k[e], cmp_w_v[e], cmp_pe_k[e], cmp_pe_v[e],
                                       k_gain[e, 0], groups, hd)
            yb = _nsa_prompt(qT, gT, kc, vcT, ks, vsT, kw, vwT, rel_bias, groups, hpg, hd)
            hp = _even_tail(hp, ya, yb, p_prompt[i, 0], w_out_even[e], tail_w)
            kvp.append(kv_rows.reshape(1, t, 4, groups, hd))
            wnp.append(win_rows[t - wbuf:].reshape(1, wbuf, 2, groups, hd))
            cvp.append(xa[t - (conv_w.shape[1] - 1):][None])
            hhp.append(h_last)
            xa_s, ga_s, q_s, kv_s, win_s, gate_s = _even_in_sample(
                hs, g_mix[i], w_in_even[e], q_gain[e], k_gain[e], dims)
            ya_s, h_s = _rglru_sample(xa_s, ga_s, state_conv[e], state_h[e], rg_w)
            yb_s = _nsa_sample(q_s, gate_s, kv_s, win_s, cache_kv[e], cache_win[e], page_table, lp,
                               rel_bias, groups, hpg, hd)
            hs = _even_tail(hs, ya_s, yb_s, p_sample[i, :, 0], w_out_even[e], tail_w)
            kvs.append(kv_s.reshape(r, 1, 4, groups, hd))
            wns.append(jnp.concatenate([cache_win[e][:, 1:], win_s.reshape(r, 1, 2, groups, hd)], axis=1))
            cvs.append(jnp.concatenate([state_conv[e][:, 1:], xa_s[:, None, :]], axis=1))
            hhs.append(h_s)
        else:
            o = i // 2
            hp, _ = _odd_layer(hp, p_prompt[i, 0], g_mix[i], w_in_odd[o], ln_v_g[o], ln_v_b[o],
                               sgu_w[o], sgu_b[o], w_out_odd[o], tail_w, single=False)
            hs, v_new = _odd_layer(hs, p_sample[i, :, 0], g_mix[i], w_in_odd[o], ln_v_g[o], ln_v_b[o],
                                   sgu_w[o], sgu_b[o], w_out_odd[o], tail_w, single=True)
            vvs.append(v_new[:, None, :])
    v_sample = jnp.stack(vvs) if vvs else jnp.zeros((0, r, 1, w_in_odd.shape[-1] // 2), F32)
    return (hp[None], hs[:, None, :], jnp.stack(kvp), jnp.stack(kvs), jnp.stack(wnp), jnp.stack(wns),
            jnp.stack(cvp), jnp.stack(cvs), jnp.stack(hhp), jnp.stack(hhs), v_sample)
```

```python
import functools
import math

import numpy as np
import jax
import jax.numpy as jnp
from jax import lax
from jax.experimental import pallas as pl
from jax.experimental.pallas import tpu as pltpu

F32 = jnp.float32
BF16 = jnp.bfloat16

EPS = 1e-6
NEG = -1e30
RG_C = 8.0
CMP_BLOCK = 32
SEL_BLOCK = 64
TOP_N = 16
WINDOW = 512
Q_BLOCK = 128
FORCE_SCORE = 1.0e4
MAX_DIST = 128
CHUNK = 128

LANES = 128
VMEM_LIMIT = 56 * 1024 * 1024

LOG2E = math.log2(math.e)
KEY_GROUP = 512
ACC_ROWS = 80
NKS_SAMPLE = 2048


def _cparams(n_axes=1):
    return pltpu.CompilerParams(dimension_semantics=("arbitrary",) * n_axes,
                                vmem_limit_bytes=VMEM_LIMIT)


def _const_spec(shape):
    nd = len(shape)
    return pl.BlockSpec(shape, lambda *_: (0,) * nd, pipeline_mode=pl.Buffered(1))


def _dot(a, b):
    return jnp.dot(a, b, preferred_element_type=F32)


def _dot_nt(a, b):
    return lax.dot_general(a, b, (((1,), (1,)), ((), ())), preferred_element_type=F32)


def _gelu(x):
    return 0.5 * x * (1.0 + jnp.tanh(math.sqrt(2.0 / math.pi) * (x + 0.044715 * (x * x * x))))


def _softplus(x):
    return jnp.maximum(x, 0.0) + jnp.log1p(jnp.exp(-jnp.abs(x)))


def _rms_rows(x, g):
    return x * lax.rsqrt(jnp.mean(x * x, axis=-1, keepdims=True) + EPS) * g


def _group_rmsnorm(x, gain, width):
    n = x.shape[-1] // width
    lane = lax.broadcasted_iota(jnp.int32, x.shape, 1)
    x2 = x * x
    ms = jnp.zeros_like(x)
    for i in range(n):
        inside = (lane >= i * width) & (lane < (i + 1) * width)
        s = jnp.sum(jnp.where(inside, x2, 0.0), axis=-1, keepdims=True)
        ms = jnp.where(inside, s, ms)
    return x * lax.rsqrt(ms * (1.0 / width) + EPS) * gain


def _bucket_np(dist, n_buckets):
    n = np.maximum(dist, 0)
    exact = n_buckets // 2
    nf = np.maximum(n, 1).astype(np.float32)
    large = exact + (np.log(nf / np.float32(exact)) / np.float32(math.log(MAX_DIST / exact))
                     * np.float32(n_buckets - exact)).astype(np.int32)
    return np.where(n < exact, n, np.minimum(large, n_buckets - 1)).astype(np.int32)


def _even_in_prompt_kernel(x_ref, g_ref, wn_ref, wt_ref, qg_ref, kg_ref,
                           xa_ref, ga_ref, kv_ref, win_ref, kc_ref, vc_ref,
                           ks_ref, kw_ref, qT_ref, vsT_ref, vwT_ref, gT_ref, *, lw, qw, kvw, hd):
    tm = x_ref.shape[0]
    hn = _rms_rows(x_ref[...], g_ref[...]).astype(BF16)
    zn = _dot(hn, wn_ref[...])
    xa_ref[...] = zn[:, :lw]
    ga_ref[...] = zn[:, lw:2 * lw]
    kv = [zn[:, 2 * lw + i * kvw: 2 * lw + (i + 1) * kvw] for i in range(6)]
    k_sel = _group_rmsnorm(kv[2], kg_ref[1:2, :], hd)
    k_win = _group_rmsnorm(kv[4], kg_ref[2:3, :], hd)
    kv_ref[:, 0 * kvw:1 * kvw] = kv[0]
    kv_ref[:, 1 * kvw:2 * kvw] = kv[1]
    kv_ref[:, 2 * kvw:3 * kvw] = k_sel
    kv_ref[:, 3 * kvw:4 * kvw] = kv[3]
    win_ref[:, :kvw] = k_win
    win_ref[:, kvw:] = kv[5]
    kc_ref[...] = kv[0]
    vc_ref[...] = kv[1]
    blk = lax.broadcasted_iota(jnp.int32, (tm, kvw), 0) // SEL_BLOCK
    col = lax.broadcasted_iota(jnp.int32, (tm, kvw), 1)
    ks_ref[:, :kvw] = k_sel.astype(BF16)
    ks_ref[:, kvw:] = jnp.where(col == blk, 1.0, 0.0).astype(BF16)
    kw_ref[...] = k_win.astype(BF16)

    zt = _dot_nt(wt_ref[...], hn)
    for h in range(qw // hd):
        blk = zt[h * hd:(h + 1) * hd, :]
        r = lax.rsqrt(jnp.mean(blk * blk, axis=0, keepdims=True) + EPS)
        qn = blk * r
        for j in range(0, tm, LANES):
            qT_ref[h * hd:(h + 1) * hd, j:j + LANES] = (
                qn[:, j:j + LANES] * qg_ref[h * hd:(h + 1) * hd, :]).astype(BF16)
    pad = ACC_ROWS - hd
    ones_rows = jnp.where(lax.broadcasted_iota(jnp.int32, (pad, tm), 0) == 0, 1.0, 0.0).astype(BF16)
    for out_ref, base in ((vsT_ref, qw), (vwT_ref, qw + kvw)):
        for g in range(kvw // hd):
            out_ref[g * ACC_ROWS:g * ACC_ROWS + hd, :] = zt[base + g * hd:base + (g + 1) * hd, :].astype(BF16)
            out_ref[g * ACC_ROWS + hd:(g + 1) * ACC_ROWS, :] = ones_rows
    gT_ref[...] = jax.nn.sigmoid(zt[qw + 2 * kvw:, :])


def _even_in_prompt(x, g_mix, w_in, q_gain, k_gain, dims, tm=512):
    t, d = x.shape
    lw, qw, kvw, hd, heads = dims
    tm = min(tm, t)
    c0, c1, c2 = 2 * lw, 2 * lw + qw, 2 * lw + qw + 6 * kvw
    wn = jnp.concatenate([w_in[:, :c0], w_in[:, c1:c2]], axis=1).astype(BF16)
    wg = w_in[:, c2:c2 + 3 * heads].reshape(d, heads, 3).transpose(0, 2, 1).reshape(d, 3 * heads)
    wt = jnp.concatenate([w_in[:, c0:c1], w_in[:, c1 + 3 * kvw:c1 + 4 * kvw],
                          w_in[:, c1 + 5 * kvw:c1 + 6 * kvw], wg,
                          jnp.zeros((d, 32 - 3 * heads), F32)], axis=1).T.astype(BF16)
    assert tm % KEY_GROUP == 0
    qg = jnp.broadcast_to((jnp.tile(q_gain, heads) * (hd ** -0.5 * LOG2E))[:, None], (qw, LANES)).astype(F32)
    vrows = (kvw // hd) * ACC_ROWS
    kg = jnp.tile(k_gain, (1, kvw // hd)).astype(F32)
    kg = jnp.concatenate([kg, jnp.zeros((5, kvw), F32)], axis=0)
    nt = wt.shape[0]
    row = lambda w: pl.BlockSpec((tm, w), lambda i: (i, 0))
    col = lambda r: pl.BlockSpec((r, tm), lambda i: (0, i))
    outs = pl.pallas_call(
        functools.partial(_even_in_prompt_kernel, lw=lw, qw=qw, kvw=kvw, hd=hd),
        grid=(t // tm,),
        in_specs=[row(d), _const_spec((1, d)), _const_spec(wn.shape), _const_spec(wt.shape),
                  _const_spec(qg.shape), _const_spec(kg.shape)],
        out_specs=[row(lw), row(lw), row(4 * kvw), row(2 * kvw), row(kvw), row(kvw),
                   row(2 * kvw), row(kvw), col(qw), col(vrows), col(vrows), col(32)],
        out_shape=[jax.ShapeDtypeStruct((t, lw), F32), jax.ShapeDtypeStruct((t, lw), F32),
                   jax.ShapeDtypeStruct((t, 4 * kvw), F32), jax.ShapeDtypeStruct((t, 2 * kvw), F32),
                   jax.ShapeDtypeStruct((t, kvw), F32), jax.ShapeDtypeStruct((t, kvw), F32),
                   jax.ShapeDtypeStruct((t, 2 * kvw), BF16), jax.ShapeDtypeStruct((t, kvw), BF16),
                   jax.ShapeDtypeStruct((qw, t), BF16), jax.ShapeDtypeStruct((vrows, t), BF16),
                   jax.ShapeDtypeStruct((vrows, t), BF16), jax.ShapeDtypeStruct((32, t), F32)],
        compiler_params=_cparams(), name="even_in_prompt",
    )(x, g_mix[None, :], wn, wt, qg, kg)
    return outs


def _even_in_sample_kernel(x_ref, g_ref, w_ref, qg_ref, kg_ref,
                           xa_ref, ga_ref, q_ref, kv_ref, win_ref, gate_ref, *, lw, qw, kvw, hd):
    hn = _rms_rows(x_ref[...], g_ref[...]).astype(BF16)
    z = _dot(hn, w_ref[...])
    xa_ref[...] = z[:, :lw]
    ga_ref[...] = z[:, lw:2 * lw]
    q_ref[...] = _group_rmsnorm(z[:, 2 * lw:2 * lw + qw], qg_ref[...], hd)
    o = 2 * lw + qw
    kv = [z[:, o + i * kvw: o + (i + 1) * kvw] for i in range(6)]
    kv_ref[:, 0 * kvw:1 * kvw] = kv[0]
    kv_ref[:, 1 * kvw:2 * kvw] = kv[1]
    kv_ref[:, 2 * kvw:3 * kvw] = _group_rmsnorm(kv[2], kg_ref[1:2, :], hd)
    kv_ref[:, 3 * kvw:4 * kvw] = kv[3]
    win_ref[:, :kvw] = _group_rmsnorm(kv[4], kg_ref[2:3, :], hd)
    win_ref[:, kvw:] = kv[5]
    gate_ref[...] = jax.nn.sigmoid(z[:, o + 6 * kvw:])


def _even_in_sample(x, g_mix, w_in, q_gain, k_gain, dims):
    r, d = x.shape
    lw, qw, kvw, hd, heads = dims
    n_in = w_in.shape[1]
    pad = (-n_in) % LANES
    w = jnp.pad(w_in, ((0, 0), (0, pad))).astype(BF16)
    gw = n_in + pad - (2 * lw + qw + 6 * kvw)
    qg = (jnp.tile(q_gain, heads) * hd ** -0.5)[None, :].astype(F32)
    kg = jnp.tile(k_gain, (1, kvw // hd)).astype(F32)
    kg = jnp.concatenate([kg, jnp.zeros((5, kvw), F32)], axis=0)
    full = lambda s: pl.BlockSpec(s, lambda i: (0,) * len(s))
    return pl.pallas_call(
        functools.partial(_even_in_sample_kernel, lw=lw, qw=qw, kvw=kvw, hd=hd),
        grid=(1,),
        in_specs=[full((r, d)), full((1, d)), full(w.shape), full(qg.shape), full(kg.shape)],
        out_specs=[full((r, lw)), full((r, lw)), full((r, qw)), full((r, 4 * kvw)),
                   full((r, 2 * kvw)), full((r, gw))],
        out_shape=[jax.ShapeDtypeStruct((r, lw), F32), jax.ShapeDtypeStruct((r, lw), F32),
                   jax.ShapeDtypeStruct((r, qw), F32), jax.ShapeDtypeStruct((r, 4 * kvw), F32),
                   jax.ShapeDtypeStruct((r, 2 * kvw), F32), jax.ShapeDtypeStruct((r, gw), F32)],
        compiler_params=_cparams(), name="even_in_sample",
    )(x, g_mix[None, :], w, qg, kg)


def _rglru_gates(conv, wa_ref, ba_ref, wx_ref, bx_ref, lam_ref):
    cb = conv.astype(BF16)
    r = jax.nn.sigmoid(_dot(cb, wa_ref[...]) + ba_ref[...])
    i = jax.nn.sigmoid(_dot(cb, wx_ref[...]) + bx_ref[...])
    log_a = -RG_C * r * _softplus(-lam_ref[...])
    a = jnp.exp(log_a)
    th = jnp.tanh(log_a)
    b = jnp.sqrt(-2.0 * th / (1.0 - th)) * (i * conv)
    return a, b


def _rglru_prompt_kernel(xa_ref, ga_ref, cw_ref, cb_ref, wa_ref, ba_ref, wx_ref, bx_ref, lam_ref,
                         ya_ref, hl_ref, xext_ref, h_ref):
    tr = xa_ref.shape[0]

    @pl.when(pl.program_id(0) == 0)
    def _():
        xext_ref[0:8, :] = jnp.zeros((8, xext_ref.shape[1]), F32)
        h_ref[...] = jnp.zeros(h_ref.shape, F32)

    x = xa_ref[...]
    xext_ref[8:8 + tr, :] = x
    conv = (cb_ref[...] + cw_ref[3:4, :] * x + cw_ref[2:3, :] * xext_ref[7:7 + tr, :]
            + cw_ref[1:2, :] * xext_ref[6:6 + tr, :] + cw_ref[0:1, :] * xext_ref[5:5 + tr, :])
    xext_ref[0:8, :] = xext_ref[tr:tr + 8, :]
    a, b = _rglru_gates(conv, wa_ref, ba_ref, wx_ref, bx_ref, lam_ref)
    row = lax.broadcasted_iota(jnp.int32, a.shape, 0)
    s = 1
    while s < tr:
        keep = row >= s
        a_sh = jnp.where(keep, pltpu.roll(a, s, 0), 1.0)
        b_sh = jnp.where(keep, pltpu.roll(b, s, 0), 0.0)
        b = a * b_sh + b
        a = a * a_sh
        s *= 2
    hs = a * h_ref[0:1, :] + b
    h_last = hs[tr - 1:tr, :]
    h_ref[...] = jnp.broadcast_to(h_last, h_ref.shape)
    hl_ref[...] = h_last
    ya_ref[...] = (hs * _gelu(ga_ref[...])).astype(BF16)


def _block_diag(w):
    n, a, b = w.shape
    eye = jnp.eye(n, dtype=w.dtype)
    return (eye[:, None, :, None] * w[:, :, None, :]).reshape(n * a, n * b)


def _rglru_weights(conv_w, conv_b, w_a, b_a, w_x, b_x, lam):
    return (conv_w.astype(F32), conv_b[None, :], _block_diag(w_a).astype(BF16), b_a[None, :],
            _block_diag(w_x).astype(BF16), b_x[None, :], lam[None, :])


def _rglru_prompt(xa, ga, weights, tr=256):
    t, lw = xa.shape
    tr = min(tr, t)
    row = pl.BlockSpec((tr, lw), lambda i: (i, 0))
    return pl.pallas_call(
        _rglru_prompt_kernel, grid=(t // tr,),
        in_specs=[row, row] + [_const_spec(w.shape) for w in weights],
        out_specs=[row, pl.BlockSpec((1, lw), lambda i: (0, 0))],
        out_shape=[jax.ShapeDtypeStruct((t, lw), BF16), jax.ShapeDtypeStruct((1, lw), F32)],
        scratch_shapes=[pltpu.VMEM((tr + 8, lw), F32), pltpu.VMEM((8, lw), F32)],
        compiler_params=_cparams(), name="rglru_prompt",
    )(xa, ga, *weights)


def _rglru_sample_kernel(xa_ref, ga_ref, sc_ref, h0_ref, cw_ref, cb_ref, wa_ref, ba_ref, wx_ref,
                         bx_ref, lam_ref, ya_ref, h_ref):
    conv = (cb_ref[...] + cw_ref[3:4, :] * xa_ref[...] + cw_ref[2:3, :] * sc_ref[2]
            + cw_ref[1:2, :] * sc_ref[1] + cw_ref[0:1, :] * sc_ref[0])
    a, b = _rglru_gates(conv, wa_ref, ba_ref, wx_ref, bx_ref, lam_ref)
    h = a * h0_ref[...] + b
    h_ref[...] = h
    ya_ref[...] = (h * _gelu(ga_ref[...])).astype(BF16)


def _rglru_sample(xa, ga, state_conv, state_h, weights):
    r, lw = xa.shape
    sc = jnp.transpose(state_conv, (1, 0, 2))
    args = (xa, ga, sc, state_h) + tuple(weights)
    full = lambda s: pl.BlockSpec(s, lambda i: (0,) * len(s))
    return pl.pallas_call(
        _rglru_sample_kernel, grid=(1,),
        in_specs=[full(a.shape) for a in args],
        out_specs=[full((r, lw)), full((r, lw))],
        out_shape=[jax.ShapeDtypeStruct((r, lw), BF16), jax.ShapeDtypeStruct((r, lw), F32)],
        compiler_params=_cparams(), name="rglru_sample",
    )(*args)


def _compress_prompt_kernel(xk_ref, xv_ref, pek_ref, pev_ref, wk_ref, wvT_ref, kg_ref,
                            kc_ref, vcT_ref, *, hd):
    xk = (xk_ref[...] + pek_ref[...]).astype(BF16)
    kc = _dot(xk, wk_ref[...])
    kc_ref[...] = _group_rmsnorm(kc, kg_ref[...], hd).astype(BF16)
    xv = (xv_ref[...] + pev_ref[...]).astype(BF16)
    vcT_ref[...] = _dot_nt(wvT_ref[...], xv).astype(BF16)


def _compress_weight(w, groups, hd):
    w3 = w.reshape(CMP_BLOCK, hd, hd)
    eye = jnp.eye(groups, dtype=w.dtype)
    full = eye[None, :, None, :, None] * w3[:, None, :, None, :]
    return full.reshape(CMP_BLOCK * groups * hd, groups * hd)


def _compress_prompt(kcmp, vcmp, w_k, w_v, pe_k, pe_v, k_gain_cmp, groups, hd):
    t, kvw = kcmp.shape
    ncb = t // CMP_BLOCK
    kdim = CMP_BLOCK * kvw
    xk = kcmp.reshape(ncb, kdim)
    xv = vcmp.reshape(ncb, kdim)
    wk = _compress_weight(w_k, groups, hd).astype(BF16)
    wvT = _compress_weight(w_v, groups, hd).T.astype(BF16)
    pek = jnp.tile(pe_k, (1, groups)).reshape(1, kdim)
    pev = jnp.tile(pe_v, (1, groups)).reshape(1, kdim)
    kg = jnp.tile(k_gain_cmp, groups)[None, :]
    nb = min(LANES, ncb)
    return pl.pallas_call(
        functools.partial(_compress_prompt_kernel, hd=hd), grid=(ncb // nb,),
        in_specs=[pl.BlockSpec((nb, kdim), lambda i: (i, 0)), pl.BlockSpec((nb, kdim), lambda i: (i, 0)),
                  _const_spec((1, kdim)), _const_spec((1, kdim)), _const_spec(wk.shape),
                  _const_spec(wvT.shape), _const_spec((1, kvw))],
        out_specs=[pl.BlockSpec((nb, kvw), lambda i: (i, 0)), pl.BlockSpec((kvw, nb), lambda i: (0, i))],
        out_shape=[jax.ShapeDtypeStruct((ncb, kvw), BF16), jax.ShapeDtypeStruct((kvw, ncb), BF16)],
        compiler_params=_cparams(), name="compress_prompt",
    )(xk, xv, pek, pev, wk, wvT, kg)


def _tile_lanes(x, n):
    return jnp.concatenate([x] * n, axis=1)


def _attend(s, shift, vT, slot, g, m_ref, acc_ref):
    m_old = m_ref[slot, g]
    m_blk = jnp.max(s, axis=0, keepdims=True)
    if shift is not None:
        m_blk = m_blk + shift
    m_new = jnp.maximum(m_old, m_blk)
    p = jnp.exp2(s - (m_new if shift is None else m_new - shift))
    acc_ref[slot, g] = jnp.exp2(m_old - m_new) * acc_ref[slot, g] + _dot(vT, p.astype(BF16))
    m_ref[slot, g] = m_new


def _bias_table_kernel(id_ref, val_ref, o_ref, *, reps):
    ids = _tile_lanes(id_ref[...], reps)
    out = jnp.zeros(ids.shape, F32)
    for k in range(val_ref.shape[0]):
        out = jnp.where(ids == k, val_ref[k:k + 1, :], out)
    o_ref[...] = out


def _bias_table(ids, vals):
    r = ids.shape[0]
    reps = vals.shape[1] // LANES
    full = lambda s: pl.BlockSpec(s, lambda i: (0,) * len(s))
    return pl.pallas_call(
        functools.partial(_bias_table_kernel, reps=reps), grid=(1,),
        in_specs=[full(ids.shape), full(vals.shape)], out_specs=full((r, vals.shape[1])),
        out_shape=jax.ShapeDtypeStruct((r, vals.shape[1]), F32),
        compiler_params=_cparams(), name="bias_table",
    )(jnp.asarray(ids, jnp.int32), vals)


def _nsa_prompt_kernel(qT_ref, gT_ref, kc_ref, vcT_ref, ks_ref, vsT_ref, kw_ref, vwT_ref,
                       b31_ref, tab_ref,
                       y_ref,
                       qaug_ref, sa_ref, sb_ref, sc_ref, p4_ref, selb_ref, ocmp_ref, m_ref, acc_ref,
                       *, groups, hpg, hd):
    qb = pl.program_id(0)
    t0 = qb * Q_BLOCK
    ncb = kc_ref.shape[0]
    nsb = ncb * CMP_BLOCK // SEL_BLOCK
    gl = hpg * Q_BLOCK
    n_top = min(TOP_N, nsb)
    kvw = groups * hd
    bpg = KEY_GROUP // SEL_BLOCK

    rows = []
    for g in range(groups):
        top = jnp.concatenate([qT_ref[(g * hpg + hh) * hd:(g * hpg + hh + 1) * hd, :]
                               for hh in range(hpg)], axis=1)
        z = jnp.zeros_like(top)
        rows.append(jnp.concatenate([top if gg == g else z for gg in range(groups)], axis=1))
    qbd = jnp.concatenate(rows, axis=0)
    qaug_ref[0:kvw, :] = qbd
    qaug_ref[kvw:2 * kvw, :] = jnp.zeros((kvw, groups * gl), BF16)
    b31 = b31_ref[...]

    def compress_and_select(rc):
        nsv = rc * CMP_BLOCK // SEL_BLOCK
        jrow = lax.broadcasted_iota(jnp.int32, (rc, Q_BLOCK), 0)
        qcol = lax.broadcasted_iota(jnp.int32, (rc, Q_BLOCK), 1)
        vis = _tile_lanes(jrow * CMP_BLOCK + (CMP_BLOCK - 1) <= t0 + qcol, groups * hpg)
        s_c = _dot(kc_ref[0:rc, :], qbd) + b31
        sc_ref[0:rc, :] = jnp.where(vis, s_c, NEG)
        odd = (qb % 2) == 1
        v_a = jnp.where(qb == 0, 0, jnp.where(odd, 1, 2))
        w_a = pl.multiple_of(jnp.where(qb == 0, 0, jnp.where(odd, 4 * qb - 4, 4 * qb - 8)), 8)
        v_b = jnp.where((qb > 0) & jnp.logical_not(odd), 0, 3)
        w_b = pl.multiple_of(jnp.where((qb > 0) & jnp.logical_not(odd), 4 * qb, 0), 8)
        sc_ref[pl.ds(w_a, 8), :] = sc_ref[pl.ds(w_a, 8), :] + tab_ref[pl.ds(pl.multiple_of(v_a * 8, 8), 8), :]
        sc_ref[pl.ds(w_b, 8), :] = sc_ref[pl.ds(w_b, 8), :] + tab_ref[pl.ds(pl.multiple_of(v_b * 8, 8), 8), :]
        s_c = sc_ref[0:rc, :]
        m_c = jnp.max(s_c, axis=0, keepdims=True)
        p_c = jnp.where(vis, jnp.exp2(s_c - m_c), 0.0)
        l_c = jnp.sum(p_c, axis=0, keepdims=True)
        p_c = p_c * jnp.where(l_c > 0.0, 1.0 / l_c, 0.0)
        for g in range(groups):
            pg = p_c[:, g * gl:(g + 1) * gl]
            ocmp_ref[g] = _dot(vcT_ref[g * hd:(g + 1) * hd, 0:rc], pg.astype(BF16))
            p4 = pg[:, 0:Q_BLOCK]
            for hh in range(1, hpg):
                p4 = p4 + pg[:, hh * Q_BLOCK:(hh + 1) * Q_BLOCK]
            p4_ref[g, 0:rc, :] = p4

        ratio = SEL_BLOCK // CMP_BLOCK
        jb = lax.broadcasted_iota(jnp.int32, (nsv, Q_BLOCK), 0)
        qc = lax.broadcasted_iota(jnp.int32, (nsv, Q_BLOCK), 1)
        cur = (t0 + qc) // SEL_BLOCK
        forced = (jb == 0) | (jb == cur) | (jb == cur - 1)
        for g in range(groups):
            imp = p4_ref[g, pl.ds(0, nsv, stride=ratio), :]
            for k in range(1, ratio):
                imp = imp + p4_ref[g, pl.ds(k, nsv, stride=ratio), :]
            imp = jnp.where(forced, FORCE_SCORE, imp)
            imp = jnp.where(jb <= cur, imp, -1.0)
            selb = jnp.full((nsv, Q_BLOCK), NEG, F32)
            for _ in range(n_top):
                mx = jnp.max(imp, axis=0, keepdims=True)
                idx = jnp.min(jnp.where(imp == mx, jb, nsv), axis=0, keepdims=True)
                hit = jb == idx
                selb = jnp.where(hit, 0.0, selb)
                imp = jnp.where(hit, -3.0e38, imp)
            selb_ref[g, 0:nsv, :] = selb
            if nsv < nsb:
                selb_ref[g, nsv:nsb, :] = jnp.full((nsb - nsv, Q_BLOCK), NEG, F32)

    tiers = list(range(LANES, ncb + 1, LANES)) if ncb % LANES == 0 else [ncb]
    n_vis = (t0 + Q_BLOCK) // CMP_BLOCK
    for i, rc in enumerate(tiers):
        lo = tiers[i - 1] if i else 0

        @pl.when((n_vis > lo) & (n_vis <= rc))
        def _(rc=rc):
            compress_and_select(rc)

    m_ref[...] = jnp.full(m_ref.shape, NEG, F32)
    acc_ref[...] = jnp.zeros(acc_ref.shape, F32)

    def put_mask_rows(g, rows8):
        tile = jnp.concatenate([rows8, jnp.zeros_like(rows8)], axis=0).astype(BF16)
        for hh in range(hpg):
            qaug_ref[kvw:kvw + 2 * bpg, g * gl + hh * Q_BLOCK:g * gl + (hh + 1) * Q_BLOCK] = tile

    def far_scores(c, dst_ref):
        for g in range(groups):
            put_mask_rows(g, selb_ref[g, pl.ds(pl.multiple_of(c * bpg, bpg), bpg), :])
        k0 = pl.multiple_of(c * KEY_GROUP, KEY_GROUP)
        dst_ref[...] = _dot(ks_ref[pl.ds(k0, KEY_GROUP), :], qaug_ref[...])

    def far_attend(c, src_ref, live_rows=None):
        k0 = pl.multiple_of(c * KEY_GROUP, KEY_GROUP)
        drop = None
        if live_rows is not None:
            drop = lax.broadcasted_iota(jnp.int32, (KEY_GROUP, Q_BLOCK), 0) >= live_rows
        for g in range(groups):
            sg = src_ref[:, g * gl:(g + 1) * gl]
            if drop is not None:
                sg = jnp.where(_tile_lanes(drop, hpg), NEG, sg)
            _attend(sg, b31[:, g * gl:(g + 1) * gl],
                    vsT_ref[g * ACC_ROWS:(g + 1) * ACC_ROWS, pl.ds(k0, KEY_GROUP)], 0, g, m_ref, acc_ref)

    far_len = jnp.maximum(t0 - Q_BLOCK, 0)
    n_far = far_len // KEY_GROUP
    live_tail = far_len - n_far * KEY_GROUP
    far_scores(0, sa_ref)

    def far_pair(j, carry):
        far_scores(2 * j + 1, sb_ref)
        far_attend(2 * j, sa_ref)
        far_scores(2 * j + 2, sa_ref)
        far_attend(2 * j + 1, sb_ref)
        return carry

    lax.fori_loop(0, n_far // 2, far_pair, 0)
    odd_far = n_far % 2 == 1

    @pl.when(odd_far)
    def _():
        far_scores(n_far, sb_ref)
        far_attend(n_far - 1, sa_ref)

    @pl.when(odd_far & (live_tail > 0))
    def _():
        far_attend(n_far, sb_ref, live_tail)

    @pl.when(jnp.logical_not(odd_far) & (live_tail > 0))
    def _():
        far_attend(n_far, sa_ref, live_tail)

    near_tab = tab_ref[_TAB_NEAR:_TAB_NEAR + 2 * Q_BLOCK, :]

    def near_piece():
        b0 = (t0 - Q_BLOCK) // SEL_BLOCK
        base = (b0 // bpg) * bpg
        nxt = jnp.minimum(base + bpg, nsb - bpg)
        jrow8 = lax.broadcasted_iota(jnp.int32, (bpg, Q_BLOCK), 0)
        for g in range(groups):
            lo = selb_ref[g, pl.ds(pl.multiple_of(base, bpg), bpg), :]
            hi = selb_ref[g, pl.ds(pl.multiple_of(nxt, bpg), bpg), :]
            put_mask_rows(g, jnp.where(jrow8 >= b0 - base, lo, hi))
        k0 = pl.multiple_of(t0 - Q_BLOCK, Q_BLOCK)
        s = _dot(ks_ref[pl.ds(k0, 2 * Q_BLOCK), :], qaug_ref[...]) + near_tab
        for g in range(groups):
            _attend(s[:, g * gl:(g + 1) * gl], None,
                    vsT_ref[g * ACC_ROWS:(g + 1) * ACC_ROWS, pl.ds(k0, 2 * Q_BLOCK)], 0, g, m_ref, acc_ref)

    @pl.when(qb == 0)
    def _():
        for g in range(groups):
            put_mask_rows(g, selb_ref[g, 0:bpg, :])
        s = _dot(ks_ref[0:Q_BLOCK, :], qaug_ref[...]) + near_tab[Q_BLOCK:2 * Q_BLOCK, :]
        for g in range(groups):
            _attend(s[:, g * gl:(g + 1) * gl], None,
                    vsT_ref[g * ACC_ROWS:(g + 1) * ACC_ROWS, 0:Q_BLOCK], 0, g, m_ref, acc_ref)

    n_wk = WINDOW + Q_BLOCK

    def window_block():
        k0 = pl.multiple_of(t0 - WINDOW, Q_BLOCK)
        s = _dot(kw_ref[pl.ds(k0, n_wk), :], qaug_ref[0:kvw, :]) + tab_ref[_TAB_WIN:_TAB_WIN + n_wk, :]
        for g in range(groups):
            _attend(s[:, g * gl:(g + 1) * gl], None,
                    vwT_ref[g * ACC_ROWS:(g + 1) * ACC_ROWS, pl.ds(k0, n_wk)], 1, g, m_ref, acc_ref)

    @pl.when(t0 >= WINDOW)
    def _():
        near_piece()
        window_block()

    @pl.when((qb >= 1) & (t0 < WINDOW))
    def _():
        near_piece()

    for i in range(WINDOW // Q_BLOCK):
        k0 = t0 - i * Q_BLOCK

        @pl.when((t0 < WINDOW) & (k0 >= 0))
        def _(i=i, k0=k0):
            k0a = pl.multiple_of(k0, Q_BLOCK)
            r0 = _TAB_WIN + WINDOW - i * Q_BLOCK
            s = _dot(kw_ref[pl.ds(k0a, Q_BLOCK), :], qaug_ref[0:kvw, :]) + tab_ref[r0:r0 + Q_BLOCK, :]
            for g in range(groups):
                _attend(s[:, g * gl:(g + 1) * gl], None,
                        vwT_ref[g * ACC_ROWS:(g + 1) * ACC_ROWS, pl.ds(k0a, Q_BLOCK)], 1, g, m_ref, acc_ref)

    heads = groups * hpg
    pieces = []
    for g in range(groups):
        gate = [jnp.concatenate([gT_ref[j * heads + g * hpg + hh: j * heads + g * hpg + hh + 1, :]
                                 for hh in range(hpg)], axis=1) for j in range(3)]
        o = gate[0] * ocmp_ref[g]
        for slot in range(2):
            acc = acc_ref[slot, g]
            o = o + gate[1 + slot] * (acc[0:hd, :] * (1.0 / acc[hd:hd + 1, :]))
        for hh in range(hpg):
            pieces.append(o[:, hh * Q_BLOCK:(hh + 1) * Q_BLOCK])
    y_ref[...] = jnp.concatenate(pieces, axis=0).T.astype(BF16)


_TAB_NEAR = 32
_TAB_WIN = _TAB_NEAR + 2 * Q_BLOCK


def _nsa_prompt_tables(rel_bias, n_lanes):
    nb, heads = rel_bias.shape
    tbl = rel_bias.astype(F32) * LOG2E
    lanes = lambda x: jnp.repeat(x, n_lanes // heads, axis=1)
    vals = jnp.concatenate([lanes(tbl), jnp.full((1, n_lanes), NEG, F32), jnp.zeros((1, n_lanes), F32),
                            lanes(tbl - tbl[nb - 1:nb])], axis=0)
    c = np.arange(Q_BLOCK)[None, :]
    i8 = np.arange(8)[:, None]
    ids = []
    for base in (-(CMP_BLOCK - 1), 3 * CMP_BLOCK + 1, 7 * CMP_BLOCK + 1):
        d = base + c - CMP_BLOCK * i8
        ids.append(np.where(d >= 0, nb + 2 + _bucket_np(d, nb), nb + 1))
    ids.append(np.full((8, Q_BLOCK), nb + 1))
    d = Q_BLOCK + c - np.arange(2 * Q_BLOCK)[:, None]
    ids.append(np.where(d >= 0, _bucket_np(d, nb), nb))
    d = WINDOW + c - np.arange(WINDOW + Q_BLOCK)[:, None]
    ids.append(np.where((d >= 0) & (d < WINDOW), _bucket_np(d, nb), nb))
    return vals[nb - 1:nb], _bias_table(np.concatenate(ids, axis=0), vals)


def _nsa_prompt(qT, gT, kc, vcT, ks, vsT, kw, vwT, rel_bias, groups, hpg, hd):
    qw, t = qT.shape
    ncb = kc.shape[0]
    nsb = t // SEL_BLOCK
    gl = hpg * Q_BLOCK
    kvw = groups * hd
    assert t % KEY_GROUP == 0 and Q_BLOCK == 4 * CMP_BLOCK and MAX_DIST <= Q_BLOCK
    tables = _nsa_prompt_tables(rel_bias, groups * gl)
    resident = (kc, vcT, ks, vsT, kw, vwT) + tuple(tables)
    return pl.pallas_call(
        functools.partial(_nsa_prompt_kernel, groups=groups, hpg=hpg, hd=hd),
        grid=(t // Q_BLOCK,),
        in_specs=[pl.BlockSpec((qw, Q_BLOCK), lambda i: (0, i)),
                  pl.BlockSpec((gT.shape[0], Q_BLOCK), lambda i: (0, i))]
                 + [_const_spec(a.shape) for a in resident],
        out_specs=pl.BlockSpec((Q_BLOCK, qw), lambda i: (i, 0)),
        out_shape=jax.ShapeDtypeStruct((t, qw), BF16),
        scratch_shapes=[pltpu.VMEM((2 * kvw, groups * gl), BF16),
                        pltpu.VMEM((KEY_GROUP, groups * gl), F32),
                        pltpu.VMEM((KEY_GROUP, groups * gl), F32),
                        pltpu.VMEM((ncb, groups * gl), F32),
                        pltpu.VMEM((groups, ncb, Q_BLOCK), F32),
                        pltpu.VMEM((groups, nsb, Q_BLOCK), F32),
                        pltpu.VMEM((groups, hd, gl), F32),
                        pltpu.VMEM((2, groups, 1, gl), F32),
                        pltpu.VMEM((2, groups, ACC_ROWS, gl), F32)],
        compiler_params=_cparams(), name="nsa_prompt",
    )(qT, gT, *resident)


def _split3(x):
    hi = x.astype(BF16)
    r1 = x - hi.astype(F32)
    mid = r1.astype(BF16)
    lo = (r1 - mid.astype(F32)).astype(BF16)
    return hi, mid, lo


def _nsa_sample_kernel(pt_ref, cache_ref, qbd_ref, qn_ref, gates_ref, kvn_ref, winn_ref, cwin_ref,
                       wkv_ref, pe_ref, perm_ref, kg_ref, gsum_ref,
                       cb_ref, sb_ref, wb_ref, b0_ref, e_ref,
                       o_ref,
                       ringa_ref, ringb_ref, bufkv_ref, sema, semb, p_ref,
                       *, hd, n_pages, page, chunk):
    b = pl.program_id(0)
    nb = pl.num_programs(0)
    kvw = kg_ref.shape[1]
    past = n_pages * page
    ncb = past // CMP_BLOCK
    nsb = past // SEL_BLOCK
    n_chunks = n_pages // chunk
    bpp = 2 * page // CMP_BLOCK

    def chunk_copy(bb, q, p):
        if q < n_chunks:
            return pltpu.make_async_copy(cache_ref.at[pt_ref[bb, q * chunk + p], 0:2 * kvw, :],
                                         ringa_ref.at[q % 2, p], sema.at[q % 2])
        ch = q - n_chunks
        return pltpu.make_async_copy(cache_ref.at[pt_ref[bb, ch * chunk + p], 2 * kvw:4 * kvw, :],
                                     ringb_ref.at[ch % 2, :, pl.ds(p * page, page)], semb.at[ch % 2])

    def for_pages(n, fn, unroll=1):
        def body(p, c):
            fn(p)
            return c
        lax.fori_loop(0, n, body, 0, unroll=unroll)

    def start_chunk(bb, q):
        for_pages(chunk, lambda p: chunk_copy(bb, q, p).start(), unroll=4)

    def wait_chunk(q):
        for_pages(chunk, lambda p: chunk_copy(b, q, p).wait(), unroll=4)

    def request_ahead(q):
        if q + 2 < 2 * n_chunks:
            start_chunk(b, q + 2)
        else:
            @pl.when(b + 1 < nb)
            def _():
                start_chunk(b + 1, q + 2 - 2 * n_chunks)

    @pl.when(b == 0)
    def _():
        start_chunk(0, 0)
        start_chunk(0, 1)

    qbd = qbd_ref[0]

    for ch in range(n_chunks):
        wait_chunk(ch)

        def regroup(q, ch=ch):
            xt = jnp.concatenate([ringa_ref[ch % 2, 2 * q], ringa_ref[ch % 2, 2 * q + 1]], axis=1)
            y = _dot_nt(perm_ref[...], (xt + pe_ref[...]).astype(BF16))
            blocks = pl.ds(pl.multiple_of((ch * (chunk // 2) + q) * bpp, bpp), bpp)
            for r in range(CMP_BLOCK):
                bufkv_ref[r, blocks, :] = y[r * bpp:(r + 1) * bpp, :]
        for_pages(chunk // 2, regroup, unroll=4)
        request_ahead(ch)

    half = ncb // 2
    zs = [jnp.zeros((half, 2 * kvw), F32), jnp.zeros((ncb - half, 2 * kvw), F32)]
    for r in range(CMP_BLOCK):
        zs[0] = zs[0] + _dot(bufkv_ref[r, 0:half, :].astype(BF16), wkv_ref[r])
        zs[1] = zs[1] + _dot(bufkv_ref[r, half:ncb, :].astype(BF16), wkv_ref[r])
    z = jnp.concatenate(zs, axis=0)
    vc = z[:, kvw:2 * kvw]
    kc = _group_rmsnorm(z[:, 0:kvw], kg_ref[...], hd).astype(BF16)
    s_c = _dot(kc, qbd) + cb_ref[...]
    m_c = jnp.max(s_c, axis=0, keepdims=True)
    p_c = jnp.exp(s_c - m_c)
    p_c = p_c * (1.0 / jnp.sum(p_c, axis=0, keepdims=True))
    o_cmp = _dot(p_c.T.astype(BF16), vc.astype(BF16))
    p_ref[...] = p_c
    ratio = SEL_BLOCK // CMP_BLOCK
    imp = p_ref[pl.ds(0, nsb, stride=ratio), :]
    for k in range(1, ratio):
        imp = imp + p_ref[pl.ds(k, nsb, stride=ratio), :]
    hi, mid, lo = _split3(imp)
    gs = gsum_ref[...]
    imp = _dot(hi, gs) + _dot(mid, gs) + _dot(lo, gs)
    jb = lax.broadcasted_iota(jnp.int32, (nsb, LANES), 0)
    imp = jnp.where((jb == 0) | (jb == nsb - 1), FORCE_SCORE, imp)
    selb = jnp.full((nsb, LANES), NEG, F32)
    for _ in range(min(TOP_N, nsb + 1) - 1):
        mx = jnp.max(imp, axis=0, keepdims=True)
        idx = jnp.min(jnp.where(imp == mx, jb, nsb), axis=0, keepdims=True)
        hit = jb == idx
        selb = jnp.where(hit, 0.0, selb)
        imp = jnp.where(hit, -3.0e38, imp)
    rows = qn_ref.shape[1]
    sel_nat = selb.T[0:rows, :]

    qn = qn_ref[0]
    qnb = qn.astype(BF16)
    kvn = kvn_ref[0]
    b0 = b0_ref[:, 0:1]
    m_s = jnp.sum(qn * kvn[:, 2 * kvw:3 * kvw], axis=1, keepdims=True) + b0
    l_s = jnp.ones_like(m_s)
    acc = jnp.broadcast_to(kvn[:, 3 * kvw:4 * kvw], (rows, kvw))
    nks = e_ref.shape[1]
    bps = nks // SEL_BLOCK
    per_chunk = chunk * page // nks
    for ch in range(n_chunks):
        wait_chunk(n_chunks + ch)
        for sub in range(per_chunk):
            c = ch * per_chunk + sub
            keys = slice(sub * nks, (sub + 1) * nks)
            rolled = sel_nat if c == 0 else pltpu.roll(sel_nat, nsb - c * bps, 1)
            mask = _dot(rolled.astype(BF16), e_ref[...])
            s = (_dot(qnb, ringb_ref[ch % 2, 0:kvw, keys].astype(BF16)) + mask
                 + sb_ref[:, c * nks:(c + 1) * nks])
            m_new = jnp.maximum(m_s, jnp.max(s, axis=1, keepdims=True))
            alpha = jnp.exp(m_s - m_new)
            p = jnp.exp(s - m_new)
            l_s = alpha * l_s + jnp.sum(p, axis=1, keepdims=True)
            acc = alpha * acc + _dot_nt(p.astype(BF16), ringb_ref[ch % 2, kvw:2 * kvw, keys].astype(BF16))
            m_s = m_new
        request_ahead(n_chunks + ch)
    o_sel = acc * (1.0 / l_s)

    winn = winn_ref[0]
    s_w = _dot(qnb, cwin_ref[0, 0:kvw, :].astype(BF16)) + wb_ref[...]
    s_wn = jnp.sum(qn * winn[:, 0:kvw], axis=1, keepdims=True) + b0
    m_w = jnp.maximum(jnp.max(s_w, axis=1, keepdims=True), s_wn)
    p_w = jnp.exp(s_w - m_w)
    p_wn = jnp.exp(s_wn - m_w)
    l_w = jnp.sum(p_w, axis=1, keepdims=True) + p_wn
    o_win = (_dot_nt(p_w.astype(BF16), cwin_ref[0, kvw:2 * kvw, :].astype(BF16))
             + p_wn * winn[:, kvw:2 * kvw]) * (1.0 / l_w)

    g = gates_ref[0]
    out = g[0] * o_cmp[0:rows, :] + g[1] * o_sel + g[2] * o_win
    o_ref[0] = out[0:8, :]


def _nsa_sample(q, gates, kv_new, win_new, cache_kv, cache_win, page_table, lp, rel_bias,
                groups, hpg, hd):
    r = q.shape[0]
    heads = groups * hpg
    kvw = groups * hd
    n_phys, page = cache_kv.shape[0], cache_kv.shape[1]
    n_pages = page_table.shape[1]
    past = n_pages * page
    wbuf = cache_win.shape[1]
    ncb, nsb = past // CMP_BLOCK, past // SEL_BLOCK
    rows = 16
    chunk = min(32, n_pages // 2)
    nks = min(NKS_SAMPLE, chunk * page)
    assert n_pages % (2 * chunk) == 0 and chunk % 8 == 0 and (chunk * page) % nks == 0 and heads <= rows
    q3 = q.reshape(r, heads, hd)
    gmask = (jnp.arange(groups)[:, None] == (jnp.arange(heads) // hpg)[None, :]).astype(F32)
    qbd = jnp.einsum('rhd,gh->rgdh', q3, gmask).reshape(r, kvw, heads)
    qbd = jnp.pad(qbd, ((0, 0), (0, 0), (0, LANES - heads))).astype(BF16)
    qn = jnp.einsum('rhd,gh->rhgd', q3, gmask).reshape(r, heads, kvw)
    qn = jnp.pad(qn, ((0, 0), (0, rows - heads), (0, 0)))
    gts = gates[:, :3 * heads].reshape(r, heads, 3).transpose(0, 2, 1)
    gts = jnp.broadcast_to(gts[:, :, :, None], (r, 3, heads, kvw))
    gts = jnp.pad(gts, ((0, 0), (0, 0), (0, rows - heads), (0, 0)))
    wk = _compress_weight(lp['cmp_w_k'], groups, hd).reshape(CMP_BLOCK, kvw, kvw)
    wv = _compress_weight(lp['cmp_w_v'], groups, hd).reshape(CMP_BLOCK, kvw, kvw)
    zero = jnp.zeros_like(wk)
    wkv = jnp.concatenate([jnp.concatenate([wk, zero], axis=2),
                           jnp.concatenate([zero, wv], axis=2)], axis=1).astype(BF16)
    pe_t = jnp.concatenate([jnp.tile(lp['pe_k'].T, (groups, 1)), jnp.tile(lp['pe_v'].T, (groups, 1))], axis=0)
    pe_t = jnp.tile(pe_t, (1, 2 * page // CMP_BLOCK))
    bpp = 2 * page // CMP_BLOCK
    dst = np.arange(2 * page)
    src = (dst % bpp) * CMP_BLOCK + dst // bpp
    perm = jnp.asarray(src[:, None] == np.arange(2 * page)[None, :], BF16)
    kg = jnp.tile(lp['k_gain'][0], groups)[None, :]
    lane_g = np.where(np.arange(LANES) < heads, np.arange(LANES) // hpg, -1)
    gsum = jnp.asarray((lane_g[:, None] == lane_g[None, :]) & (lane_g[:, None] >= 0), BF16)
    nbk = rel_bias.shape[0]
    tbl = rel_bias.astype(F32)
    d_c = (past - (np.arange(ncb) * CMP_BLOCK + CMP_BLOCK - 1))[:, None]
    vals_t = jnp.concatenate([tbl, jnp.broadcast_to(tbl[:, :1], (nbk, LANES - heads))], axis=1)
    cb = _bias_table(np.broadcast_to(_bucket_np(d_c, nbk), (ncb, LANES)), vals_t)
    vals_n = jnp.concatenate([jnp.repeat(tbl, LANES, axis=1), jnp.full((1, heads * LANES), NEG, F32)], axis=0)

    def head_rows(ids):
        out = _bias_table(ids, vals_n).reshape(ids.shape[0], heads, LANES)
        out = jnp.transpose(out, (1, 0, 2)).reshape(heads, ids.shape[0] * LANES)
        return jnp.pad(out, ((0, rows - heads), (0, 0)))

    sb = head_rows(_bucket_np(past - np.arange(past), nbk).reshape(past // LANES, LANES))
    d_w = wbuf - np.arange(wbuf)
    wb = head_rows(np.where(d_w < WINDOW, _bucket_np(d_w, nbk), nbk).reshape(wbuf // LANES, LANES))
    b0 = jnp.pad(jnp.broadcast_to(tbl[0][:, None], (heads, LANES)), ((0, rows - heads), (0, 0)))
    blk = np.arange(nsb)[:, None]
    spread = jnp.asarray((blk == np.arange(nks)[None, :] // SEL_BLOCK), BF16)

    cache_t = jnp.transpose(cache_kv, (0, 2, 3, 4, 1)).reshape(n_phys, 4 * kvw, page)
    cwin_t = jnp.transpose(cache_win, (0, 2, 3, 4, 1)).reshape(r, 2 * kvw, wbuf)
    kvn = kv_new.reshape(r, 1, 4 * kvw)
    winn = win_new.reshape(r, 1, 2 * kvw)
    per_b = lambda s: pl.BlockSpec((1,) + s, lambda b, pt: (b,) + (0,) * len(s))
    const = lambda a: pl.BlockSpec(a.shape, lambda b, pt: (0,) * a.ndim, pipeline_mode=pl.Buffered(1))
    consts = (wkv, pe_t, perm, kg, gsum, cb, sb, wb, b0, spread)
    grid_spec = pltpu.PrefetchScalarGridSpec(
        num_scalar_prefetch=1, grid=(r,),
        in_specs=[pl.BlockSpec(memory_space=pl.ANY), per_b((kvw, LANES)), per_b((rows, kvw)),
                  per_b((3, rows, kvw)), per_b((1, 4 * kvw)), per_b((1, 2 * kvw)),
                  per_b((2 * kvw, wbuf))]
                 + [const(a) for a in consts],
        out_specs=per_b((8, kvw)),
        scratch_shapes=[pltpu.VMEM((2, chunk, 2 * kvw, page), F32),
                        pltpu.VMEM((2, 2 * kvw, chunk * page), F32),
                        pltpu.VMEM((CMP_BLOCK, ncb, 2 * kvw), F32),
                        pltpu.SemaphoreType.DMA((2,)), pltpu.SemaphoreType.DMA((2,)),
                        pltpu.VMEM((ncb, LANES), F32)])
    o8 = pl.pallas_call(
        functools.partial(_nsa_sample_kernel, hd=hd, n_pages=n_pages, page=page, chunk=chunk),
        grid_spec=grid_spec,
        out_shape=jax.ShapeDtypeStruct((r, 8, kvw), F32),
        compiler_params=_cparams(), name="nsa_sample",
    )(page_table, cache_t, qbd, qn, gts, kvn, winn, cwin_t, *consts)
    o4 = o8[:, :heads, :].reshape(r, heads, groups, hd)
    y = jnp.take_along_axis(o4, (jnp.arange(heads) // hpg)[None, :, None, None], axis=2)
    return y.reshape(r, heads * hd).astype(BF16)


def _mlp_ple(h, p_ref, gm_ref, wu_ref, wd_ref, gp_ref, wg_ref, wp_ref):
    up = _dot(_rms_rows(h, gm_ref[...]).astype(BF16), wu_ref[...])
    act = jnp.square(jnp.maximum(up, 0.0)).astype(BF16)
    h = h + _dot(act, wd_ref[...])
    gate = jax.nn.sigmoid(_dot(_rms_rows(h, gp_ref[...]).astype(BF16), wg_ref[...]))
    return h + gate * _dot(p_ref[...].astype(BF16), wp_ref[...])


def _even_tail_kernel(h_ref, ya_ref, yb_ref, p_ref, woa_ref, wob_ref,
                      gm_ref, wu_ref, wd_ref, gp_ref, wg_ref, wp_ref, o_ref):
    h = h_ref[...] + _dot(ya_ref[...], woa_ref[...]) + _dot(yb_ref[...], wob_ref[...])
    o_ref[...] = _mlp_ple(h, p_ref, gm_ref, wu_ref, wd_ref, gp_ref, wg_ref, wp_ref)


def _tail_weights(g_mlp, w_up, w_down, g_ple, w_gate, w_proj):
    return (g_mlp[None, :], w_up.astype(BF16), w_down.astype(BF16), g_ple[None, :],
            w_gate.astype(BF16), w_proj.astype(BF16))


def _even_tail(h, ya, yb, p, w_out, tail_w, tm=256):
    t, d = h.shape
    tm = min(tm, t)
    lw = ya.shape[1]
    woa, wob = w_out[:lw].astype(BF16), w_out[lw:].astype(BF16)
    row = lambda w: pl.BlockSpec((tm, w), lambda i: (i, 0))
    weights = (woa, wob) + tuple(tail_w)
    return pl.pallas_call(
        _even_tail_kernel, grid=(t // tm,),
        in_specs=[row(d), row(lw), row(yb.shape[1]), row(p.shape[1])]
                 + [_const_spec(w.shape) for w in weights],
        out_specs=row(d), out_shape=jax.ShapeDtypeStruct((t, d), F32),
        compiler_params=_cparams(), name="even_tail",
    )(h, ya, yb, p, *weights)


def _odd_kernel(h_ref, p_ref, gx_ref, wi_ref, lg_ref, lb_ref, sw_ref, sb_ref, wo_ref,
                gm_ref, wu_ref, wd_ref, gp_ref, wg_ref, wp_ref, o_ref, v_ref, *, sw_groups, single):
    h = h_ref[...]
    tm = h.shape[0]
    z = _gelu(_dot(_rms_rows(h, gx_ref[...]).astype(BF16), wi_ref[...]))
    width = z.shape[1] // 2
    u, v = z[:, :width], z[:, width:]
    mu = jnp.mean(v, axis=-1, keepdims=True)
    vc = v - mu
    vn = vc * lax.rsqrt(jnp.mean(vc * vc, axis=-1, keepdims=True) + EPS) * lg_ref[...] + lb_ref[...]
    v_ref[...] = vn
    gw = width // sw_groups
    if single:
        s = vn * sw_ref[...] + sb_ref[...]
    else:
        vb = vn.astype(BF16)
        parts = []
        for c in range(tm // CHUNK):
            cols = [_dot(sw_ref[g], vb[c * CHUNK:(c + 1) * CHUNK, g * gw:(g + 1) * gw])
                    for g in range(sw_groups)]
            parts.append(jnp.concatenate(cols, axis=1) + sb_ref[...])
        s = jnp.concatenate(parts, axis=0)
    h = h + _dot((u * s).astype(BF16), wo_ref[...])
    o_ref[...] = _mlp_ple(h, p_ref, gm_ref, wu_ref, wd_ref, gp_ref, wg_ref, wp_ref)


def _odd_layer(h, p, g_mix, w_in, ln_g, ln_b, sgu_w, sgu_b, w_out, tail_w, single, tm=256):
    t, d = h.shape
    tm = min(tm, t)
    ng, ch, _ = sgu_w.shape
    width = w_in.shape[1] // 2
    gw = width // ng
    if single:
        sw = jnp.repeat(sgu_w[:, 0, 0], gw)[None, :]
        sb = jnp.repeat(sgu_b[:, 0], gw)[None, :]
    else:
        sw = (sgu_w * jnp.tril(jnp.ones((ch, ch), sgu_w.dtype))).astype(BF16)
        sb = jnp.repeat(sgu_b.T, gw, axis=1)
    weights = (g_mix[None, :], w_in.astype(BF16), ln_g[None, :], ln_b[None, :], sw, sb,
               w_out.astype(BF16)) + tuple(tail_w)
    row = lambda w: pl.BlockSpec((tm, w), lambda i: (i, 0))
    return pl.pallas_call(
        functools.partial(_odd_kernel, sw_groups=ng, single=single), grid=(t // tm,),
        in_specs=[row(d), row(p.shape[1])] + [_const_spec(w.shape) for w in weights],
        out_specs=[row(d), row(width)],
        out_shape=[jax.ShapeDtypeStruct((t, d), F32), jax.ShapeDtypeStruct((t, width), F32)],
        compiler_params=_cparams(), name="odd_single" if single else "odd_prompt",
    )(h, p, *weights)


def kernel(x_prompt, x_sample, cache_kv, cache_win, state_conv, state_h, page_table, p_prompt, p_sample,
           rel_bias, g_mix, g_mlp, w_up, w_down, g_ple, w_ple_gate, w_ple_proj,
           w_in_even, w_out_even, conv_w, conv_b, rg_w_a, rg_b_a, rg_w_x, rg_b_x, rg_lambda,
           q_gain, k_gain, cmp_w_k, cmp_w_v, cmp_pe_k, cmp_pe_v,
           w_in_odd, ln_v_g, ln_v_b, sgu_w, sgu_b, w_out_odd):
    bsz, t, d = x_prompt.shape
    r = x_sample.shape[0]
    assert bsz == 1 and x_sample.shape[1] == 1
    depth = g_mix.shape[0]
    lw = conv_w.shape[-1]
    hd = q_gain.shape[-1]
    groups = cache_kv.shape[4]
    heads = rel_bias.shape[1]
    hpg = heads // groups
    qw, kvw = heads * hd, groups * hd
    dims = (lw, qw, kvw, hd, heads)
    wbuf = cache_win.shape[2]

    hp = x_prompt.reshape(t, d)
    hs = x_sample.reshape(r, d)
    kvp, kvs, wnp, wns, cvp, cvs, hhp, hhs, vvs = [], [], [], [], [], [], [], [], []
    for i in range(depth):
        tail_w = _tail_weights(g_mlp[i], w_up[i], w_down[i], g_ple[i], w_ple_gate[i], w_ple_proj[i])
        if i % 2 == 0:
            e = i // 2
            lp = {'cmp_w_k': cmp_w_k[e], 'cmp_w_v': cmp_w_v[e], 'pe_k': cmp_pe_k[e], 'pe_v': cmp_pe_v[e],
                  'k_gain': k_gain[e]}
            rg_w = _rglru_weights(conv_w[e], conv_b[e], rg_w_a[e], rg_b_a[e], rg_w_x[e], rg_b_x[e],
                                  rg_lambda[e])
            (xa, ga, kv_rows, win_rows, kcmp, vcmp, ks, kw, qT, vsT, vwT, gT) = _even_in_prompt(
                hp, g_mix[i], w_in_even[e], q_gain[e], k_gain[e], dims)
            ya, h_last = _rglru_prompt(xa, ga, rg_w)
            kc, vcT = _compress_prompt(kcmp, vcmp, cmp_w_k[e], cmp_w_v[e], cmp_pe_k[e], cmp_pe_v[e],
                                       k_gain[e, 0], groups, hd)
            yb = _nsa_prompt(qT, gT, kc, vcT, ks, vsT, kw, vwT, rel_bias, groups, hpg, hd)
            hp = _even_tail(hp, ya, yb, p_prompt[i, 0], w_out_even[e], tail_w)
            kvp.append(kv_rows.reshape(1, t, 4, groups, hd))
            wnp.append(win_rows[t - wbuf:].reshape(1, wbuf, 2, groups, hd))
            cvp.append(xa[t - (conv_w.shape[1] - 1):][None])
            hhp.append(h_last)
            xa_s, ga_s, q_s, kv_s, win_s, gate_s = _even_in_sample(
                hs, g_mix[i], w_in_even[e], q_gain[e], k_gain[e], dims)
            ya_s, h_s = _rglru_sample(xa_s, ga_s, state_conv[e], state_h[e], rg_w)
            yb_s = _nsa_sample(q_s, gate_s, kv_s, win_s, cache_kv[e], cache_win[e], page_table, lp,
                               rel_bias, groups, hpg, hd)
            hs = _even_tail(hs, ya_s, yb_s, p_sample[i, :, 0], w_out_even[e], tail_w)
            kvs.append(kv_s.reshape(r, 1, 4, groups, hd))
            wns.append(jnp.concatenate([cache_win[e][:, 1:], win_s.reshape(r, 1, 2, groups, hd)], axis=1))
            cvs.append(jnp.concatenate([state_conv[e][:, 1:], xa_s[:, None, :]], axis=1))
            hhs.append(h_s)
        else:
            o = i // 2
            hp, _ = _odd_layer(hp, p_prompt[i, 0], g_mix[i], w_in_odd[o], ln_v_g[o], ln_v_b[o],
                               sgu_w[o], sgu_b[o], w_out_odd[o], tail_w, single=False)
            hs, v_new = _odd_layer(hs, p_sample[i, :, 0], g_mix[i], w_in_odd[o], ln_v_g[o], ln_v_b[o],
                                   sgu_w[o], sgu_b[o], w_out_odd[o], tail_w, single=True)
            vvs.append(v_new[:, None, :])
    v_sample = jnp.stack(vvs) if vvs else jnp.zeros((0, r, 1, w_in_odd.shape[-1] // 2), F32)
    return (hp[None], hs[:, None, :], jnp.stack(kvp), jnp.stack(kvs), jnp.stack(wnp), jnp.stack(wns),
            jnp.stack(cvp), jnp.stack(cvs), jnp.stack(hhp), jnp.stack(hhs), v_sample)
```

```python
import functools
import math

import numpy as np
import jax
import jax.numpy as jnp
from jax import lax
from jax.experimental import pallas as pl
from jax.experimental.pallas import tpu as pltpu

F32 = jnp.float32
BF16 = jnp.bfloat16

EPS = 1e-6
NEG = -1e30
RG_C = 8.0
CMP_BLOCK = 32
SEL_BLOCK = 64
TOP_N = 16
WINDOW = 512
Q_BLOCK = 128
FORCE_SCORE = 1.0e4
MAX_DIST = 128
CHUNK = 128

LANES = 128
VMEM_LIMIT = 56 * 1024 * 1024

LOG2E = math.log2(math.e)
KEY_GROUP = 512
ACC_ROWS = 80
NKS_SAMPLE = 2048


def _cparams(n_axes=1):
    return pltpu.CompilerParams(dimension_semantics=("arbitrary",) * n_axes,
                                vmem_limit_bytes=VMEM_LIMIT)


def _const_spec(shape):
    nd = len(shape)
    return pl.BlockSpec(shape, lambda *_: (0,) * nd, pipeline_mode=pl.Buffered(1))


def _dot(a, b):
    return jnp.dot(a, b, preferred_element_type=F32)


def _dot_nt(a, b):
    return lax.dot_general(a, b, (((1,), (1,)), ((), ())), preferred_element_type=F32)


def _gelu(x):
    return 0.5 * x * (1.0 + jnp.tanh(math.sqrt(2.0 / math.pi) * (x + 0.044715 * (x * x * x))))


def _softplus(x):
    return jnp.maximum(x, 0.0) + jnp.log1p(jnp.exp(-jnp.abs(x)))


def _rms_rows(x, g):
    return x * lax.rsqrt(jnp.mean(x * x, axis=-1, keepdims=True) + EPS) * g


def _group_rmsnorm(x, gain, width):
    n = x.shape[-1] // width
    lane = lax.broadcasted_iota(jnp.int32, x.shape, 1)
    x2 = x * x
    ms = jnp.zeros_like(x)
    for i in range(n):
        inside = (lane >= i * width) & (lane < (i + 1) * width)
        s = jnp.sum(jnp.where(inside, x2, 0.0), axis=-1, keepdims=True)
        ms = jnp.where(inside, s, ms)
    return x * lax.rsqrt(ms * (1.0 / width) + EPS) * gain


def _bucket_np(dist, n_buckets):
    n = np.maximum(dist, 0)
    exact = n_buckets // 2
    nf = np.maximum(n, 1).astype(np.float32)
    large = exact + (np.log(nf / np.float32(exact)) / np.float32(math.log(MAX_DIST / exact))
                     * np.float32(n_buckets - exact)).astype(np.int32)
    return np.where(n < exact, n, np.minimum(large, n_buckets - 1)).astype(np.int32)


def _even_in_prompt_kernel(x_ref, g_ref, wn_ref, wt_ref, qg_ref, kg_ref,
                           xa_ref, ga_ref, kv_ref, win_ref, kc_ref, vc_ref,
                           ks_ref, kw_ref, qT_ref, vsT_ref, vwT_ref, gT_ref, *, lw, qw, kvw, hd):
    tm = x_ref.shape[0]
    hn = _rms_rows(x_ref[...], g_ref[...]).astype(BF16)
    zn = _dot(hn, wn_ref[...])
    xa_ref[...] = zn[:, :lw]
    ga_ref[...] = zn[:, lw:2 * lw]
    kv = [zn[:, 2 * lw + i * kvw: 2 * lw + (i + 1) * kvw] for i in range(6)]
    k_sel = _group_rmsnorm(kv[2], kg_ref[1:2, :], hd)
    k_win = _group_rmsnorm(kv[4], kg_ref[2:3, :], hd)
    kv_ref[:, 0 * kvw:1 * kvw] = kv[0]
    kv_ref[:, 1 * kvw:2 * kvw] = kv[1]
    kv_ref[:, 2 * kvw:3 * kvw] = k_sel
    kv_ref[:, 3 * kvw:4 * kvw] = kv[3]
    win_ref[:, :kvw] = k_win
    win_ref[:, kvw:] = kv[5]
    kc_ref[...] = kv[0]
    vc_ref[...] = kv[1]
    blk = lax.broadcasted_iota(jnp.int32, (tm, kvw), 0) // SEL_BLOCK
    col = lax.broadcasted_iota(jnp.int32, (tm, kvw), 1)
    ks_ref[:, :kvw] = k_sel.astype(BF16)
    ks_ref[:, kvw:] = jnp.where(col == blk, 1.0, 0.0).astype(BF16)
    kw_ref[...] = k_win.astype(BF16)

    zt = _dot_nt(wt_ref[...], hn)
    for h in range(qw // hd):
        blk = zt[h * hd:(h + 1) * hd, :]
        r = lax.rsqrt(jnp.mean(blk * blk, axis=0, keepdims=True) + EPS)
        qn = blk * r
        for j in range(0, tm, LANES):
            qT_ref[h * hd:(h + 1) * hd, j:j + LANES] = (
                qn[:, j:j + LANES] * qg_ref[h * hd:(h + 1) * hd, :]).astype(BF16)
    pad = ACC_ROWS - hd
    ones_rows = jnp.where(lax.broadcasted_iota(jnp.int32, (pad, tm), 0) == 0, 1.0, 0.0).astype(BF16)
    for out_ref, base in ((vsT_ref, qw), (vwT_ref, qw + kvw)):
        for g in range(kvw // hd):
            out_ref[g * ACC_ROWS:g * ACC_ROWS + hd, :] = zt[base + g * hd:base + (g + 1) * hd, :].astype(BF16)
            out_ref[g * ACC_ROWS + hd:(g + 1) * ACC_ROWS, :] = ones_rows
    gT_ref[...] = jax.nn.sigmoid(zt[qw + 2 * kvw:, :])


def _even_in_prompt(x, g_mix, w_in, q_gain, k_gain, dims, tm=512):
    t, d = x.shape
    lw, qw, kvw, hd, heads = dims
    tm = min(tm, t)
    c0, c1, c2 = 2 * lw, 2 * lw + qw, 2 * lw + qw + 6 * kvw
    wn = jnp.concatenate([w_in[:, :c0], w_in[:, c1:c2]], axis=1).astype(BF16)
    wg = w_in[:, c2:c2 + 3 * heads].reshape(d, heads, 3).transpose(0, 2, 1).reshape(d, 3 * heads)
    wt = jnp.concatenate([w_in[:, c0:c1], w_in[:, c1 + 3 * kvw:c1 + 4 * kvw],
                          w_in[:, c1 + 5 * kvw:c1 + 6 * kvw], wg,
                          jnp.zeros((d, 32 - 3 * heads), F32)], axis=1).T.astype(BF16)
    assert tm % KEY_GROUP == 0
    qg = jnp.broadcast_to((jnp.tile(q_gain, heads) * (hd ** -0.5 * LOG2E))[:, None], (qw, LANES)).astype(F32)
    vrows = (kvw // hd) * ACC_ROWS
    kg = jnp.tile(k_gain, (1, kvw // hd)).astype(F32)
    kg = jnp.concatenate([kg, jnp.zeros((5, kvw), F32)], axis=0)
    nt = wt.shape[0]
    row = lambda w: pl.BlockSpec((tm, w), lambda i: (i, 0))
    col = lambda r: pl.BlockSpec((r, tm), lambda i: (0, i))
    outs = pl.pallas_call(
        functools.partial(_even_in_prompt_kernel, lw=lw, qw=qw, kvw=kvw, hd=hd),
        grid=(t // tm,),
        in_specs=[row(d), _const_spec((1, d)), _const_spec(wn.shape), _const_spec(wt.shape),
                  _const_spec(qg.shape), _const_spec(kg.shape)],
        out_specs=[row(lw), row(lw), row(4 * kvw), row(2 * kvw), row(kvw), row(kvw),
                   row(2 * kvw), row(kvw), col(qw), col(vrows), col(vrows), col(32)],
        out_shape=[jax.ShapeDtypeStruct((t, lw), F32), jax.ShapeDtypeStruct((t, lw), F32),
                   jax.ShapeDtypeStruct((t, 4 * kvw), F32), jax.ShapeDtypeStruct((t, 2 * kvw), F32),
                   jax.ShapeDtypeStruct((t, kvw), F32), jax.ShapeDtypeStruct((t, kvw), F32),
                   jax.ShapeDtypeStruct((t, 2 * kvw), BF16), jax.ShapeDtypeStruct((t, kvw), BF16),
                   jax.ShapeDtypeStruct((qw, t), BF16), jax.ShapeDtypeStruct((vrows, t), BF16),
                   jax.ShapeDtypeStruct((vrows, t), BF16), jax.ShapeDtypeStruct((32, t), F32)],
        compiler_params=_cparams(), name="even_in_prompt",
    )(x, g_mix[None, :], wn, wt, qg, kg)
    return outs


def _even_in_sample_kernel(x_ref, g_ref, w_ref, qg_ref, kg_ref,
                           xa_ref, ga_ref, q_ref, kv_ref, win_ref, gate_ref, *, lw, qw, kvw, hd):
    hn = _rms_rows(x_ref[...], g_ref[...]).astype(BF16)
    z = _dot(hn, w_ref[...])
    xa_ref[...] = z[:, :lw]
    ga_ref[...] = z[:, lw:2 * lw]
    q_ref[...] = _group_rmsnorm(z[:, 2 * lw:2 * lw + qw], qg_ref[...], hd)
    o = 2 * lw + qw
    kv = [z[:, o + i * kvw: o + (i + 1) * kvw] for i in range(6)]
    kv_ref[:, 0 * kvw:1 * kvw] = kv[0]
    kv_ref[:, 1 * kvw:2 * kvw] = kv[1]
    kv_ref[:, 2 * kvw:3 * kvw] = _group_rmsnorm(kv[2], kg_ref[1:2, :], hd)
    kv_ref[:, 3 * kvw:4 * kvw] = kv[3]
    win_ref[:, :kvw] = _group_rmsnorm(kv[4], kg_ref[2:3, :], hd)
    win_ref[:, kvw:] = kv[5]
    gate_ref[...] = jax.nn.sigmoid(z[:, o + 6 * kvw:])


def _even_in_sample(x, g_mix, w_in, q_gain, k_gain, dims):
    r, d = x.shape
    lw, qw, kvw, hd, heads = dims
    n_in = w_in.shape[1]
    pad = (-n_in) % LANES
    w = jnp.pad(w_in, ((0, 0), (0, pad))).astype(BF16)
    gw = n_in + pad - (2 * lw + qw + 6 * kvw)
    qg = (jnp.tile(q_gain, heads) * hd ** -0.5)[None, :].astype(F32)
    kg = jnp.tile(k_gain, (1, kvw // hd)).astype(F32)
    kg = jnp.concatenate([kg, jnp.zeros((5, kvw), F32)], axis=0)
    full = lambda s: pl.BlockSpec(s, lambda i: (0,) * len(s))
    return pl.pallas_call(
        functools.partial(_even_in_sample_kernel, lw=lw, qw=qw, kvw=kvw, hd=hd),
        grid=(1,),
        in_specs=[full((r, d)), full((1, d)), full(w.shape), full(qg.shape), full(kg.shape)],
        out_specs=[full((r, lw)), full((r, lw)), full((r, qw)), full((r, 4 * kvw)),
                   full((r, 2 * kvw)), full((r, gw))],
        out_shape=[jax.ShapeDtypeStruct((r, lw), F32), jax.ShapeDtypeStruct((r, lw), F32),
                   jax.ShapeDtypeStruct((r, qw), F32), jax.ShapeDtypeStruct((r, 4 * kvw), F32),
                   jax.ShapeDtypeStruct((r, 2 * kvw), F32), jax.ShapeDtypeStruct((r, gw), F32)],
        compiler_params=_cparams(), name="even_in_sample",
    )(x, g_mix[None, :], w, qg, kg)


def _rglru_gates(conv, wa_ref, ba_ref, wx_ref, bx_ref, lam_ref):
    cb = conv.astype(BF16)
    r = jax.nn.sigmoid(_dot(cb, wa_ref[...]) + ba_ref[...])
    i = jax.nn.sigmoid(_dot(cb, wx_ref[...]) + bx_ref[...])
    log_a = -RG_C * r * _softplus(-lam_ref[...])
    a = jnp.exp(log_a)
    th = jnp.tanh(log_a)
    b = jnp.sqrt(-2.0 * th / (1.0 - th)) * (i * conv)
    return a, b


def _rglru_prompt_kernel(xa_ref, ga_ref, cw_ref, cb_ref, wa_ref, ba_ref, wx_ref, bx_ref, lam_ref,
                         ya_ref, hl_ref, xext_ref, h_ref):
    tr = xa_ref.shape[0]

    @pl.when(pl.program_id(0) == 0)
    def _():
        xext_ref[0:8, :] = jnp.zeros((8, xext_ref.shape[1]), F32)
        h_ref[...] = jnp.zeros(h_ref.shape, F32)

    x = xa_ref[...]
    xext_ref[8:8 + tr, :] = x
    conv = (cb_ref[...] + cw_ref[3:4, :] * x + cw_ref[2:3, :] * xext_ref[7:7 + tr, :]
            + cw_ref[1:2, :] * xext_ref[6:6 + tr, :] + cw_ref[0:1, :] * xext_ref[5:5 + tr, :])
    xext_ref[0:8, :] = xext_ref[tr:tr + 8, :]
    a, b = _rglru_gates(conv, wa_ref, ba_ref, wx_ref, bx_ref, lam_ref)
    sub = 8
    lw = a.shape[1]
    a = a.reshape(tr // sub, sub, lw)
    b = b.reshape(tr // sub, sub, lw)
    pos = lax.broadcasted_iota(jnp.int32, a.shape, 1)
    s = 1
    while s < sub:
        keep = pos >= s
        a_sh = jnp.where(keep, pltpu.roll(a, s, 1), 1.0)
        b_sh = jnp.where(keep, pltpu.roll(b, s, 1), 0.0)
        b = a * b_sh + b
        a = a * a_sh
        s *= 2
    h_in = h_ref[0:1, :]
    tiles = []
    for i in range(tr // sub):
        h_tile = a[i] * h_in + b[i]
        tiles.append(h_tile)
        h_in = h_tile[sub - 1:sub, :]
    hs = jnp.concatenate(tiles, axis=0)
    h_last = h_in
    h_ref[...] = jnp.broadcast_to(h_last, h_ref.shape)
    hl_ref[...] = h_last
    ya_ref[...] = (hs * _gelu(ga_ref[...])).astype(BF16)


def _block_diag(w):
    n, a, b = w.shape
    eye = jnp.eye(n, dtype=w.dtype)
    return (eye[:, None, :, None] * w[:, :, None, :]).reshape(n * a, n * b)


def _rglru_weights(conv_w, conv_b, w_a, b_a, w_x, b_x, lam):
    return (conv_w.astype(F32), conv_b[None, :], _block_diag(w_a).astype(BF16), b_a[None, :],
            _block_diag(w_x).astype(BF16), b_x[None, :], lam[None, :])


def _rglru_prompt(xa, ga, weights, tr=256):
    t, lw = xa.shape
    tr = min(tr, t)
    row = pl.BlockSpec((tr, lw), lambda i: (i, 0))
    return pl.pallas_call(
        _rglru_prompt_kernel, grid=(t // tr,),
        in_specs=[row, row] + [_const_spec(w.shape) for w in weights],
        out_specs=[row, pl.BlockSpec((1, lw), lambda i: (0, 0))],
        out_shape=[jax.ShapeDtypeStruct((t, lw), BF16), jax.ShapeDtypeStruct((1, lw), F32)],
        scratch_shapes=[pltpu.VMEM((tr + 8, lw), F32), pltpu.VMEM((8, lw), F32)],
        compiler_params=_cparams(), name="rglru_prompt",
    )(xa, ga, *weights)


def _rglru_sample_kernel(xa_ref, ga_ref, sc_ref, h0_ref, cw_ref, cb_ref, wa_ref, ba_ref, wx_ref,
                         bx_ref, lam_ref, ya_ref, h_ref):
    conv = (cb_ref[...] + cw_ref[3:4, :] * xa_ref[...] + cw_ref[2:3, :] * sc_ref[2]
            + cw_ref[1:2, :] * sc_ref[1] + cw_ref[0:1, :] * sc_ref[0])
    a, b = _rglru_gates(conv, wa_ref, ba_ref, wx_ref, bx_ref, lam_ref)
    h = a * h0_ref[...] + b
    h_ref[...] = h
    ya_ref[...] = (h * _gelu(ga_ref[...])).astype(BF16)


def _rglru_sample(xa, ga, state_conv, state_h, weights):
    r, lw = xa.shape
    sc = jnp.transpose(state_conv, (1, 0, 2))
    args = (xa, ga, sc, state_h) + tuple(weights)
    full = lambda s: pl.BlockSpec(s, lambda i: (0,) * len(s))
    return pl.pallas_call(
        _rglru_sample_kernel, grid=(1,),
        in_specs=[full(a.shape) for a in args],
        out_specs=[full((r, lw)), full((r, lw))],
        out_shape=[jax.ShapeDtypeStruct((r, lw), BF16), jax.ShapeDtypeStruct((r, lw), F32)],
        compiler_params=_cparams(), name="rglru_sample",
    )(*args)


def _compress_prompt_kernel(xk_ref, xv_ref, pek_ref, pev_ref, wk_ref, wvT_ref, kg_ref,
                            kc_ref, vcT_ref, *, hd):
    xk = (xk_ref[...] + pek_ref[...]).astype(BF16)
    kc = _dot(xk, wk_ref[...])
    kc_ref[...] = _group_rmsnorm(kc, kg_ref[...], hd).astype(BF16)
    xv = (xv_ref[...] + pev_ref[...]).astype(BF16)
    vcT_ref[...] = _dot_nt(wvT_ref[...], xv).astype(BF16)


def _compress_weight(w, groups, hd):
    w3 = w.reshape(CMP_BLOCK, hd, hd)
    eye = jnp.eye(groups, dtype=w.dtype)
    full = eye[None, :, None, :, None] * w3[:, None, :, None, :]
    return full.reshape(CMP_BLOCK * groups * hd, groups * hd)


def _compress_prompt(kcmp, vcmp, w_k, w_v, pe_k, pe_v, k_gain_cmp, groups, hd):
    t, kvw = kcmp.shape
    ncb = t // CMP_BLOCK
    kdim = CMP_BLOCK * kvw
    xk = kcmp.reshape(ncb, kdim)
    xv = vcmp.reshape(ncb, kdim)
    wk = _compress_weight(w_k, groups, hd).astype(BF16)
    wvT = _compress_weight(w_v, groups, hd).T.astype(BF16)
    pek = jnp.tile(pe_k, (1, groups)).reshape(1, kdim)
    pev = jnp.tile(pe_v, (1, groups)).reshape(1, kdim)
    kg = jnp.tile(k_gain_cmp, groups)[None, :]
    nb = min(LANES, ncb)
    return pl.pallas_call(
        functools.partial(_compress_prompt_kernel, hd=hd), grid=(ncb // nb,),
        in_specs=[pl.BlockSpec((nb, kdim), lambda i: (i, 0)), pl.BlockSpec((nb, kdim), lambda i: (i, 0)),
                  _const_spec((1, kdim)), _const_spec((1, kdim)), _const_spec(wk.shape),
                  _const_spec(wvT.shape), _const_spec((1, kvw))],
        out_specs=[pl.BlockSpec((nb, kvw), lambda i: (i, 0)), pl.BlockSpec((kvw, nb), lambda i: (0, i))],
        out_shape=[jax.ShapeDtypeStruct((ncb, kvw), BF16), jax.ShapeDtypeStruct((kvw, ncb), BF16)],
        compiler_params=_cparams(), name="compress_prompt",
    )(xk, xv, pek, pev, wk, wvT, kg)


def _tile_lanes(x, n):
    return jnp.concatenate([x] * n, axis=1)


def _attend(s, shift, vT, slot, g, m_ref, acc_ref, m_blk=None):
    m_old = m_ref[slot, g]
    if m_blk is None:
        m_blk = jnp.max(s, axis=0, keepdims=True)
    if shift is not None:
        m_blk = m_blk + shift
    m_new = jnp.maximum(m_old, m_blk)
    p = jnp.exp2(s - (m_new if shift is None else m_new - shift)).astype(BF16)
    acc_ref[slot, g] = jnp.exp2(m_old - m_new) * acc_ref[slot, g] + _dot(vT, p)
    m_ref[slot, g] = m_new


def _bias_table_kernel(id_ref, val_ref, o_ref, *, reps):
    ids = _tile_lanes(id_ref[...], reps)
    out = jnp.zeros(ids.shape, F32)
    for k in range(val_ref.shape[0]):
        out = jnp.where(ids == k, val_ref[k:k + 1, :], out)
    o_ref[...] = out


def _bias_table(ids, vals):
    r = ids.shape[0]
    reps = vals.shape[1] // LANES
    full = lambda s: pl.BlockSpec(s, lambda i: (0,) * len(s))
    return pl.pallas_call(
        functools.partial(_bias_table_kernel, reps=reps), grid=(1,),
        in_specs=[full(ids.shape), full(vals.shape)], out_specs=full((r, vals.shape[1])),
        out_shape=jax.ShapeDtypeStruct((r, vals.shape[1]), F32),
        compiler_params=_cparams(), name="bias_table",
    )(jnp.asarray(ids, jnp.int32), vals)


def _nsa_prompt_kernel(qT_ref, gT_ref, kc_ref, vcT_ref, ks_ref, vsT_ref, kw_ref, vwT_ref,
                       b31_ref, tab_ref,
                       y_ref,
                       qaug_ref, sa_ref, sb_ref, sc_ref, p4_ref, selb_ref, ocmp_ref, m_ref, acc_ref,
                       *, groups, hpg, hd):
    qb = pl.program_id(0)
    t0 = qb * Q_BLOCK
    ncb = kc_ref.shape[0]
    nsb = ncb * CMP_BLOCK // SEL_BLOCK
    gl = hpg * Q_BLOCK
    n_top = min(TOP_N, nsb)
    kvw = groups * hd
    bpg = KEY_GROUP // SEL_BLOCK

    rows = []
    for g in range(groups):
        top = jnp.concatenate([qT_ref[(g * hpg + hh) * hd:(g * hpg + hh + 1) * hd, :]
                               for hh in range(hpg)], axis=1)
        z = jnp.zeros_like(top)
        rows.append(jnp.concatenate([top if gg == g else z for gg in range(groups)], axis=1))
    qbd = jnp.concatenate(rows, axis=0)
    qaug_ref[0:kvw, :] = qbd
    qaug_ref[kvw:2 * kvw, :] = jnp.zeros((kvw, groups * gl), BF16)
    b31 = b31_ref[...]

    def compress_and_select(rc):
        nsv = rc * CMP_BLOCK // SEL_BLOCK
        jrow = lax.broadcasted_iota(jnp.int32, (rc, Q_BLOCK), 0)
        qcol = lax.broadcasted_iota(jnp.int32, (rc, Q_BLOCK), 1)
        vis = _tile_lanes(jrow * CMP_BLOCK + (CMP_BLOCK - 1) <= t0 + qcol, groups * hpg)
        s_c = _dot(kc_ref[0:rc, :], qbd) + b31
        sc_ref[0:rc, :] = jnp.where(vis, s_c, NEG)
        odd = (qb % 2) == 1
        v_a = jnp.where(qb == 0, 0, jnp.where(odd, 1, 2))
        w_a = pl.multiple_of(jnp.where(qb == 0, 0, jnp.where(odd, 4 * qb - 4, 4 * qb - 8)), 8)
        v_b = jnp.where((qb > 0) & jnp.logical_not(odd), 0, 3)
        w_b = pl.multiple_of(jnp.where((qb > 0) & jnp.logical_not(odd), 4 * qb, 0), 8)
        sc_ref[pl.ds(w_a, 8), :] = sc_ref[pl.ds(w_a, 8), :] + tab_ref[pl.ds(pl.multiple_of(v_a * 8, 8), 8), :]
        sc_ref[pl.ds(w_b, 8), :] = sc_ref[pl.ds(w_b, 8), :] + tab_ref[pl.ds(pl.multiple_of(v_b * 8, 8), 8), :]
        s_c = sc_ref[0:rc, :]
        m_c = jnp.max(s_c, axis=0, keepdims=True)
        p_c = jnp.where(vis, jnp.exp2(s_c - m_c), 0.0)
        l_c = jnp.sum(p_c, axis=0, keepdims=True)
        p_c = p_c * jnp.where(l_c > 0.0, 1.0 / l_c, 0.0)
        for g in range(groups):
            pg = p_c[:, g * gl:(g + 1) * gl]
            ocmp_ref[g] = _dot(vcT_ref[g * hd:(g + 1) * hd, 0:rc], pg.astype(BF16))
            p4 = pg[:, 0:Q_BLOCK]
            for hh in range(1, hpg):
                p4 = p4 + pg[:, hh * Q_BLOCK:(hh + 1) * Q_BLOCK]
            p4_ref[g, 0:rc, :] = p4

        ratio = SEL_BLOCK // CMP_BLOCK
        jb = lax.broadcasted_iota(jnp.int32, (nsv, Q_BLOCK), 0)
        qc = lax.broadcasted_iota(jnp.int32, (nsv, Q_BLOCK), 1)
        cur = (t0 + qc) // SEL_BLOCK
        forced = (jb == 0) | (jb == cur) | (jb == cur - 1)
        for g in range(groups):
            imp = p4_ref[g, pl.ds(0, nsv, stride=ratio), :]
            for k in range(1, ratio):
                imp = imp + p4_ref[g, pl.ds(k, nsv, stride=ratio), :]
            imp = jnp.where(forced, FORCE_SCORE, imp)
            imp = jnp.where(jb <= cur, imp, -1.0)
            selb = jnp.full((nsv, Q_BLOCK), NEG, F32)
            for _ in range(n_top):
                mx = jnp.max(imp, axis=0, keepdims=True)
                idx = jnp.min(jnp.where(imp == mx, jb, nsv), axis=0, keepdims=True)
                hit = jb == idx
                selb = jnp.where(hit, 0.0, selb)
                imp = jnp.where(hit, -3.0e38, imp)
            selb_ref[g, 0:nsv, :] = selb
            if nsv < nsb:
                selb_ref[g, nsv:nsb, :] = jnp.full((nsb - nsv, Q_BLOCK), NEG, F32)

    tiers = list(range(LANES, ncb + 1, LANES)) if ncb % LANES == 0 else [ncb]
    n_vis = (t0 + Q_BLOCK) // CMP_BLOCK
    for i, rc in enumerate(tiers):
        lo = tiers[i - 1] if i else 0

        @pl.when((n_vis > lo) & (n_vis <= rc))
        def _(rc=rc):
            compress_and_select(rc)

    m_ref[...] = jnp.full(m_ref.shape, NEG, F32)
    acc_ref[...] = jnp.zeros(acc_ref.shape, F32)

    def put_mask_rows(g, rows8):
        tile = jnp.concatenate([rows8, jnp.zeros_like(rows8)], axis=0).astype(BF16)
        for hh in range(hpg):
            qaug_ref[kvw:kvw + 2 * bpg, g * gl + hh * Q_BLOCK:g * gl + (hh + 1) * Q_BLOCK] = tile

    def far_scores(c, dst_ref):
        for g in range(groups):
            put_mask_rows(g, selb_ref[g, pl.ds(pl.multiple_of(c * bpg, bpg), bpg), :])
        k0 = pl.multiple_of(c * KEY_GROUP, KEY_GROUP)
        s = _dot(ks_ref[pl.ds(k0, KEY_GROUP), :], qaug_ref[...])
        dst_ref[0:KEY_GROUP, :] = s
        dst_ref[KEY_GROUP:KEY_GROUP + 8, :] = jnp.broadcast_to(jnp.max(s, axis=0, keepdims=True),
                                                               (8, groups * gl))

    def far_attend(c, src_ref, live_rows=None):
        k0 = pl.multiple_of(c * KEY_GROUP, KEY_GROUP)
        drop = None
        if live_rows is not None:
            drop = lax.broadcasted_iota(jnp.int32, (KEY_GROUP, Q_BLOCK), 0) >= live_rows
        for g in range(groups):
            sg = src_ref[0:KEY_GROUP, g * gl:(g + 1) * gl]
            m_blk = src_ref[KEY_GROUP:KEY_GROUP + 1, g * gl:(g + 1) * gl]
            if drop is not None:
                sg = jnp.where(_tile_lanes(drop, hpg), NEG, sg)
                m_blk = None
            _attend(sg, b31[:, g * gl:(g + 1) * gl],
                    vsT_ref[g * ACC_ROWS:(g + 1) * ACC_ROWS, pl.ds(k0, KEY_GROUP)], 0, g, m_ref, acc_ref,
                    m_blk=m_blk)

    far_len = jnp.maximum(t0 - Q_BLOCK, 0)
    n_far = far_len // KEY_GROUP
    live_tail = far_len - n_far * KEY_GROUP
    far_scores(0, sa_ref)

    def far_pair(j, carry):
        far_scores(2 * j + 1, sb_ref)
        far_attend(2 * j, sa_ref)
        far_scores(2 * j + 2, sa_ref)
        far_attend(2 * j + 1, sb_ref)
        return carry

    lax.fori_loop(0, n_far // 2, far_pair, 0)
    odd_far = n_far % 2 == 1

    @pl.when(odd_far)
    def _():
        far_scores(n_far, sb_ref)
        far_attend(n_far - 1, sa_ref)

    @pl.when(odd_far & (live_tail > 0))
    def _():
        far_attend(n_far, sb_ref, live_tail)

    @pl.when(jnp.logical_not(odd_far) & (live_tail > 0))
    def _():
        far_attend(n_far, sa_ref, live_tail)

    near_tab = tab_ref[_TAB_NEAR:_TAB_NEAR + 2 * Q_BLOCK, :]

    def near_piece():
        b0 = (t0 - Q_BLOCK) // SEL_BLOCK
        base = (b0 // bpg) * bpg
        nxt = jnp.minimum(base + bpg, nsb - bpg)
        jrow8 = lax.broadcasted_iota(jnp.int32, (bpg, Q_BLOCK), 0)
        for g in range(groups):
            lo = selb_ref[g, pl.ds(pl.multiple_of(base, bpg), bpg), :]
            hi = selb_ref[g, pl.ds(pl.multiple_of(nxt, bpg), bpg), :]
            put_mask_rows(g, jnp.where(jrow8 >= b0 - base, lo, hi))
        k0 = pl.multiple_of(t0 - Q_BLOCK, Q_BLOCK)
        s = _dot(ks_ref[pl.ds(k0, 2 * Q_BLOCK), :], qaug_ref[...]) + near_tab
        for g in range(groups):
            _attend(s[:, g * gl:(g + 1) * gl], None,
                    vsT_ref[g * ACC_ROWS:(g + 1) * ACC_ROWS, pl.ds(k0, 2 * Q_BLOCK)], 0, g, m_ref, acc_ref)

    @pl.when(qb == 0)
    def _():
        for g in range(groups):
            put_mask_rows(g, selb_ref[g, 0:bpg, :])
        s = _dot(ks_ref[0:Q_BLOCK, :], qaug_ref[...]) + near_tab[Q_BLOCK:2 * Q_BLOCK, :]
        for g in range(groups):
            _attend(s[:, g * gl:(g + 1) * gl], None,
                    vsT_ref[g * ACC_ROWS:(g + 1) * ACC_ROWS, 0:Q_BLOCK], 0, g, m_ref, acc_ref)

    n_wk = WINDOW + Q_BLOCK

    def window_block():
        k0 = pl.multiple_of(t0 - WINDOW, Q_BLOCK)
        s = _dot(kw_ref[pl.ds(k0, n_wk), :], qaug_ref[0:kvw, :]) + tab_ref[_TAB_WIN:_TAB_WIN + n_wk, :]
        for g in range(groups):
            _attend(s[:, g * gl:(g + 1) * gl], None,
                    vwT_ref[g * ACC_ROWS:(g + 1) * ACC_ROWS, pl.ds(k0, n_wk)], 1, g, m_ref, acc_ref)

    @pl.when(t0 >= WINDOW)
    def _():
        near_piece()
        window_block()

    @pl.when((qb >= 1) & (t0 < WINDOW))
    def _():
        near_piece()

    for i in range(WINDOW // Q_BLOCK):
        k0 = t0 - i * Q_BLOCK

        @pl.when((t0 < WINDOW) & (k0 >= 0))
        def _(i=i, k0=k0):
            k0a = pl.multiple_of(k0, Q_BLOCK)
            r0 = _TAB_WIN + WINDOW - i * Q_BLOCK
            s = _dot(kw_ref[pl.ds(k0a, Q_BLOCK), :], qaug_ref[0:kvw, :]) + tab_ref[r0:r0 + Q_BLOCK, :]
            for g in range(groups):
                _attend(s[:, g * gl:(g + 1) * gl], None,
                        vwT_ref[g * ACC_ROWS:(g + 1) * ACC_ROWS, pl.ds(k0a, Q_BLOCK)], 1, g, m_ref, acc_ref)

    heads = groups * hpg
    pieces = []
    for g in range(groups):
        gate = [jnp.concatenate([gT_ref[j * heads + g * hpg + hh: j * heads + g * hpg + hh + 1, :]
                                 for hh in range(hpg)], axis=1) for j in range(3)]
        o = gate[0] * ocmp_ref[g]
        for slot in range(2):
            acc = acc_ref[slot, g]
            o = o + gate[1 + slot] * (acc[0:hd, :] * (1.0 / acc[hd:hd + 1, :]))
        for hh in range(hpg):
            pieces.append(o[:, hh * Q_BLOCK:(hh + 1) * Q_BLOCK])
    y_ref[...] = jnp.concatenate(pieces, axis=0).T.astype(BF16)


_TAB_NEAR = 32
_TAB_WIN = _TAB_NEAR + 2 * Q_BLOCK


def _nsa_prompt_tables(rel_bias, n_lanes):
    nb, heads = rel_bias.shape
    tbl = rel_bias.astype(F32) * LOG2E
    lanes = lambda x: jnp.repeat(x, n_lanes // heads, axis=1)
    vals = jnp.concatenate([lanes(tbl), jnp.full((1, n_lanes), NEG, F32), jnp.zeros((1, n_lanes), F32),
                            lanes(tbl - tbl[nb - 1:nb])], axis=0)
    c = np.arange(Q_BLOCK)[None, :]
    i8 = np.arange(8)[:, None]
    ids = []
    for base in (-(CMP_BLOCK - 1), 3 * CMP_BLOCK + 1, 7 * CMP_BLOCK + 1):
        d = base + c - CMP_BLOCK * i8
        ids.append(np.where(d >= 0, nb + 2 + _bucket_np(d, nb), nb + 1))
    ids.append(np.full((8, Q_BLOCK), nb + 1))
    d = Q_BLOCK + c - np.arange(2 * Q_BLOCK)[:, None]
    ids.append(np.where(d >= 0, _bucket_np(d, nb), nb))
    d = WINDOW + c - np.arange(WINDOW + Q_BLOCK)[:, None]
    ids.append(np.where((d >= 0) & (d < WINDOW), _bucket_np(d, nb), nb))
    return vals[nb - 1:nb], _bias_table(np.concatenate(ids, axis=0), vals)


def _nsa_prompt(qT, gT, kc, vcT, ks, vsT, kw, vwT, rel_bias, groups, hpg, hd):
    qw, t = qT.shape
    ncb = kc.shape[0]
    nsb = t // SEL_BLOCK
    gl = hpg * Q_BLOCK
    kvw = groups * hd
    assert t % KEY_GROUP == 0 and Q_BLOCK == 4 * CMP_BLOCK and MAX_DIST <= Q_BLOCK
    tables = _nsa_prompt_tables(rel_bias, groups * gl)
    resident = (kc, vcT, ks, vsT, kw, vwT) + tuple(tables)
    return pl.pallas_call(
        functools.partial(_nsa_prompt_kernel, groups=groups, hpg=hpg, hd=hd),
        grid=(t // Q_BLOCK,),
        in_specs=[pl.BlockSpec((qw, Q_BLOCK), lambda i: (0, i)),
                  pl.BlockSpec((gT.shape[0], Q_BLOCK), lambda i: (0, i))]
                 + [_const_spec(a.shape) for a in resident],
        out_specs=pl.BlockSpec((Q_BLOCK, qw), lambda i: (i, 0)),
        out_shape=jax.ShapeDtypeStruct((t, qw), BF16),
        scratch_shapes=[pltpu.VMEM((2 * kvw, groups * gl), BF16),
                        pltpu.VMEM((KEY_GROUP + 8, groups * gl), F32),
                        pltpu.VMEM((KEY_GROUP + 8, groups * gl), F32),
                        pltpu.VMEM((ncb, groups * gl), F32),
                        pltpu.VMEM((groups, ncb, Q_BLOCK), F32),
                        pltpu.VMEM((groups, nsb, Q_BLOCK), F32),
                        pltpu.VMEM((groups, hd, gl), F32),
                        pltpu.VMEM((2, groups, 1, gl), F32),
                        pltpu.VMEM((2, groups, ACC_ROWS, gl), F32)],
        compiler_params=_cparams(), name="nsa_prompt",
    )(qT, gT, *resident)


def _split3(x):
    hi = x.astype(BF16)
    r1 = x - hi.astype(F32)
    mid = r1.astype(BF16)
    lo = (r1 - mid.astype(F32)).astype(BF16)
    return hi, mid, lo


def _nsa_sample_kernel(pt_ref, cache_ref, qbd_ref, qn_ref, gates_ref, kvn_ref, winn_ref, cwin_ref,
                       wkv_ref, pe_ref, perm_ref, kg_ref, gsum_ref,
                       cb_ref, sb_ref, wb_ref, b0_ref, e_ref,
                       o_ref,
                       ringa_ref, ringb_ref, bufkv_ref, sema, semb, p_ref,
                       *, hd, n_pages, page, chunk):
    b = pl.program_id(0)
    nb = pl.num_programs(0)
    kvw = kg_ref.shape[1]
    past = n_pages * page
    ncb = past // CMP_BLOCK
    nsb = past // SEL_BLOCK
    n_chunks = n_pages // chunk
    bpp = 2 * page // CMP_BLOCK

    def chunk_copy(bb, q, p):
        if q < n_chunks:
            return pltpu.make_async_copy(cache_ref.at[pt_ref[bb, q * chunk + p], 0:2 * kvw, :],
                                         ringa_ref.at[q % 2, p], sema.at[q % 2])
        ch = q - n_chunks
        return pltpu.make_async_copy(cache_ref.at[pt_ref[bb, ch * chunk + p], 2 * kvw:4 * kvw, :],
                                     ringb_ref.at[ch % 2, :, pl.ds(p * page, page)], semb.at[ch % 2])

    def for_pages(n, fn, unroll=1):
        def body(p, c):
            fn(p)
            return c
        lax.fori_loop(0, n, body, 0, unroll=unroll)

    def start_chunk(bb, q):
        for_pages(chunk, lambda p: chunk_copy(bb, q, p).start(), unroll=4)

    def wait_chunk(q):
        for_pages(chunk, lambda p: chunk_copy(b, q, p).wait(), unroll=4)

    def request_ahead(q):
        if q + 2 < 2 * n_chunks:
            start_chunk(b, q + 2)
        else:
            @pl.when(b + 1 < nb)
            def _():
                start_chunk(b + 1, q + 2 - 2 * n_chunks)

    @pl.when(b == 0)
    def _():
        start_chunk(0, 0)
        start_chunk(0, 1)

    qbd = qbd_ref[0]

    for ch in range(n_chunks):
        wait_chunk(ch)

        def regroup(q, ch=ch):
            xt = jnp.concatenate([ringa_ref[ch % 2, 2 * q], ringa_ref[ch % 2, 2 * q + 1]], axis=1)
            y = _dot_nt(perm_ref[...], (xt + pe_ref[...]).astype(BF16))
            blocks = pl.ds(pl.multiple_of((ch * (chunk // 2) + q) * bpp, bpp), bpp)
            for r in range(CMP_BLOCK):
                bufkv_ref[r, blocks, :] = y[r * bpp:(r + 1) * bpp, :]
        for_pages(chunk // 2, regroup, unroll=4)
        request_ahead(ch)

    half = ncb // 2
    zs = [jnp.zeros((half, 2 * kvw), F32), jnp.zeros((ncb - half, 2 * kvw), F32)]
    for r in range(CMP_BLOCK):
        zs[0] = zs[0] + _dot(bufkv_ref[r, 0:half, :].astype(BF16), wkv_ref[r])
        zs[1] = zs[1] + _dot(bufkv_ref[r, half:ncb, :].astype(BF16), wkv_ref[r])
    z = jnp.concatenate(zs, axis=0)
    vc = z[:, kvw:2 * kvw]
    kc = _group_rmsnorm(z[:, 0:kvw], kg_ref[...], hd).astype(BF16)
    s_c = _dot(kc, qbd) + cb_ref[...]
    m_c = jnp.max(s_c, axis=0, keepdims=True)
    p_c = jnp.exp(s_c - m_c)
    p_c = p_c * (1.0 / jnp.sum(p_c, axis=0, keepdims=True))
    o_cmp = _dot(p_c.T.astype(BF16), vc.astype(BF16))
    p_ref[...] = p_c
    ratio = SEL_BLOCK // CMP_BLOCK
    imp = p_ref[pl.ds(0, nsb, stride=ratio), :]
    for k in range(1, ratio):
        imp = imp + p_ref[pl.ds(k, nsb, stride=ratio), :]
    hi, mid, lo = _split3(imp)
    gs = gsum_ref[...]
    imp = _dot(hi, gs) + _dot(mid, gs) + _dot(lo, gs)
    jb = lax.broadcasted_iota(jnp.int32, (nsb, LANES), 0)
    imp = jnp.where((jb == 0) | (jb == nsb - 1), FORCE_SCORE, imp)
    selb = jnp.full((nsb, LANES), NEG, F32)
    for _ in range(min(TOP_N, nsb + 1) - 1):
        mx = jnp.max(imp, axis=0, keepdims=True)
        idx = jnp.min(jnp.where(imp == mx, jb, nsb), axis=0, keepdims=True)
        hit = jb == idx
        selb = jnp.where(hit, 0.0, selb)
        imp = jnp.where(hit, -3.0e38, imp)
    rows = qn_ref.shape[1]
    sel_nat = selb.T[0:rows, :]

    qn = qn_ref[0]
    qnb = qn.astype(BF16)
    kvn = kvn_ref[0]
    b0 = b0_ref[:, 0:1]
    m_s = jnp.sum(qn * kvn[:, 2 * kvw:3 * kvw], axis=1, keepdims=True) + b0
    l_s = jnp.ones_like(m_s)
    acc = jnp.broadcast_to(kvn[:, 3 * kvw:4 * kvw], (rows, kvw))
    nks = e_ref.shape[1]
    bps = nks // SEL_BLOCK
    per_chunk = chunk * page // nks
    for ch in range(n_chunks):
        wait_chunk(n_chunks + ch)
        for sub in range(per_chunk):
            c = ch * per_chunk + sub
            keys = slice(sub * nks, (sub + 1) * nks)
            rolled = sel_nat if c == 0 else pltpu.roll(sel_nat, nsb - c * bps, 1)
            mask = _dot(rolled.astype(BF16), e_ref[...])
            s = (_dot(qnb, ringb_ref[ch % 2, 0:kvw, keys].astype(BF16)) + mask
                 + sb_ref[:, c * nks:(c + 1) * nks])
            m_new = jnp.maximum(m_s, jnp.max(s, axis=1, keepdims=True))
            alpha = jnp.exp(m_s - m_new)
            p = jnp.exp(s - m_new)
            l_s = alpha * l_s + jnp.sum(p, axis=1, keepdims=True)
            acc = alpha * acc + _dot_nt(p.astype(BF16), ringb_ref[ch % 2, kvw:2 * kvw, keys].astype(BF16))
            m_s = m_new
        request_ahead(n_chunks + ch)
    o_sel = acc * (1.0 / l_s)

    winn = winn_ref[0]
    s_w = _dot(qnb, cwin_ref[0, 0:kvw, :].astype(BF16)) + wb_ref[...]
    s_wn = jnp.sum(qn * winn[:, 0:kvw], axis=1, keepdims=True) + b0
    m_w = jnp.maximum(jnp.max(s_w, axis=1, keepdims=True), s_wn)
    p_w = jnp.exp(s_w - m_w)
    p_wn = jnp.exp(s_wn - m_w)
    l_w = jnp.sum(p_w, axis=1, keepdims=True) + p_wn
    o_win = (_dot_nt(p_w.astype(BF16), cwin_ref[0, kvw:2 * kvw, :].astype(BF16))
             + p_wn * winn[:, kvw:2 * kvw]) * (1.0 / l_w)

    g = gates_ref[0]
    out = g[0] * o_cmp[0:rows, :] + g[1] * o_sel + g[2] * o_win
    o_ref[0] = out[0:8, :]


def _nsa_sample(q, gates, kv_new, win_new, cache_kv, cache_win, page_table, lp, rel_bias,
                groups, hpg, hd):
    r = q.shape[0]
    heads = groups * hpg
    kvw = groups * hd
    n_phys, page = cache_kv.shape[0], cache_kv.shape[1]
    n_pages = page_table.shape[1]
    past = n_pages * page
    wbuf = cache_win.shape[1]
    ncb, nsb = past // CMP_BLOCK, past // SEL_BLOCK
    rows = 16
    chunk = min(32, n_pages // 2)
    nks = min(NKS_SAMPLE, chunk * page)
    assert n_pages % (2 * chunk) == 0 and chunk % 8 == 0 and (chunk * page) % nks == 0 and heads <= rows
    q3 = q.reshape(r, heads, hd)
    gmask = (jnp.arange(groups)[:, None] == (jnp.arange(heads) // hpg)[None, :]).astype(F32)
    qbd = jnp.einsum('rhd,gh->rgdh', q3, gmask).reshape(r, kvw, heads)
    qbd = jnp.pad(qbd, ((0, 0), (0, 0), (0, LANES - heads))).astype(BF16)
    qn = jnp.einsum('rhd,gh->rhgd', q3, gmask).reshape(r, heads, kvw)
    qn = jnp.pad(qn, ((0, 0), (0, rows - heads), (0, 0)))
    gts = gates[:, :3 * heads].reshape(r, heads, 3).transpose(0, 2, 1)
    gts = jnp.broadcast_to(gts[:, :, :, None], (r, 3, heads, kvw))
    gts = jnp.pad(gts, ((0, 0), (0, 0), (0, rows - heads), (0, 0)))
    wk = _compress_weight(lp['cmp_w_k'], groups, hd).reshape(CMP_BLOCK, kvw, kvw)
    wv = _compress_weight(lp['cmp_w_v'], groups, hd).reshape(CMP_BLOCK, kvw, kvw)
    zero = jnp.zeros_like(wk)
    wkv = jnp.concatenate([jnp.concatenate([wk, zero], axis=2),
                           jnp.concatenate([zero, wv], axis=2)], axis=1).astype(BF16)
    pe_t = jnp.concatenate([jnp.tile(lp['pe_k'].T, (groups, 1)), jnp.tile(lp['pe_v'].T, (groups, 1))], axis=0)
    pe_t = jnp.tile(pe_t, (1, 2 * page // CMP_BLOCK))
    bpp = 2 * page // CMP_BLOCK
    dst = np.arange(2 * page)
    src = (dst % bpp) * CMP_BLOCK + dst // bpp
    perm = jnp.asarray(src[:, None] == np.arange(2 * page)[None, :], BF16)
    kg = jnp.tile(lp['k_gain'][0], groups)[None, :]
    lane_g = np.where(np.arange(LANES) < heads, np.arange(LANES) // hpg, -1)
    gsum = jnp.asarray((lane_g[:, None] == lane_g[None, :]) & (lane_g[:, None] >= 0), BF16)
    nbk = rel_bias.shape[0]
    tbl = rel_bias.astype(F32)
    d_c = (past - (np.arange(ncb) * CMP_BLOCK + CMP_BLOCK - 1))[:, None]
    vals_t = jnp.concatenate([tbl, jnp.broadcast_to(tbl[:, :1], (nbk, LANES - heads))], axis=1)
    cb = _bias_table(np.broadcast_to(_bucket_np(d_c, nbk), (ncb, LANES)), vals_t)
    vals_n = jnp.concatenate([jnp.repeat(tbl, LANES, axis=1), jnp.full((1, heads * LANES), NEG, F32)], axis=0)

    def head_rows(ids):
        out = _bias_table(ids, vals_n).reshape(ids.shape[0], heads, LANES)
        out = jnp.transpose(out, (1, 0, 2)).reshape(heads, ids.shape[0] * LANES)
        return jnp.pad(out, ((0, rows - heads), (0, 0)))

    sb = head_rows(_bucket_np(past - np.arange(past), nbk).reshape(past // LANES, LANES))
    d_w = wbuf - np.arange(wbuf)
    wb = head_rows(np.where(d_w < WINDOW, _bucket_np(d_w, nbk), nbk).reshape(wbuf // LANES, LANES))
    b0 = jnp.pad(jnp.broadcast_to(tbl[0][:, None], (heads, LANES)), ((0, rows - heads), (0, 0)))
    blk = np.arange(nsb)[:, None]
    spread = jnp.asarray((blk == np.arange(nks)[None, :] // SEL_BLOCK), BF16)

    cache_t = jnp.transpose(cache_kv, (0, 2, 3, 4, 1)).reshape(n_phys, 4 * kvw, page)
    cwin_t = jnp.transpose(cache_win, (0, 2, 3, 4, 1)).reshape(r, 2 * kvw, wbuf)
    kvn = kv_new.reshape(r, 1, 4 * kvw)
    winn = win_new.reshape(r, 1, 2 * kvw)
    per_b = lambda s: pl.BlockSpec((1,) + s, lambda b, pt: (b,) + (0,) * len(s))
    const = lambda a: pl.BlockSpec(a.shape, lambda b, pt: (0,) * a.ndim, pipeline_mode=pl.Buffered(1))
    consts = (wkv, pe_t, perm, kg, gsum, cb, sb, wb, b0, spread)
    grid_spec = pltpu.PrefetchScalarGridSpec(
        num_scalar_prefetch=1, grid=(r,),
        in_specs=[pl.BlockSpec(memory_space=pl.ANY), per_b((kvw, LANES)), per_b((rows, kvw)),
                  per_b((3, rows, kvw)), per_b((1, 4 * kvw)), per_b((1, 2 * kvw)),
                  per_b((2 * kvw, wbuf))]
                 + [const(a) for a in consts],
        out_specs=per_b((8, kvw)),
        scratch_shapes=[pltpu.VMEM((2, chunk, 2 * kvw, page), F32),
                        pltpu.VMEM((2, 2 * kvw, chunk * page), F32),
                        pltpu.VMEM((CMP_BLOCK, ncb, 2 * kvw), F32),
                        pltpu.SemaphoreType.DMA((2,)), pltpu.SemaphoreType.DMA((2,)),
                        pltpu.VMEM((ncb, LANES), F32)])
    o8 = pl.pallas_call(
        functools.partial(_nsa_sample_kernel, hd=hd, n_pages=n_pages, page=page, chunk=chunk),
        grid_spec=grid_spec,
        out_shape=jax.ShapeDtypeStruct((r, 8, kvw), F32),
        compiler_params=_cparams(), name="nsa_sample",
    )(page_table, cache_t, qbd, qn, gts, kvn, winn, cwin_t, *consts)
    o4 = o8[:, :heads, :].reshape(r, heads, groups, hd)
    y = jnp.take_along_axis(o4, (jnp.arange(heads) // hpg)[None, :, None, None], axis=2)
    return y.reshape(r, heads * hd).astype(BF16)


def _mlp_ple(h, p_ref, gm_ref, wu_ref, wd_ref, gp_ref, wg_ref, wp_ref):
    up = _dot(_rms_rows(h, gm_ref[...]).astype(BF16), wu_ref[...])
    act = jnp.square(jnp.maximum(up, 0.0)).astype(BF16)
    h = h + _dot(act, wd_ref[...])
    gate = jax.nn.sigmoid(_dot(_rms_rows(h, gp_ref[...]).astype(BF16), wg_ref[...]))
    return h + gate * _dot(p_ref[...].astype(BF16), wp_ref[...])


def _even_tail_kernel(h_ref, ya_ref, yb_ref, p_ref, woa_ref, wob_ref,
                      gm_ref, wu_ref, wd_ref, gp_ref, wg_ref, wp_ref, o_ref):
    h = h_ref[...] + _dot(ya_ref[...], woa_ref[...]) + _dot(yb_ref[...], wob_ref[...])
    o_ref[...] = _mlp_ple(h, p_ref, gm_ref, wu_ref, wd_ref, gp_ref, wg_ref, wp_ref)


def _tail_weights(g_mlp, w_up, w_down, g_ple, w_gate, w_proj):
    return (g_mlp[None, :], w_up.astype(BF16), w_down.astype(BF16), g_ple[None, :],
            w_gate.astype(BF16), w_proj.astype(BF16))


def _even_tail(h, ya, yb, p, w_out, tail_w, tm=256):
    t, d = h.shape
    tm = min(tm, t)
    lw = ya.shape[1]
    woa, wob = w_out[:lw].astype(BF16), w_out[lw:].astype(BF16)
    row = lambda w: pl.BlockSpec((tm, w), lambda i: (i, 0))
    weights = (woa, wob) + tuple(tail_w)
    return pl.pallas_call(
        _even_tail_kernel, grid=(t // tm,),
        in_specs=[row(d), row(lw), row(yb.shape[1]), row(p.shape[1])]
                 + [_const_spec(w.shape) for w in weights],
        out_specs=row(d), out_shape=jax.ShapeDtypeStruct((t, d), F32),
        compiler_params=_cparams(), name="even_tail",
    )(h, ya, yb, p, *weights)


def _odd_kernel(h_ref, p_ref, gx_ref, wi_ref, lg_ref, lb_ref, sw_ref, sb_ref, wo_ref,
                gm_ref, wu_ref, wd_ref, gp_ref, wg_ref, wp_ref, o_ref, v_ref, *, sw_groups, single):
    h = h_ref[...]
    tm = h.shape[0]
    z = _gelu(_dot(_rms_rows(h, gx_ref[...]).astype(BF16), wi_ref[...]))
    width = z.shape[1] // 2
    u, v = z[:, :width], z[:, width:]
    mu = jnp.mean(v, axis=-1, keepdims=True)
    vc = v - mu
    vn = vc * lax.rsqrt(jnp.mean(vc * vc, axis=-1, keepdims=True) + EPS) * lg_ref[...] + lb_ref[...]
    v_ref[...] = vn
    gw = width // sw_groups
    if single:
        s = vn * sw_ref[...] + sb_ref[...]
    else:
        vb = vn.astype(BF16)
        parts = []
        for c in range(tm // CHUNK):
            cols = [_dot(sw_ref[g], vb[c * CHUNK:(c + 1) * CHUNK, g * gw:(g + 1) * gw])
                    for g in range(sw_groups)]
            parts.append(jnp.concatenate(cols, axis=1) + sb_ref[...])
        s = jnp.concatenate(parts, axis=0)
    h = h + _dot((u * s).astype(BF16), wo_ref[...])
    o_ref[...] = _mlp_ple(h, p_ref, gm_ref, wu_ref, wd_ref, gp_ref, wg_ref, wp_ref)


def _odd_layer(h, p, g_mix, w_in, ln_g, ln_b, sgu_w, sgu_b, w_out, tail_w, single, tm=256):
    t, d = h.shape
    tm = min(tm, t)
    ng, ch, _ = sgu_w.shape
    width = w_in.shape[1] // 2
    gw = width // ng
    if single:
        sw = jnp.repeat(sgu_w[:, 0, 0], gw)[None, :]
        sb = jnp.repeat(sgu_b[:, 0], gw)[None, :]
    else:
        sw = (sgu_w * jnp.tril(jnp.ones((ch, ch), sgu_w.dtype))).astype(BF16)
        sb = jnp.repeat(sgu_b.T, gw, axis=1)
    weights = (g_mix[None, :], w_in.astype(BF16), ln_g[None, :], ln_b[None, :], sw, sb,
               w_out.astype(BF16)) + tuple(tail_w)
    row = lambda w: pl.BlockSpec((tm, w), lambda i: (i, 0))
    return pl.pallas_call(
        functools.partial(_odd_kernel, sw_groups=ng, single=single), grid=(t // tm,),
        in_specs=[row(d), row(p.shape[1])] + [_const_spec(w.shape) for w in weights],
        out_specs=[row(d), row(width)],
        out_shape=[jax.ShapeDtypeStruct((t, d), F32), jax.ShapeDtypeStruct((t, width), F32)],
        compiler_params=_cparams(), name="odd_single" if single else "odd_prompt",
    )(h, p, *weights)


def kernel(x_prompt, x_sample, cache_kv, cache_win, state_conv, state_h, page_table, p_prompt, p_sample,
           rel_bias, g_mix, g_mlp, w_up, w_down, g_ple, w_ple_gate, w_ple_proj,
           w_in_even, w_out_even, conv_w, conv_b, rg_w_a, rg_b_a, rg_w_x, rg_b_x, rg_lambda,
           q_gain, k_gain, cmp_w_k, cmp_w_v, cmp_pe_k, cmp_pe_v,
           w_in_odd, ln_v_g, ln_v_b, sgu_w, sgu_b, w_out_odd):
    bsz, t, d = x_prompt.shape
    r = x_sample.shape[0]
    assert bsz == 1 and x_sample.shape[1] == 1
    depth = g_mix.shape[0]
    lw = conv_w.shape[-1]
    hd = q_gain.shape[-1]
    groups = cache_kv.shape[4]
    heads = rel_bias.shape[1]
    hpg = heads // groups
    qw, kvw = heads * hd, groups * hd
    dims = (lw, qw, kvw, hd, heads)
    wbuf = cache_win.shape[2]

    hp = x_prompt.reshape(t, d)
    hs = x_sample.reshape(r, d)
    kvp, kvs, wnp, wns, cvp, cvs, hhp, hhs, vvs = [], [], [], [], [], [], [], [], []
    for i in range(depth):
        tail_w = _tail_weights(g_mlp[i], w_up[i], w_down[i], g_ple[i], w_ple_gate[i], w_ple_proj[i])
        if i % 2 == 0:
            e = i // 2
            lp = {'cmp_w_k': cmp_w_k[e], 'cmp_w_v': cmp_w_v[e], 'pe_k': cmp_pe_k[e], 'pe_v': cmp_pe_v[e],
                  'k_gain': k_gain[e]}
            rg_w = _rglru_weights(conv_w[e], conv_b[e], rg_w_a[e], rg_b_a[e], rg_w_x[e], rg_b_x[e],
                                  rg_lambda[e])
            (xa, ga, kv_rows, win_rows, kcmp, vcmp, ks, kw, qT, vsT, vwT, gT) = _even_in_prompt(
                hp, g_mix[i], w_in_even[e], q_gain[e], k_gain[e], dims)
            ya, h_last = _rglru_prompt(xa, ga, rg_w)
            kc, vcT = _compress_prompt(kcmp, vcmp, cmp_w_k[e], cmp_w_v[e], cmp_pe_k[e], cmp_pe_v[e],
                                       k_gain[e, 0], groups, hd)
            yb = _nsa_prompt(qT, gT, kc, vcT, ks, vsT, kw, vwT, rel_bias, groups, hpg, hd)
            hp = _even_tail(hp, ya, yb, p_prompt[i, 0], w_out_even[e], tail_w)
            kvp.append(kv_rows.reshape(1, t, 4, groups, hd))
            wnp.append(win_rows[t - wbuf:].reshape(1, wbuf, 2, groups, hd))
            cvp.append(xa[t - (conv_w.shape[1] - 1):][None])
            hhp.append(h_last)
            xa_s, ga_s, q_s, kv_s, win_s, gate_s = _even_in_sample(
                hs, g_mix[i], w_in_even[e], q_gain[e], k_gain[e], dims)
            ya_s, h_s = _rglru_sample(xa_s, ga_s, state_conv[e], state_h[e], rg_w)
            yb_s = _nsa_sample(q_s, gate_s, kv_s, win_s, cache_kv[e], cache_win[e], page_table, lp,
                               rel_bias, groups, hpg, hd)
            hs = _even_tail(hs, ya_s, yb_s, p_sample[i, :, 0], w_out_even[e], tail_w)
            kvs.append(kv_s.reshape(r, 1, 4, groups, hd))
            wns.append(jnp.concatenate([cache_win[e][:, 1:], win_s.reshape(r, 1, 2, groups, hd)], axis=1))
            cvs.append(jnp.concatenate([state_conv[e][:, 1:], xa_s[:, None, :]], axis=1))
            hhs.append(h_s)
        else:
            o = i // 2
            hp, _ = _odd_layer(hp, p_prompt[i, 0], g_mix[i], w_in_odd[o], ln_v_g[o], ln_v_b[o],
                               sgu_w[o], sgu_b[o], w_out_odd[o], tail_w, single=False)
            hs, v_new = _odd_layer(hs, p_sample[i, :, 0], g_mix[i], w_in_odd[o], ln_v_g[o], ln_v_b[o],
                                   sgu_w[o], sgu_b[o], w_out_odd[o], tail_w, single=True)
            vvs.append(v_new[:, None, :])
    v_sample = jnp.stack(vvs) if vvs else jnp.zeros((0, r, 1, w_in_odd.shape[-1] // 2), F32)
    return (hp[None], hs[:, None, :], jnp.stack(kvp), jnp.stack(kvs), jnp.stack(wnp), jnp.stack(wns),
            jnp.stack(cvp), jnp.stack(cvs), jnp.stack(hhp), jnp.stack(hhs), v_sample)
```

```python
import functools
import math

import numpy as np
import jax
import jax.numpy as jnp
from jax import lax
from jax.experimental import pallas as pl
from jax.experimental.pallas import tpu as pltpu

F32 = jnp.float32
BF16 = jnp.bfloat16

EPS = 1e-6
NEG = -1e30
RG_C = 8.0
CMP_BLOCK = 32
SEL_BLOCK = 64
TOP_N = 16
WINDOW = 512
Q_BLOCK = 128
FORCE_SCORE = 1.0e4
MAX_DIST = 128
CHUNK = 128

LANES = 128
VMEM_LIMIT = 56 * 1024 * 1024

LOG2E = math.log2(math.e)
KEY_GROUP = 512
ACC_ROWS = 80
NKS_SAMPLE = 2048


def _cparams(n_axes=1):
    return pltpu.CompilerParams(dimension_semantics=("arbitrary",) * n_axes,
                                vmem_limit_bytes=VMEM_LIMIT)


def _const_spec(shape):
    nd = len(shape)
    return pl.BlockSpec(shape, lambda *_: (0,) * nd, pipeline_mode=pl.Buffered(1))


def _dot(a, b):
    return jnp.dot(a, b, preferred_element_type=F32)


def _dot_nt(a, b):
    return lax.dot_general(a, b, (((1,), (1,)), ((), ())), preferred_element_type=F32)


def _gelu(x):
    return 0.5 * x * (1.0 + jnp.tanh(math.sqrt(2.0 / math.pi) * (x + 0.044715 * (x * x * x))))


def _softplus(x):
    return jnp.maximum(x, 0.0) + jnp.log1p(jnp.exp(-jnp.abs(x)))


def _rms_rows(x, g):
    return x * lax.rsqrt(jnp.mean(x * x, axis=-1, keepdims=True) + EPS) * g


def _group_rmsnorm(x, gain, width):
    n = x.shape[-1] // width
    lane = lax.broadcasted_iota(jnp.int32, x.shape, 1)
    x2 = x * x
    ms = jnp.zeros_like(x)
    for i in range(n):
        inside = (lane >= i * width) & (lane < (i + 1) * width)
        s = jnp.sum(jnp.where(inside, x2, 0.0), axis=-1, keepdims=True)
        ms = jnp.where(inside, s, ms)
    return x * lax.rsqrt(ms * (1.0 / width) + EPS) * gain


def _bucket_np(dist, n_buckets):
    n = np.maximum(dist, 0)
    exact = n_buckets // 2
    nf = np.maximum(n, 1).astype(np.float32)
    large = exact + (np.log(nf / np.float32(exact)) / np.float32(math.log(MAX_DIST / exact))
                     * np.float32(n_buckets - exact)).astype(np.int32)
    return np.where(n < exact, n, np.minimum(large, n_buckets - 1)).astype(np.int32)


def _even_in_prompt_kernel(x_ref, g_ref, wn_ref, wt_ref, qg_ref, kg_ref,
                           xa_ref, ga_ref, kv_ref, win_ref, kc_ref, vc_ref,
                           ks_ref, kw_ref, qT_ref, vsT_ref, vwT_ref, gT_ref, *, lw, qw, kvw, hd):
    tm = x_ref.shape[0]
    hn = _rms_rows(x_ref[...], g_ref[...]).astype(BF16)
    zn = _dot(hn, wn_ref[...])
    xa_ref[...] = zn[:, :lw]
    ga_ref[...] = zn[:, lw:2 * lw]
    kv = [zn[:, 2 * lw + i * kvw: 2 * lw + (i + 1) * kvw] for i in range(6)]
    k_sel = _group_rmsnorm(kv[2], kg_ref[1:2, :], hd)
    k_win = _group_rmsnorm(kv[4], kg_ref[2:3, :], hd)
    kv_ref[:, 0 * kvw:1 * kvw] = kv[0]
    kv_ref[:, 1 * kvw:2 * kvw] = kv[1]
    kv_ref[:, 2 * kvw:3 * kvw] = k_sel
    kv_ref[:, 3 * kvw:4 * kvw] = kv[3]
    win_ref[:, :kvw] = k_win
    win_ref[:, kvw:] = kv[5]
    kc_ref[...] = kv[0]
    vc_ref[...] = kv[1]
    blk = lax.broadcasted_iota(jnp.int32, (tm, kvw), 0) // SEL_BLOCK
    col = lax.broadcasted_iota(jnp.int32, (tm, kvw), 1)
    ks_ref[:, :kvw] = k_sel.astype(BF16)
    ks_ref[:, kvw:] = jnp.where(col == blk, 1.0, 0.0).astype(BF16)
    kw_ref[...] = k_win.astype(BF16)

    zt = _dot_nt(wt_ref[...], hn)
    for h in range(qw // hd):
        blk = zt[h * hd:(h + 1) * hd, :]
        r = lax.rsqrt(jnp.mean(blk * blk, axis=0, keepdims=True) + EPS)
        qn = blk * r
        for j in range(0, tm, LANES):
            qT_ref[h * hd:(h + 1) * hd, j:j + LANES] = (
                qn[:, j:j + LANES] * qg_ref[h * hd:(h + 1) * hd, :]).astype(BF16)
    pad = ACC_ROWS - hd
    ones_rows = jnp.where(lax.broadcasted_iota(jnp.int32, (pad, tm), 0) == 0, 1.0, 0.0).astype(BF16)
    for out_ref, base in ((vsT_ref, qw), (vwT_ref, qw + kvw)):
        for g in range(kvw // hd):
            out_ref[g * ACC_ROWS:g * ACC_ROWS + hd, :] = zt[base + g * hd:base + (g + 1) * hd, :].astype(BF16)
            out_ref[g * ACC_ROWS + hd:(g + 1) * ACC_ROWS, :] = ones_rows
    gT_ref[...] = jax.nn.sigmoid(zt[qw + 2 * kvw:, :])


def _even_in_prompt(x, g_mix, w_in, q_gain, k_gain, dims, tm=512):
    t, d = x.shape
    lw, qw, kvw, hd, heads = dims
    tm = min(tm, t)
    c0, c1, c2 = 2 * lw, 2 * lw + qw, 2 * lw + qw + 6 * kvw
    wn = jnp.concatenate([w_in[:, :c0], w_in[:, c1:c2]], axis=1).astype(BF16)
    wg = w_in[:, c2:c2 + 3 * heads].reshape(d, heads, 3).transpose(0, 2, 1).reshape(d, 3 * heads)
    wt = jnp.concatenate([w_in[:, c0:c1], w_in[:, c1 + 3 * kvw:c1 + 4 * kvw],
                          w_in[:, c1 + 5 * kvw:c1 + 6 * kvw], wg,
                          jnp.zeros((d, 32 - 3 * heads), F32)], axis=1).T.astype(BF16)
    assert tm % KEY_GROUP == 0
    qg = jnp.broadcast_to((jnp.tile(q_gain, heads) * (hd ** -0.5 * LOG2E))[:, None], (qw, LANES)).astype(F32)
    vrows = (kvw // hd) * ACC_ROWS
    kg = jnp.tile(k_gain, (1, kvw // hd)).astype(F32)
    kg = jnp.concatenate([kg, jnp.zeros((5, kvw), F32)], axis=0)
    nt = wt.shape[0]
    row = lambda w: pl.BlockSpec((tm, w), lambda i: (i, 0))
    col = lambda r: pl.BlockSpec((r, tm), lambda i: (0, i))
    outs = pl.pallas_call(
        functools.partial(_even_in_prompt_kernel, lw=lw, qw=qw, kvw=kvw, hd=hd),
        grid=(t // tm,),
        in_specs=[row(d), _const_spec((1, d)), _const_spec(wn.shape), _const_spec(wt.shape),
                  _const_spec(qg.shape), _const_spec(kg.shape)],
        out_specs=[row(lw), row(lw), row(4 * kvw), row(2 * kvw), row(kvw), row(kvw),
                   row(2 * kvw), row(kvw), col(qw), col(vrows), col(vrows), col(32)],
        out_shape=[jax.ShapeDtypeStruct((t, lw), F32), jax.ShapeDtypeStruct((t, lw), F32),
                   jax.ShapeDtypeStruct((t, 4 * kvw), F32), jax.ShapeDtypeStruct((t, 2 * kvw), F32),
                   jax.ShapeDtypeStruct((t, kvw), F32), jax.ShapeDtypeStruct((t, kvw), F32),
                   jax.ShapeDtypeStruct((t, 2 * kvw), BF16), jax.ShapeDtypeStruct((t, kvw), BF16),
                   jax.ShapeDtypeStruct((qw, t), BF16), jax.ShapeDtypeStruct((vrows, t), BF16),
                   jax.ShapeDtypeStruct((vrows, t), BF16), jax.ShapeDtypeStruct((32, t), F32)],
        compiler_params=_cparams(), name="even_in_prompt",
    )(x, g_mix[None, :], wn, wt, qg, kg)
    return outs


def _even_in_sample_kernel(x_ref, g_ref, w_ref, qg_ref, kg_ref,
                           xa_ref, ga_ref, q_ref, kv_ref, win_ref, gate_ref, *, lw, qw, kvw, hd):
    hn = _rms_rows(x_ref[...], g_ref[...]).astype(BF16)
    z = _dot(hn, w_ref[...])
    xa_ref[...] = z[:, :lw]
    ga_ref[...] = z[:, lw:2 * lw]
    q_ref[...] = _group_rmsnorm(z[:, 2 * lw:2 * lw + qw], qg_ref[...], hd)
    o = 2 * lw + qw
    kv = [z[:, o + i * kvw: o + (i + 1) * kvw] for i in range(6)]
    kv_ref[:, 0 * kvw:1 * kvw] = kv[0]
    kv_ref[:, 1 * kvw:2 * kvw] = kv[1]
    kv_ref[:, 2 * kvw:3 * kvw] = _group_rmsnorm(kv[2], kg_ref[1:2, :], hd)
    kv_ref[:, 3 * kvw:4 * kvw] = kv[3]
    win_ref[:, :kvw] = _group_rmsnorm(kv[4], kg_ref[2:3, :], hd)
    win_ref[:, kvw:] = kv[5]
    gate_ref[...] = jax.nn.sigmoid(z[:, o + 6 * kvw:])


def _even_in_sample(x, g_mix, w_in, q_gain, k_gain, dims):
    r, d = x.shape
    lw, qw, kvw, hd, heads = dims
    n_in = w_in.shape[1]
    pad = (-n_in) % LANES
    w = jnp.pad(w_in, ((0, 0), (0, pad))).astype(BF16)
    gw = n_in + pad - (2 * lw + qw + 6 * kvw)
    qg = (jnp.tile(q_gain, heads) * hd ** -0.5)[None, :].astype(F32)
    kg = jnp.tile(k_gain, (1, kvw // hd)).astype(F32)
    kg = jnp.concatenate([kg, jnp.zeros((5, kvw), F32)], axis=0)
    full = lambda s: pl.BlockSpec(s, lambda i: (0,) * len(s))
    return pl.pallas_call(
        functools.partial(_even_in_sample_kernel, lw=lw, qw=qw, kvw=kvw, hd=hd),
        grid=(1,),
        in_specs=[full((r, d)), full((1, d)), full(w.shape), full(qg.shape), full(kg.shape)],
        out_specs=[full((r, lw)), full((r, lw)), full((r, qw)), full((r, 4 * kvw)),
                   full((r, 2 * kvw)), full((r, gw))],
        out_shape=[jax.ShapeDtypeStruct((r, lw), F32), jax.ShapeDtypeStruct((r, lw), F32),
                   jax.ShapeDtypeStruct((r, qw), F32), jax.ShapeDtypeStruct((r, 4 * kvw), F32),
                   jax.ShapeDtypeStruct((r, 2 * kvw), F32), jax.ShapeDtypeStruct((r, gw), F32)],
        compiler_params=_cparams(), name="even_in_sample",
    )(x, g_mix[None, :], w, qg, kg)


def _rglru_gates(conv, wa_ref, ba_ref, wx_ref, bx_ref, lam_ref):
    cb = conv.astype(BF16)
    r = jax.nn.sigmoid(_dot(cb, wa_ref[...]) + ba_ref[...])
    i = jax.nn.sigmoid(_dot(cb, wx_ref[...]) + bx_ref[...])
    log_a = -RG_C * r * _softplus(-lam_ref[...])
    a = jnp.exp(log_a)
    th = jnp.tanh(log_a)
    b = jnp.sqrt(-2.0 * th / (1.0 - th)) * (i * conv)
    return a, b


def _rglru_prompt_kernel(xa_ref, ga_ref, cw_ref, cb_ref, wa_ref, ba_ref, wx_ref, bx_ref, lam_ref,
                         ya_ref, hl_ref, xext_ref, h_ref):
    tr = xa_ref.shape[0]

    @pl.when(pl.program_id(0) == 0)
    def _():
        xext_ref[0:8, :] = jnp.zeros((8, xext_ref.shape[1]), F32)
        h_ref[...] = jnp.zeros(h_ref.shape, F32)

    x = xa_ref[...]
    xext_ref[8:8 + tr, :] = x
    conv = (cb_ref[...] + cw_ref[3:4, :] * x + cw_ref[2:3, :] * xext_ref[7:7 + tr, :]
            + cw_ref[1:2, :] * xext_ref[6:6 + tr, :] + cw_ref[0:1, :] * xext_ref[5:5 + tr, :])
    xext_ref[0:8, :] = xext_ref[tr:tr + 8, :]
    a, b = _rglru_gates(conv, wa_ref, ba_ref, wx_ref, bx_ref, lam_ref)
    sub = 8
    lw = a.shape[1]
    a = a.reshape(tr // sub, sub, lw)
    b = b.reshape(tr // sub, sub, lw)
    pos = lax.broadcasted_iota(jnp.int32, a.shape, 1)
    s = 1
    while s < sub:
        keep = pos >= s
        a_sh = jnp.where(keep, pltpu.roll(a, s, 1), 1.0)
        b_sh = jnp.where(keep, pltpu.roll(b, s, 1), 0.0)
        b = a * b_sh + b
        a = a * a_sh
        s *= 2
    h_in = h_ref[0:1, :]
    tiles = []
    for i in range(tr // sub):
        h_tile = a[i] * h_in + b[i]
        tiles.append(h_tile)
        h_in = h_tile[sub - 1:sub, :]
    hs = jnp.concatenate(tiles, axis=0)
    h_last = h_in
    h_ref[...] = jnp.broadcast_to(h_last, h_ref.shape)
    hl_ref[...] = h_last
    ya_ref[...] = (hs * _gelu(ga_ref[...])).astype(BF16)


def _block_diag(w):
    n, a, b = w.shape
    eye = jnp.eye(n, dtype=w.dtype)
    return (eye[:, None, :, None] * w[:, :, None, :]).reshape(n * a, n * b)


def _rglru_weights(conv_w, conv_b, w_a, b_a, w_x, b_x, lam):
    return (conv_w.astype(F32), conv_b[None, :], _block_diag(w_a).astype(BF16), b_a[None, :],
            _block_diag(w_x).astype(BF16), b_x[None, :], lam[None, :])


def _rglru_prompt(xa, ga, weights, tr=256):
    t, lw = xa.shape
    tr = min(tr, t)
    row = pl.BlockSpec((tr, lw), lambda i: (i, 0))
    return pl.pallas_call(
        _rglru_prompt_kernel, grid=(t // tr,),
        in_specs=[row, row] + [_const_spec(w.shape) for w in weights],
        out_specs=[row, pl.BlockSpec((1, lw), lambda i: (0, 0))],
        out_shape=[jax.ShapeDtypeStruct((t, lw), BF16), jax.ShapeDtypeStruct((1, lw), F32)],
        scratch_shapes=[pltpu.VMEM((tr + 8, lw), F32), pltpu.VMEM((8, lw), F32)],
        compiler_params=_cparams(), name="rglru_prompt",
    )(xa, ga, *weights)


def _rglru_sample_kernel(xa_ref, ga_ref, sc_ref, h0_ref, cw_ref, cb_ref, wa_ref, ba_ref, wx_ref,
                         bx_ref, lam_ref, ya_ref, h_ref):
    conv = (cb_ref[...] + cw_ref[3:4, :] * xa_ref[...] + cw_ref[2:3, :] * sc_ref[2]
            + cw_ref[1:2, :] * sc_ref[1] + cw_ref[0:1, :] * sc_ref[0])
    a, b = _rglru_gates(conv, wa_ref, ba_ref, wx_ref, bx_ref, lam_ref)
    h = a * h0_ref[...] + b
    h_ref[...] = h
    ya_ref[...] = (h * _gelu(ga_ref[...])).astype(BF16)


def _rglru_sample(xa, ga, state_conv, state_h, weights):
    r, lw = xa.shape
    sc = jnp.transpose(state_conv, (1, 0, 2))
    args = (xa, ga, sc, state_h) + tuple(weights)
    full = lambda s: pl.BlockSpec(s, lambda i: (0,) * len(s))
    return pl.pallas_call(
        _rglru_sample_kernel, grid=(1,),
        in_specs=[full(a.shape) for a in args],
        out_specs=[full((r, lw)), full((r, lw))],
        out_shape=[jax.ShapeDtypeStruct((r, lw), BF16), jax.ShapeDtypeStruct((r, lw), F32)],
        compiler_params=_cparams(), name="rglru_sample",
    )(*args)


def _compress_prompt_kernel(xk_ref, xv_ref, pek_ref, pev_ref, wk_ref, wvT_ref, kg_ref,
                            kc_ref, vcT_ref, *, hd):
    xk = (xk_ref[...] + pek_ref[...]).astype(BF16)
    kc = _dot(xk, wk_ref[...])
    kc_ref[...] = _group_rmsnorm(kc, kg_ref[...], hd).astype(BF16)
    xv = (xv_ref[...] + pev_ref[...]).astype(BF16)
    vcT_ref[...] = _dot_nt(wvT_ref[...], xv).astype(BF16)


def _compress_weight(w, groups, hd):
    w3 = w.reshape(CMP_BLOCK, hd, hd)
    eye = jnp.eye(groups, dtype=w.dtype)
    full = eye[None, :, None, :, None] * w3[:, None, :, None, :]
    return full.reshape(CMP_BLOCK * groups * hd, groups * hd)


def _compress_prompt(kcmp, vcmp, w_k, w_v, pe_k, pe_v, k_gain_cmp, groups, hd):
    t, kvw = kcmp.shape
    ncb = t // CMP_BLOCK
    kdim = CMP_BLOCK * kvw
    xk = kcmp.reshape(ncb, kdim)
    xv = vcmp.reshape(ncb, kdim)
    wk = _compress_weight(w_k, groups, hd).astype(BF16)
    wvT = _compress_weight(w_v, groups, hd).T.astype(BF16)
    pek = jnp.tile(pe_k, (1, groups)).reshape(1, kdim)
    pev = jnp.tile(pe_v, (1, groups)).reshape(1, kdim)
    kg = jnp.tile(k_gain_cmp, groups)[None, :]
    nb = min(LANES, ncb)
    return pl.pallas_call(
        functools.partial(_compress_prompt_kernel, hd=hd), grid=(ncb // nb,),
        in_specs=[pl.BlockSpec((nb, kdim), lambda i: (i, 0)), pl.BlockSpec((nb, kdim), lambda i: (i, 0)),
                  _const_spec((1, kdim)), _const_spec((1, kdim)), _const_spec(wk.shape),
                  _const_spec(wvT.shape), _const_spec((1, kvw))],
        out_specs=[pl.BlockSpec((nb, kvw), lambda i: (i, 0)), pl.BlockSpec((kvw, nb), lambda i: (0, i))],
        out_shape=[jax.ShapeDtypeStruct((ncb, kvw), BF16), jax.ShapeDtypeStruct((kvw, ncb), BF16)],
        compiler_params=_cparams(), name="compress_prompt",
    )(xk, xv, pek, pev, wk, wvT, kg)


def _tile_lanes(x, n):
    return jnp.concatenate([x] * n, axis=1)


def _attend(s, shift, vT, slot, g, m_ref, acc_ref, m_blk=None):
    m_old = m_ref[slot, g]
    if m_blk is None:
        m_blk = jnp.max(s, axis=0, keepdims=True)
    if shift is not None:
        m_blk = m_blk + shift
    m_new = jnp.maximum(m_old, m_blk)
    p = jnp.exp2(s - (m_new if shift is None else m_new - shift)).astype(BF16)
    acc_ref[slot, g] = jnp.exp2(m_old - m_new) * acc_ref[slot, g] + _dot(vT, p)
    m_ref[slot, g] = m_new


def _bias_table_kernel(id_ref, val_ref, o_ref, *, reps):
    ids = _tile_lanes(id_ref[...], reps)
    out = jnp.zeros(ids.shape, F32)
    for k in range(val_ref.shape[0]):
        out = jnp.where(ids == k, val_ref[k:k + 1, :], out)
    o_ref[...] = out


def _bias_table(ids, vals):
    r = ids.shape[0]
    reps = vals.shape[1] // LANES
    full = lambda s: pl.BlockSpec(s, lambda i: (0,) * len(s))
    return pl.pallas_call(
        functools.partial(_bias_table_kernel, reps=reps), grid=(1,),
        in_specs=[full(ids.shape), full(vals.shape)], out_specs=full((r, vals.shape[1])),
        out_shape=jax.ShapeDtypeStruct((r, vals.shape[1]), F32),
        compiler_params=_cparams(), name="bias_table",
    )(jnp.asarray(ids, jnp.int32), vals)


def _nsa_prompt_kernel(qT_ref, gT_ref, kc_ref, vcT_ref, ks_ref, vsT_ref, kw_ref, vwT_ref,
                       b31_ref, tab_ref,
                       y_ref,
                       qaug_ref, sa_ref, sb_ref, sc_ref, p4_ref, selb_ref, ocmp_ref, m_ref, acc_ref,
                       *, groups, hpg, hd):
    qb = pl.program_id(0)
    t0 = qb * Q_BLOCK
    ncb = kc_ref.shape[0]
    nsb = ncb * CMP_BLOCK // SEL_BLOCK
    gl = hpg * Q_BLOCK
    n_top = min(TOP_N, nsb)
    kvw = groups * hd
    bpg = KEY_GROUP // SEL_BLOCK

    rows = []
    for g in range(groups):
        top = jnp.concatenate([qT_ref[(g * hpg + hh) * hd:(g * hpg + hh + 1) * hd, :]
                               for hh in range(hpg)], axis=1)
        z = jnp.zeros_like(top)
        rows.append(jnp.concatenate([top if gg == g else z for gg in range(groups)], axis=1))
    qbd = jnp.concatenate(rows, axis=0)
    qaug_ref[0:kvw, :] = qbd
    qaug_ref[kvw:2 * kvw, :] = jnp.zeros((kvw, groups * gl), BF16)
    b31 = b31_ref[...]

    def compress_and_select(rc):
        nsv = rc * CMP_BLOCK // SEL_BLOCK
        jrow = lax.broadcasted_iota(jnp.int32, (rc, Q_BLOCK), 0)
        qcol = lax.broadcasted_iota(jnp.int32, (rc, Q_BLOCK), 1)
        vis = _tile_lanes(jrow * CMP_BLOCK + (CMP_BLOCK - 1) <= t0 + qcol, groups * hpg)
        s_c = _dot(kc_ref[0:rc, :], qbd) + b31
        sc_ref[0:rc, :] = jnp.where(vis, s_c, NEG)
        odd = (qb % 2) == 1
        v_a = jnp.where(qb == 0, 0, jnp.where(odd, 1, 2))
        w_a = pl.multiple_of(jnp.where(qb == 0, 0, jnp.where(odd, 4 * qb - 4, 4 * qb - 8)), 8)
        v_b = jnp.where((qb > 0) & jnp.logical_not(odd), 0, 3)
        w_b = pl.multiple_of(jnp.where((qb > 0) & jnp.logical_not(odd), 4 * qb, 0), 8)
        sc_ref[pl.ds(w_a, 8), :] = sc_ref[pl.ds(w_a, 8), :] + tab_ref[pl.ds(pl.multiple_of(v_a * 8, 8), 8), :]
        sc_ref[pl.ds(w_b, 8), :] = sc_ref[pl.ds(w_b, 8), :] + tab_ref[pl.ds(pl.multiple_of(v_b * 8, 8), 8), :]
        s_c = sc_ref[0:rc, :]
        m_c = jnp.max(s_c, axis=0, keepdims=True)
        p_c = jnp.where(vis, jnp.exp2(s_c - m_c), 0.0)
        l_c = jnp.sum(p_c, axis=0, keepdims=True)
        p_c = p_c * jnp.where(l_c > 0.0, 1.0 / l_c, 0.0)
        for g in range(groups):
            pg = p_c[:, g * gl:(g + 1) * gl]
            ocmp_ref[g] = _dot(vcT_ref[g * hd:(g + 1) * hd, 0:rc], pg.astype(BF16))
            p4 = pg[:, 0:Q_BLOCK]
            for hh in range(1, hpg):
                p4 = p4 + pg[:, hh * Q_BLOCK:(hh + 1) * Q_BLOCK]
            p4_ref[g, 0:rc, :] = p4

        ratio = SEL_BLOCK // CMP_BLOCK
        jb = lax.broadcasted_iota(jnp.int32, (nsv, Q_BLOCK), 0)
        qc = lax.broadcasted_iota(jnp.int32, (nsv, Q_BLOCK), 1)
        cur = (t0 + qc) // SEL_BLOCK
        forced = (jb == 0) | (jb == cur) | (jb == cur - 1)
        for g in range(groups):
            imp = p4_ref[g, pl.ds(0, nsv, stride=ratio), :]
            for k in range(1, ratio):
                imp = imp + p4_ref[g, pl.ds(k, nsv, stride=ratio), :]
            imp = jnp.where(forced, FORCE_SCORE, imp)
            imp = jnp.where(jb <= cur, imp, -1.0)
            selb = jnp.full((nsv, Q_BLOCK), NEG, F32)
            for _ in range(n_top):
                mx = jnp.max(imp, axis=0, keepdims=True)
                idx = jnp.min(jnp.where(imp == mx, jb, nsv), axis=0, keepdims=True)
                hit = jb == idx
                selb = jnp.where(hit, 0.0, selb)
                imp = jnp.where(hit, -3.0e38, imp)
            selb_ref[g, 0:nsv, :] = selb
            if nsv < nsb:
                selb_ref[g, nsv:nsb, :] = jnp.full((nsb - nsv, Q_BLOCK), NEG, F32)

    m_ref[...] = jnp.full(m_ref.shape, NEG, F32)
    acc_ref[...] = jnp.zeros(acc_ref.shape, F32)
    n_wk = WINDOW + Q_BLOCK

    def window_block():
        k0 = pl.multiple_of(t0 - WINDOW, Q_BLOCK)
        s = _dot(kw_ref[pl.ds(k0, n_wk), :], qaug_ref[0:kvw, :]) + tab_ref[_TAB_WIN:_TAB_WIN + n_wk, :]
        for g in range(groups):
            _attend(s[:, g * gl:(g + 1) * gl], None,
                    vwT_ref[g * ACC_ROWS:(g + 1) * ACC_ROWS, pl.ds(k0, n_wk)], 1, g, m_ref, acc_ref)

    tiers = list(range(LANES, ncb + 1, LANES)) if ncb % LANES == 0 else [ncb]
    n_vis = (t0 + Q_BLOCK) // CMP_BLOCK
    for i, rc in enumerate(tiers):
        lo = tiers[i - 1] if i else 0
        in_tier = (n_vis > lo) & (n_vis <= rc)

        @pl.when(in_tier & (t0 >= WINDOW))
        def _(rc=rc):
            compress_and_select(rc)
            window_block()

        if lo * CMP_BLOCK < WINDOW + Q_BLOCK:
            @pl.when(in_tier & (t0 < WINDOW))
            def _(rc=rc):
                compress_and_select(rc)


    def put_mask_rows(g, rows8):
        tile = jnp.concatenate([rows8, jnp.zeros_like(rows8)], axis=0).astype(BF16)
        for hh in range(hpg):
            qaug_ref[kvw:kvw + 2 * bpg, g * gl + hh * Q_BLOCK:g * gl + (hh + 1) * Q_BLOCK] = tile

    def far_scores(c, dst_ref):
        for g in range(groups):
            put_mask_rows(g, selb_ref[g, pl.ds(pl.multiple_of(c * bpg, bpg), bpg), :])
        k0 = pl.multiple_of(c * KEY_GROUP, KEY_GROUP)
        s = _dot(ks_ref[pl.ds(k0, KEY_GROUP), :], qaug_ref[...])
        dst_ref[0:KEY_GROUP, :] = s
        dst_ref[KEY_GROUP:KEY_GROUP + 8, :] = jnp.broadcast_to(jnp.max(s, axis=0, keepdims=True),
                                                               (8, groups * gl))

    def far_attend(c, src_ref, live_rows=None):
        k0 = pl.multiple_of(c * KEY_GROUP, KEY_GROUP)
        drop = None
        if live_rows is not None:
            drop = lax.broadcasted_iota(jnp.int32, (KEY_GROUP, Q_BLOCK), 0) >= live_rows
        for g in range(groups):
            sg = src_ref[0:KEY_GROUP, g * gl:(g + 1) * gl]
            m_blk = src_ref[KEY_GROUP:KEY_GROUP + 1, g * gl:(g + 1) * gl]
            if drop is not None:
                sg = jnp.where(_tile_lanes(drop, hpg), NEG, sg)
                m_blk = None
            _attend(sg, b31[:, g * gl:(g + 1) * gl],
                    vsT_ref[g * ACC_ROWS:(g + 1) * ACC_ROWS, pl.ds(k0, KEY_GROUP)], 0, g, m_ref, acc_ref,
                    m_blk=m_blk)

    far_len = jnp.maximum(t0 - Q_BLOCK, 0)
    n_far = far_len // KEY_GROUP
    live_tail = far_len - n_far * KEY_GROUP
    far_scores(0, sa_ref)

    def far_pair(j, carry):
        far_scores(2 * j + 1, sb_ref)
        far_attend(2 * j, sa_ref)
        far_scores(2 * j + 2, sa_ref)
        far_attend(2 * j + 1, sb_ref)
        return carry

    lax.fori_loop(0, n_far // 2, far_pair, 0)
    odd_far = n_far % 2 == 1

    @pl.when(odd_far)
    def _():
        far_scores(n_far, sb_ref)
        far_attend(n_far - 1, sa_ref)

    @pl.when(odd_far & (live_tail > 0))
    def _():
        far_attend(n_far, sb_ref, live_tail)

    @pl.when(jnp.logical_not(odd_far) & (live_tail > 0))
    def _():
        far_attend(n_far, sa_ref, live_tail)

    near_tab = tab_ref[_TAB_NEAR:_TAB_NEAR + 2 * Q_BLOCK, :]

    def near_piece():
        b0 = (t0 - Q_BLOCK) // SEL_BLOCK
        base = (b0 // bpg) * bpg
        nxt = jnp.minimum(base + bpg, nsb - bpg)
        jrow8 = lax.broadcasted_iota(jnp.int32, (bpg, Q_BLOCK), 0)
        for g in range(groups):
            lo = selb_ref[g, pl.ds(pl.multiple_of(base, bpg), bpg), :]
            hi = selb_ref[g, pl.ds(pl.multiple_of(nxt, bpg), bpg), :]
            put_mask_rows(g, jnp.where(jrow8 >= b0 - base, lo, hi))
        k0 = pl.multiple_of(t0 - Q_BLOCK, Q_BLOCK)
        s = _dot(ks_ref[pl.ds(k0, 2 * Q_BLOCK), :], qaug_ref[...]) + near_tab
        for g in range(groups):
            _attend(s[:, g * gl:(g + 1) * gl], None,
                    vsT_ref[g * ACC_ROWS:(g + 1) * ACC_ROWS, pl.ds(k0, 2 * Q_BLOCK)], 0, g, m_ref, acc_ref)

    @pl.when(qb == 0)
    def _():
        for g in range(groups):
            put_mask_rows(g, selb_ref[g, 0:bpg, :])
        s = _dot(ks_ref[0:Q_BLOCK, :], qaug_ref[...]) + near_tab[Q_BLOCK:2 * Q_BLOCK, :]
        for g in range(groups):
            _attend(s[:, g * gl:(g + 1) * gl], None,
                    vsT_ref[g * ACC_ROWS:(g + 1) * ACC_ROWS, 0:Q_BLOCK], 0, g, m_ref, acc_ref)

    @pl.when(qb >= 1)
    def _():
        near_piece()

    for i in range(WINDOW // Q_BLOCK):
        k0 = t0 - i * Q_BLOCK

        @pl.when((t0 < WINDOW) & (k0 >= 0))
        def _(i=i, k0=k0):
            k0a = pl.multiple_of(k0, Q_BLOCK)
            r0 = _TAB_WIN + WINDOW - i * Q_BLOCK
            s = _dot(kw_ref[pl.ds(k0a, Q_BLOCK), :], qaug_ref[0:kvw, :]) + tab_ref[r0:r0 + Q_BLOCK, :]
            for g in range(groups):
                _attend(s[:, g * gl:(g + 1) * gl], None,
                        vwT_ref[g * ACC_ROWS:(g + 1) * ACC_ROWS, pl.ds(k0a, Q_BLOCK)], 1, g, m_ref, acc_ref)

    heads = groups * hpg
    pieces = []
    for g in range(groups):
        gate = [jnp.concatenate([gT_ref[j * heads + g * hpg + hh: j * heads + g * hpg + hh + 1, :]
                                 for hh in range(hpg)], axis=1) for j in range(3)]
        o = gate[0] * ocmp_ref[g]
        for slot in range(2):
            acc = acc_ref[slot, g]
            o = o + gate[1 + slot] * (acc[0:hd, :] * (1.0 / acc[hd:hd + 1, :]))
        for hh in range(hpg):
            pieces.append(o[:, hh * Q_BLOCK:(hh + 1) * Q_BLOCK])
    y_ref[...] = jnp.concatenate(pieces, axis=0).T.astype(BF16)


_TAB_NEAR = 32
_TAB_WIN = _TAB_NEAR + 2 * Q_BLOCK


def _nsa_prompt_tables(rel_bias, n_lanes):
    nb, heads = rel_bias.shape
    tbl = rel_bias.astype(F32) * LOG2E
    lanes = lambda x: jnp.repeat(x, n_lanes // heads, axis=1)
    vals = jnp.concatenate([lanes(tbl), jnp.full((1, n_lanes), NEG, F32), jnp.zeros((1, n_lanes), F32),
                            lanes(tbl - tbl[nb - 1:nb])], axis=0)
    c = np.arange(Q_BLOCK)[None, :]
    i8 = np.arange(8)[:, None]
    ids = []
    for base in (-(CMP_BLOCK - 1), 3 * CMP_BLOCK + 1, 7 * CMP_BLOCK + 1):
        d = base + c - CMP_BLOCK * i8
        ids.append(np.where(d >= 0, nb + 2 + _bucket_np(d, nb), nb + 1))
    ids.append(np.full((8, Q_BLOCK), nb + 1))
    d = Q_BLOCK + c - np.arange(2 * Q_BLOCK)[:, None]
    ids.append(np.where(d >= 0, _bucket_np(d, nb), nb))
    d = WINDOW + c - np.arange(WINDOW + Q_BLOCK)[:, None]
    ids.append(np.where((d >= 0) & (d < WINDOW), _bucket_np(d, nb), nb))
    return vals[nb - 1:nb], _bias_table(np.concatenate(ids, axis=0), vals)


def _nsa_prompt(qT, gT, kc, vcT, ks, vsT, kw, vwT, rel_bias, groups, hpg, hd):
    qw, t = qT.shape
    ncb = kc.shape[0]
    nsb = t // SEL_BLOCK
    gl = hpg * Q_BLOCK
    kvw = groups * hd
    assert t % KEY_GROUP == 0 and Q_BLOCK == 4 * CMP_BLOCK and MAX_DIST <= Q_BLOCK
    tables = _nsa_prompt_tables(rel_bias, groups * gl)
    resident = (kc, vcT, ks, vsT, kw, vwT) + tuple(tables)
    return pl.pallas_call(
        functools.partial(_nsa_prompt_kernel, groups=groups, hpg=hpg, hd=hd),
        grid=(t // Q_BLOCK,),
        in_specs=[pl.BlockSpec((qw, Q_BLOCK), lambda i: (0, i)),
                  pl.BlockSpec((gT.shape[0], Q_BLOCK), lambda i: (0, i))]
                 + [_const_spec(a.shape) for a in resident],
        out_specs=pl.BlockSpec((Q_BLOCK, qw), lambda i: (i, 0)),
        out_shape=jax.ShapeDtypeStruct((t, qw), BF16),
        scratch_shapes=[pltpu.VMEM((2 * kvw, groups * gl), BF16),
                        pltpu.VMEM((KEY_GROUP + 8, groups * gl), F32),
                        pltpu.VMEM((KEY_GROUP + 8, groups * gl), F32),
                        pltpu.VMEM((ncb, groups * gl), F32),
                        pltpu.VMEM((groups, ncb, Q_BLOCK), F32),
                        pltpu.VMEM((groups, nsb, Q_BLOCK), F32),
                        pltpu.VMEM((groups, hd, gl), F32),
                        pltpu.VMEM((2, groups, 1, gl), F32),
                        pltpu.VMEM((2, groups, ACC_ROWS, gl), F32)],
        compiler_params=_cparams(), name="nsa_prompt",
    )(qT, gT, *resident)


def _split3(x):
    hi = x.astype(BF16)
    r1 = x - hi.astype(F32)
    mid = r1.astype(BF16)
    lo = (r1 - mid.astype(F32)).astype(BF16)
    return hi, mid, lo


def _nsa_sample_kernel(pt_ref, cache_ref, qbd_ref, qn_ref, gates_ref, kvn_ref, winn_ref, cwin_ref,
                       wkv_ref, pe_ref, perm_ref, kg_ref, gsum_ref,
                       cb_ref, sb_ref, wb_ref, b0_ref, e_ref,
                       o_ref,
                       ringa_ref, ringb_ref, bufkv_ref, sema, semb, p_ref,
                       *, hd, n_pages, page, chunk):
    b = pl.program_id(0)
    nb = pl.num_programs(0)
    kvw = kg_ref.shape[1]
    past = n_pages * page
    ncb = past // CMP_BLOCK
    nsb = past // SEL_BLOCK
    n_chunks = n_pages // chunk
    bpp = 2 * page // CMP_BLOCK

    def chunk_copy(bb, q, p):
        if q < n_chunks:
            return pltpu.make_async_copy(cache_ref.at[pt_ref[bb, q * chunk + p], 0:2 * kvw, :],
                                         ringa_ref.at[q % 2, p], sema.at[q % 2])
        ch = q - n_chunks
        return pltpu.make_async_copy(cache_ref.at[pt_ref[bb, ch * chunk + p], 2 * kvw:4 * kvw, :],
                                     ringb_ref.at[ch % 2, :, pl.ds(p * page, page)], semb.at[ch % 2])

    def for_pages(n, fn, unroll=1):
        def body(p, c):
            fn(p)
            return c
        lax.fori_loop(0, n, body, 0, unroll=unroll)

    def start_chunk(bb, q):
        for_pages(chunk, lambda p: chunk_copy(bb, q, p).start(), unroll=4)

    def wait_chunk(q):
        for_pages(chunk, lambda p: chunk_copy(b, q, p).wait(), unroll=4)

    def request_ahead(q):
        if q + 2 < 2 * n_chunks:
            start_chunk(b, q + 2)
        else:
            @pl.when(b + 1 < nb)
            def _():
                start_chunk(b + 1, q + 2 - 2 * n_chunks)

    @pl.when(b == 0)
    def _():
        start_chunk(0, 0)
        start_chunk(0, 1)

    qbd = qbd_ref[0]

    for ch in range(n_chunks):
        wait_chunk(ch)

        def regroup(q, ch=ch):
            xt = jnp.concatenate([ringa_ref[ch % 2, 2 * q], ringa_ref[ch % 2, 2 * q + 1]], axis=1)
            y = _dot_nt(perm_ref[...], (xt + pe_ref[...]).astype(BF16))
            blocks = pl.ds(pl.multiple_of((ch * (chunk // 2) + q) * bpp, bpp), bpp)
            for r in range(CMP_BLOCK):
                bufkv_ref[r, blocks, :] = y[r * bpp:(r + 1) * bpp, :]
        for_pages(chunk // 2, regroup, unroll=4)
        request_ahead(ch)

    half = ncb // 2
    zs = [jnp.zeros((half, 2 * kvw), F32), jnp.zeros((ncb - half, 2 * kvw), F32)]
    for r in range(CMP_BLOCK):
        zs[0] = zs[0] + _dot(bufkv_ref[r, 0:half, :].astype(BF16), wkv_ref[r])
        zs[1] = zs[1] + _dot(bufkv_ref[r, half:ncb, :].astype(BF16), wkv_ref[r])
    z = jnp.concatenate(zs, axis=0)
    vc = z[:, kvw:2 * kvw]
    kc = _group_rmsnorm(z[:, 0:kvw], kg_ref[...], hd).astype(BF16)
    s_c = _dot(kc, qbd) + cb_ref[...]
    m_c = jnp.max(s_c, axis=0, keepdims=True)
    p_c = jnp.exp(s_c - m_c)
    p_c = p_c * (1.0 / jnp.sum(p_c, axis=0, keepdims=True))
    o_cmp = _dot(p_c.T.astype(BF16), vc.astype(BF16))
    p_ref[...] = p_c
    ratio = SEL_BLOCK // CMP_BLOCK
    imp = p_ref[pl.ds(0, nsb, stride=ratio), :]
    for k in range(1, ratio):
        imp = imp + p_ref[pl.ds(k, nsb, stride=ratio), :]
    hi, mid, lo = _split3(imp)
    gs = gsum_ref[...]
    imp = _dot(hi, gs) + _dot(mid, gs) + _dot(lo, gs)
    jb = lax.broadcasted_iota(jnp.int32, (nsb, LANES), 0)
    imp = jnp.where((jb == 0) | (jb == nsb - 1), FORCE_SCORE, imp)
    selb = jnp.full((nsb, LANES), NEG, F32)
    for _ in range(min(TOP_N, nsb + 1) - 1):
        mx = jnp.max(imp, axis=0, keepdims=True)
        idx = jnp.min(jnp.where(imp == mx, jb, nsb), axis=0, keepdims=True)
        hit = jb == idx
        selb = jnp.where(hit, 0.0, selb)
        imp = jnp.where(hit, -3.0e38, imp)
    rows = qn_ref.shape[1]
    sel_nat = selb.T[0:rows, :]

    qn = qn_ref[0]
    qnb = qn.astype(BF16)
    kvn = kvn_ref[0]
    b0 = b0_ref[:, 0:1]
    m_s = jnp.sum(qn * kvn[:, 2 * kvw:3 * kvw], axis=1, keepdims=True) + b0
    l_s = jnp.ones_like(m_s)
    acc = jnp.broadcast_to(kvn[:, 3 * kvw:4 * kvw], (rows, kvw))
    nks = e_ref.shape[1]
    bps = nks // SEL_BLOCK
    per_chunk = chunk * page // nks
    for ch in range(n_chunks):
        wait_chunk(n_chunks + ch)
        for sub in range(per_chunk):
            c = ch * per_chunk + sub
            keys = slice(sub * nks, (sub + 1) * nks)
            rolled = sel_nat if c == 0 else pltpu.roll(sel_nat, nsb - c * bps, 1)
            mask = _dot(rolled.astype(BF16), e_ref[...])
            s = (_dot(qnb, ringb_ref[ch % 2, 0:kvw, keys].astype(BF16)) + mask
                 + sb_ref[:, c * nks:(c + 1) * nks])
            m_new = jnp.maximum(m_s, jnp.max(s, axis=1, keepdims=True))
            alpha = jnp.exp(m_s - m_new)
            p = jnp.exp(s - m_new)
            l_s = alpha * l_s + jnp.sum(p, axis=1, keepdims=True)
            acc = alpha * acc + _dot_nt(p.astype(BF16), ringb_ref[ch % 2, kvw:2 * kvw, keys].astype(BF16))
            m_s = m_new
        request_ahead(n_chunks + ch)
    o_sel = acc * (1.0 / l_s)

    winn = winn_ref[0]
    s_w = _dot(qnb, cwin_ref[0, 0:kvw, :].astype(BF16)) + wb_ref[...]
    s_wn = jnp.sum(qn * winn[:, 0:kvw], axis=1, keepdims=True) + b0
    m_w = jnp.maximum(jnp.max(s_w, axis=1, keepdims=True), s_wn)
    p_w = jnp.exp(s_w - m_w)
    p_wn = jnp.exp(s_wn - m_w)
    l_w = jnp.sum(p_w, axis=1, keepdims=True) + p_wn
    o_win = (_dot_nt(p_w.astype(BF16), cwin_ref[0, kvw:2 * kvw, :].astype(BF16))
             + p_wn * winn[:, kvw:2 * kvw]) * (1.0 / l_w)

    g = gates_ref[0]
    out = g[0] * o_cmp[0:rows, :] + g[1] * o_sel + g[2] * o_win
    o_ref[0] = out[0:8, :]


def _nsa_sample(q, gates, kv_new, win_new, cache_kv, cache_win, page_table, lp, rel_bias,
                groups, hpg, hd):
    r = q.shape[0]
    heads = groups * hpg
    kvw = groups * hd
    n_phys, page = cache_kv.shape[0], cache_kv.shape[1]
    n_pages = page_table.shape[1]
    past = n_pages * page
    wbuf = cache_win.shape[1]
    ncb, nsb = past // CMP_BLOCK, past // SEL_BLOCK
    rows = 16
    chunk = min(32, n_pages // 2)
    nks = min(NKS_SAMPLE, chunk * page)
    assert n_pages % (2 * chunk) == 0 and chunk % 8 == 0 and (chunk * page) % nks == 0 and heads <= rows
    q3 = q.reshape(r, heads, hd)
    gmask = (jnp.arange(groups)[:, None] == (jnp.arange(heads) // hpg)[None, :]).astype(F32)
    qbd = jnp.einsum('rhd,gh->rgdh', q3, gmask).reshape(r, kvw, heads)
    qbd = jnp.pad(qbd, ((0, 0), (0, 0), (0, LANES - heads))).astype(BF16)
    qn = jnp.einsum('rhd,gh->rhgd', q3, gmask).reshape(r, heads, kvw)
    qn = jnp.pad(qn, ((0, 0), (0, rows - heads), (0, 0)))
    gts = gates[:, :3 * heads].reshape(r, heads, 3).transpose(0, 2, 1)
    gts = jnp.broadcast_to(gts[:, :, :, None], (r, 3, heads, kvw))
    gts = jnp.pad(gts, ((0, 0), (0, 0), (0, rows - heads), (0, 0)))
    wk = _compress_weight(lp['cmp_w_k'], groups, hd).reshape(CMP_BLOCK, kvw, kvw)
    wv = _compress_weight(lp['cmp_w_v'], groups, hd).reshape(CMP_BLOCK, kvw, kvw)
    zero = jnp.zeros_like(wk)
    wkv = jnp.concatenate([jnp.concatenate([wk, zero], axis=2),
                           jnp.concatenate([zero, wv], axis=2)], axis=1).astype(BF16)
    pe_t = jnp.concatenate([jnp.tile(lp['pe_k'].T, (groups, 1)), jnp.tile(lp['pe_v'].T, (groups, 1))], axis=0)
    pe_t = jnp.tile(pe_t, (1, 2 * page // CMP_BLOCK))
    bpp = 2 * page // CMP_BLOCK
    dst = np.arange(2 * page)
    src = (dst % bpp) * CMP_BLOCK + dst // bpp
    perm = jnp.asarray(src[:, None] == np.arange(2 * page)[None, :], BF16)
    kg = jnp.tile(lp['k_gain'][0], groups)[None, :]
    lane_g = np.where(np.arange(LANES) < heads, np.arange(LANES) // hpg, -1)
    gsum = jnp.asarray((lane_g[:, None] == lane_g[None, :]) & (lane_g[:, None] >= 0), BF16)
    nbk = rel_bias.shape[0]
    tbl = rel_bias.astype(F32)
    d_c = (past - (np.arange(ncb) * CMP_BLOCK + CMP_BLOCK - 1))[:, None]
    vals_t = jnp.concatenate([tbl, jnp.broadcast_to(tbl[:, :1], (nbk, LANES - heads))], axis=1)
    cb = _bias_table(np.broadcast_to(_bucket_np(d_c, nbk), (ncb, LANES)), vals_t)
    vals_n = jnp.concatenate([jnp.repeat(tbl, LANES, axis=1), jnp.full((1, heads * LANES), NEG, F32)], axis=0)

    def head_rows(ids):
        out = _bias_table(ids, vals_n).reshape(ids.shape[0], heads, LANES)
        out = jnp.transpose(out, (1, 0, 2)).reshape(heads, ids.shape[0] * LANES)
        return jnp.pad(out, ((0, rows - heads), (0, 0)))

    sb = head_rows(_bucket_np(past - np.arange(past), nbk).reshape(past // LANES, LANES))
    d_w = wbuf - np.arange(wbuf)
    wb = head_rows(np.where(d_w < WINDOW, _bucket_np(d_w, nbk), nbk).reshape(wbuf // LANES, LANES))
    b0 = jnp.pad(jnp.broadcast_to(tbl[0][:, None], (heads, LANES)), ((0, rows - heads), (0, 0)))
    blk = np.arange(nsb)[:, None]
    spread = jnp.asarray((blk == np.arange(nks)[None, :] // SEL_BLOCK), BF16)

    cache_t = jnp.transpose(cache_kv, (0, 2, 3, 4, 1)).reshape(n_phys, 4 * kvw, page)
    cwin_t = jnp.transpose(cache_win, (0, 2, 3, 4, 1)).reshape(r, 2 * kvw, wbuf)
    kvn = kv_new.reshape(r, 1, 4 * kvw)
    winn = win_new.reshape(r, 1, 2 * kvw)
    per_b = lambda s: pl.BlockSpec((1,) + s, lambda b, pt: (b,) + (0,) * len(s))
    const = lambda a: pl.BlockSpec(a.shape, lambda b, pt: (0,) * a.ndim, pipeline_mode=pl.Buffered(1))
    consts = (wkv, pe_t, perm, kg, gsum, cb, sb, wb, b0, spread)
    grid_spec = pltpu.PrefetchScalarGridSpec(
        num_scalar_prefetch=1, grid=(r,),
        in_specs=[pl.BlockSpec(memory_space=pl.ANY), per_b((kvw, LANES)), per_b((rows, kvw)),
                  per_b((3, rows, kvw)), per_b((1, 4 * kvw)), per_b((1, 2 * kvw)),
                  per_b((2 * kvw, wbuf))]
                 + [const(a) for a in consts],
        out_specs=per_b((8, kvw)),
        scratch_shapes=[pltpu.VMEM((2, chunk, 2 * kvw, page), F32),
                        pltpu.VMEM((2, 2 * kvw, chunk * page), F32),
                        pltpu.VMEM((CMP_BLOCK, ncb, 2 * kvw), F32),
                        pltpu.SemaphoreType.DMA((2,)), pltpu.SemaphoreType.DMA((2,)),
                        pltpu.VMEM((ncb, LANES), F32)])
    o8 = pl.pallas_call(
        functools.partial(_nsa_sample_kernel, hd=hd, n_pages=n_pages, page=page, chunk=chunk),
        grid_spec=grid_spec,
        out_shape=jax.ShapeDtypeStruct((r, 8, kvw), F32),
        compiler_params=_cparams(), name="nsa_sample",
    )(page_table, cache_t, qbd, qn, gts, kvn, winn, cwin_t, *consts)
    o4 = o8[:, :heads, :].reshape(r, heads, groups, hd)
    y = jnp.take_along_axis(o4, (jnp.arange(heads) // hpg)[None, :, None, None], axis=2)
    return y.reshape(r, heads * hd).astype(BF16)


def _mlp_ple(h, p_ref, gm_ref, wu_ref, wd_ref, gp_ref, wg_ref, wp_ref):
    up = _dot(_rms_rows(h, gm_ref[...]).astype(BF16), wu_ref[...])
    act = jnp.square(jnp.maximum(up, 0.0)).astype(BF16)
    h = h + _dot(act, wd_ref[...])
    gate = jax.nn.sigmoid(_dot(_rms_rows(h, gp_ref[...]).astype(BF16), wg_ref[...]))
    return h + gate * _dot(p_ref[...].astype(BF16), wp_ref[...])


def _even_tail_kernel(h_ref, ya_ref, yb_ref, p_ref, woa_ref, wob_ref,
                      gm_ref, wu_ref, wd_ref, gp_ref, wg_ref, wp_ref, o_ref):
    h = h_ref[...] + _dot(ya_ref[...], woa_ref[...]) + _dot(yb_ref[...], wob_ref[...])
    o_ref[...] = _mlp_ple(h, p_ref, gm_ref, wu_ref, wd_ref, gp_ref, wg_ref, wp_ref)


def _tail_weights(g_mlp, w_up, w_down, g_ple, w_gate, w_proj):
    return (g_mlp[None, :], w_up.astype(BF16), w_down.astype(BF16), g_ple[None, :],
            w_gate.astype(BF16), w_proj.astype(BF16))


def _ple_spec(p_all, layer, tm):
    if p_all.shape[1] == 1:
        return pl.BlockSpec((None, None, tm, p_all.shape[3]), lambda i: (layer, 0, i, 0))
    return pl.BlockSpec((None, tm, None, p_all.shape[3]), lambda i: (layer, i, 0, 0))


def _even_tail(h, ya, yb, p_all, layer, w_out, tail_w, tm=256):
    t, d = h.shape
    tm = min(tm, t)
    lw = ya.shape[1]
    woa, wob = w_out[:lw].astype(BF16), w_out[lw:].astype(BF16)
    row = lambda w: pl.BlockSpec((tm, w), lambda i: (i, 0))
    weights = (woa, wob) + tuple(tail_w)
    return pl.pallas_call(
        _even_tail_kernel, grid=(t // tm,),
        in_specs=[row(d), row(lw), row(yb.shape[1]), _ple_spec(p_all, layer, tm)]
                 + [_const_spec(w.shape) for w in weights],
        out_specs=row(d), out_shape=jax.ShapeDtypeStruct((t, d), F32),
        compiler_params=_cparams(), name="even_tail",
    )(h, ya, yb, p_all, *weights)


def _odd_kernel(h_ref, p_ref, gx_ref, wi_ref, lg_ref, lb_ref, sw_ref, sb_ref, wo_ref,
                gm_ref, wu_ref, wd_ref, gp_ref, wg_ref, wp_ref, o_ref, v_ref, *, sw_groups, single):
    h = h_ref[...]
    tm = h.shape[0]
    z = _gelu(_dot(_rms_rows(h, gx_ref[...]).astype(BF16), wi_ref[...]))
    width = z.shape[1] // 2
    u, v = z[:, :width], z[:, width:]
    mu = jnp.mean(v, axis=-1, keepdims=True)
    vc = v - mu
    vn = vc * lax.rsqrt(jnp.mean(vc * vc, axis=-1, keepdims=True) + EPS) * lg_ref[...] + lb_ref[...]
    v_ref[...] = vn
    gw = width // sw_groups
    if single:
        s = vn * sw_ref[...] + sb_ref[...]
    else:
        vb = vn.astype(BF16)
        parts = []
        for c in range(tm // CHUNK):
            cols = [_dot(sw_ref[g], vb[c * CHUNK:(c + 1) * CHUNK, g * gw:(g + 1) * gw])
                    for g in range(sw_groups)]
            parts.append(jnp.concatenate(cols, axis=1) + sb_ref[...])
        s = jnp.concatenate(parts, axis=0)
    h = h + _dot((u * s).astype(BF16), wo_ref[...])
    o_ref[...] = _mlp_ple(h, p_ref, gm_ref, wu_ref, wd_ref, gp_ref, wg_ref, wp_ref)


def _odd_layer(h, p_all, layer, g_mix, w_in, ln_g, ln_b, sgu_w, sgu_b, w_out, tail_w, single, tm=256):
    t, d = h.shape
    tm = min(tm, t)
    ng, ch, _ = sgu_w.shape
    width = w_in.shape[1] // 2
    gw = width // ng
    if single:
        sw = jnp.repeat(sgu_w[:, 0, 0], gw)[None, :]
        sb = jnp.repeat(sgu_b[:, 0], gw)[None, :]
    else:
        sw = (sgu_w * jnp.tril(jnp.ones((ch, ch), sgu_w.dtype))).astype(BF16)
        sb = jnp.repeat(sgu_b.T, gw, axis=1)
    weights = (g_mix[None, :], w_in.astype(BF16), ln_g[None, :], ln_b[None, :], sw, sb,
               w_out.astype(BF16)) + tuple(tail_w)
    row = lambda w: pl.BlockSpec((tm, w), lambda i: (i, 0))
    return pl.pallas_call(
        functools.partial(_odd_kernel, sw_groups=ng, single=single), grid=(t // tm,),
        in_specs=[row(d), _ple_spec(p_all, layer, tm)] + [_const_spec(w.shape) for w in weights],
        out_specs=[row(d), row(width)],
        out_shape=[jax.ShapeDtypeStruct((t, d), F32), jax.ShapeDtypeStruct((t, width), F32)],
        compiler_params=_cparams(), name="odd_single" if single else "odd_prompt",
    )(h, p_all, *weights)


def kernel(x_prompt, x_sample, cache_kv, cache_win, state_conv, state_h, page_table, p_prompt, p_sample,
           rel_bias, g_mix, g_mlp, w_up, w_down, g_ple, w_ple_gate, w_ple_proj,
           w_in_even, w_out_even, conv_w, conv_b, rg_w_a, rg_b_a, rg_w_x, rg_b_x, rg_lambda,
           q_gain, k_gain, cmp_w_k, cmp_w_v, cmp_pe_k, cmp_pe_v,
           w_in_odd, ln_v_g, ln_v_b, sgu_w, sgu_b, w_out_odd):
    bsz, t, d = x_prompt.shape
    r = x_sample.shape[0]
    assert bsz == 1 and x_sample.shape[1] == 1
    depth = g_mix.shape[0]
    lw = conv_w.shape[-1]
    hd = q_gain.shape[-1]
    groups = cache_kv.shape[4]
    heads = rel_bias.shape[1]
    hpg = heads // groups
    qw, kvw = heads * hd, groups * hd
    dims = (lw, qw, kvw, hd, heads)
    wbuf = cache_win.shape[2]

    hp = x_prompt.reshape(t, d)
    hs = x_sample.reshape(r, d)
    kvp, kvs, wnp, wns, cvp, cvs, hhp, hhs, vvs = [], [], [], [], [], [], [], [], []
    for i in range(depth):
        tail_w = _tail_weights(g_mlp[i], w_up[i], w_down[i], g_ple[i], w_ple_gate[i], w_ple_proj[i])
        if i % 2 == 0:
            e = i // 2
            lp = {'cmp_w_k': cmp_w_k[e], 'cmp_w_v': cmp_w_v[e], 'pe_k': cmp_pe_k[e], 'pe_v': cmp_pe_v[e],
                  'k_gain': k_gain[e]}
            rg_w = _rglru_weights(conv_w[e], conv_b[e], rg_w_a[e], rg_b_a[e], rg_w_x[e], rg_b_x[e],
                                  rg_lambda[e])
            (xa, ga, kv_rows, win_rows, kcmp, vcmp, ks, kw, qT, vsT, vwT, gT) = _even_in_prompt(
                hp, g_mix[i], w_in_even[e], q_gain[e], k_gain[e], dims)
            ya, h_last = _rglru_prompt(xa, ga, rg_w)
            kc, vcT = _compress_prompt(kcmp, vcmp, cmp_w_k[e], cmp_w_v[e], cmp_pe_k[e], cmp_pe_v[e],
                                       k_gain[e, 0], groups, hd)
            yb = _nsa_prompt(qT, gT, kc, vcT, ks, vsT, kw, vwT, rel_bias, groups, hpg, hd)
            hp = _even_tail(hp, ya, yb, p_prompt, i, w_out_even[e], tail_w)
            kvp.append(kv_rows.reshape(1, t, 4, groups, hd))
            wnp.append(win_rows[t - wbuf:].reshape(1, wbuf, 2, groups, hd))
            cvp.append(xa[t - (conv_w.shape[1] - 1):][None])
            hhp.append(h_last)
            xa_s, ga_s, q_s, kv_s, win_s, gate_s = _even_in_sample(
                hs, g_mix[i], w_in_even[e], q_gain[e], k_gain[e], dims)
            ya_s, h_s = _rglru_sample(xa_s, ga_s, state_conv[e], state_h[e], rg_w)
            yb_s = _nsa_sample(q_s, gate_s, kv_s, win_s, cache_kv[e], cache_win[e], page_table, lp,
                               rel_bias, groups, hpg, hd)
            hs = _even_tail(hs, ya_s, yb_s, p_sample, i, w_out_even[e], tail_w)
            kvs.append(kv_s.reshape(r, 1, 4, groups, hd))
            wns.append(jnp.concatenate([cache_win[e][:, 1:], win_s.reshape(r, 1, 2, groups, hd)], axis=1))
            cvs.append(jnp.concatenate([state_conv[e][:, 1:], xa_s[:, None, :]], axis=1))
            hhs.append(h_s)
        else:
            o = i // 2
            hp, _ = _odd_layer(hp, p_prompt, i, g_mix[i], w_in_odd[o], ln_v_g[o], ln_v_b[o],
                               sgu_w[o], sgu_b[o], w_out_odd[o], tail_w, single=False)
            hs, v_new = _odd_layer(hs, p_sample, i, g_mix[i], w_in_odd[o], ln_v_g[o], ln_v_b[o],
                                   sgu_w[o], sgu_b[o], w_out_odd[o], tail_w, single=True)
            vvs.append(v_new[:, None, :])
    v_sample = jnp.stack(vvs) if vvs else jnp.zeros((0, r, 1, w_in_odd.shape[-1] // 2), F32)
    return (hp[None], hs[:, None, :], jnp.stack(kvp), jnp.stack(kvs), jnp.stack(wnp), jnp.stack(wns),
            jnp.stack(cvp), jnp.stack(cvs), jnp.stack(hhp), jnp.stack(hhs), v_sample)
```

```python
import functools
import math

import numpy as np
import jax
import jax.numpy as jnp
from jax import lax
from jax.experimental import pallas as pl
from jax.experimental.pallas import tpu as pltpu

F32 = jnp.float32
BF16 = jnp.bfloat16

EPS = 1e-6
NEG = -1e30
RG_C = 8.0
CMP_BLOCK = 32
SEL_BLOCK = 64
TOP_N = 16
WINDOW = 512
Q_BLOCK = 128
FORCE_SCORE = 1.0e4
MAX_DIST = 128
CHUNK = 128

LANES = 128
VMEM_LIMIT = 56 * 1024 * 1024

LOG2E = math.log2(math.e)
KEY_GROUP = 512
ACC_ROWS = 80
NKS_SAMPLE = 4096


def _cparams(n_axes=1):
    return pltpu.CompilerParams(dimension_semantics=("arbitrary",) * n_axes,
                                vmem_limit_bytes=VMEM_LIMIT)


def _const_spec(shape):
    nd = len(shape)
    return pl.BlockSpec(shape, lambda *_: (0,) * nd, pipeline_mode=pl.Buffered(1))


def _dot(a, b):
    return jnp.dot(a, b, preferred_element_type=F32)


def _dot_nt(a, b):
    return lax.dot_general(a, b, (((1,), (1,)), ((), ())), preferred_element_type=F32)


def _gelu(x):
    return 0.5 * x * (1.0 + jnp.tanh(math.sqrt(2.0 / math.pi) * (x + 0.044715 * (x * x * x))))


def _softplus(x):
    return jnp.maximum(x, 0.0) + jnp.log1p(jnp.exp(-jnp.abs(x)))


def _rms_rows(x, g):
    return x * lax.rsqrt(jnp.mean(x * x, axis=-1, keepdims=True) + EPS) * g


def _group_rmsnorm(x, gain, width):
    n = x.shape[-1] // width
    lane = lax.broadcasted_iota(jnp.int32, x.shape, 1)
    x2 = x * x
    ms = jnp.zeros_like(x)
    for i in range(n):
        inside = (lane >= i * width) & (lane < (i + 1) * width)
        s = jnp.sum(jnp.where(inside, x2, 0.0), axis=-1, keepdims=True)
        ms = jnp.where(inside, s, ms)
    return x * lax.rsqrt(ms * (1.0 / width) + EPS) * gain


def _bucket_np(dist, n_buckets):
    n = np.maximum(dist, 0)
    exact = n_buckets // 2
    nf = np.maximum(n, 1).astype(np.float32)
    large = exact + (np.log(nf / np.float32(exact)) / np.float32(math.log(MAX_DIST / exact))
                     * np.float32(n_buckets - exact)).astype(np.int32)
    return np.where(n < exact, n, np.minimum(large, n_buckets - 1)).astype(np.int32)


def _even_in_prompt_kernel(x_ref, g_ref, wn_ref, wt_ref, qg_ref, kg_ref,
                           xa_ref, ga_ref, kv_ref, win_ref, kc_ref, vc_ref,
                           ks_ref, kw_ref, qT_ref, vsT_ref, vwT_ref, gT_ref, *, lw, qw, kvw, hd):
    tm = x_ref.shape[0]
    hn = _rms_rows(x_ref[...], g_ref[...]).astype(BF16)
    zn = _dot(hn, wn_ref[...])
    xa_ref[...] = zn[:, :lw]
    ga_ref[...] = zn[:, lw:2 * lw]
    kv = [zn[:, 2 * lw + i * kvw: 2 * lw + (i + 1) * kvw] for i in range(6)]
    k_sel = _group_rmsnorm(kv[2], kg_ref[1:2, :], hd)
    k_win = _group_rmsnorm(kv[4], kg_ref[2:3, :], hd)
    kv_ref[:, 0 * kvw:1 * kvw] = kv[0]
    kv_ref[:, 1 * kvw:2 * kvw] = kv[1]
    kv_ref[:, 2 * kvw:3 * kvw] = k_sel
    kv_ref[:, 3 * kvw:4 * kvw] = kv[3]
    win_ref[:, :kvw] = k_win
    win_ref[:, kvw:] = kv[5]
    kc_ref[...] = kv[0]
    vc_ref[...] = kv[1]
    blk = lax.broadcasted_iota(jnp.int32, (tm, kvw), 0) // SEL_BLOCK
    col = lax.broadcasted_iota(jnp.int32, (tm, kvw), 1)
    ks_ref[:, :kvw] = k_sel.astype(BF16)
    ks_ref[:, kvw:] = jnp.where(col == blk, 1.0, 0.0).astype(BF16)
    kw_ref[...] = k_win.astype(BF16)

    zt = _dot_nt(wt_ref[...], hn)
    for h in range(qw // hd):
        blk = zt[h * hd:(h + 1) * hd, :]
        r = lax.rsqrt(jnp.mean(blk * blk, axis=0, keepdims=True) + EPS)
        qn = blk * r
        for j in range(0, tm, LANES):
            qT_ref[h * hd:(h + 1) * hd, j:j + LANES] = (
                qn[:, j:j + LANES] * qg_ref[h * hd:(h + 1) * hd, :]).astype(BF16)
    pad = ACC_ROWS - hd
    ones_rows = jnp.where(lax.broadcasted_iota(jnp.int32, (pad, tm), 0) == 0, 1.0, 0.0).astype(BF16)
    for out_ref, base in ((vsT_ref, qw), (vwT_ref, qw + kvw)):
        for g in range(kvw // hd):
            out_ref[g * ACC_ROWS:g * ACC_ROWS + hd, :] = zt[base + g * hd:base + (g + 1) * hd, :].astype(BF16)
            out_ref[g * ACC_ROWS + hd:(g + 1) * ACC_ROWS, :] = ones_rows
    gT_ref[...] = jax.nn.sigmoid(zt[qw + 2 * kvw:, :])


def _even_in_prompt(x, g_mix, w_in, q_gain, k_gain, dims, tm=512):
    t, d = x.shape
    lw, qw, kvw, hd, heads = dims
    tm = min(tm, t)
    c0, c1, c2 = 2 * lw, 2 * lw + qw, 2 * lw + qw + 6 * kvw
    wn = jnp.concatenate([w_in[:, :c0], w_in[:, c1:c2]], axis=1).astype(BF16)
    wg = w_in[:, c2:c2 + 3 * heads].reshape(d, heads, 3).transpose(0, 2, 1).reshape(d, 3 * heads)
    wt = jnp.concatenate([w_in[:, c0:c1], w_in[:, c1 + 3 * kvw:c1 + 4 * kvw],
                          w_in[:, c1 + 5 * kvw:c1 + 6 * kvw], wg,
                          jnp.zeros((d, 32 - 3 * heads), F32)], axis=1).T.astype(BF16)
    assert tm % KEY_GROUP == 0
    qg = jnp.broadcast_to((jnp.tile(q_gain, heads) * (hd ** -0.5 * LOG2E))[:, None], (qw, LANES)).astype(F32)
    vrows = (kvw // hd) * ACC_ROWS
    kg = jnp.tile(k_gain, (1, kvw // hd)).astype(F32)
    kg = jnp.concatenate([kg, jnp.zeros((5, kvw), F32)], axis=0)
    nt = wt.shape[0]
    row = lambda w: pl.BlockSpec((tm, w), lambda i: (i, 0))
    col = lambda r: pl.BlockSpec((r, tm), lambda i: (0, i))
    outs = pl.pallas_call(
        functools.partial(_even_in_prompt_kernel, lw=lw, qw=qw, kvw=kvw, hd=hd),
        grid=(t // tm,),
        in_specs=[row(d), _const_spec((1, d)), _const_spec(wn.shape), _const_spec(wt.shape),
                  _const_spec(qg.shape), _const_spec(kg.shape)],
        out_specs=[row(lw), row(lw), row(4 * kvw), row(2 * kvw), row(kvw), row(kvw),
                   row(2 * kvw), row(kvw), col(qw), col(vrows), col(vrows), col(32)],
        out_shape=[jax.ShapeDtypeStruct((t, lw), F32), jax.ShapeDtypeStruct((t, lw), F32),
                   jax.ShapeDtypeStruct((t, 4 * kvw), F32), jax.ShapeDtypeStruct((t, 2 * kvw), F32),
                   jax.ShapeDtypeStruct((t, kvw), F32), jax.ShapeDtypeStruct((t, kvw), F32),
                   jax.ShapeDtypeStruct((t, 2 * kvw), BF16), jax.ShapeDtypeStruct((t, kvw), BF16),
                   jax.ShapeDtypeStruct((qw, t), BF16), jax.ShapeDtypeStruct((vrows, t), BF16),
                   jax.ShapeDtypeStruct((vrows, t), BF16), jax.ShapeDtypeStruct((32, t), F32)],
        compiler_params=_cparams(), name="even_in_prompt",
    )(x, g_mix[None, :], wn, wt, qg, kg)
    return outs


def _even_in_sample_kernel(x_ref, g_ref, w_ref, qg_ref, kg_ref,
                           xa_ref, ga_ref, q_ref, kv_ref, win_ref, gate_ref, *, lw, qw, kvw, hd):
    hn = _rms_rows(x_ref[...], g_ref[...]).astype(BF16)
    z = _dot(hn, w_ref[...])
    xa_ref[...] = z[:, :lw]
    ga_ref[...] = z[:, lw:2 * lw]
    q_ref[...] = _group_rmsnorm(z[:, 2 * lw:2 * lw + qw], qg_ref[...], hd)
    o = 2 * lw + qw
    kv = [z[:, o + i * kvw: o + (i + 1) * kvw] for i in range(6)]
    kv_ref[:, 0 * kvw:1 * kvw] = kv[0]
    kv_ref[:, 1 * kvw:2 * kvw] = kv[1]
    kv_ref[:, 2 * kvw:3 * kvw] = _group_rmsnorm(kv[2], kg_ref[1:2, :], hd)
    kv_ref[:, 3 * kvw:4 * kvw] = kv[3]
    win_ref[:, :kvw] = _group_rmsnorm(kv[4], kg_ref[2:3, :], hd)
    win_ref[:, kvw:] = kv[5]
    gate_ref[...] = jax.nn.sigmoid(z[:, o + 6 * kvw:])


def _even_in_sample(x, g_mix, w_in, q_gain, k_gain, dims):
    r, d = x.shape
    lw, qw, kvw, hd, heads = dims
    n_in = w_in.shape[1]
    pad = (-n_in) % LANES
    w = jnp.pad(w_in, ((0, 0), (0, pad))).astype(BF16)
    gw = n_in + pad - (2 * lw + qw + 6 * kvw)
    qg = (jnp.tile(q_gain, heads) * hd ** -0.5)[None, :].astype(F32)
    kg = jnp.tile(k_gain, (1, kvw // hd)).astype(F32)
    kg = jnp.concatenate([kg, jnp.zeros((5, kvw), F32)], axis=0)
    full = lambda s: pl.BlockSpec(s, lambda i: (0,) * len(s))
    return pl.pallas_call(
        functools.partial(_even_in_sample_kernel, lw=lw, qw=qw, kvw=kvw, hd=hd),
        grid=(1,),
        in_specs=[full((r, d)), full((1, d)), full(w.shape), full(qg.shape), full(kg.shape)],
        out_specs=[full((r, lw)), full((r, lw)), full((r, qw)), full((r, 4 * kvw)),
                   full((r, 2 * kvw)), full((r, gw))],
        out_shape=[jax.ShapeDtypeStruct((r, lw), F32), jax.ShapeDtypeStruct((r, lw), F32),
                   jax.ShapeDtypeStruct((r, qw), F32), jax.ShapeDtypeStruct((r, 4 * kvw), F32),
                   jax.ShapeDtypeStruct((r, 2 * kvw), F32), jax.ShapeDtypeStruct((r, gw), F32)],
        compiler_params=_cparams(), name="even_in_sample",
    )(x, g_mix[None, :], w, qg, kg)


def _rglru_gates(conv, wa_ref, ba_ref, wx_ref, bx_ref, lam_ref):
    cb = conv.astype(BF16)
    r = jax.nn.sigmoid(_dot(cb, wa_ref[...]) + ba_ref[...])
    i = jax.nn.sigmoid(_dot(cb, wx_ref[...]) + bx_ref[...])
    log_a = -RG_C * r * _softplus(-lam_ref[...])
    a = jnp.exp(log_a)
    th = jnp.tanh(log_a)
    b = jnp.sqrt(-2.0 * th / (1.0 - th)) * (i * conv)
    return a, b


def _rglru_prompt_kernel(xa_ref, ga_ref, cw_ref, cb_ref, wa_ref, ba_ref, wx_ref, bx_ref, lam_ref,
                         ya_ref, hl_ref, xext_ref, h_ref):
    tr = xa_ref.shape[0]

    @pl.when(pl.program_id(0) == 0)
    def _():
        xext_ref[0:8, :] = jnp.zeros((8, xext_ref.shape[1]), F32)
        h_ref[...] = jnp.zeros(h_ref.shape, F32)

    x = xa_ref[...]
    xext_ref[8:8 + tr, :] = x
    conv = (cb_ref[...] + cw_ref[3:4, :] * x + cw_ref[2:3, :] * xext_ref[7:7 + tr, :]
            + cw_ref[1:2, :] * xext_ref[6:6 + tr, :] + cw_ref[0:1, :] * xext_ref[5:5 + tr, :])
    xext_ref[0:8, :] = xext_ref[tr:tr + 8, :]
    a, b = _rglru_gates(conv, wa_ref, ba_ref, wx_ref, bx_ref, lam_ref)
    sub = 8
    lw = a.shape[1]
    a = a.reshape(tr // sub, sub, lw)
    b = b.reshape(tr // sub, sub, lw)
    pos = lax.broadcasted_iota(jnp.int32, a.shape, 1)
    s = 1
    while s < sub:
        keep = pos >= s
        a_sh = jnp.where(keep, pltpu.roll(a, s, 1), 1.0)
        b_sh = jnp.where(keep, pltpu.roll(b, s, 1), 0.0)
        b = a * b_sh + b
        a = a * a_sh
        s *= 2
    h_in = h_ref[0:1, :]
    tiles = []
    for i in range(tr // sub):
        h_tile = a[i] * h_in + b[i]
        tiles.append(h_tile)
        h_in = h_tile[sub - 1:sub, :]
    hs = jnp.concatenate(tiles, axis=0)
    h_last = h_in
    h_ref[...] = jnp.broadcast_to(h_last, h_ref.shape)
    hl_ref[...] = h_last
    ya_ref[...] = (hs * _gelu(ga_ref[...])).astype(BF16)


def _block_diag(w):
    n, a, b = w.shape
    eye = jnp.eye(n, dtype=w.dtype)
    return (eye[:, None, :, None] * w[:, :, None, :]).reshape(n * a, n * b)


def _rglru_weights(conv_w, conv_b, w_a, b_a, w_x, b_x, lam):
    return (conv_w.astype(F32), conv_b[None, :], _block_diag(w_a).astype(BF16), b_a[None, :],
            _block_diag(w_x).astype(BF16), b_x[None, :], lam[None, :])


def _rglru_prompt(xa, ga, weights, tr=256):
    t, lw = xa.shape
    tr = min(tr, t)
    row = pl.BlockSpec((tr, lw), lambda i: (i, 0))
    return pl.pallas_call(
        _rglru_prompt_kernel, grid=(t // tr,),
        in_specs=[row, row] + [_const_spec(w.shape) for w in weights],
        out_specs=[row, pl.BlockSpec((1, lw), lambda i: (0, 0))],
        out_shape=[jax.ShapeDtypeStruct((t, lw), BF16), jax.ShapeDtypeStruct((1, lw), F32)],
        scratch_shapes=[pltpu.VMEM((tr + 8, lw), F32), pltpu.VMEM((8, lw), F32)],
        compiler_params=_cparams(), name="rglru_prompt",
    )(xa, ga, *weights)


def _rglru_sample_kernel(xa_ref, ga_ref, sc_ref, h0_ref, cw_ref, cb_ref, wa_ref, ba_ref, wx_ref,
                         bx_ref, lam_ref, ya_ref, h_ref):
    conv = (cb_ref[...] + cw_ref[3:4, :] * xa_ref[...] + cw_ref[2:3, :] * sc_ref[2]
            + cw_ref[1:2, :] * sc_ref[1] + cw_ref[0:1, :] * sc_ref[0])
    a, b = _rglru_gates(conv, wa_ref, ba_ref, wx_ref, bx_ref, lam_ref)
    h = a * h0_ref[...] + b
    h_ref[...] = h
    ya_ref[...] = (h * _gelu(ga_ref[...])).astype(BF16)


def _rglru_sample(xa, ga, state_conv, state_h, weights):
    r, lw = xa.shape
    sc = jnp.transpose(state_conv, (1, 0, 2))
    args = (xa, ga, sc, state_h) + tuple(weights)
    full = lambda s: pl.BlockSpec(s, lambda i: (0,) * len(s))
    return pl.pallas_call(
        _rglru_sample_kernel, grid=(1,),
        in_specs=[full(a.shape) for a in args],
        out_specs=[full((r, lw)), full((r, lw))],
        out_shape=[jax.ShapeDtypeStruct((r, lw), BF16), jax.ShapeDtypeStruct((r, lw), F32)],
        compiler_params=_cparams(), name="rglru_sample",
    )(*args)


def _compress_prompt_kernel(xk_ref, xv_ref, pek_ref, pev_ref, wk_ref, wvT_ref, kg_ref,
                            kc_ref, vcT_ref, *, hd):
    xk = (xk_ref[...] + pek_ref[...]).astype(BF16)
    kc = _dot(xk, wk_ref[...])
    kc_ref[...] = _group_rmsnorm(kc, kg_ref[...], hd).astype(BF16)
    xv = (xv_ref[...] + pev_ref[...]).astype(BF16)
    vcT_ref[...] = _dot_nt(wvT_ref[...], xv).astype(BF16)


def _compress_weight(w, groups, hd):
    w3 = w.reshape(CMP_BLOCK, hd, hd)
    eye = jnp.eye(groups, dtype=w.dtype)
    full = eye[None, :, None, :, None] * w3[:, None, :, None, :]
    return full.reshape(CMP_BLOCK * groups * hd, groups * hd)


def _compress_prompt(kcmp, vcmp, w_k, w_v, pe_k, pe_v, k_gain_cmp, groups, hd):
    t, kvw = kcmp.shape
    ncb = t // CMP_BLOCK
    kdim = CMP_BLOCK * kvw
    xk = kcmp.reshape(ncb, kdim)
    xv = vcmp.reshape(ncb, kdim)
    wk = _compress_weight(w_k, groups, hd).astype(BF16)
    wvT = _compress_weight(w_v, groups, hd).T.astype(BF16)
    pek = jnp.tile(pe_k, (1, groups)).reshape(1, kdim)
    pev = jnp.tile(pe_v, (1, groups)).reshape(1, kdim)
    kg = jnp.tile(k_gain_cmp, groups)[None, :]
    nb = min(LANES, ncb)
    return pl.pallas_call(
        functools.partial(_compress_prompt_kernel, hd=hd), grid=(ncb // nb,),
        in_specs=[pl.BlockSpec((nb, kdim), lambda i: (i, 0)), pl.BlockSpec((nb, kdim), lambda i: (i, 0)),
                  _const_spec((1, kdim)), _const_spec((1, kdim)), _const_spec(wk.shape),
                  _const_spec(wvT.shape), _const_spec((1, kvw))],
        out_specs=[pl.BlockSpec((nb, kvw), lambda i: (i, 0)), pl.BlockSpec((kvw, nb), lambda i: (0, i))],
        out_shape=[jax.ShapeDtypeStruct((ncb, kvw), BF16), jax.ShapeDtypeStruct((kvw, ncb), BF16)],
        compiler_params=_cparams(), name="compress_prompt",
    )(xk, xv, pek, pev, wk, wvT, kg)


def _tile_lanes(x, n):
    return jnp.concatenate([x] * n, axis=1)


def _attend(s, shift, vT, slot, g, m_ref, acc_ref, m_blk=None):
    m_old = m_ref[slot, g]
    if m_blk is None:
        m_blk = jnp.max(s, axis=0, keepdims=True)
    if shift is not None:
        m_blk = m_blk + shift
    m_new = jnp.maximum(m_old, m_blk)
    p = jnp.exp2(s - (m_new if shift is None else m_new - shift)).astype(BF16)
    acc_ref[slot, g] = jnp.exp2(m_old - m_new) * acc_ref[slot, g] + _dot(vT, p)
    m_ref[slot, g] = m_new


def _bias_table_kernel(id_ref, val_ref, o_ref, *, reps):
    ids = _tile_lanes(id_ref[...], reps)
    out = jnp.zeros(ids.shape, F32)
    for k in range(val_ref.shape[0]):
        out = jnp.where(ids == k, val_ref[k:k + 1, :], out)
    o_ref[...] = out


def _bias_table(ids, vals):
    r = ids.shape[0]
    reps = vals.shape[1] // LANES
    full = lambda s: pl.BlockSpec(s, lambda i: (0,) * len(s))
    return pl.pallas_call(
        functools.partial(_bias_table_kernel, reps=reps), grid=(1,),
        in_specs=[full(ids.shape), full(vals.shape)], out_specs=full((r, vals.shape[1])),
        out_shape=jax.ShapeDtypeStruct((r, vals.shape[1]), F32),
        compiler_params=_cparams(), name="bias_table",
    )(jnp.asarray(ids, jnp.int32), vals)


def _nsa_prompt_kernel(qT_ref, gT_ref, kc_ref, vcT_ref, ks_ref, vsT_ref, kw_ref, vwT_ref,
                       b31_ref, tab_ref,
                       y_ref,
                       qaug_ref, sa_ref, sb_ref, sc_ref, p4_ref, selb_ref, ocmp_ref, m_ref, acc_ref,
                       *, groups, hpg, hd):
    qb = pl.program_id(0)
    t0 = qb * Q_BLOCK
    ncb = kc_ref.shape[0]
    nsb = ncb * CMP_BLOCK // SEL_BLOCK
    gl = hpg * Q_BLOCK
    n_top = min(TOP_N, nsb)
    kvw = groups * hd
    bpg = KEY_GROUP // SEL_BLOCK

    rows = []
    for g in range(groups):
        top = jnp.concatenate([qT_ref[(g * hpg + hh) * hd:(g * hpg + hh + 1) * hd, :]
                               for hh in range(hpg)], axis=1)
        z = jnp.zeros_like(top)
        rows.append(jnp.concatenate([top if gg == g else z for gg in range(groups)], axis=1))
    qbd = jnp.concatenate(rows, axis=0)
    qaug_ref[0:kvw, :] = qbd
    qaug_ref[kvw:2 * kvw, :] = jnp.zeros((kvw, groups * gl), BF16)
    b31 = b31_ref[...]

    def compress_and_select(rc):
        nsv = rc * CMP_BLOCK // SEL_BLOCK
        jrow = lax.broadcasted_iota(jnp.int32, (rc, Q_BLOCK), 0)
        qcol = lax.broadcasted_iota(jnp.int32, (rc, Q_BLOCK), 1)
        vis = _tile_lanes(jrow * CMP_BLOCK + (CMP_BLOCK - 1) <= t0 + qcol, groups * hpg)
        s_c = _dot(kc_ref[0:rc, :], qbd) + b31
        sc_ref[0:rc, :] = jnp.where(vis, s_c, NEG)
        odd = (qb % 2) == 1
        v_a = jnp.where(qb == 0, 0, jnp.where(odd, 1, 2))
        w_a = pl.multiple_of(jnp.where(qb == 0, 0, jnp.where(odd, 4 * qb - 4, 4 * qb - 8)), 8)
        v_b = jnp.where((qb > 0) & jnp.logical_not(odd), 0, 3)
        w_b = pl.multiple_of(jnp.where((qb > 0) & jnp.logical_not(odd), 4 * qb, 0), 8)
        sc_ref[pl.ds(w_a, 8), :] = sc_ref[pl.ds(w_a, 8), :] + tab_ref[pl.ds(pl.multiple_of(v_a * 8, 8), 8), :]
        sc_ref[pl.ds(w_b, 8), :] = sc_ref[pl.ds(w_b, 8), :] + tab_ref[pl.ds(pl.multiple_of(v_b * 8, 8), 8), :]
        s_c = sc_ref[0:rc, :]
        m_c = jnp.max(s_c, axis=0, keepdims=True)
        p_c = jnp.where(vis, jnp.exp2(s_c - m_c), 0.0)
        l_c = jnp.sum(p_c, axis=0, keepdims=True)
        p_c = p_c * jnp.where(l_c > 0.0, 1.0 / l_c, 0.0)
        for g in range(groups):
            pg = p_c[:, g * gl:(g + 1) * gl]
            ocmp_ref[g] = _dot(vcT_ref[g * hd:(g + 1) * hd, 0:rc], pg.astype(BF16))
            p4 = pg[:, 0:Q_BLOCK]
            for hh in range(1, hpg):
                p4 = p4 + pg[:, hh * Q_BLOCK:(hh + 1) * Q_BLOCK]
            p4_ref[g, 0:rc, :] = p4

        ratio = SEL_BLOCK // CMP_BLOCK
        jb = lax.broadcasted_iota(jnp.int32, (nsv, Q_BLOCK), 0)
        qc = lax.broadcasted_iota(jnp.int32, (nsv, Q_BLOCK), 1)
        cur = (t0 + qc) // SEL_BLOCK
        forced = (jb == 0) | (jb == cur) | (jb == cur - 1)
        for g in range(groups):
            imp = p4_ref[g, pl.ds(0, nsv, stride=ratio), :]
            for k in range(1, ratio):
                imp = imp + p4_ref[g, pl.ds(k, nsv, stride=ratio), :]
            imp = jnp.where(forced, FORCE_SCORE, imp)
            imp = jnp.where(jb <= cur, imp, -1.0)
            selb = jnp.full((nsv, Q_BLOCK), NEG, F32)
            for _ in range(n_top):
                mx = jnp.max(imp, axis=0, keepdims=True)
                idx = jnp.min(jnp.where(imp == mx, jb, nsv), axis=0, keepdims=True)
                hit = jb == idx
                selb = jnp.where(hit, 0.0, selb)
                imp = jnp.where(hit, -3.0e38, imp)
            selb_ref[g, 0:nsv, :] = selb
            if nsv < nsb:
                selb_ref[g, nsv:nsb, :] = jnp.full((nsb - nsv, Q_BLOCK), NEG, F32)

    m_ref[...] = jnp.full(m_ref.shape, NEG, F32)
    acc_ref[...] = jnp.zeros(acc_ref.shape, F32)
    n_wk = WINDOW + Q_BLOCK

    def window_block():
        k0 = pl.multiple_of(t0 - WINDOW, Q_BLOCK)
        s = _dot(kw_ref[pl.ds(k0, n_wk), :], qaug_ref[0:kvw, :]) + tab_ref[_TAB_WIN:_TAB_WIN + n_wk, :]
        for g in range(groups):
            _attend(s[:, g * gl:(g + 1) * gl], None,
                    vwT_ref[g * ACC_ROWS:(g + 1) * ACC_ROWS, pl.ds(k0, n_wk)], 1, g, m_ref, acc_ref)

    tiers = list(range(LANES, ncb + 1, LANES)) if ncb % LANES == 0 else [ncb]
    n_vis = (t0 + Q_BLOCK) // CMP_BLOCK
    for i, rc in enumerate(tiers):
        lo = tiers[i - 1] if i else 0
        in_tier = (n_vis > lo) & (n_vis <= rc)

        @pl.when(in_tier & (t0 >= WINDOW))
        def _(rc=rc):
            compress_and_select(rc)
            window_block()

        if lo * CMP_BLOCK < WINDOW + Q_BLOCK:
            @pl.when(in_tier & (t0 < WINDOW))
            def _(rc=rc):
                compress_and_select(rc)


    def put_mask_rows(g, rows8):
        tile = jnp.concatenate([rows8, jnp.zeros_like(rows8)], axis=0).astype(BF16)
        for hh in range(hpg):
            qaug_ref[kvw:kvw + 2 * bpg, g * gl + hh * Q_BLOCK:g * gl + (hh + 1) * Q_BLOCK] = tile

    def far_scores(c, dst_ref):
        for g in range(groups):
            put_mask_rows(g, selb_ref[g, pl.ds(pl.multiple_of(c * bpg, bpg), bpg), :])
        k0 = pl.multiple_of(c * KEY_GROUP, KEY_GROUP)
        s = _dot(ks_ref[pl.ds(k0, KEY_GROUP), :], qaug_ref[...])
        dst_ref[0:KEY_GROUP, :] = s
        dst_ref[KEY_GROUP:KEY_GROUP + 8, :] = jnp.broadcast_to(jnp.max(s, axis=0, keepdims=True),
                                                               (8, groups * gl))

    def far_attend(c, src_ref, live_rows=None):
        k0 = pl.multiple_of(c * KEY_GROUP, KEY_GROUP)
        drop = None
        if live_rows is not None:
            drop = lax.broadcasted_iota(jnp.int32, (KEY_GROUP, Q_BLOCK), 0) >= live_rows
        for g in range(groups):
            sg = src_ref[0:KEY_GROUP, g * gl:(g + 1) * gl]
            m_blk = src_ref[KEY_GROUP:KEY_GROUP + 1, g * gl:(g + 1) * gl]
            if drop is not None:
                sg = jnp.where(_tile_lanes(drop, hpg), NEG, sg)
                m_blk = None
            _attend(sg, b31[:, g * gl:(g + 1) * gl],
                    vsT_ref[g * ACC_ROWS:(g + 1) * ACC_ROWS, pl.ds(k0, KEY_GROUP)], 0, g, m_ref, acc_ref,
                    m_blk=m_blk)

    far_len = jnp.maximum(t0 - Q_BLOCK, 0)
    n_far = far_len // KEY_GROUP
    live_tail = far_len - n_far * KEY_GROUP
    far_scores(0, sa_ref)

    def far_pair(j, carry):
        far_scores(2 * j + 1, sb_ref)
        far_attend(2 * j, sa_ref)
        far_scores(2 * j + 2, sa_ref)
        far_attend(2 * j + 1, sb_ref)
        return carry

    lax.fori_loop(0, n_far // 2, far_pair, 0)
    odd_far = n_far % 2 == 1

    @pl.when(odd_far)
    def _():
        far_scores(n_far, sb_ref)
        far_attend(n_far - 1, sa_ref)

    @pl.when(odd_far & (live_tail > 0))
    def _():
        far_attend(n_far, sb_ref, live_tail)

    @pl.when(jnp.logical_not(odd_far) & (live_tail > 0))
    def _():
        far_attend(n_far, sa_ref, live_tail)

    near_tab = tab_ref[_TAB_NEAR:_TAB_NEAR + 2 * Q_BLOCK, :]

    def near_piece():
        b0 = (t0 - Q_BLOCK) // SEL_BLOCK
        base = (b0 // bpg) * bpg
        nxt = jnp.minimum(base + bpg, nsb - bpg)
        jrow8 = lax.broadcasted_iota(jnp.int32, (bpg, Q_BLOCK), 0)
        for g in range(groups):
            lo = selb_ref[g, pl.ds(pl.multiple_of(base, bpg), bpg), :]
            hi = selb_ref[g, pl.ds(pl.multiple_of(nxt, bpg), bpg), :]
            put_mask_rows(g, jnp.where(jrow8 >= b0 - base, lo, hi))
        k0 = pl.multiple_of(t0 - Q_BLOCK, Q_BLOCK)
        s = _dot(ks_ref[pl.ds(k0, 2 * Q_BLOCK), :], qaug_ref[...]) + near_tab
        for g in range(groups):
            _attend(s[:, g * gl:(g + 1) * gl], None,
                    vsT_ref[g * ACC_ROWS:(g + 1) * ACC_ROWS, pl.ds(k0, 2 * Q_BLOCK)], 0, g, m_ref, acc_ref)

    @pl.when(qb == 0)
    def _():
        for g in range(groups):
            put_mask_rows(g, selb_ref[g, 0:bpg, :])
        s = _dot(ks_ref[0:Q_BLOCK, :], qaug_ref[...]) + near_tab[Q_BLOCK:2 * Q_BLOCK, :]
        for g in range(groups):
            _attend(s[:, g * gl:(g + 1) * gl], None,
                    vsT_ref[g * ACC_ROWS:(g + 1) * ACC_ROWS, 0:Q_BLOCK], 0, g, m_ref, acc_ref)

    @pl.when(qb >= 1)
    def _():
        near_piece()

    for i in range(WINDOW // Q_BLOCK):
        k0 = t0 - i * Q_BLOCK

        @pl.when((t0 < WINDOW) & (k0 >= 0))
        def _(i=i, k0=k0):
            k0a = pl.multiple_of(k0, Q_BLOCK)
            r0 = _TAB_WIN + WINDOW - i * Q_BLOCK
            s = _dot(kw_ref[pl.ds(k0a, Q_BLOCK), :], qaug_ref[0:kvw, :]) + tab_ref[r0:r0 + Q_BLOCK, :]
            for g in range(groups):
                _attend(s[:, g * gl:(g + 1) * gl], None,
                        vwT_ref[g * ACC_ROWS:(g + 1) * ACC_ROWS, pl.ds(k0a, Q_BLOCK)], 1, g, m_ref, acc_ref)

    heads = groups * hpg
    pieces = []
    for g in range(groups):
        gate = [jnp.concatenate([gT_ref[j * heads + g * hpg + hh: j * heads + g * hpg + hh + 1, :]
                                 for hh in range(hpg)], axis=1) for j in range(3)]
        o = gate[0] * ocmp_ref[g]
        for slot in range(2):
            acc = acc_ref[slot, g]
            o = o + gate[1 + slot] * (acc[0:hd, :] * (1.0 / acc[hd:hd + 1, :]))
        for hh in range(hpg):
            pieces.append(o[:, hh * Q_BLOCK:(hh + 1) * Q_BLOCK])
    y_ref[...] = jnp.concatenate(pieces, axis=0).T.astype(BF16)


_TAB_NEAR = 32
_TAB_WIN = _TAB_NEAR + 2 * Q_BLOCK


def _nsa_prompt_tables(rel_bias, n_lanes):
    nb, heads = rel_bias.shape
    tbl = rel_bias.astype(F32) * LOG2E
    lanes = lambda x: jnp.repeat(x, n_lanes // heads, axis=1)
    vals = jnp.concatenate([lanes(tbl), jnp.full((1, n_lanes), NEG, F32), jnp.zeros((1, n_lanes), F32),
                            lanes(tbl - tbl[nb - 1:nb])], axis=0)
    c = np.arange(Q_BLOCK)[None, :]
    i8 = np.arange(8)[:, None]
    ids = []
    for base in (-(CMP_BLOCK - 1), 3 * CMP_BLOCK + 1, 7 * CMP_BLOCK + 1):
        d = base + c - CMP_BLOCK * i8
        ids.append(np.where(d >= 0, nb + 2 + _bucket_np(d, nb), nb + 1))
    ids.append(np.full((8, Q_BLOCK), nb + 1))
    d = Q_BLOCK + c - np.arange(2 * Q_BLOCK)[:, None]
    ids.append(np.where(d >= 0, _bucket_np(d, nb), nb))
    d = WINDOW + c - np.arange(WINDOW + Q_BLOCK)[:, None]
    ids.append(np.where((d >= 0) & (d < WINDOW), _bucket_np(d, nb), nb))
    return vals[nb - 1:nb], _bias_table(np.concatenate(ids, axis=0), vals)


def _nsa_prompt(qT, gT, kc, vcT, ks, vsT, kw, vwT, rel_bias, groups, hpg, hd):
    qw, t = qT.shape
    ncb = kc.shape[0]
    nsb = t // SEL_BLOCK
    gl = hpg * Q_BLOCK
    kvw = groups * hd
    assert t % KEY_GROUP == 0 and Q_BLOCK == 4 * CMP_BLOCK and MAX_DIST <= Q_BLOCK
    tables = _nsa_prompt_tables(rel_bias, groups * gl)
    resident = (kc, vcT, ks, vsT, kw, vwT) + tuple(tables)
    return pl.pallas_call(
        functools.partial(_nsa_prompt_kernel, groups=groups, hpg=hpg, hd=hd),
        grid=(t // Q_BLOCK,),
        in_specs=[pl.BlockSpec((qw, Q_BLOCK), lambda i: (0, i)),
                  pl.BlockSpec((gT.shape[0], Q_BLOCK), lambda i: (0, i))]
                 + [_const_spec(a.shape) for a in resident],
        out_specs=pl.BlockSpec((Q_BLOCK, qw), lambda i: (i, 0)),
        out_shape=jax.ShapeDtypeStruct((t, qw), BF16),
        scratch_shapes=[pltpu.VMEM((2 * kvw, groups * gl), BF16),
                        pltpu.VMEM((KEY_GROUP + 8, groups * gl), F32),
                        pltpu.VMEM((KEY_GROUP + 8, groups * gl), F32),
                        pltpu.VMEM((ncb, groups * gl), F32),
                        pltpu.VMEM((groups, ncb, Q_BLOCK), F32),
                        pltpu.VMEM((groups, nsb, Q_BLOCK), F32),
                        pltpu.VMEM((groups, hd, gl), F32),
                        pltpu.VMEM((2, groups, 1, gl), F32),
                        pltpu.VMEM((2, groups, ACC_ROWS, gl), F32)],
        compiler_params=_cparams(), name="nsa_prompt",
    )(qT, gT, *resident)


def _split3(x):
    hi = x.astype(BF16)
    r1 = x - hi.astype(F32)
    mid = r1.astype(BF16)
    lo = (r1 - mid.astype(F32)).astype(BF16)
    return hi, mid, lo


def _nsa_sample_kernel(pt_ref, cache_ref, qbd_ref, qn_ref, gates_ref, kvn_ref, winn_ref, cwin_ref,
                       wkv_ref, pe_ref, perm_ref, kg_ref, gsum_ref,
                       cb_ref, sb_ref, wb_ref, b0_ref, e_ref,
                       o_ref,
                       ringa_ref, ringb_ref, bufkv_ref, sema, semb, p_ref,
                       *, hd, n_pages, page, chunk):
    b = pl.program_id(0)
    nb = pl.num_programs(0)
    kvw = kg_ref.shape[1]
    past = n_pages * page
    ncb = past // CMP_BLOCK
    nsb = past // SEL_BLOCK
    n_chunks = n_pages // chunk
    bpp = 2 * page // CMP_BLOCK

    def chunk_copy(bb, q, p):
        if q < n_chunks:
            return pltpu.make_async_copy(cache_ref.at[pt_ref[bb, q * chunk + p], 0:2 * kvw, :],
                                         ringa_ref.at[q % 2, p], sema.at[q % 2])
        ch = q - n_chunks
        return pltpu.make_async_copy(cache_ref.at[pt_ref[bb, ch * chunk + p], 2 * kvw:4 * kvw, :],
                                     ringb_ref.at[ch % 2, :, pl.ds(p * page, page)], semb.at[ch % 2])

    def for_pages(n, fn, unroll=1):
        def body(p, c):
            fn(p)
            return c
        lax.fori_loop(0, n, body, 0, unroll=unroll)

    def start_chunk(bb, q):
        for_pages(chunk, lambda p: chunk_copy(bb, q, p).start(), unroll=4)

    def wait_chunk(q):
        for_pages(chunk, lambda p: chunk_copy(b, q, p).wait(), unroll=4)

    def request_ahead(q):
        if q + 2 < 2 * n_chunks:
            start_chunk(b, q + 2)
        else:
            @pl.when(b + 1 < nb)
            def _():
                start_chunk(b + 1, q + 2 - 2 * n_chunks)

    @pl.when(b == 0)
    def _():
        start_chunk(0, 0)
        start_chunk(0, 1)

    qbd = qbd_ref[0]

    for ch in range(n_chunks):
        wait_chunk(ch)

        def regroup(q, ch=ch):
            xt = jnp.concatenate([ringa_ref[ch % 2, 2 * q], ringa_ref[ch % 2, 2 * q + 1]], axis=1)
            y = _dot_nt(perm_ref[...], (xt + pe_ref[...]).astype(BF16))
            blocks = pl.ds(pl.multiple_of((ch * (chunk // 2) + q) * bpp, bpp), bpp)
            for r in range(CMP_BLOCK):
                bufkv_ref[r, blocks, :] = y[r * bpp:(r + 1) * bpp, :]
        for_pages(chunk // 2, regroup, unroll=chunk // 2)
        request_ahead(ch)

    half = ncb // 2
    zs = [jnp.zeros((half, 2 * kvw), F32), jnp.zeros((ncb - half, 2 * kvw), F32)]
    for r in range(CMP_BLOCK):
        zs[0] = zs[0] + _dot(bufkv_ref[r, 0:half, :].astype(BF16), wkv_ref[r])
        zs[1] = zs[1] + _dot(bufkv_ref[r, half:ncb, :].astype(BF16), wkv_ref[r])
    z = jnp.concatenate(zs, axis=0)
    vc = z[:, kvw:2 * kvw]
    kc = _group_rmsnorm(z[:, 0:kvw], kg_ref[...], hd).astype(BF16)
    s_c = _dot(kc, qbd) + cb_ref[...]
    m_c = jnp.max(s_c, axis=0, keepdims=True)
    p_c = jnp.exp(s_c - m_c)
    p_c = p_c * (1.0 / jnp.sum(p_c, axis=0, keepdims=True))
    o_cmp = _dot(p_c.T.astype(BF16), vc.astype(BF16))
    p_ref[...] = p_c
    ratio = SEL_BLOCK // CMP_BLOCK
    imp = p_ref[pl.ds(0, nsb, stride=ratio), :]
    for k in range(1, ratio):
        imp = imp + p_ref[pl.ds(k, nsb, stride=ratio), :]
    hi, mid, lo = _split3(imp)
    gs = gsum_ref[...]
    imp = _dot(hi, gs) + _dot(mid, gs) + _dot(lo, gs)
    jb = lax.broadcasted_iota(jnp.int32, (nsb, LANES), 0)
    imp = jnp.where((jb == 0) | (jb == nsb - 1), FORCE_SCORE, imp)
    selb = jnp.full((nsb, LANES), NEG, F32)
    for _ in range(min(TOP_N, nsb + 1) - 1):
        mx = jnp.max(imp, axis=0, keepdims=True)
        idx = jnp.min(jnp.where(imp == mx, jb, nsb), axis=0, keepdims=True)
        hit = jb == idx
        selb = jnp.where(hit, 0.0, selb)
        imp = jnp.where(hit, -3.0e38, imp)
    rows = qn_ref.shape[1]
    sel_nat = selb.T[0:rows, :]

    qn = qn_ref[0]
    qnb = qn.astype(BF16)
    kvn = kvn_ref[0]
    b0 = b0_ref[:, 0:1]
    m_s = jnp.sum(qn * kvn[:, 2 * kvw:3 * kvw], axis=1, keepdims=True) + b0
    l_s = jnp.ones_like(m_s)
    acc = jnp.broadcast_to(kvn[:, 3 * kvw:4 * kvw], (rows, kvw))
    nks = e_ref.shape[1]
    bps = nks // SEL_BLOCK
    per_chunk = chunk * page // nks
    for ch in range(n_chunks):
        wait_chunk(n_chunks + ch)
        for sub in range(per_chunk):
            c = ch * per_chunk + sub
            keys = slice(sub * nks, (sub + 1) * nks)
            rolled = sel_nat if c == 0 else pltpu.roll(sel_nat, nsb - c * bps, 1)
            mask = _dot(rolled.astype(BF16), e_ref[...])
            s = (_dot(qnb, ringb_ref[ch % 2, 0:kvw, keys].astype(BF16)) + mask
                 + sb_ref[:, c * nks:(c + 1) * nks])
            m_new = jnp.maximum(m_s, jnp.max(s, axis=1, keepdims=True))
            alpha = jnp.exp(m_s - m_new)
            p = jnp.exp(s - m_new)
            l_s = alpha * l_s + jnp.sum(p, axis=1, keepdims=True)
            acc = alpha * acc + _dot_nt(p.astype(BF16), ringb_ref[ch % 2, kvw:2 * kvw, keys].astype(BF16))
            m_s = m_new
        request_ahead(n_chunks + ch)
    o_sel = acc * (1.0 / l_s)

    winn = winn_ref[0]
    s_w = _dot(qnb, cwin_ref[0, 0:kvw, :].astype(BF16)) + wb_ref[...]
    s_wn = jnp.sum(qn * winn[:, 0:kvw], axis=1, keepdims=True) + b0
    m_w = jnp.maximum(jnp.max(s_w, axis=1, keepdims=True), s_wn)
    p_w = jnp.exp(s_w - m_w)
    p_wn = jnp.exp(s_wn - m_w)
    l_w = jnp.sum(p_w, axis=1, keepdims=True) + p_wn
    o_win = (_dot_nt(p_w.astype(BF16), cwin_ref[0, kvw:2 * kvw, :].astype(BF16))
             + p_wn * winn[:, kvw:2 * kvw]) * (1.0 / l_w)

    g = gates_ref[0]
    out = g[0] * o_cmp[0:rows, :] + g[1] * o_sel + g[2] * o_win
    o_ref[0] = out[0:8, :]


def _nsa_sample(q, gates, kv_new, win_new, cache_kv, cache_win, page_table, lp, rel_bias,
                groups, hpg, hd):
    r = q.shape[0]
    heads = groups * hpg
    kvw = groups * hd
    n_phys, page = cache_kv.shape[0], cache_kv.shape[1]
    n_pages = page_table.shape[1]
    past = n_pages * page
    wbuf = cache_win.shape[1]
    ncb, nsb = past // CMP_BLOCK, past // SEL_BLOCK
    rows = 16
    chunk = min(32, n_pages // 2)
    nks = min(NKS_SAMPLE, chunk * page)
    assert n_pages % (2 * chunk) == 0 and chunk % 8 == 0 and (chunk * page) % nks == 0 and heads <= rows
    q3 = q.reshape(r, heads, hd)
    gmask = (jnp.arange(groups)[:, None] == (jnp.arange(heads) // hpg)[None, :]).astype(F32)
    qbd = jnp.einsum('rhd,gh->rgdh', q3, gmask).reshape(r, kvw, heads)
    qbd = jnp.pad(qbd, ((0, 0), (0, 0), (0, LANES - heads))).astype(BF16)
    qn = jnp.einsum('rhd,gh->rhgd', q3, gmask).reshape(r, heads, kvw)
    qn = jnp.pad(qn, ((0, 0), (0, rows - heads), (0, 0)))
    gts = gates[:, :3 * heads].reshape(r, heads, 3).transpose(0, 2, 1)
    gts = jnp.broadcast_to(gts[:, :, :, None], (r, 3, heads, kvw))
    gts = jnp.pad(gts, ((0, 0), (0, 0), (0, rows - heads), (0, 0)))
    wk = _compress_weight(lp['cmp_w_k'], groups, hd).reshape(CMP_BLOCK, kvw, kvw)
    wv = _compress_weight(lp['cmp_w_v'], groups, hd).reshape(CMP_BLOCK, kvw, kvw)
    zero = jnp.zeros_like(wk)
    wkv = jnp.concatenate([jnp.concatenate([wk, zero], axis=2),
                           jnp.concatenate([zero, wv], axis=2)], axis=1).astype(BF16)
    pe_t = jnp.concatenate([jnp.tile(lp['pe_k'].T, (groups, 1)), jnp.tile(lp['pe_v'].T, (groups, 1))], axis=0)
    pe_t = jnp.tile(pe_t, (1, 2 * page // CMP_BLOCK))
    bpp = 2 * page // CMP_BLOCK
    dst = np.arange(2 * page)
    src = (dst % bpp) * CMP_BLOCK + dst // bpp
    perm = jnp.asarray(src[:, None] == np.arange(2 * page)[None, :], BF16)
    kg = jnp.tile(lp['k_gain'][0], groups)[None, :]
    lane_g = np.where(np.arange(LANES) < heads, np.arange(LANES) // hpg, -1)
    gsum = jnp.asarray((lane_g[:, None] == lane_g[None, :]) & (lane_g[:, None] >= 0), BF16)
    nbk = rel_bias.shape[0]
    tbl = rel_bias.astype(F32)
    d_c = (past - (np.arange(ncb) * CMP_BLOCK + CMP_BLOCK - 1))[:, None]
    vals_t = jnp.concatenate([tbl, jnp.broadcast_to(tbl[:, :1], (nbk, LANES - heads))], axis=1)
    cb = _bias_table(np.broadcast_to(_bucket_np(d_c, nbk), (ncb, LANES)), vals_t)
    vals_n = jnp.concatenate([jnp.repeat(tbl, LANES, axis=1), jnp.full((1, heads * LANES), NEG, F32)], axis=0)

    def head_rows(ids):
        out = _bias_table(ids, vals_n).reshape(ids.shape[0], heads, LANES)
        out = jnp.transpose(out, (1, 0, 2)).reshape(heads, ids.shape[0] * LANES)
        return jnp.pad(out, ((0, rows - heads), (0, 0)))

    sb = head_rows(_bucket_np(past - np.arange(past), nbk).reshape(past // LANES, LANES))
    d_w = wbuf - np.arange(wbuf)
    wb = head_rows(np.where(d_w < WINDOW, _bucket_np(d_w, nbk), nbk).reshape(wbuf // LANES, LANES))
    b0 = jnp.pad(jnp.broadcast_to(tbl[0][:, None], (heads, LANES)), ((0, rows - heads), (0, 0)))
    blk = np.arange(nsb)[:, None]
    spread = jnp.asarray((blk == np.arange(nks)[None, :] // SEL_BLOCK), BF16)

    cache_t = jnp.transpose(cache_kv, (0, 2, 3, 4, 1)).reshape(n_phys, 4 * kvw, page)
    cwin_t = jnp.transpose(cache_win, (0, 2, 3, 4, 1)).reshape(r, 2 * kvw, wbuf)
    kvn = kv_new.reshape(r, 1, 4 * kvw)
    winn = win_new.reshape(r, 1, 2 * kvw)
    per_b = lambda s: pl.BlockSpec((1,) + s, lambda b, pt: (b,) + (0,) * len(s))
    const = lambda a: pl.BlockSpec(a.shape, lambda b, pt: (0,) * a.ndim, pipeline_mode=pl.Buffered(1))
    consts = (wkv, pe_t, perm, kg, gsum, cb, sb, wb, b0, spread)
    grid_spec = pltpu.PrefetchScalarGridSpec(
        num_scalar_prefetch=1, grid=(r,),
        in_specs=[pl.BlockSpec(memory_space=pl.ANY), per_b((kvw, LANES)), per_b((rows, kvw)),
                  per_b((3, rows, kvw)), per_b((1, 4 * kvw)), per_b((1, 2 * kvw)),
                  per_b((2 * kvw, wbuf))]
                 + [const(a) for a in consts],
        out_specs=per_b((8, kvw)),
        scratch_shapes=[pltpu.VMEM((2, chunk, 2 * kvw, page), F32),
                        pltpu.VMEM((2, 2 * kvw, chunk * page), F32),
                        pltpu.VMEM((CMP_BLOCK, ncb, 2 * kvw), F32),
                        pltpu.SemaphoreType.DMA((2,)), pltpu.SemaphoreType.DMA((2,)),
                        pltpu.VMEM((ncb, LANES), F32)])
    o8 = pl.pallas_call(
        functools.partial(_nsa_sample_kernel, hd=hd, n_pages=n_pages, page=page, chunk=chunk),
        grid_spec=grid_spec,
        out_shape=jax.ShapeDtypeStruct((r, 8, kvw), F32),
        compiler_params=_cparams(), name="nsa_sample",
    )(page_table, cache_t, qbd, qn, gts, kvn, winn, cwin_t, *consts)
    o4 = o8[:, :heads, :].reshape(r, heads, groups, hd)
    y = jnp.take_along_axis(o4, (jnp.arange(heads) // hpg)[None, :, None, None], axis=2)
    return y.reshape(r, heads * hd).astype(BF16)


def _mlp_ple(h, p_ref, gm_ref, wu_ref, wd_ref, gp_ref, wg_ref, wp_ref):
    up = _dot(_rms_rows(h, gm_ref[...]).astype(BF16), wu_ref[...])
    act = jnp.square(jnp.maximum(up, 0.0)).astype(BF16)
    h = h + _dot(act, wd_ref[...])
    gate = jax.nn.sigmoid(_dot(_rms_rows(h, gp_ref[...]).astype(BF16), wg_ref[...]))
    return h + gate * _dot(p_ref[...].astype(BF16), wp_ref[...])


def _even_tail_kernel(h_ref, ya_ref, yb_ref, p_ref, woa_ref, wob_ref,
                      gm_ref, wu_ref, wd_ref, gp_ref, wg_ref, wp_ref, o_ref):
    h = h_ref[...] + _dot(ya_ref[...], woa_ref[...]) + _dot(yb_ref[...], wob_ref[...])
    o_ref[...] = _mlp_ple(h, p_ref, gm_ref, wu_ref, wd_ref, gp_ref, wg_ref, wp_ref)


def _tail_weights(g_mlp, w_up, w_down, g_ple, w_gate, w_proj):
    return (g_mlp[None, :], w_up.astype(BF16), w_down.astype(BF16), g_ple[None, :],
            w_gate.astype(BF16), w_proj.astype(BF16))


def _ple_spec(p_all, layer, tm):
    if p_all.shape[1] == 1:
        return pl.BlockSpec((None, None, tm, p_all.shape[3]), lambda i: (layer, 0, i, 0))
    return pl.BlockSpec((None, tm, None, p_all.shape[3]), lambda i: (layer, i, 0, 0))


def _even_tail(h, ya, yb, p_all, layer, w_out, tail_w, tm=256):
    t, d = h.shape
    tm = min(tm, t)
    lw = ya.shape[1]
    woa, wob = w_out[:lw].astype(BF16), w_out[lw:].astype(BF16)
    row = lambda w: pl.BlockSpec((tm, w), lambda i: (i, 0))
    weights = (woa, wob) + tuple(tail_w)
    return pl.pallas_call(
        _even_tail_kernel, grid=(t // tm,),
        in_specs=[row(d), row(lw), row(yb.shape[1]), _ple_spec(p_all, layer, tm)]
                 + [_const_spec(w.shape) for w in weights],
        out_specs=row(d), out_shape=jax.ShapeDtypeStruct((t, d), F32),
        compiler_params=_cparams(), name="even_tail",
    )(h, ya, yb, p_all, *weights)


def _odd_kernel(h_ref, p_ref, gx_ref, wi_ref, lg_ref, lb_ref, sw_ref, sb_ref, wo_ref,
                gm_ref, wu_ref, wd_ref, gp_ref, wg_ref, wp_ref, o_ref, v_ref, *, sw_groups, single):
    h = h_ref[...]
    tm = h.shape[0]
    z = _gelu(_dot(_rms_rows(h, gx_ref[...]).astype(BF16), wi_ref[...]))
    width = z.shape[1] // 2
    u, v = z[:, :width], z[:, width:]
    mu = jnp.mean(v, axis=-1, keepdims=True)
    vc = v - mu
    vn = vc * lax.rsqrt(jnp.mean(vc * vc, axis=-1, keepdims=True) + EPS) * lg_ref[...] + lb_ref[...]
    v_ref[...] = vn
    gw = width // sw_groups
    if single:
        s = vn * sw_ref[...] + sb_ref[...]
    else:
        vb = vn.astype(BF16)
        parts = []
        for c in range(tm // CHUNK):
            cols = [_dot(sw_ref[g], vb[c * CHUNK:(c + 1) * CHUNK, g * gw:(g + 1) * gw])
                    for g in range(sw_groups)]
            parts.append(jnp.concatenate(cols, axis=1) + sb_ref[...])
        s = jnp.concatenate(parts, axis=0)
    h = h + _dot((u * s).astype(BF16), wo_ref[...])
    o_ref[...] = _mlp_ple(h, p_ref, gm_ref, wu_ref, wd_ref, gp_ref, wg_ref, wp_ref)


def _odd_layer(h, p_all, layer, g_mix, w_in, ln_g, ln_b, sgu_w, sgu_b, w_out, tail_w, single, tm=256):
    t, d = h.shape
    tm = min(tm, t)
    ng, ch, _ = sgu_w.shape
    width = w_in.shape[1] // 2
    gw = width // ng
    if single:
        sw = jnp.repeat(sgu_w[:, 0, 0], gw)[None, :]
        sb = jnp.repeat(sgu_b[:, 0], gw)[None, :]
    else:
        sw = (sgu_w * jnp.tril(jnp.ones((ch, ch), sgu_w.dtype))).astype(BF16)
        sb = jnp.repeat(sgu_b.T, gw, axis=1)
    weights = (g_mix[None, :], w_in.astype(BF16), ln_g[None, :], ln_b[None, :], sw, sb,
               w_out.astype(BF16)) + tuple(tail_w)
    row = lambda w: pl.BlockSpec((tm, w), lambda i: (i, 0))
    return pl.pallas_call(
        functools.partial(_odd_kernel, sw_groups=ng, single=single), grid=(t // tm,),
        in_specs=[row(d), _ple_spec(p_all, layer, tm)] + [_const_spec(w.shape) for w in weights],
        out_specs=[row(d), row(width)],
        out_shape=[jax.ShapeDtypeStruct((t, d), F32), jax.ShapeDtypeStruct((t, width), F32)],
        compiler_params=_cparams(), name="odd_single" if single else "odd_prompt",
    )(h, p_all, *weights)


def kernel(x_prompt, x_sample, cache_kv, cache_win, state_conv, state_h, page_table, p_prompt, p_sample,
           rel_bias, g_mix, g_mlp, w_up, w_down, g_ple, w_ple_gate, w_ple_proj,
           w_in_even, w_out_even, conv_w, conv_b, rg_w_a, rg_b_a, rg_w_x, rg_b_x, rg_lambda,
           q_gain, k_gain, cmp_w_k, cmp_w_v, cmp_pe_k, cmp_pe_v,
           w_in_odd, ln_v_g, ln_v_b, sgu_w, sgu_b, w_out_odd):
    bsz, t, d = x_prompt.shape
    r = x_sample.shape[0]
    assert bsz == 1 and x_sample.shape[1] == 1
    depth = g_mix.shape[0]
    lw = conv_w.shape[-1]
    hd = q_gain.shape[-1]
    groups = cache_kv.shape[4]
    heads = rel_bias.shape[1]
    hpg = heads // groups
    qw, kvw = heads * hd, groups * hd
    dims = (lw, qw, kvw, hd, heads)
    wbuf = cache_win.shape[2]

    hp = x_prompt.reshape(t, d)
    hs = x_sample.reshape(r, d)
    kvp, kvs, wnp, wns, cvp, cvs, hhp, hhs, vvs = [], [], [], [], [], [], [], [], []
    for i in range(depth):
        tail_w = _tail_weights(g_mlp[i], w_up[i], w_down[i], g_ple[i], w_ple_gate[i], w_ple_proj[i])
        if i % 2 == 0:
            e = i // 2
            lp = {'cmp_w_k': cmp_w_k[e], 'cmp_w_v': cmp_w_v[e], 'pe_k': cmp_pe_k[e], 'pe_v': cmp_pe_v[e],
                  'k_gain': k_gain[e]}
            rg_w = _rglru_weights(conv_w[e], conv_b[e], rg_w_a[e], rg_b_a[e], rg_w_x[e], rg_b_x[e],
                                  rg_lambda[e])
            (xa, ga, kv_rows, win_rows, kcmp, vcmp, ks, kw, qT, vsT, vwT, gT) = _even_in_prompt(
                hp, g_mix[i], w_in_even[e], q_gain[e], k_gain[e], dims)
            ya, h_last = _rglru_prompt(xa, ga, rg_w)
            kc, vcT = _compress_prompt(kcmp, vcmp, cmp_w_k[e], cmp_w_v[e], cmp_pe_k[e], cmp_pe_v[e],
                                       k_gain[e, 0], groups, hd)
            yb = _nsa_prompt(qT, gT, kc, vcT, ks, vsT, kw, vwT, rel_bias, groups, hpg, hd)
            hp = _even_tail(hp, ya, yb, p_prompt, i, w_out_even[e], tail_w)
            kvp.append(kv_rows.reshape(1, t, 4, groups, hd))
            wnp.append(win_rows[t - wbuf:].reshape(1, wbuf, 2, groups, hd))
            cvp.append(xa[t - (conv_w.shape[1] - 1):][None])
            hhp.append(h_last)
            xa_s, ga_s, q_s, kv_s, win_s, gate_s = _even_in_sample(
                hs, g_mix[i], w_in_even[e], q_gain[e], k_gain[e], dims)
            ya_s, h_s = _rglru_sample(xa_s, ga_s, state_conv[e], state_h[e], rg_w)
            yb_s = _nsa_sample(q_s, gate_s, kv_s, win_s, cache_kv[e], cache_win[e], page_table, lp,
                               rel_bias, groups, hpg, hd)
            hs = _even_tail(hs, ya_s, yb_s, p_sample, i, w_out_even[e], tail_w)
            kvs.append(kv_s.reshape(r, 1, 4, groups, hd))
            wns.append(jnp.concatenate([cache_win[e][:, 1:], win_s.reshape(r, 1, 2, groups, hd)], axis=1))
            cvs.append(jnp.concatenate([state_conv[e][:, 1:], xa_s[:, None, :]], axis=1))
            hhs.append(h_s)
        else:
            o = i // 2
            hp, _ = _odd_layer(hp, p_prompt, i, g_mix[i], w_in_odd[o], ln_v_g[o], ln_v_b[o],
                               sgu_w[o], sgu_b[o], w_out_odd[o], tail_w, single=False)
            hs, v_new = _odd_layer(hs, p_sample, i, g_mix[i], w_in_odd[o], ln_v_g[o], ln_v_b[o],
                                   sgu_w[o], sgu_b[o], w_out_odd[o], tail_w, single=True)
            vvs.append(v_new[:, None, :])
    v_sample = jnp.stack(vvs) if vvs else jnp.zeros((0, r, 1, w_in_odd.shape[-1] // 2), F32)
    return (hp[None], hs[:, None, :], jnp.stack(kvp), jnp.stack(kvs), jnp.stack(wnp), jnp.stack(wns),
            jnp.stack(cvp), jnp.stack(cvs), jnp.stack(hhp), jnp.stack(hhs), v_sample)
```

```python
import functools
import math

import numpy as np
import jax
import jax.numpy as jnp
from jax import lax
from jax.experimental import pallas as pl
from jax.experimental.pallas import tpu as pltpu

F32 = jnp.float32
BF16 = jnp.bfloat16

EPS = 1e-6
NEG = -1e30
RG_C = 8.0
CMP_BLOCK = 32
SEL_BLOCK = 64
TOP_N = 16
WINDOW = 512
Q_BLOCK = 128
FORCE_SCORE = 1.0e4
MAX_DIST = 128
CHUNK = 128

LANES = 128
VMEM_LIMIT = 56 * 1024 * 1024

LOG2E = math.log2(math.e)
KEY_GROUP = 512
ACC_ROWS = 80
NKS_SAMPLE = 4096


def _cparams(n_axes=1):
    return pltpu.CompilerParams(dimension_semantics=("arbitrary",) * n_axes,
                                vmem_limit_bytes=VMEM_LIMIT)


def _const_spec(shape):
    nd = len(shape)
    return pl.BlockSpec(shape, lambda *_: (0,) * nd, pipeline_mode=pl.Buffered(1))


def _dot(a, b):
    return jnp.dot(a, b, preferred_element_type=F32)


def _dot_nt(a, b):
    return lax.dot_general(a, b, (((1,), (1,)), ((), ())), preferred_element_type=F32)


def _gelu(x):
    return 0.5 * x * (1.0 + jnp.tanh(math.sqrt(2.0 / math.pi) * (x + 0.044715 * (x * x * x))))


def _softplus(x):
    return jnp.maximum(x, 0.0) + jnp.log1p(jnp.exp(-jnp.abs(x)))


def _rms_rows(x, g):
    return x * lax.rsqrt(jnp.mean(x * x, axis=-1, keepdims=True) + EPS) * g


def _group_rmsnorm(x, gain, width):
    n = x.shape[-1] // width
    lane = lax.broadcasted_iota(jnp.int32, x.shape, 1)
    x2 = x * x
    ms = jnp.zeros_like(x)
    for i in range(n):
        inside = (lane >= i * width) & (lane < (i + 1) * width)
        s = jnp.sum(jnp.where(inside, x2, 0.0), axis=-1, keepdims=True)
        ms = jnp.where(inside, s, ms)
    return x * lax.rsqrt(ms * (1.0 / width) + EPS) * gain


def _bucket_np(dist, n_buckets):
    n = np.maximum(dist, 0)
    exact = n_buckets // 2
    nf = np.maximum(n, 1).astype(np.float32)
    large = exact + (np.log(nf / np.float32(exact)) / np.float32(math.log(MAX_DIST / exact))
                     * np.float32(n_buckets - exact)).astype(np.int32)
    return np.where(n < exact, n, np.minimum(large, n_buckets - 1)).astype(np.int32)


def _even_in_prompt_kernel(x_ref, g_ref, wn_ref, wt_ref, qg_ref, kg_ref,
                           xa_ref, ga_ref, kv_ref, win_ref, kc_ref, vc_ref,
                           ks_ref, kw_ref, qT_ref, vsT_ref, vwT_ref, gT_ref, *, lw, qw, kvw, hd):
    tm = x_ref.shape[0]
    hn = _rms_rows(x_ref[...], g_ref[...]).astype(BF16)
    zn = _dot(hn, wn_ref[...])
    xa_ref[...] = zn[:, :lw]
    ga_ref[...] = zn[:, lw:2 * lw]
    kv = [zn[:, 2 * lw + i * kvw: 2 * lw + (i + 1) * kvw] for i in range(6)]
    k_sel = _group_rmsnorm(kv[2], kg_ref[1:2, :], hd)
    k_win = _group_rmsnorm(kv[4], kg_ref[2:3, :], hd)
    kv_ref[:, 0 * kvw:1 * kvw] = kv[0]
    kv_ref[:, 1 * kvw:2 * kvw] = kv[1]
    kv_ref[:, 2 * kvw:3 * kvw] = k_sel
    kv_ref[:, 3 * kvw:4 * kvw] = kv[3]
    win_ref[:, :kvw] = k_win
    win_ref[:, kvw:] = kv[5]
    kc_ref[...] = kv[0]
    vc_ref[...] = kv[1]
    blk = lax.broadcasted_iota(jnp.int32, (tm, kvw), 0) // SEL_BLOCK
    col = lax.broadcasted_iota(jnp.int32, (tm, kvw), 1)
    ks_ref[:, :kvw] = k_sel.astype(BF16)
    ks_ref[:, kvw:] = jnp.where(col == blk, 1.0, 0.0).astype(BF16)
    kw_ref[...] = k_win.astype(BF16)

    zt = _dot_nt(wt_ref[...], hn)
    for h in range(qw // hd):
        blk = zt[h * hd:(h + 1) * hd, :]
        r = lax.rsqrt(jnp.mean(blk * blk, axis=0, keepdims=True) + EPS)
        qn = blk * r
        for j in range(0, tm, LANES):
            qT_ref[h * hd:(h + 1) * hd, j:j + LANES] = (
                qn[:, j:j + LANES] * qg_ref[h * hd:(h + 1) * hd, :]).astype(BF16)
    pad = ACC_ROWS - hd
    ones_rows = jnp.where(lax.broadcasted_iota(jnp.int32, (pad, tm), 0) == 0, 1.0, 0.0).astype(BF16)
    for out_ref, base in ((vsT_ref, qw), (vwT_ref, qw + kvw)):
        for g in range(kvw // hd):
            out_ref[g * ACC_ROWS:g * ACC_ROWS + hd, :] = zt[base + g * hd:base + (g + 1) * hd, :].astype(BF16)
            out_ref[g * ACC_ROWS + hd:(g + 1) * ACC_ROWS, :] = ones_rows
    gT_ref[...] = jax.nn.sigmoid(zt[qw + 2 * kvw:, :])


def _even_in_prompt(x, g_mix, w_in, q_gain, k_gain, dims, tm=512):
    t, d = x.shape
    lw, qw, kvw, hd, heads = dims
    tm = min(tm, t)
    c0, c1, c2 = 2 * lw, 2 * lw + qw, 2 * lw + qw + 6 * kvw
    wn = jnp.concatenate([w_in[:, :c0], w_in[:, c1:c2]], axis=1).astype(BF16)
    wg = w_in[:, c2:c2 + 3 * heads].reshape(d, heads, 3).transpose(0, 2, 1).reshape(d, 3 * heads)
    wt = jnp.concatenate([w_in[:, c0:c1], w_in[:, c1 + 3 * kvw:c1 + 4 * kvw],
                          w_in[:, c1 + 5 * kvw:c1 + 6 * kvw], wg,
                          jnp.zeros((d, 32 - 3 * heads), F32)], axis=1).T.astype(BF16)
    assert tm % KEY_GROUP == 0
    qg = jnp.broadcast_to((jnp.tile(q_gain, heads) * (hd ** -0.5 * LOG2E))[:, None], (qw, LANES)).astype(F32)
    vrows = (kvw // hd) * ACC_ROWS
    kg = jnp.tile(k_gain, (1, kvw // hd)).astype(F32)
    kg = jnp.concatenate([kg, jnp.zeros((5, kvw), F32)], axis=0)
    nt = wt.shape[0]
    row = lambda w: pl.BlockSpec((tm, w), lambda i: (i, 0))
    col = lambda r: pl.BlockSpec((r, tm), lambda i: (0, i))
    outs = pl.pallas_call(
        functools.partial(_even_in_prompt_kernel, lw=lw, qw=qw, kvw=kvw, hd=hd),
        grid=(t // tm,),
        in_specs=[row(d), _const_spec((1, d)), _const_spec(wn.shape), _const_spec(wt.shape),
                  _const_spec(qg.shape), _const_spec(kg.shape)],
        out_specs=[row(lw), row(lw), row(4 * kvw), row(2 * kvw), row(kvw), row(kvw),
                   row(2 * kvw), row(kvw), col(qw), col(vrows), col(vrows), col(32)],
        out_shape=[jax.ShapeDtypeStruct((t, lw), F32), jax.ShapeDtypeStruct((t, lw), F32),
                   jax.ShapeDtypeStruct((t, 4 * kvw), F32), jax.ShapeDtypeStruct((t, 2 * kvw), F32),
                   jax.ShapeDtypeStruct((t, kvw), F32), jax.ShapeDtypeStruct((t, kvw), F32),
                   jax.ShapeDtypeStruct((t, 2 * kvw), BF16), jax.ShapeDtypeStruct((t, kvw), BF16),
                   jax.ShapeDtypeStruct((qw, t), BF16), jax.ShapeDtypeStruct((vrows, t), BF16),
                   jax.ShapeDtypeStruct((vrows, t), BF16), jax.ShapeDtypeStruct((32, t), F32)],
        compiler_params=_cparams(), name="even_in_prompt",
    )(x, g_mix[None, :], wn, wt, qg, kg)
    return outs


def _even_in_sample_kernel(x_ref, g_ref, w_ref, qg_ref, kg_ref,
                           xa_ref, ga_ref, q_ref, kv_ref, win_ref, gate_ref, *, lw, qw, kvw, hd):
    hn = _rms_rows(x_ref[...], g_ref[...]).astype(BF16)
    z = _dot(hn, w_ref[...])
    xa_ref[...] = z[:, :lw]
    ga_ref[...] = z[:, lw:2 * lw]
    q_ref[...] = _group_rmsnorm(z[:, 2 * lw:2 * lw + qw], qg_ref[...], hd)
    o = 2 * lw + qw
    kv = [z[:, o + i * kvw: o + (i + 1) * kvw] for i in range(6)]
    kv_ref[:, 0 * kvw:1 * kvw] = kv[0]
    kv_ref[:, 1 * kvw:2 * kvw] = kv[1]
    kv_ref[:, 2 * kvw:3 * kvw] = _group_rmsnorm(kv[2], kg_ref[1:2, :], hd)
    kv_ref[:, 3 * kvw:4 * kvw] = kv[3]
    win_ref[:, :kvw] = _group_rmsnorm(kv[4], kg_ref[2:3, :], hd)
    win_ref[:, kvw:] = kv[5]
    gate_ref[...] = jax.nn.sigmoid(z[:, o + 6 * kvw:])


def _even_in_sample(x, g_mix, w_in, q_gain, k_gain, dims):
    r, d = x.shape
    lw, qw, kvw, hd, heads = dims
    n_in = w_in.shape[1]
    pad = (-n_in) % LANES
    w = jnp.pad(w_in, ((0, 0), (0, pad))).astype(BF16)
    gw = n_in + pad - (2 * lw + qw + 6 * kvw)
    qg = (jnp.tile(q_gain, heads) * hd ** -0.5)[None, :].astype(F32)
    kg = jnp.tile(k_gain, (1, kvw // hd)).astype(F32)
    kg = jnp.concatenate([kg, jnp.zeros((5, kvw), F32)], axis=0)
    full = lambda s: pl.BlockSpec(s, lambda i: (0,) * len(s))
    return pl.pallas_call(
        functools.partial(_even_in_sample_kernel, lw=lw, qw=qw, kvw=kvw, hd=hd),
        grid=(1,),
        in_specs=[full((r, d)), full((1, d)), full(w.shape), full(qg.shape), full(kg.shape)],
        out_specs=[full((r, lw)), full((r, lw)), full((r, qw)), full((r, 4 * kvw)),
                   full((r, 2 * kvw)), full((r, gw))],
        out_shape=[jax.ShapeDtypeStruct((r, lw), F32), jax.ShapeDtypeStruct((r, lw), F32),
                   jax.ShapeDtypeStruct((r, qw), F32), jax.ShapeDtypeStruct((r, 4 * kvw), F32),
                   jax.ShapeDtypeStruct((r, 2 * kvw), F32), jax.ShapeDtypeStruct((r, gw), F32)],
        compiler_params=_cparams(), name="even_in_sample",
    )(x, g_mix[None, :], w, qg, kg)


def _rglru_gates(conv, wa_ref, ba_ref, wx_ref, bx_ref, lam_ref):
    cb = conv.astype(BF16)
    r = jax.nn.sigmoid(_dot(cb, wa_ref[...]) + ba_ref[...])
    i = jax.nn.sigmoid(_dot(cb, wx_ref[...]) + bx_ref[...])
    log_a = -RG_C * r * _softplus(-lam_ref[...])
    a = jnp.exp(log_a)
    th = jnp.tanh(log_a)
    b = jnp.sqrt(-2.0 * th / (1.0 - th)) * (i * conv)
    return a, b


def _rglru_prompt_kernel(xa_ref, ga_ref, cw_ref, cb_ref, wa_ref, ba_ref, wx_ref, bx_ref, lam_ref,
                         ya_ref, hl_ref, xext_ref, h_ref):
    tr = xa_ref.shape[0]

    @pl.when(pl.program_id(0) == 0)
    def _():
        xext_ref[0:8, :] = jnp.zeros((8, xext_ref.shape[1]), F32)
        h_ref[...] = jnp.zeros(h_ref.shape, F32)

    x = xa_ref[...]
    xext_ref[8:8 + tr, :] = x
    conv = (cb_ref[...] + cw_ref[3:4, :] * x + cw_ref[2:3, :] * xext_ref[7:7 + tr, :]
            + cw_ref[1:2, :] * xext_ref[6:6 + tr, :] + cw_ref[0:1, :] * xext_ref[5:5 + tr, :])
    xext_ref[0:8, :] = xext_ref[tr:tr + 8, :]
    a, b = _rglru_gates(conv, wa_ref, ba_ref, wx_ref, bx_ref, lam_ref)
    sub = 8
    lw = a.shape[1]
    a = a.reshape(tr // sub, sub, lw)
    b = b.reshape(tr // sub, sub, lw)
    pos = lax.broadcasted_iota(jnp.int32, a.shape, 1)
    s = 1
    while s < sub:
        keep = pos >= s
        a_sh = jnp.where(keep, pltpu.roll(a, s, 1), 1.0)
        b_sh = jnp.where(keep, pltpu.roll(b, s, 1), 0.0)
        b = a * b_sh + b
        a = a * a_sh
        s *= 2
    h_in = h_ref[0:1, :]
    tiles = []
    for i in range(tr // sub):
        h_tile = a[i] * h_in + b[i]
        tiles.append(h_tile)
        h_in = h_tile[sub - 1:sub, :]
    hs = jnp.concatenate(tiles, axis=0)
    h_last = h_in
    h_ref[...] = jnp.broadcast_to(h_last, h_ref.shape)
    hl_ref[...] = h_last
    ya_ref[...] = (hs * _gelu(ga_ref[...])).astype(BF16)


def _block_diag(w):
    n, a, b = w.shape
    eye = jnp.eye(n, dtype=w.dtype)
    return (eye[:, None, :, None] * w[:, :, None, :]).reshape(n * a, n * b)


def _rglru_weights(conv_w, conv_b, w_a, b_a, w_x, b_x, lam):
    return (conv_w.astype(F32), conv_b[None, :], _block_diag(w_a).astype(BF16), b_a[None, :],
            _block_diag(w_x).astype(BF16), b_x[None, :], lam[None, :])


def _rglru_prompt(xa, ga, weights, tr=256):
    t, lw = xa.shape
    tr = min(tr, t)
    row = pl.BlockSpec((tr, lw), lambda i: (i, 0))
    return pl.pallas_call(
        _rglru_prompt_kernel, grid=(t // tr,),
        in_specs=[row, row] + [_const_spec(w.shape) for w in weights],
        out_specs=[row, pl.BlockSpec((1, lw), lambda i: (0, 0))],
        out_shape=[jax.ShapeDtypeStruct((t, lw), BF16), jax.ShapeDtypeStruct((1, lw), F32)],
        scratch_shapes=[pltpu.VMEM((tr + 8, lw), F32), pltpu.VMEM((8, lw), F32)],
        compiler_params=_cparams(), name="rglru_prompt",
    )(xa, ga, *weights)


def _rglru_sample_kernel(xa_ref, ga_ref, sc_ref, h0_ref, cw_ref, cb_ref, wa_ref, ba_ref, wx_ref,
                         bx_ref, lam_ref, ya_ref, h_ref):
    conv = (cb_ref[...] + cw_ref[3:4, :] * xa_ref[...] + cw_ref[2:3, :] * sc_ref[2]
            + cw_ref[1:2, :] * sc_ref[1] + cw_ref[0:1, :] * sc_ref[0])
    a, b = _rglru_gates(conv, wa_ref, ba_ref, wx_ref, bx_ref, lam_ref)
    h = a * h0_ref[...] + b
    h_ref[...] = h
    ya_ref[...] = (h * _gelu(ga_ref[...])).astype(BF16)


def _rglru_sample(xa, ga, state_conv, state_h, weights):
    r, lw = xa.shape
    sc = jnp.transpose(state_conv, (1, 0, 2))
    args = (xa, ga, sc, state_h) + tuple(weights)
    full = lambda s: pl.BlockSpec(s, lambda i: (0,) * len(s))
    return pl.pallas_call(
        _rglru_sample_kernel, grid=(1,),
        in_specs=[full(a.shape) for a in args],
        out_specs=[full((r, lw)), full((r, lw))],
        out_shape=[jax.ShapeDtypeStruct((r, lw), BF16), jax.ShapeDtypeStruct((r, lw), F32)],
        compiler_params=_cparams(), name="rglru_sample",
    )(*args)


def _compress_prompt_kernel(xk_ref, xv_ref, pek_ref, pev_ref, wk_ref, wvT_ref, kg_ref,
                            kc_ref, vcT_ref, *, hd):
    xk = (xk_ref[...] + pek_ref[...]).astype(BF16)
    kc = _dot(xk, wk_ref[...])
    kc_ref[...] = _group_rmsnorm(kc, kg_ref[...], hd).astype(BF16)
    xv = (xv_ref[...] + pev_ref[...]).astype(BF16)
    vcT_ref[...] = _dot_nt(wvT_ref[...], xv).astype(BF16)


def _compress_weight(w, groups, hd):
    w3 = w.reshape(CMP_BLOCK, hd, hd)
    eye = jnp.eye(groups, dtype=w.dtype)
    full = eye[None, :, None, :, None] * w3[:, None, :, None, :]
    return full.reshape(CMP_BLOCK * groups * hd, groups * hd)


def _compress_prompt(kcmp, vcmp, w_k, w_v, pe_k, pe_v, k_gain_cmp, groups, hd):
    t, kvw = kcmp.shape
    ncb = t // CMP_BLOCK
    kdim = CMP_BLOCK * kvw
    xk = kcmp.reshape(ncb, kdim)
    xv = vcmp.reshape(ncb, kdim)
    wk = _compress_weight(w_k, groups, hd).astype(BF16)
    wvT = _compress_weight(w_v, groups, hd).T.astype(BF16)
    pek = jnp.tile(pe_k, (1, groups)).reshape(1, kdim)
    pev = jnp.tile(pe_v, (1, groups)).reshape(1, kdim)
    kg = jnp.tile(k_gain_cmp, groups)[None, :]
    nb = min(LANES, ncb)
    return pl.pallas_call(
        functools.partial(_compress_prompt_kernel, hd=hd), grid=(ncb // nb,),
        in_specs=[pl.BlockSpec((nb, kdim), lambda i: (i, 0)), pl.BlockSpec((nb, kdim), lambda i: (i, 0)),
                  _const_spec((1, kdim)), _const_spec((1, kdim)), _const_spec(wk.shape),
                  _const_spec(wvT.shape), _const_spec((1, kvw))],
        out_specs=[pl.BlockSpec((nb, kvw), lambda i: (i, 0)), pl.BlockSpec((kvw, nb), lambda i: (0, i))],
        out_shape=[jax.ShapeDtypeStruct((ncb, kvw), BF16), jax.ShapeDtypeStruct((kvw, ncb), BF16)],
        compiler_params=_cparams(), name="compress_prompt",
    )(xk, xv, pek, pev, wk, wvT, kg)


def _tile_lanes(x, n):
    return jnp.concatenate([x] * n, axis=1)


def _attend(s, shift, vT, slot, g, m_ref, acc_ref, m_blk=None):
    m_old = m_ref[slot, g]
    if m_blk is None:
        m_blk = jnp.max(s, axis=0, keepdims=True)
    if shift is not None:
        m_blk = m_blk + shift
    m_new = jnp.maximum(m_old, m_blk)
    p = jnp.exp2(s - (m_new if shift is None else m_new - shift)).astype(BF16)
    acc_ref[slot, g] = jnp.exp2(m_old - m_new) * acc_ref[slot, g] + _dot(vT, p)
    m_ref[slot, g] = m_new


def _bias_table_kernel(id_ref, val_ref, o_ref, *, reps):
    ids = _tile_lanes(id_ref[...], reps)
    out = jnp.zeros(ids.shape, F32)
    for k in range(val_ref.shape[0]):
        out = jnp.where(ids == k, val_ref[k:k + 1, :], out)
    o_ref[...] = out


def _bias_table(ids, vals):
    r = ids.shape[0]
    reps = vals.shape[1] // LANES
    full = lambda s: pl.BlockSpec(s, lambda i: (0,) * len(s))
    return pl.pallas_call(
        functools.partial(_bias_table_kernel, reps=reps), grid=(1,),
        in_specs=[full(ids.shape), full(vals.shape)], out_specs=full((r, vals.shape[1])),
        out_shape=jax.ShapeDtypeStruct((r, vals.shape[1]), F32),
        compiler_params=_cparams(), name="bias_table",
    )(jnp.asarray(ids, jnp.int32), vals)


def _nsa_prompt_kernel(qT_ref, gT_ref, kc_ref, vcT_ref, ks_ref, vsT_ref, kw_ref, vwT_ref,
                       b31_ref, tab_ref,
                       y_ref,
                       qaug_ref, sa_ref, sb_ref, sc_ref, p4_ref, selb_ref, ocmp_ref, m_ref, acc_ref,
                       *, groups, hpg, hd):
    qb = pl.program_id(0)
    t0 = qb * Q_BLOCK
    ncb = kc_ref.shape[0]
    nsb = ncb * CMP_BLOCK // SEL_BLOCK
    gl = hpg * Q_BLOCK
    n_top = min(TOP_N, nsb)
    kvw = groups * hd
    bpg = KEY_GROUP // SEL_BLOCK

    rows = []
    for g in range(groups):
        top = jnp.concatenate([qT_ref[(g * hpg + hh) * hd:(g * hpg + hh + 1) * hd, :]
                               for hh in range(hpg)], axis=1)
        z = jnp.zeros_like(top)
        rows.append(jnp.concatenate([top if gg == g else z for gg in range(groups)], axis=1))
    qbd = jnp.concatenate(rows, axis=0)
    qaug_ref[0:kvw, :] = qbd
    qaug_ref[kvw:2 * kvw, :] = jnp.zeros((kvw, groups * gl), BF16)
    b31 = b31_ref[...]

    def compress_and_select(rc):
        nsv = rc * CMP_BLOCK // SEL_BLOCK
        jrow = lax.broadcasted_iota(jnp.int32, (rc, Q_BLOCK), 0)
        qcol = lax.broadcasted_iota(jnp.int32, (rc, Q_BLOCK), 1)
        vis = _tile_lanes(jrow * CMP_BLOCK + (CMP_BLOCK - 1) <= t0 + qcol, groups * hpg)
        s_c = _dot(kc_ref[0:rc, :], qbd) + b31
        sc_ref[0:rc, :] = jnp.where(vis, s_c, NEG)
        odd = (qb % 2) == 1
        v_a = jnp.where(qb == 0, 0, jnp.where(odd, 1, 2))
        w_a = pl.multiple_of(jnp.where(qb == 0, 0, jnp.where(odd, 4 * qb - 4, 4 * qb - 8)), 8)
        v_b = jnp.where((qb > 0) & jnp.logical_not(odd), 0, 3)
        w_b = pl.multiple_of(jnp.where((qb > 0) & jnp.logical_not(odd), 4 * qb, 0), 8)
        sc_ref[pl.ds(w_a, 8), :] = sc_ref[pl.ds(w_a, 8), :] + tab_ref[pl.ds(pl.multiple_of(v_a * 8, 8), 8), :]
        sc_ref[pl.ds(w_b, 8), :] = sc_ref[pl.ds(w_b, 8), :] + tab_ref[pl.ds(pl.multiple_of(v_b * 8, 8), 8), :]
        s_c = sc_ref[0:rc, :]
        m_c = jnp.max(s_c, axis=0, keepdims=True)
        p_c = jnp.where(vis, jnp.exp2(s_c - m_c), 0.0)
        l_c = jnp.sum(p_c, axis=0, keepdims=True)
        p_c = p_c * jnp.where(l_c > 0.0, 1.0 / l_c, 0.0)
        for g in range(groups):
            pg = p_c[:, g * gl:(g + 1) * gl]
            ocmp_ref[g] = _dot(vcT_ref[g * hd:(g + 1) * hd, 0:rc], pg.astype(BF16))
            p4 = pg[:, 0:Q_BLOCK]
            for hh in range(1, hpg):
                p4 = p4 + pg[:, hh * Q_BLOCK:(hh + 1) * Q_BLOCK]
            p4_ref[g, 0:rc, :] = p4

        ratio = SEL_BLOCK // CMP_BLOCK
        jb = lax.broadcasted_iota(jnp.int32, (nsv, Q_BLOCK), 0)
        qc = lax.broadcasted_iota(jnp.int32, (nsv, Q_BLOCK), 1)
        cur = (t0 + qc) // SEL_BLOCK
        forced = (jb == 0) | (jb == cur) | (jb == cur - 1)
        for g in range(groups):
            imp = p4_ref[g, pl.ds(0, nsv, stride=ratio), :]
            for k in range(1, ratio):
                imp = imp + p4_ref[g, pl.ds(k, nsv, stride=ratio), :]
            imp = jnp.where(forced, FORCE_SCORE, imp)
            imp = jnp.where(jb <= cur, imp, -1.0)
            selb = jnp.full((nsv, Q_BLOCK), NEG, F32)
            for _ in range(n_top):
                mx = jnp.max(imp, axis=0, keepdims=True)
                idx = jnp.min(jnp.where(imp == mx, jb, nsv), axis=0, keepdims=True)
                hit = jb == idx
                selb = jnp.where(hit, 0.0, selb)
                imp = jnp.where(hit, -3.0e38, imp)
            selb_ref[g, 0:nsv, :] = selb
            if nsv < nsb:
                selb_ref[g, nsv:nsb, :] = jnp.full((nsb - nsv, Q_BLOCK), NEG, F32)

    m_ref[...] = jnp.full(m_ref.shape, NEG, F32)
    acc_ref[...] = jnp.zeros(acc_ref.shape, F32)
    n_wk = WINDOW + Q_BLOCK

    def window_block():
        k0 = pl.multiple_of(t0 - WINDOW, Q_BLOCK)
        s = _dot(kw_ref[pl.ds(k0, n_wk), :], qaug_ref[0:kvw, :]) + tab_ref[_TAB_WIN:_TAB_WIN + n_wk, :]
        for g in range(groups):
            _attend(s[:, g * gl:(g + 1) * gl], None,
                    vwT_ref[g * ACC_ROWS:(g + 1) * ACC_ROWS, pl.ds(k0, n_wk)], 1, g, m_ref, acc_ref)

    def put_mask_rows(g, rows8):
        tile = jnp.concatenate([rows8, jnp.zeros_like(rows8)], axis=0).astype(BF16)
        for hh in range(hpg):
            qaug_ref[kvw:kvw + 2 * bpg, g * gl + hh * Q_BLOCK:g * gl + (hh + 1) * Q_BLOCK] = tile

    def far_scores(c, dst_ref):
        for g in range(groups):
            put_mask_rows(g, selb_ref[g, pl.ds(pl.multiple_of(c * bpg, bpg), bpg), :])
        k0 = pl.multiple_of(c * KEY_GROUP, KEY_GROUP)
        s = _dot(ks_ref[pl.ds(k0, KEY_GROUP), :], qaug_ref[...])
        dst_ref[0:KEY_GROUP, :] = s
        dst_ref[KEY_GROUP:KEY_GROUP + 8, :] = jnp.broadcast_to(jnp.max(s, axis=0, keepdims=True),
                                                               (8, groups * gl))

    tiers = list(range(LANES, ncb + 1, LANES)) if ncb % LANES == 0 else [ncb]
    n_vis = (t0 + Q_BLOCK) // CMP_BLOCK
    for i, rc in enumerate(tiers):
        lo = tiers[i - 1] if i else 0
        in_tier = (n_vis > lo) & (n_vis <= rc)

        @pl.when(in_tier & (t0 >= WINDOW))
        def _(rc=rc):
            compress_and_select(rc)
            window_block()
            far_scores(0, sa_ref)

        if lo * CMP_BLOCK < WINDOW + Q_BLOCK:
            @pl.when(in_tier & (t0 < WINDOW))
            def _(rc=rc):
                compress_and_select(rc)
                far_scores(0, sa_ref)


    def far_attend(c, src_ref, live_rows=None):
        k0 = pl.multiple_of(c * KEY_GROUP, KEY_GROUP)
        drop = None
        if live_rows is not None:
            drop = lax.broadcasted_iota(jnp.int32, (KEY_GROUP, Q_BLOCK), 0) >= live_rows
        for g in range(groups):
            sg = src_ref[0:KEY_GROUP, g * gl:(g + 1) * gl]
            m_blk = src_ref[KEY_GROUP:KEY_GROUP + 1, g * gl:(g + 1) * gl]
            if drop is not None:
                sg = jnp.where(_tile_lanes(drop, hpg), NEG, sg)
                m_blk = None
            _attend(sg, b31[:, g * gl:(g + 1) * gl],
                    vsT_ref[g * ACC_ROWS:(g + 1) * ACC_ROWS, pl.ds(k0, KEY_GROUP)], 0, g, m_ref, acc_ref,
                    m_blk=m_blk)

    far_len = jnp.maximum(t0 - Q_BLOCK, 0)
    n_far = far_len // KEY_GROUP
    live_tail = far_len - n_far * KEY_GROUP

    def far_pair(j, carry):
        far_scores(2 * j + 1, sb_ref)
        far_attend(2 * j, sa_ref)
        far_scores(2 * j + 2, sa_ref)
        far_attend(2 * j + 1, sb_ref)
        return carry

    lax.fori_loop(0, n_far // 2, far_pair, 0)

    near_tab = tab_ref[_TAB_NEAR:_TAB_NEAR + 2 * Q_BLOCK, :]

    def near_piece():
        b0 = (t0 - Q_BLOCK) // SEL_BLOCK
        base = (b0 // bpg) * bpg
        nxt = jnp.minimum(base + bpg, nsb - bpg)
        jrow8 = lax.broadcasted_iota(jnp.int32, (bpg, Q_BLOCK), 0)
        for g in range(groups):
            lo = selb_ref[g, pl.ds(pl.multiple_of(base, bpg), bpg), :]
            hi = selb_ref[g, pl.ds(pl.multiple_of(nxt, bpg), bpg), :]
            put_mask_rows(g, jnp.where(jrow8 >= b0 - base, lo, hi))
        k0 = pl.multiple_of(t0 - Q_BLOCK, Q_BLOCK)
        s = _dot(ks_ref[pl.ds(k0, 2 * Q_BLOCK), :], qaug_ref[...]) + near_tab
        for g in range(groups):
            _attend(s[:, g * gl:(g + 1) * gl], None,
                    vsT_ref[g * ACC_ROWS:(g + 1) * ACC_ROWS, pl.ds(k0, 2 * Q_BLOCK)], 0, g, m_ref, acc_ref)

    @pl.when(qb == 0)
    def _():
        for g in range(groups):
            put_mask_rows(g, selb_ref[g, 0:bpg, :])
        s = _dot(ks_ref[0:Q_BLOCK, :], qaug_ref[...]) + near_tab[Q_BLOCK:2 * Q_BLOCK, :]
        for g in range(groups):
            _attend(s[:, g * gl:(g + 1) * gl], None,
                    vsT_ref[g * ACC_ROWS:(g + 1) * ACC_ROWS, 0:Q_BLOCK], 0, g, m_ref, acc_ref)

    odd_far = n_far % 2 == 1
    has_tail = live_tail > 0

    @pl.when(odd_far & has_tail)
    def _():
        far_scores(n_far, sb_ref)
        far_attend(n_far - 1, sa_ref)
        far_attend(n_far, sb_ref, live_tail)
        near_piece()

    @pl.when(odd_far & jnp.logical_not(has_tail))
    def _():
        far_attend(n_far - 1, sa_ref)
        near_piece()

    @pl.when(jnp.logical_not(odd_far) & has_tail)
    def _():
        far_attend(n_far, sa_ref, live_tail)
        near_piece()

    @pl.when(jnp.logical_not(odd_far) & jnp.logical_not(has_tail) & (qb >= 1))
    def _():
        near_piece()

    for i in range(WINDOW // Q_BLOCK):
        k0 = t0 - i * Q_BLOCK

        @pl.when((t0 < WINDOW) & (k0 >= 0))
        def _(i=i, k0=k0):
            k0a = pl.multiple_of(k0, Q_BLOCK)
            r0 = _TAB_WIN + WINDOW - i * Q_BLOCK
            s = _dot(kw_ref[pl.ds(k0a, Q_BLOCK), :], qaug_ref[0:kvw, :]) + tab_ref[r0:r0 + Q_BLOCK, :]
            for g in range(groups):
                _attend(s[:, g * gl:(g + 1) * gl], None,
                        vwT_ref[g * ACC_ROWS:(g + 1) * ACC_ROWS, pl.ds(k0a, Q_BLOCK)], 1, g, m_ref, acc_ref)

    heads = groups * hpg
    pieces = []
    for g in range(groups):
        gate = [jnp.concatenate([gT_ref[j * heads + g * hpg + hh: j * heads + g * hpg + hh + 1, :]
                                 for hh in range(hpg)], axis=1) for j in range(3)]
        o = gate[0] * ocmp_ref[g]
        for slot in range(2):
            acc = acc_ref[slot, g]
            o = o + gate[1 + slot] * (acc[0:hd, :] * (1.0 / acc[hd:hd + 1, :]))
        for hh in range(hpg):
            pieces.append(o[:, hh * Q_BLOCK:(hh + 1) * Q_BLOCK])
    y_ref[...] = jnp.concatenate(pieces, axis=0).T.astype(BF16)


_TAB_NEAR = 32
_TAB_WIN = _TAB_NEAR + 2 * Q_BLOCK


def _nsa_prompt_tables(rel_bias, n_lanes):
    nb, heads = rel_bias.shape
    tbl = rel_bias.astype(F32) * LOG2E
    lanes = lambda x: jnp.repeat(x, n_lanes // heads, axis=1)
    vals = jnp.concatenate([lanes(tbl), jnp.full((1, n_lanes), NEG, F32), jnp.zeros((1, n_lanes), F32),
                            lanes(tbl - tbl[nb - 1:nb])], axis=0)
    c = np.arange(Q_BLOCK)[None, :]
    i8 = np.arange(8)[:, None]
    ids = []
    for base in (-(CMP_BLOCK - 1), 3 * CMP_BLOCK + 1, 7 * CMP_BLOCK + 1):
        d = base + c - CMP_BLOCK * i8
        ids.append(np.where(d >= 0, nb + 2 + _bucket_np(d, nb), nb + 1))
    ids.append(np.full((8, Q_BLOCK), nb + 1))
    d = Q_BLOCK + c - np.arange(2 * Q_BLOCK)[:, None]
    ids.append(np.where(d >= 0, _bucket_np(d, nb), nb))
    d = WINDOW + c - np.arange(WINDOW + Q_BLOCK)[:, None]
    ids.append(np.where((d >= 0) & (d < WINDOW), _bucket_np(d, nb), nb))
    return vals[nb - 1:nb], _bias_table(np.concatenate(ids, axis=0), vals)


def _nsa_prompt(qT, gT, kc, vcT, ks, vsT, kw, vwT, rel_bias, groups, hpg, hd):
    qw, t = qT.shape
    ncb = kc.shape[0]
    nsb = t // SEL_BLOCK
    gl = hpg * Q_BLOCK
    kvw = groups * hd
    assert t % KEY_GROUP == 0 and Q_BLOCK == 4 * CMP_BLOCK and MAX_DIST <= Q_BLOCK
    tables = _nsa_prompt_tables(rel_bias, groups * gl)
    resident = (kc, vcT, ks, vsT, kw, vwT) + tuple(tables)
    return pl.pallas_call(
        functools.partial(_nsa_prompt_kernel, groups=groups, hpg=hpg, hd=hd),
        grid=(t // Q_BLOCK,),
        in_specs=[pl.BlockSpec((qw, Q_BLOCK), lambda i: (0, i)),
                  pl.BlockSpec((gT.shape[0], Q_BLOCK), lambda i: (0, i))]
                 + [_const_spec(a.shape) for a in resident],
        out_specs=pl.BlockSpec((Q_BLOCK, qw), lambda i: (i, 0)),
        out_shape=jax.ShapeDtypeStruct((t, qw), BF16),
        scratch_shapes=[pltpu.VMEM((2 * kvw, groups * gl), BF16),
                        pltpu.VMEM((KEY_GROUP + 8, groups * gl), F32),
                        pltpu.VMEM((KEY_GROUP + 8, groups * gl), F32),
                        pltpu.VMEM((ncb, groups * gl), F32),
                        pltpu.VMEM((groups, ncb, Q_BLOCK), F32),
                        pltpu.VMEM((groups, nsb, Q_BLOCK), F32),
                        pltpu.VMEM((groups, hd, gl), F32),
                        pltpu.VMEM((2, groups, 1, gl), F32),
                        pltpu.VMEM((2, groups, ACC_ROWS, gl), F32)],
        compiler_params=_cparams(), name="nsa_prompt",
    )(qT, gT, *resident)


def _split3(x):
    hi = x.astype(BF16)
    r1 = x - hi.astype(F32)
    mid = r1.astype(BF16)
    lo = (r1 - mid.astype(F32)).astype(BF16)
    return hi, mid, lo


def _nsa_sample_kernel(pt_ref, cache_ref, qbd_ref, qn_ref, gates_ref, kvn_ref, winn_ref, cwin_ref,
                       wkv_ref, pe_ref, perm_ref, kg_ref, gsum_ref,
                       cb_ref, sb_ref, wb_ref, b0_ref, e_ref,
                       o_ref,
                       ringa_ref, ringb_ref, bufkv_ref, sema, semb, p_ref,
                       *, hd, n_pages, page, chunk):
    b = pl.program_id(0)
    nb = pl.num_programs(0)
    kvw = kg_ref.shape[1]
    past = n_pages * page
    ncb = past // CMP_BLOCK
    nsb = past // SEL_BLOCK
    n_chunks = n_pages // chunk
    bpp = 2 * page // CMP_BLOCK

    def chunk_copy(bb, q, p):
        if q < n_chunks:
            return pltpu.make_async_copy(cache_ref.at[pt_ref[bb, q * chunk + p], 0:2 * kvw, :],
                                         ringa_ref.at[q % 2, p], sema.at[q % 2])
        ch = q - n_chunks
        return pltpu.make_async_copy(cache_ref.at[pt_ref[bb, ch * chunk + p], 2 * kvw:4 * kvw, :],
                                     ringb_ref.at[ch % 2, :, pl.ds(p * page, page)], semb.at[ch % 2])

    def for_pages(n, fn, unroll=1):
        def body(p, c):
            fn(p)
            return c
        lax.fori_loop(0, n, body, 0, unroll=unroll)

    def start_chunk(bb, q):
        for_pages(chunk, lambda p: chunk_copy(bb, q, p).start(), unroll=8)

    def wait_chunk(q):
        for_pages(chunk, lambda p: chunk_copy(b, q, p).wait(), unroll=chunk)

    def request_ahead(q):
        if q + 2 < 2 * n_chunks:
            start_chunk(b, q + 2)
        else:
            @pl.when(b + 1 < nb)
            def _():
                start_chunk(b + 1, q + 2 - 2 * n_chunks)

    @pl.when(b == 0)
    def _():
        start_chunk(0, 0)
        start_chunk(0, 1)

    qbd = qbd_ref[0]

    for ch in range(n_chunks):
        wait_chunk(ch)

        def regroup(q, ch=ch):
            xt = jnp.concatenate([ringa_ref[ch % 2, 2 * q], ringa_ref[ch % 2, 2 * q + 1]], axis=1)
            y = _dot_nt(perm_ref[...], (xt + pe_ref[...]).astype(BF16))
            blocks = pl.ds(pl.multiple_of((ch * (chunk // 2) + q) * bpp, bpp), bpp)
            for r in range(CMP_BLOCK):
                bufkv_ref[r, blocks, :] = y[r * bpp:(r + 1) * bpp, :]
        for_pages(chunk // 2, regroup, unroll=chunk // 2)
        request_ahead(ch)

    half = ncb // 2
    zs = [jnp.zeros((half, 2 * kvw), F32), jnp.zeros((ncb - half, 2 * kvw), F32)]
    for r in range(CMP_BLOCK):
        zs[0] = zs[0] + _dot(bufkv_ref[r, 0:half, :].astype(BF16), wkv_ref[r])
        zs[1] = zs[1] + _dot(bufkv_ref[r, half:ncb, :].astype(BF16), wkv_ref[r])
    z = jnp.concatenate(zs, axis=0)
    vc = z[:, kvw:2 * kvw]
    kc = _group_rmsnorm(z[:, 0:kvw], kg_ref[...], hd).astype(BF16)
    s_c = _dot(kc, qbd) + cb_ref[...]
    m_c = jnp.max(s_c, axis=0, keepdims=True)
    p_c = jnp.exp(s_c - m_c)
    p_c = p_c * (1.0 / jnp.sum(p_c, axis=0, keepdims=True))
    o_cmp = _dot(p_c.T.astype(BF16), vc.astype(BF16))
    p_ref[...] = p_c
    ratio = SEL_BLOCK // CMP_BLOCK
    imp = p_ref[pl.ds(0, nsb, stride=ratio), :]
    for k in range(1, ratio):
        imp = imp + p_ref[pl.ds(k, nsb, stride=ratio), :]
    hi, mid, lo = _split3(imp)
    gs = gsum_ref[...]
    imp = _dot(hi, gs) + _dot(mid, gs) + _dot(lo, gs)
    jb = lax.broadcasted_iota(jnp.int32, (nsb, LANES), 0)
    imp = jnp.where((jb == 0) | (jb == nsb - 1), FORCE_SCORE, imp)
    selb = jnp.full((nsb, LANES), NEG, F32)
    for _ in range(min(TOP_N, nsb + 1) - 1):
        mx = jnp.max(imp, axis=0, keepdims=True)
        idx = jnp.min(jnp.where(imp == mx, jb, nsb), axis=0, keepdims=True)
        hit = jb == idx
        selb = jnp.where(hit, 0.0, selb)
        imp = jnp.where(hit, -3.0e38, imp)
    rows = qn_ref.shape[1]
    sel_nat = selb.T[0:rows, :]

    qn = qn_ref[0]
    qnb = qn.astype(BF16)
    kvn = kvn_ref[0]
    b0 = b0_ref[:, 0:1]
    m_s = jnp.sum(qn * kvn[:, 2 * kvw:3 * kvw], axis=1, keepdims=True) + b0
    l_s = jnp.ones_like(m_s)
    acc = jnp.broadcast_to(kvn[:, 3 * kvw:4 * kvw], (rows, kvw))
    nks = e_ref.shape[1]
    bps = nks // SEL_BLOCK
    per_chunk = chunk * page // nks
    for ch in range(n_chunks):
        wait_chunk(n_chunks + ch)
        for sub in range(per_chunk):
            c = ch * per_chunk + sub
            keys = slice(sub * nks, (sub + 1) * nks)
            rolled = sel_nat if c == 0 else pltpu.roll(sel_nat, nsb - c * bps, 1)
            mask = _dot(rolled.astype(BF16), e_ref[...])
            s = (_dot(qnb, ringb_ref[ch % 2, 0:kvw, keys].astype(BF16)) + mask
                 + sb_ref[:, c * nks:(c + 1) * nks])
            m_new = jnp.maximum(m_s, jnp.max(s, axis=1, keepdims=True))
            alpha = jnp.exp(m_s - m_new)
            p = jnp.exp(s - m_new)
            l_s = alpha * l_s + jnp.sum(p, axis=1, keepdims=True)
            acc = alpha * acc + _dot_nt(p.astype(BF16), ringb_ref[ch % 2, kvw:2 * kvw, keys].astype(BF16))
            m_s = m_new
        request_ahead(n_chunks + ch)
    o_sel = acc * (1.0 / l_s)

    winn = winn_ref[0]
    s_w = _dot(qnb, cwin_ref[0, 0:kvw, :].astype(BF16)) + wb_ref[...]
    s_wn = jnp.sum(qn * winn[:, 0:kvw], axis=1, keepdims=True) + b0
    m_w = jnp.maximum(jnp.max(s_w, axis=1, keepdims=True), s_wn)
    p_w = jnp.exp(s_w - m_w)
    p_wn = jnp.exp(s_wn - m_w)
    l_w = jnp.sum(p_w, axis=1, keepdims=True) + p_wn
    o_win = (_dot_nt(p_w.astype(BF16), cwin_ref[0, kvw:2 * kvw, :].astype(BF16))
             + p_wn * winn[:, kvw:2 * kvw]) * (1.0 / l_w)

    g = gates_ref[0]
    out = g[0] * o_cmp[0:rows, :] + g[1] * o_sel + g[2] * o_win
    o_ref[0] = out[0:8, :]


def _nsa_sample(q, gates, kv_new, win_new, cache_kv, cache_win, page_table, lp, rel_bias,
                groups, hpg, hd):
    r = q.shape[0]
    heads = groups * hpg
    kvw = groups * hd
    n_phys, page = cache_kv.shape[0], cache_kv.shape[1]
    n_pages = page_table.shape[1]
    past = n_pages * page
    wbuf = cache_win.shape[1]
    ncb, nsb = past // CMP_BLOCK, past // SEL_BLOCK
    rows = 16
    chunk = min(32, n_pages // 2)
    nks = min(NKS_SAMPLE, chunk * page)
    assert n_pages % (2 * chunk) == 0 and chunk % 8 == 0 and (chunk * page) % nks == 0 and heads <= rows
    q3 = q.reshape(r, heads, hd)
    gmask = (jnp.arange(groups)[:, None] == (jnp.arange(heads) // hpg)[None, :]).astype(F32)
    qbd = jnp.einsum('rhd,gh->rgdh', q3, gmask).reshape(r, kvw, heads)
    qbd = jnp.pad(qbd, ((0, 0), (0, 0), (0, LANES - heads))).astype(BF16)
    qn = jnp.einsum('rhd,gh->rhgd', q3, gmask).reshape(r, heads, kvw)
    qn = jnp.pad(qn, ((0, 0), (0, rows - heads), (0, 0)))
    gts = gates[:, :3 * heads].reshape(r, heads, 3).transpose(0, 2, 1)
    gts = jnp.broadcast_to(gts[:, :, :, None], (r, 3, heads, kvw))
    gts = jnp.pad(gts, ((0, 0), (0, 0), (0, rows - heads), (0, 0)))
    wk = _compress_weight(lp['cmp_w_k'], groups, hd).reshape(CMP_BLOCK, kvw, kvw)
    wv = _compress_weight(lp['cmp_w_v'], groups, hd).reshape(CMP_BLOCK, kvw, kvw)
    zero = jnp.zeros_like(wk)
    wkv = jnp.concatenate([jnp.concatenate([wk, zero], axis=2),
                           jnp.concatenate([zero, wv], axis=2)], axis=1).astype(BF16)
    pe_t = jnp.concatenate([jnp.tile(lp['pe_k'].T, (groups, 1)), jnp.tile(lp['pe_v'].T, (groups, 1))], axis=0)
    pe_t = jnp.tile(pe_t, (1, 2 * page // CMP_BLOCK))
    bpp = 2 * page // CMP_BLOCK
    dst = np.arange(2 * page)
    src = (dst % bpp) * CMP_BLOCK + dst // bpp
    perm = jnp.asarray(src[:, None] == np.arange(2 * page)[None, :], BF16)
    kg = jnp.tile(lp['k_gain'][0], groups)[None, :]
    lane_g = np.where(np.arange(LANES) < heads, np.arange(LANES) // hpg, -1)
    gsum = jnp.asarray((lane_g[:, None] == lane_g[None, :]) & (lane_g[:, None] >= 0), BF16)
    nbk = rel_bias.shape[0]
    tbl = rel_bias.astype(F32)
    d_c = (past - (np.arange(ncb) * CMP_BLOCK + CMP_BLOCK - 1))[:, None]
    vals_t = jnp.concatenate([tbl, jnp.broadcast_to(tbl[:, :1], (nbk, LANES - heads))], axis=1)
    cb = _bias_table(np.broadcast_to(_bucket_np(d_c, nbk), (ncb, LANES)), vals_t)
    vals_n = jnp.concatenate([jnp.repeat(tbl, LANES, axis=1), jnp.full((1, heads * LANES), NEG, F32)], axis=0)

    def head_rows(ids):
        out = _bias_table(ids, vals_n).reshape(ids.shape[0], heads, LANES)
        out = jnp.transpose(out, (1, 0, 2)).reshape(heads, ids.shape[0] * LANES)
        return jnp.pad(out, ((0, rows - heads), (0, 0)))

    sb = head_rows(_bucket_np(past - np.arange(past), nbk).reshape(past // LANES, LANES))
    d_w = wbuf - np.arange(wbuf)
    wb = head_rows(np.where(d_w < WINDOW, _bucket_np(d_w, nbk), nbk).reshape(wbuf // LANES, LANES))
    b0 = jnp.pad(jnp.broadcast_to(tbl[0][:, None], (heads, LANES)), ((0, rows - heads), (0, 0)))
    blk = np.arange(nsb)[:, None]
    spread = jnp.asarray((blk == np.arange(nks)[None, :] // SEL_BLOCK), BF16)

    cache_t = jnp.transpose(cache_kv, (0, 2, 3, 4, 1)).reshape(n_phys, 4 * kvw, page)
    cwin_t = jnp.transpose(cache_win, (0, 2, 3, 4, 1)).reshape(r, 2 * kvw, wbuf)
    kvn = kv_new.reshape(r, 1, 4 * kvw)
    winn = win_new.reshape(r, 1, 2 * kvw)
    per_b = lambda s: pl.BlockSpec((1,) + s, lambda b, pt: (b,) + (0,) * len(s))
    const = lambda a: pl.BlockSpec(a.shape, lambda b, pt: (0,) * a.ndim, pipeline_mode=pl.Buffered(1))
    consts = (wkv, pe_t, perm, kg, gsum, cb, sb, wb, b0, spread)
    grid_spec = pltpu.PrefetchScalarGridSpec(
        num_scalar_prefetch=1, grid=(r,),
        in_specs=[pl.BlockSpec(memory_space=pl.ANY), per_b((kvw, LANES)), per_b((rows, kvw)),
                  per_b((3, rows, kvw)), per_b((1, 4 * kvw)), per_b((1, 2 * kvw)),
                  per_b((2 * kvw, wbuf))]
                 + [const(a) for a in consts],
        out_specs=per_b((8, kvw)),
        scratch_shapes=[pltpu.VMEM((2, chunk, 2 * kvw, page), F32),
                        pltpu.VMEM((2, 2 * kvw, chunk * page), F32),
                        pltpu.VMEM((CMP_BLOCK, ncb, 2 * kvw), F32),
                        pltpu.SemaphoreType.DMA((2,)), pltpu.SemaphoreType.DMA((2,)),
                        pltpu.VMEM((ncb, LANES), F32)])
    o8 = pl.pallas_call(
        functools.partial(_nsa_sample_kernel, hd=hd, n_pages=n_pages, page=page, chunk=chunk),
        grid_spec=grid_spec,
        out_shape=jax.ShapeDtypeStruct((r, 8, kvw), F32),
        compiler_params=_cparams(), name="nsa_sample",
    )(page_table, cache_t, qbd, qn, gts, kvn, winn, cwin_t, *consts)
    o4 = o8[:, :heads, :].reshape(r, heads, groups, hd)
    y = jnp.take_along_axis(o4, (jnp.arange(heads) // hpg)[None, :, None, None], axis=2)
    return y.reshape(r, heads * hd).astype(BF16)


def _mlp_ple(h, p_ref, gm_ref, wu_ref, wd_ref, gp_ref, wg_ref, wp_ref):
    up = _dot(_rms_rows(h, gm_ref[...]).astype(BF16), wu_ref[...])
    act = jnp.square(jnp.maximum(up, 0.0)).astype(BF16)
    h = h + _dot(act, wd_ref[...])
    gate = jax.nn.sigmoid(_dot(_rms_rows(h, gp_ref[...]).astype(BF16), wg_ref[...]))
    return h + gate * _dot(p_ref[...].astype(BF16), wp_ref[...])


def _even_tail_kernel(h_ref, ya_ref, yb_ref, p_ref, woa_ref, wob_ref,
                      gm_ref, wu_ref, wd_ref, gp_ref, wg_ref, wp_ref, o_ref):
    h = h_ref[...] + _dot(ya_ref[...], woa_ref[...]) + _dot(yb_ref[...], wob_ref[...])
    o_ref[...] = _mlp_ple(h, p_ref, gm_ref, wu_ref, wd_ref, gp_ref, wg_ref, wp_ref)


def _tail_weights(g_mlp, w_up, w_down, g_ple, w_gate, w_proj):
    return (g_mlp[None, :], w_up.astype(BF16), w_down.astype(BF16), g_ple[None, :],
            w_gate.astype(BF16), w_proj.astype(BF16))


def _ple_spec(p_all, layer, tm):
    if p_all.shape[1] == 1:
        return pl.BlockSpec((None, None, tm, p_all.shape[3]), lambda i: (layer, 0, i, 0))
    return pl.BlockSpec((None, tm, None, p_all.shape[3]), lambda i: (layer, i, 0, 0))


def _even_tail(h, ya, yb, p_all, layer, w_out, tail_w, tm=256):
    t, d = h.shape
    tm = min(tm, t)
    lw = ya.shape[1]
    woa, wob = w_out[:lw].astype(BF16), w_out[lw:].astype(BF16)
    row = lambda w: pl.BlockSpec((tm, w), lambda i: (i, 0))
    weights = (woa, wob) + tuple(tail_w)
    return pl.pallas_call(
        _even_tail_kernel, grid=(t // tm,),
        in_specs=[row(d), row(lw), row(yb.shape[1]), _ple_spec(p_all, layer, tm)]
                 + [_const_spec(w.shape) for w in weights],
        out_specs=row(d), out_shape=jax.ShapeDtypeStruct((t, d), F32),
        compiler_params=_cparams(), name="even_tail",
    )(h, ya, yb, p_all, *weights)


def _odd_kernel(h_ref, p_ref, gx_ref, wi_ref, lg_ref, lb_ref, sw_ref, sb_ref, wo_ref,
                gm_ref, wu_ref, wd_ref, gp_ref, wg_ref, wp_ref, o_ref, v_ref, *, sw_groups, single):
    h = h_ref[...]
    tm = h.shape[0]
    z = _gelu(_dot(_rms_rows(h, gx_ref[...]).astype(BF16), wi_ref[...]))
    width = z.shape[1] // 2
    u, v = z[:, :width], z[:, width:]
    mu = jnp.mean(v, axis=-1, keepdims=True)
    vc = v - mu
    vn = vc * lax.rsqrt(jnp.mean(vc * vc, axis=-1, keepdims=True) + EPS) * lg_ref[...] + lb_ref[...]
    v_ref[...] = vn
    gw = width // sw_groups
    if single:
        s = vn * sw_ref[...] + sb_ref[...]
    else:
        vb = vn.astype(BF16)
        parts = []
        for c in range(tm // CHUNK):
            cols = [_dot(sw_ref[g], vb[c * CHUNK:(c + 1) * CHUNK, g * gw:(g + 1) * gw])
                    for g in range(sw_groups)]
            parts.append(jnp.concatenate(cols, axis=1) + sb_ref[...])
        s = jnp.concatenate(parts, axis=0)
    h = h + _dot((u * s).astype(BF16), wo_ref[...])
    o_ref[...] = _mlp_ple(h, p_ref, gm_ref, wu_ref, wd_ref, gp_ref, wg_ref, wp_ref)


def _odd_layer(h, p_all, layer, g_mix, w_in, ln_g, ln_b, sgu_w, sgu_b, w_out, tail_w, single, tm=256):
    t, d = h.shape
    tm = min(tm, t)
    ng, ch, _ = sgu_w.shape
    width = w_in.shape[1] // 2
    gw = width // ng
    if single:
        sw = jnp.repeat(sgu_w[:, 0, 0], gw)[None, :]
        sb = jnp.repeat(sgu_b[:, 0], gw)[None, :]
    else:
        sw = (sgu_w * jnp.tril(jnp.ones((ch, ch), sgu_w.dtype))).astype(BF16)
        sb = jnp.repeat(sgu_b.T, gw, axis=1)
    weights = (g_mix[None, :], w_in.astype(BF16), ln_g[None, :], ln_b[None, :], sw, sb,
               w_out.astype(BF16)) + tuple(tail_w)
    row = lambda w: pl.BlockSpec((tm, w), lambda i: (i, 0))
    return pl.pallas_call(
        functools.partial(_odd_kernel, sw_groups=ng, single=single), grid=(t // tm,),
        in_specs=[row(d), _ple_spec(p_all, layer, tm)] + [_const_spec(w.shape) for w in weights],
        out_specs=[row(d), row(width)],
        out_shape=[jax.ShapeDtypeStruct((t, d), F32), jax.ShapeDtypeStruct((t, width), F32)],
        compiler_params=_cparams(), name="odd_single" if single else "odd_prompt",
    )(h, p_all, *weights)


def kernel(x_prompt, x_sample, cache_kv, cache_win, state_conv, state_h, page_table, p_prompt, p_sample,
           rel_bias, g_mix, g_mlp, w_up, w_down, g_ple, w_ple_gate, w_ple_proj,
           w_in_even, w_out_even, conv_w, conv_b, rg_w_a, rg_b_a, rg_w_x, rg_b_x, rg_lambda,
           q_gain, k_gain, cmp_w_k, cmp_w_v, cmp_pe_k, cmp_pe_v,
           w_in_odd, ln_v_g, ln_v_b, sgu_w, sgu_b, w_out_odd):
    bsz, t, d = x_prompt.shape
    r = x_sample.shape[0]
    assert bsz == 1 and x_sample.shape[1] == 1
    depth = g_mix.shape[0]
    lw = conv_w.shape[-1]
    hd = q_gain.shape[-1]
    groups = cache_kv.shape[4]
    heads = rel_bias.shape[1]
    hpg = heads // groups
    qw, kvw = heads * hd, groups * hd
    dims = (lw, qw, kvw, hd, heads)
    wbuf = cache_win.shape[2]

    hp = x_prompt.reshape(t, d)
    hs = x_sample.reshape(r, d)
    kvp, kvs, wnp, wns, cvp, cvs, hhp, hhs, vvs = [], [], [], [], [], [], [], [], []
    for i in range(depth):
        tail_w = _tail_weights(g_mlp[i], w_up[i], w_down[i], g_ple[i], w_ple_gate[i], w_ple_proj[i])
        if i % 2 == 0:
            e = i // 2
            lp = {'cmp_w_k': cmp_w_k[e], 'cmp_w_v': cmp_w_v[e], 'pe_k': cmp_pe_k[e], 'pe_v': cmp_pe_v[e],
                  'k_gain': k_gain[e]}
            rg_w = _rglru_weights(conv_w[e], conv_b[e], rg_w_a[e], rg_b_a[e], rg_w_x[e], rg_b_x[e],
                                  rg_lambda[e])
            (xa, ga, kv_rows, win_rows, kcmp, vcmp, ks, kw, qT, vsT, vwT, gT) = _even_in_prompt(
                hp, g_mix[i], w_in_even[e], q_gain[e], k_gain[e], dims)
            ya, h_last = _rglru_prompt(xa, ga, rg_w)
            kc, vcT = _compress_prompt(kcmp, vcmp, cmp_w_k[e], cmp_w_v[e], cmp_pe_k[e], cmp_pe_v[e],
                                       k_gain[e, 0], groups, hd)
            yb = _nsa_prompt(qT, gT, kc, vcT, ks, vsT, kw, vwT, rel_bias, groups, hpg, hd)
            hp = _even_tail(hp, ya, yb, p_prompt, i, w_out_even[e], tail_w)
            kvp.append(kv_rows.reshape(1, t, 4, groups, hd))
            wnp.append(win_rows[t - wbuf:].reshape(1, wbuf, 2, groups, hd))
            cvp.append(xa[t - (conv_w.shape[1] - 1):][None])
            hhp.append(h_last)
            xa_s, ga_s, q_s, kv_s, win_s, gate_s = _even_in_sample(
                hs, g_mix[i], w_in_even[e], q_gain[e], k_gain[e], dims)
            ya_s, h_s = _rglru_sample(xa_s, ga_s, state_conv[e], state_h[e], rg_w)
            yb_s = _nsa_sample(q_s, gate_s, kv_s, win_s, cache_kv[e], cache_win[e], page_table, lp,
                               rel_bias, groups, hpg, hd)
            hs = _even_tail(hs, ya_s, yb_s, p_sample, i, w_out_even[e], tail_w)
            kvs.append(kv_s.reshape(r, 1, 4, groups, hd))
            wns.append(jnp.concatenate([cache_win[e][:, 1:], win_s.reshape(r, 1, 2, groups, hd)], axis=1))
            cvs.append(jnp.concatenate([state_conv[e][:, 1:], xa_s[:, None, :]], axis=1))
            hhs.append(h_s)
        else:
            o = i // 2
            hp, _ = _odd_layer(hp, p_prompt, i, g_mix[i], w_in_odd[o], ln_v_g[o], ln_v_b[o],
                               sgu_w[o], sgu_b[o], w_out_odd[o], tail_w, single=False)
            hs, v_new = _odd_layer(hs, p_sample, i, g_mix[i], w_in_odd[o], ln_v_g[o], ln_v_b[o],
                                   sgu_w[o], sgu_b[o], w_out_odd[o], tail_w, single=True)
            vvs.append(v_new[:, None, :])
    v_sample = jnp.stack(vvs) if vvs else jnp.zeros((0, r, 1, w_in_odd.shape[-1] // 2), F32)
    return (hp[None], hs[:, None, :], jnp.stack(kvp), jnp.stack(kvs), jnp.stack(wnp), jnp.stack(wns),
            jnp.stack(cvp), jnp.stack(cvs), jnp.stack(hhp), jnp.stack(hhs), v_sample)
```

```python
import functools
import math

import numpy as np
import jax
import jax.numpy as jnp
from jax import lax
from jax.experimental import pallas as pl
from jax.experimental.pallas import tpu as pltpu

F32 = jnp.float32
BF16 = jnp.bfloat16

EPS = 1e-6
NEG = -1e30
RG_C = 8.0
CMP_BLOCK = 32
SEL_BLOCK = 64
TOP_N = 16
WINDOW = 512
Q_BLOCK = 128
FORCE_SCORE = 1.0e4
MAX_DIST = 128
CHUNK = 128

LANES = 128
VMEM_LIMIT = 56 * 1024 * 1024

LOG2E = math.log2(math.e)
KEY_GROUP = 512
ACC_ROWS = 80
NKS_SAMPLE = 4096


def _cparams(n_axes=1):
    return pltpu.CompilerParams(dimension_semantics=("arbitrary",) * n_axes,
                                vmem_limit_bytes=VMEM_LIMIT)


def _const_spec(shape):
    nd = len(shape)
    return pl.BlockSpec(shape, lambda *_: (0,) * nd, pipeline_mode=pl.Buffered(1))


def _dot(a, b):
    return jnp.dot(a, b, preferred_element_type=F32)


def _dot_nt(a, b):
    return lax.dot_general(a, b, (((1,), (1,)), ((), ())), preferred_element_type=F32)


def _gelu(x):
    return 0.5 * x * (1.0 + jnp.tanh(math.sqrt(2.0 / math.pi) * (x + 0.044715 * (x * x * x))))


def _softplus(x):
    return jnp.maximum(x, 0.0) + jnp.log1p(jnp.exp(-jnp.abs(x)))


def _rms_rows(x, g):
    return x * lax.rsqrt(jnp.mean(x * x, axis=-1, keepdims=True) + EPS) * g


def _group_rmsnorm(x, gain, width):
    n = x.shape[-1] // width
    lane = lax.broadcasted_iota(jnp.int32, x.shape, 1)
    x2 = x * x
    ms = jnp.zeros_like(x)
    for i in range(n):
        inside = (lane >= i * width) & (lane < (i + 1) * width)
        s = jnp.sum(jnp.where(inside, x2, 0.0), axis=-1, keepdims=True)
        ms = jnp.where(inside, s, ms)
    return x * lax.rsqrt(ms * (1.0 / width) + EPS) * gain


def _bucket_np(dist, n_buckets):
    n = np.maximum(dist, 0)
    exact = n_buckets // 2
    nf = np.maximum(n, 1).astype(np.float32)
    large = exact + (np.log(nf / np.float32(exact)) / np.float32(math.log(MAX_DIST / exact))
                     * np.float32(n_buckets - exact)).astype(np.int32)
    return np.where(n < exact, n, np.minimum(large, n_buckets - 1)).astype(np.int32)


def _even_in_prompt_kernel(x_ref, g_ref, wn_ref, wt_ref, qg_ref, kg_ref,
                           xa_ref, ga_ref, kv_ref, win_ref, kc_ref, vc_ref,
                           ks_ref, kw_ref, qT_ref, vsT_ref, vwT_ref, gT_ref, *, lw, qw, kvw, hd):
    tm = x_ref.shape[0]
    hn = _rms_rows(x_ref[...], g_ref[...]).astype(BF16)
    zn = _dot(hn, wn_ref[...])
    xa_ref[...] = zn[:, :lw]
    ga_ref[...] = zn[:, lw:2 * lw]
    kv = [zn[:, 2 * lw + i * kvw: 2 * lw + (i + 1) * kvw] for i in range(6)]
    k_sel = _group_rmsnorm(kv[2], kg_ref[1:2, :], hd)
    k_win = _group_rmsnorm(kv[4], kg_ref[2:3, :], hd)
    kv_ref[:, 0 * kvw:1 * kvw] = kv[0]
    kv_ref[:, 1 * kvw:2 * kvw] = kv[1]
    kv_ref[:, 2 * kvw:3 * kvw] = k_sel
    kv_ref[:, 3 * kvw:4 * kvw] = kv[3]
    win_ref[:, :kvw] = k_win
    win_ref[:, kvw:] = kv[5]
    kc_ref[...] = kv[0]
    vc_ref[...] = kv[1]
    blk = lax.broadcasted_iota(jnp.int32, (tm, kvw), 0) // SEL_BLOCK
    col = lax.broadcasted_iota(jnp.int32, (tm, kvw), 1)
    ks_ref[:, :kvw] = k_sel.astype(BF16)
    ks_ref[:, kvw:] = jnp.where(col == blk, 1.0, 0.0).astype(BF16)
    kw_ref[...] = k_win.astype(BF16)

    zt = _dot_nt(wt_ref[...], hn)
    for h in range(qw // hd):
        blk = zt[h * hd:(h + 1) * hd, :]
        r = lax.rsqrt(jnp.mean(blk * blk, axis=0, keepdims=True) + EPS)
        qn = blk * r
        for j in range(0, tm, LANES):
            qT_ref[h * hd:(h + 1) * hd, j:j + LANES] = (
                qn[:, j:j + LANES] * qg_ref[h * hd:(h + 1) * hd, :]).astype(BF16)
    pad = ACC_ROWS - hd
    ones_rows = jnp.where(lax.broadcasted_iota(jnp.int32, (pad, tm), 0) == 0, 1.0, 0.0).astype(BF16)
    for out_ref, base in ((vsT_ref, qw), (vwT_ref, qw + kvw)):
        for g in range(kvw // hd):
            out_ref[g * ACC_ROWS:g * ACC_ROWS + hd, :] = zt[base + g * hd:base + (g + 1) * hd, :].astype(BF16)
            out_ref[g * ACC_ROWS + hd:(g + 1) * ACC_ROWS, :] = ones_rows
    gT_ref[...] = jax.nn.sigmoid(zt[qw + 2 * kvw:, :])


def _even_in_prompt(x, g_mix, w_in, q_gain, k_gain, dims, tm=512):
    t, d = x.shape
    lw, qw, kvw, hd, heads = dims
    tm = min(tm, t)
    c0, c1, c2 = 2 * lw, 2 * lw + qw, 2 * lw + qw + 6 * kvw
    wn = jnp.concatenate([w_in[:, :c0], w_in[:, c1:c2]], axis=1).astype(BF16)
    wg = w_in[:, c2:c2 + 3 * heads].reshape(d, heads, 3).transpose(0, 2, 1).reshape(d, 3 * heads)
    wt = jnp.concatenate([w_in[:, c0:c1], w_in[:, c1 + 3 * kvw:c1 + 4 * kvw],
                          w_in[:, c1 + 5 * kvw:c1 + 6 * kvw], wg,
                          jnp.zeros((d, 32 - 3 * heads), F32)], axis=1).T.astype(BF16)
    assert tm % KEY_GROUP == 0 and ACC_ROWS > hd and ACC_ROWS % 16 == 0
    qg = jnp.broadcast_to((jnp.tile(q_gain, heads) * (hd ** -0.5 * LOG2E))[:, None], (qw, LANES)).astype(F32)
    vrows = (kvw // hd) * ACC_ROWS
    kg = jnp.tile(k_gain, (1, kvw // hd)).astype(F32)
    kg = jnp.concatenate([kg, jnp.zeros((5, kvw), F32)], axis=0)
    nt = wt.shape[0]
    row = lambda w: pl.BlockSpec((tm, w), lambda i: (i, 0))
    col = lambda r: pl.BlockSpec((r, tm), lambda i: (0, i))
    outs = pl.pallas_call(
        functools.partial(_even_in_prompt_kernel, lw=lw, qw=qw, kvw=kvw, hd=hd),
        grid=(t // tm,),
        in_specs=[row(d), _const_spec((1, d)), _const_spec(wn.shape), _const_spec(wt.shape),
                  _const_spec(qg.shape), _const_spec(kg.shape)],
        out_specs=[row(lw), row(lw), row(4 * kvw), row(2 * kvw), row(kvw), row(kvw),
                   row(2 * kvw), row(kvw), col(qw), col(vrows), col(vrows), col(32)],
        out_shape=[jax.ShapeDtypeStruct((t, lw), F32), jax.ShapeDtypeStruct((t, lw), F32),
                   jax.ShapeDtypeStruct((t, 4 * kvw), F32), jax.ShapeDtypeStruct((t, 2 * kvw), F32),
                   jax.ShapeDtypeStruct((t, kvw), F32), jax.ShapeDtypeStruct((t, kvw), F32),
                   jax.ShapeDtypeStruct((t, 2 * kvw), BF16), jax.ShapeDtypeStruct((t, kvw), BF16),
                   jax.ShapeDtypeStruct((qw, t), BF16), jax.ShapeDtypeStruct((vrows, t), BF16),
                   jax.ShapeDtypeStruct((vrows, t), BF16), jax.ShapeDtypeStruct((32, t), F32)],
        compiler_params=_cparams(), name="even_in_prompt",
    )(x, g_mix[None, :], wn, wt, qg, kg)
    return outs


def _even_in_sample_kernel(x_ref, g_ref, w_ref, qg_ref, kg_ref,
                           xa_ref, ga_ref, q_ref, kv_ref, win_ref, gate_ref, *, lw, qw, kvw, hd):
    hn = _rms_rows(x_ref[...], g_ref[...]).astype(BF16)
    z = _dot(hn, w_ref[...])
    xa_ref[...] = z[:, :lw]
    ga_ref[...] = z[:, lw:2 * lw]
    q_ref[...] = _group_rmsnorm(z[:, 2 * lw:2 * lw + qw], qg_ref[...], hd)
    o = 2 * lw + qw
    kv = [z[:, o + i * kvw: o + (i + 1) * kvw] for i in range(6)]
    kv_ref[:, 0 * kvw:1 * kvw] = kv[0]
    kv_ref[:, 1 * kvw:2 * kvw] = kv[1]
    kv_ref[:, 2 * kvw:3 * kvw] = _group_rmsnorm(kv[2], kg_ref[1:2, :], hd)
    kv_ref[:, 3 * kvw:4 * kvw] = kv[3]
    win_ref[:, :kvw] = _group_rmsnorm(kv[4], kg_ref[2:3, :], hd)
    win_ref[:, kvw:] = kv[5]
    gate_ref[...] = jax.nn.sigmoid(z[:, o + 6 * kvw:])


def _even_in_sample(x, g_mix, w_in, q_gain, k_gain, dims):
    r, d = x.shape
    lw, qw, kvw, hd, heads = dims
    n_in = w_in.shape[1]
    pad = (-n_in) % LANES
    w = jnp.pad(w_in, ((0, 0), (0, pad))).astype(BF16)
    gw = n_in + pad - (2 * lw + qw + 6 * kvw)
    qg = (jnp.tile(q_gain, heads) * hd ** -0.5)[None, :].astype(F32)
    kg = jnp.tile(k_gain, (1, kvw // hd)).astype(F32)
    kg = jnp.concatenate([kg, jnp.zeros((5, kvw), F32)], axis=0)
    full = lambda s: pl.BlockSpec(s, lambda i: (0,) * len(s))
    return pl.pallas_call(
        functools.partial(_even_in_sample_kernel, lw=lw, qw=qw, kvw=kvw, hd=hd),
        grid=(1,),
        in_specs=[full((r, d)), full((1, d)), full(w.shape), full(qg.shape), full(kg.shape)],
        out_specs=[full((r, lw)), full((r, lw)), full((r, qw)), full((r, 4 * kvw)),
                   full((r, 2 * kvw)), full((r, gw))],
        out_shape=[jax.ShapeDtypeStruct((r, lw), F32), jax.ShapeDtypeStruct((r, lw), F32),
                   jax.ShapeDtypeStruct((r, qw), F32), jax.ShapeDtypeStruct((r, 4 * kvw), F32),
                   jax.ShapeDtypeStruct((r, 2 * kvw), F32), jax.ShapeDtypeStruct((r, gw), F32)],
        compiler_params=_cparams(), name="even_in_sample",
    )(x, g_mix[None, :], w, qg, kg)


def _rglru_gates(conv, wa_ref, ba_ref, wx_ref, bx_ref, lam_ref):
    cb = conv.astype(BF16)
    r = jax.nn.sigmoid(_dot(cb, wa_ref[...]) + ba_ref[...])
    i = jax.nn.sigmoid(_dot(cb, wx_ref[...]) + bx_ref[...])
    log_a = -RG_C * r * _softplus(-lam_ref[...])
    a = jnp.exp(log_a)
    th = jnp.tanh(log_a)
    b = jnp.sqrt(-2.0 * th / (1.0 - th)) * (i * conv)
    return a, b


def _rglru_prompt_kernel(xa_ref, ga_ref, cw_ref, cb_ref, wa_ref, ba_ref, wx_ref, bx_ref, lam_ref,
                         ya_ref, hl_ref, xext_ref, h_ref):
    tr = xa_ref.shape[0]

    @pl.when(pl.program_id(0) == 0)
    def _():
        xext_ref[0:8, :] = jnp.zeros((8, xext_ref.shape[1]), F32)
        h_ref[...] = jnp.zeros(h_ref.shape, F32)

    x = xa_ref[...]
    xext_ref[8:8 + tr, :] = x
    conv = (cb_ref[...] + cw_ref[3:4, :] * x + cw_ref[2:3, :] * xext_ref[7:7 + tr, :]
            + cw_ref[1:2, :] * xext_ref[6:6 + tr, :] + cw_ref[0:1, :] * xext_ref[5:5 + tr, :])
    xext_ref[0:8, :] = xext_ref[tr:tr + 8, :]
    a, b = _rglru_gates(conv, wa_ref, ba_ref, wx_ref, bx_ref, lam_ref)
    sub = 8
    lw = a.shape[1]
    a = a.reshape(tr // sub, sub, lw)
    b = b.reshape(tr // sub, sub, lw)
    pos = lax.broadcasted_iota(jnp.int32, a.shape, 1)
    s = 1
    while s < sub:
        keep = pos >= s
        a_sh = jnp.where(keep, pltpu.roll(a, s, 1), 1.0)
        b_sh = jnp.where(keep, pltpu.roll(b, s, 1), 0.0)
        b = a * b_sh + b
        a = a * a_sh
        s *= 2
    h_in = h_ref[0:1, :]
    tiles = []
    for i in range(tr // sub):
        h_tile = a[i] * h_in + b[i]
        tiles.append(h_tile)
        h_in = h_tile[sub - 1:sub, :]
    hs = jnp.concatenate(tiles, axis=0)
    h_last = h_in
    h_ref[...] = jnp.broadcast_to(h_last, h_ref.shape)
    hl_ref[...] = h_last
    ya_ref[...] = (hs * _gelu(ga_ref[...])).astype(BF16)


def _block_diag(w):
    n, a, b = w.shape
    eye = jnp.eye(n, dtype=w.dtype)
    return (eye[:, None, :, None] * w[:, :, None, :]).reshape(n * a, n * b)


def _rglru_weights(conv_w, conv_b, w_a, b_a, w_x, b_x, lam):
    return (conv_w.astype(F32), conv_b[None, :], _block_diag(w_a).astype(BF16), b_a[None, :],
            _block_diag(w_x).astype(BF16), b_x[None, :], lam[None, :])


def _rglru_prompt(xa, ga, weights, tr=256):
    t, lw = xa.shape
    tr = min(tr, t)
    row = pl.BlockSpec((tr, lw), lambda i: (i, 0))
    return pl.pallas_call(
        _rglru_prompt_kernel, grid=(t // tr,),
        in_specs=[row, row] + [_const_spec(w.shape) for w in weights],
        out_specs=[row, pl.BlockSpec((1, lw), lambda i: (0, 0))],
        out_shape=[jax.ShapeDtypeStruct((t, lw), BF16), jax.ShapeDtypeStruct((1, lw), F32)],
        scratch_shapes=[pltpu.VMEM((tr + 8, lw), F32), pltpu.VMEM((8, lw), F32)],
        compiler_params=_cparams(), name="rglru_prompt",
    )(xa, ga, *weights)


def _rglru_sample_kernel(xa_ref, ga_ref, sc_ref, h0_ref, cw_ref, cb_ref, wa_ref, ba_ref, wx_ref,
                         bx_ref, lam_ref, ya_ref, h_ref):
    conv = (cb_ref[...] + cw_ref[3:4, :] * xa_ref[...] + cw_ref[2:3, :] * sc_ref[2]
            + cw_ref[1:2, :] * sc_ref[1] + cw_ref[0:1, :] * sc_ref[0])
    a, b = _rglru_gates(conv, wa_ref, ba_ref, wx_ref, bx_ref, lam_ref)
    h = a * h0_ref[...] + b
    h_ref[...] = h
    ya_ref[...] = (h * _gelu(ga_ref[...])).astype(BF16)


def _rglru_sample(xa, ga, state_conv, state_h, weights):
    r, lw = xa.shape
    sc = jnp.transpose(state_conv, (1, 0, 2))
    args = (xa, ga, sc, state_h) + tuple(weights)
    full = lambda s: pl.BlockSpec(s, lambda i: (0,) * len(s))
    return pl.pallas_call(
        _rglru_sample_kernel, grid=(1,),
        in_specs=[full(a.shape) for a in args],
        out_specs=[full((r, lw)), full((r, lw))],
        out_shape=[jax.ShapeDtypeStruct((r, lw), BF16), jax.ShapeDtypeStruct((r, lw), F32)],
        compiler_params=_cparams(), name="rglru_sample",
    )(*args)


def _compress_prompt_kernel(xk_ref, xv_ref, pek_ref, pev_ref, wk_ref, wvT_ref, kg_ref,
                            kc_ref, vcT_ref, *, hd):
    xk = (xk_ref[...] + pek_ref[...]).astype(BF16)
    kc = _dot(xk, wk_ref[...])
    kc_ref[...] = _group_rmsnorm(kc, kg_ref[...], hd).astype(BF16)
    xv = (xv_ref[...] + pev_ref[...]).astype(BF16)
    vcT_ref[...] = _dot_nt(wvT_ref[...], xv).astype(BF16)


def _compress_weight(w, groups, hd):
    w3 = w.reshape(CMP_BLOCK, hd, hd)
    eye = jnp.eye(groups, dtype=w.dtype)
    full = eye[None, :, None, :, None] * w3[:, None, :, None, :]
    return full.reshape(CMP_BLOCK * groups * hd, groups * hd)


def _compress_prompt(kcmp, vcmp, w_k, w_v, pe_k, pe_v, k_gain_cmp, groups, hd):
    t, kvw = kcmp.shape
    ncb = t // CMP_BLOCK
    kdim = CMP_BLOCK * kvw
    xk = kcmp.reshape(ncb, kdim)
    xv = vcmp.reshape(ncb, kdim)
    wk = _compress_weight(w_k, groups, hd).astype(BF16)
    wvT = _compress_weight(w_v, groups, hd).T.astype(BF16)
    pek = jnp.tile(pe_k, (1, groups)).reshape(1, kdim)
    pev = jnp.tile(pe_v, (1, groups)).reshape(1, kdim)
    kg = jnp.tile(k_gain_cmp, groups)[None, :]
    nb = min(LANES, ncb)
    return pl.pallas_call(
        functools.partial(_compress_prompt_kernel, hd=hd), grid=(ncb // nb,),
        in_specs=[pl.BlockSpec((nb, kdim), lambda i: (i, 0)), pl.BlockSpec((nb, kdim), lambda i: (i, 0)),
                  _const_spec((1, kdim)), _const_spec((1, kdim)), _const_spec(wk.shape),
                  _const_spec(wvT.shape), _const_spec((1, kvw))],
        out_specs=[pl.BlockSpec((nb, kvw), lambda i: (i, 0)), pl.BlockSpec((kvw, nb), lambda i: (0, i))],
        out_shape=[jax.ShapeDtypeStruct((ncb, kvw), BF16), jax.ShapeDtypeStruct((kvw, ncb), BF16)],
        compiler_params=_cparams(), name="compress_prompt",
    )(xk, xv, pek, pev, wk, wvT, kg)


def _tile_lanes(x, n):
    return jnp.concatenate([x] * n, axis=1)


def _attend(s, shift, vT, slot, g, m_ref, acc_ref, m_blk=None):
    m_old = m_ref[slot, g]
    if m_blk is None:
        m_blk = jnp.max(s, axis=0, keepdims=True)
    if shift is not None:
        m_blk = m_blk + shift
    m_new = jnp.maximum(m_old, m_blk)
    p = jnp.exp2(s - (m_new if shift is None else m_new - shift)).astype(BF16)
    acc_ref[slot, g] = jnp.exp2(m_old - m_new) * acc_ref[slot, g] + _dot(vT, p)
    m_ref[slot, g] = m_new


def _bias_table_kernel(id_ref, val_ref, o_ref, *, reps):
    ids = _tile_lanes(id_ref[...], reps)
    out = jnp.zeros(ids.shape, F32)
    for k in range(val_ref.shape[0]):
        out = jnp.where(ids == k, val_ref[k:k + 1, :], out)
    o_ref[...] = out


def _bias_table(ids, vals):
    r = ids.shape[0]
    reps = vals.shape[1] // LANES
    full = lambda s: pl.BlockSpec(s, lambda i: (0,) * len(s))
    return pl.pallas_call(
        functools.partial(_bias_table_kernel, reps=reps), grid=(1,),
        in_specs=[full(ids.shape), full(vals.shape)], out_specs=full((r, vals.shape[1])),
        out_shape=jax.ShapeDtypeStruct((r, vals.shape[1]), F32),
        compiler_params=_cparams(), name="bias_table",
    )(jnp.asarray(ids, jnp.int32), vals)


def _nsa_prompt_kernel(qT_ref, gT_ref, kc_ref, vcT_ref, ks_ref, vsT_ref, kw_ref, vwT_ref,
                       b31_ref, tab_ref,
                       y_ref,
                       qaug_ref, sa_ref, sb_ref, sc_ref, p4_ref, selb_ref, ocmp_ref, m_ref, acc_ref,
                       *, groups, hpg, hd):
    qb = pl.program_id(0)
    t0 = qb * Q_BLOCK
    ncb = kc_ref.shape[0]
    nsb = ncb * CMP_BLOCK // SEL_BLOCK
    gl = hpg * Q_BLOCK
    n_top = min(TOP_N, nsb)
    kvw = groups * hd
    bpg = KEY_GROUP // SEL_BLOCK

    rows = []
    for g in range(groups):
        top = jnp.concatenate([qT_ref[(g * hpg + hh) * hd:(g * hpg + hh + 1) * hd, :]
                               for hh in range(hpg)], axis=1)
        z = jnp.zeros_like(top)
        rows.append(jnp.concatenate([top if gg == g else z for gg in range(groups)], axis=1))
    qbd = jnp.concatenate(rows, axis=0)
    qaug_ref[0:kvw, :] = qbd
    qaug_ref[kvw:2 * kvw, :] = jnp.zeros((kvw, groups * gl), BF16)
    b31 = b31_ref[...]

    def compress_and_select(rc):
        nsv = rc * CMP_BLOCK // SEL_BLOCK
        jrow = lax.broadcasted_iota(jnp.int32, (rc, Q_BLOCK), 0)
        qcol = lax.broadcasted_iota(jnp.int32, (rc, Q_BLOCK), 1)
        vis = _tile_lanes(jrow * CMP_BLOCK + (CMP_BLOCK - 1) <= t0 + qcol, groups * hpg)
        s_c = _dot(kc_ref[0:rc, :], qbd) + b31
        sc_ref[0:rc, :] = jnp.where(vis, s_c, NEG)
        odd = (qb % 2) == 1
        v_a = jnp.where(qb == 0, 0, jnp.where(odd, 1, 2))
        w_a = pl.multiple_of(jnp.where(qb == 0, 0, jnp.where(odd, 4 * qb - 4, 4 * qb - 8)), 8)
        v_b = jnp.where((qb > 0) & jnp.logical_not(odd), 0, 3)
        w_b = pl.multiple_of(jnp.where((qb > 0) & jnp.logical_not(odd), 4 * qb, 0), 8)
        sc_ref[pl.ds(w_a, 8), :] = sc_ref[pl.ds(w_a, 8), :] + tab_ref[pl.ds(pl.multiple_of(v_a * 8, 8), 8), :]
        sc_ref[pl.ds(w_b, 8), :] = sc_ref[pl.ds(w_b, 8), :] + tab_ref[pl.ds(pl.multiple_of(v_b * 8, 8), 8), :]
        s_c = sc_ref[0:rc, :]
        m_c = jnp.max(s_c, axis=0, keepdims=True)
        p_c = jnp.where(vis, jnp.exp2(s_c - m_c), 0.0)
        l_c = jnp.sum(p_c, axis=0, keepdims=True)
        p_c = p_c * jnp.where(l_c > 0.0, 1.0 / l_c, 0.0)
        for g in range(groups):
            pg = p_c[:, g * gl:(g + 1) * gl]
            ocmp_ref[g] = _dot(vcT_ref[g * hd:(g + 1) * hd, 0:rc], pg.astype(BF16))
            p4 = pg[:, 0:Q_BLOCK]
            for hh in range(1, hpg):
                p4 = p4 + pg[:, hh * Q_BLOCK:(hh + 1) * Q_BLOCK]
            p4_ref[g, 0:rc, :] = p4

        ratio = SEL_BLOCK // CMP_BLOCK
        jb = lax.broadcasted_iota(jnp.int32, (nsv, Q_BLOCK), 0)
        qc = lax.broadcasted_iota(jnp.int32, (nsv, Q_BLOCK), 1)
        cur = (t0 + qc) // SEL_BLOCK
        forced = (jb == 0) | (jb == cur) | (jb == cur - 1)
        for g in range(groups):
            imp = p4_ref[g, pl.ds(0, nsv, stride=ratio), :]
            for k in range(1, ratio):
                imp = imp + p4_ref[g, pl.ds(k, nsv, stride=ratio), :]
            imp = jnp.where(forced, FORCE_SCORE, imp)
            imp = jnp.where(jb <= cur, imp, -1.0)
            selb = jnp.full((nsv, Q_BLOCK), NEG, F32)
            for _ in range(n_top):
                mx = jnp.max(imp, axis=0, keepdims=True)
                idx = jnp.min(jnp.where(imp == mx, jb, nsv), axis=0, keepdims=True)
                hit = jb == idx
                selb = jnp.where(hit, 0.0, selb)
                imp = jnp.where(hit, -3.0e38, imp)
            selb_ref[g, 0:nsv, :] = selb
            if nsv < nsb:
                selb_ref[g, nsv:nsb, :] = jnp.full((nsb - nsv, Q_BLOCK), NEG, F32)

    m_ref[...] = jnp.full(m_ref.shape, NEG, F32)
    acc_ref[...] = jnp.zeros(acc_ref.shape, F32)
    n_wk = WINDOW + Q_BLOCK

    def window_block():
        k0 = pl.multiple_of(t0 - WINDOW, Q_BLOCK)
        s = _dot(kw_ref[pl.ds(k0, n_wk), :], qaug_ref[0:kvw, :]) + tab_ref[_TAB_WIN:_TAB_WIN + n_wk, :]
        for g in range(groups):
            _attend(s[:, g * gl:(g + 1) * gl], None,
                    vwT_ref[g * ACC_ROWS:(g + 1) * ACC_ROWS, pl.ds(k0, n_wk)], 1, g, m_ref, acc_ref)

    def put_mask_rows(g, rows8):
        tile = jnp.concatenate([rows8, jnp.zeros_like(rows8)], axis=0).astype(BF16)
        for hh in range(hpg):
            qaug_ref[kvw:kvw + 2 * bpg, g * gl + hh * Q_BLOCK:g * gl + (hh + 1) * Q_BLOCK] = tile

    def far_scores(c, dst_ref):
        for g in range(groups):
            put_mask_rows(g, selb_ref[g, pl.ds(pl.multiple_of(c * bpg, bpg), bpg), :])
        k0 = pl.multiple_of(c * KEY_GROUP, KEY_GROUP)
        s = _dot(ks_ref[pl.ds(k0, KEY_GROUP), :], qaug_ref[...])
        dst_ref[0:KEY_GROUP, :] = s
        dst_ref[KEY_GROUP:KEY_GROUP + 8, :] = jnp.broadcast_to(jnp.max(s, axis=0, keepdims=True),
                                                               (8, groups * gl))

    tiers = list(range(LANES, ncb + 1, LANES)) if ncb % LANES == 0 else [ncb]
    n_vis = (t0 + Q_BLOCK) // CMP_BLOCK
    for i, rc in enumerate(tiers):
        lo = tiers[i - 1] if i else 0
        in_tier = (n_vis > lo) & (n_vis <= rc)

        @pl.when(in_tier & (t0 >= WINDOW))
        def _(rc=rc):
            compress_and_select(rc)
            window_block()
            far_scores(0, sa_ref)

        if lo * CMP_BLOCK < WINDOW + Q_BLOCK:
            @pl.when(in_tier & (t0 < WINDOW))
            def _(rc=rc):
                compress_and_select(rc)
                far_scores(0, sa_ref)


    def far_attend(c, src_ref, live_rows=None):
        k0 = pl.multiple_of(c * KEY_GROUP, KEY_GROUP)
        drop = None
        if live_rows is not None:
            drop = lax.broadcasted_iota(jnp.int32, (KEY_GROUP, Q_BLOCK), 0) >= live_rows
        for g in range(groups):
            sg = src_ref[0:KEY_GROUP, g * gl:(g + 1) * gl]
            m_blk = src_ref[KEY_GROUP:KEY_GROUP + 1, g * gl:(g + 1) * gl]
            if drop is not None:
                sg = jnp.where(_tile_lanes(drop, hpg), NEG, sg)
                m_blk = None
            _attend(sg, b31[:, g * gl:(g + 1) * gl],
                    vsT_ref[g * ACC_ROWS:(g + 1) * ACC_ROWS, pl.ds(k0, KEY_GROUP)], 0, g, m_ref, acc_ref,
                    m_blk=m_blk)

    far_len = jnp.maximum(t0 - Q_BLOCK, 0)
    n_far = far_len // KEY_GROUP
    live_tail = far_len - n_far * KEY_GROUP

    def far_pair(j, carry):
        far_scores(2 * j + 1, sb_ref)
        far_attend(2 * j, sa_ref)
        far_scores(2 * j + 2, sa_ref)
        far_attend(2 * j + 1, sb_ref)
        return carry

    lax.fori_loop(0, n_far // 2, far_pair, 0)

    near_tab = tab_ref[_TAB_NEAR:_TAB_NEAR + 2 * Q_BLOCK, :]

    def near_piece():
        b0 = (t0 - Q_BLOCK) // SEL_BLOCK
        base = (b0 // bpg) * bpg
        nxt = jnp.minimum(base + bpg, nsb - bpg)
        jrow8 = lax.broadcasted_iota(jnp.int32, (bpg, Q_BLOCK), 0)
        for g in range(groups):
            lo = selb_ref[g, pl.ds(pl.multiple_of(base, bpg), bpg), :]
            hi = selb_ref[g, pl.ds(pl.multiple_of(nxt, bpg), bpg), :]
            put_mask_rows(g, jnp.where(jrow8 >= b0 - base, lo, hi))
        k0 = pl.multiple_of(t0 - Q_BLOCK, Q_BLOCK)
        s = _dot(ks_ref[pl.ds(k0, 2 * Q_BLOCK), :], qaug_ref[...]) + near_tab
        for g in range(groups):
            _attend(s[:, g * gl:(g + 1) * gl], None,
                    vsT_ref[g * ACC_ROWS:(g + 1) * ACC_ROWS, pl.ds(k0, 2 * Q_BLOCK)], 0, g, m_ref, acc_ref)

    @pl.when(qb == 0)
    def _():
        for g in range(groups):
            put_mask_rows(g, selb_ref[g, 0:bpg, :])
        s = _dot(ks_ref[0:Q_BLOCK, :], qaug_ref[...]) + near_tab[Q_BLOCK:2 * Q_BLOCK, :]
        for g in range(groups):
            _attend(s[:, g * gl:(g + 1) * gl], None,
                    vsT_ref[g * ACC_ROWS:(g + 1) * ACC_ROWS, 0:Q_BLOCK], 0, g, m_ref, acc_ref)

    odd_far = n_far % 2 == 1
    has_tail = live_tail > 0

    @pl.when(odd_far & has_tail)
    def _():
        far_scores(n_far, sb_ref)
        far_attend(n_far - 1, sa_ref)
        far_attend(n_far, sb_ref, live_tail)
        near_piece()

    @pl.when(odd_far & jnp.logical_not(has_tail))
    def _():
        far_attend(n_far - 1, sa_ref)
        near_piece()

    @pl.when(jnp.logical_not(odd_far) & has_tail)
    def _():
        far_attend(n_far, sa_ref, live_tail)
        near_piece()

    @pl.when(jnp.logical_not(odd_far) & jnp.logical_not(has_tail) & (qb >= 1))
    def _():
        near_piece()

    for i in range(WINDOW // Q_BLOCK):
        k0 = t0 - i * Q_BLOCK

        @pl.when((t0 < WINDOW) & (k0 >= 0))
        def _(i=i, k0=k0):
            k0a = pl.multiple_of(k0, Q_BLOCK)
            r0 = _TAB_WIN + WINDOW - i * Q_BLOCK
            s = _dot(kw_ref[pl.ds(k0a, Q_BLOCK), :], qaug_ref[0:kvw, :]) + tab_ref[r0:r0 + Q_BLOCK, :]
            for g in range(groups):
                _attend(s[:, g * gl:(g + 1) * gl], None,
                        vwT_ref[g * ACC_ROWS:(g + 1) * ACC_ROWS, pl.ds(k0a, Q_BLOCK)], 1, g, m_ref, acc_ref)

    heads = groups * hpg
    pieces = []
    for g in range(groups):
        gate = [jnp.concatenate([gT_ref[j * heads + g * hpg + hh: j * heads + g * hpg + hh + 1, :]
                                 for hh in range(hpg)], axis=1) for j in range(3)]
        o = gate[0] * ocmp_ref[g]
        for slot in range(2):
            acc = acc_ref[slot, g]
            o = o + gate[1 + slot] * (acc[0:hd, :] * (1.0 / acc[hd:hd + 1, :]))
        for hh in range(hpg):
            pieces.append(o[:, hh * Q_BLOCK:(hh + 1) * Q_BLOCK])
    y_ref[...] = jnp.concatenate(pieces, axis=0).T.astype(BF16)


_TAB_NEAR = 32
_TAB_WIN = _TAB_NEAR + 2 * Q_BLOCK


def _nsa_prompt_tables(rel_bias, n_lanes):
    nb, heads = rel_bias.shape
    tbl = rel_bias.astype(F32) * LOG2E
    lanes = lambda x: jnp.repeat(x, n_lanes // heads, axis=1)
    vals = jnp.concatenate([lanes(tbl), jnp.full((1, n_lanes), NEG, F32), jnp.zeros((1, n_lanes), F32),
                            lanes(tbl - tbl[nb - 1:nb])], axis=0)
    c = np.arange(Q_BLOCK)[None, :]
    i8 = np.arange(8)[:, None]
    ids = []
    for base in (-(CMP_BLOCK - 1), 3 * CMP_BLOCK + 1, 7 * CMP_BLOCK + 1):
        d = base + c - CMP_BLOCK * i8
        ids.append(np.where(d >= 0, nb + 2 + _bucket_np(d, nb), nb + 1))
    ids.append(np.full((8, Q_BLOCK), nb + 1))
    d = Q_BLOCK + c - np.arange(2 * Q_BLOCK)[:, None]
    ids.append(np.where(d >= 0, _bucket_np(d, nb), nb))
    d = WINDOW + c - np.arange(WINDOW + Q_BLOCK)[:, None]
    ids.append(np.where((d >= 0) & (d < WINDOW), _bucket_np(d, nb), nb))
    return vals[nb - 1:nb], _bias_table(np.concatenate(ids, axis=0), vals)


def _nsa_prompt(qT, gT, kc, vcT, ks, vsT, kw, vwT, rel_bias, groups, hpg, hd):
    qw, t = qT.shape
    ncb = kc.shape[0]
    nsb = t // SEL_BLOCK
    gl = hpg * Q_BLOCK
    kvw = groups * hd
    assert t % KEY_GROUP == 0 and Q_BLOCK == 4 * CMP_BLOCK and MAX_DIST <= Q_BLOCK
    tables = _nsa_prompt_tables(rel_bias, groups * gl)
    resident = (kc, vcT, ks, vsT, kw, vwT) + tuple(tables)
    return pl.pallas_call(
        functools.partial(_nsa_prompt_kernel, groups=groups, hpg=hpg, hd=hd),
        grid=(t // Q_BLOCK,),
        in_specs=[pl.BlockSpec((qw, Q_BLOCK), lambda i: (0, i)),
                  pl.BlockSpec((gT.shape[0], Q_BLOCK), lambda i: (0, i))]
                 + [_const_spec(a.shape) for a in resident],
        out_specs=pl.BlockSpec((Q_BLOCK, qw), lambda i: (i, 0)),
        out_shape=jax.ShapeDtypeStruct((t, qw), BF16),
        scratch_shapes=[pltpu.VMEM((2 * kvw, groups * gl), BF16),
                        pltpu.VMEM((KEY_GROUP + 8, groups * gl), F32),
                        pltpu.VMEM((KEY_GROUP + 8, groups * gl), F32),
                        pltpu.VMEM((ncb, groups * gl), F32),
                        pltpu.VMEM((groups, ncb, Q_BLOCK), F32),
                        pltpu.VMEM((groups, nsb, Q_BLOCK), F32),
                        pltpu.VMEM((groups, hd, gl), F32),
                        pltpu.VMEM((2, groups, 1, gl), F32),
                        pltpu.VMEM((2, groups, ACC_ROWS, gl), F32)],
        compiler_params=_cparams(), name="nsa_prompt",
    )(qT, gT, *resident)


def _split3(x):
    hi = x.astype(BF16)
    r1 = x - hi.astype(F32)
    mid = r1.astype(BF16)
    lo = (r1 - mid.astype(F32)).astype(BF16)
    return hi, mid, lo


def _nsa_sample_kernel(pt_ref, cache_ref, qbd_ref, qn_ref, gates_ref, kvn_ref, winn_ref, cwin_ref,
                       wkv_ref, pe_ref, perm_ref, kg_ref, gsum_ref,
                       cb_ref, sb_ref, wb_ref, b0_ref, e_ref,
                       o_ref,
                       ringa_ref, ringb_ref, bufkv_ref, sema, semb, p_ref,
                       *, hd, n_pages, page, chunk):
    b = pl.program_id(0)
    nb = pl.num_programs(0)
    kvw = kg_ref.shape[1]
    past = n_pages * page
    ncb = past // CMP_BLOCK
    nsb = past // SEL_BLOCK
    n_chunks = n_pages // chunk
    bpp = 2 * page // CMP_BLOCK

    def chunk_copy(bb, q, p):
        if q < n_chunks:
            return pltpu.make_async_copy(cache_ref.at[pt_ref[bb, q * chunk + p], 0:2 * kvw, :],
                                         ringa_ref.at[q % 2, p], sema.at[q % 2])
        ch = q - n_chunks
        return pltpu.make_async_copy(cache_ref.at[pt_ref[bb, ch * chunk + p], 2 * kvw:4 * kvw, :],
                                     ringb_ref.at[ch % 2, :, pl.ds(p * page, page)], semb.at[ch % 2])

    def for_pages(n, fn, unroll=1):
        def body(p, c):
            fn(p)
            return c
        lax.fori_loop(0, n, body, 0, unroll=unroll)

    def start_chunk(bb, q):
        for_pages(chunk, lambda p: chunk_copy(bb, q, p).start(), unroll=8)

    def wait_chunk(q):
        for_pages(chunk, lambda p: chunk_copy(b, q, p).wait(), unroll=chunk)

    def request_ahead(q):
        if q + 2 < 2 * n_chunks:
            start_chunk(b, q + 2)
        else:
            @pl.when(b + 1 < nb)
            def _():
                start_chunk(b + 1, q + 2 - 2 * n_chunks)

    @pl.when(b == 0)
    def _():
        start_chunk(0, 0)
        start_chunk(0, 1)

    qbd = qbd_ref[0]

    for ch in range(n_chunks):
        wait_chunk(ch)

        def regroup(q, ch=ch):
            xt = jnp.concatenate([ringa_ref[ch % 2, 2 * q], ringa_ref[ch % 2, 2 * q + 1]], axis=1)
            y = _dot_nt(perm_ref[...], (xt + pe_ref[...]).astype(BF16))
            blocks = pl.ds(pl.multiple_of((ch * (chunk // 2) + q) * bpp, bpp), bpp)
            for r in range(CMP_BLOCK):
                bufkv_ref[r, blocks, :] = y[r * bpp:(r + 1) * bpp, :]
        for_pages(chunk // 2, regroup, unroll=chunk // 2)
        request_ahead(ch)

    half = ncb // 2
    zs = [jnp.zeros((half, 2 * kvw), F32), jnp.zeros((ncb - half, 2 * kvw), F32)]
    for r in range(CMP_BLOCK):
        zs[0] = zs[0] + _dot(bufkv_ref[r, 0:half, :].astype(BF16), wkv_ref[r])
        zs[1] = zs[1] + _dot(bufkv_ref[r, half:ncb, :].astype(BF16), wkv_ref[r])
    z = jnp.concatenate(zs, axis=0)
    vc = z[:, kvw:2 * kvw]
    kc = _group_rmsnorm(z[:, 0:kvw], kg_ref[...], hd).astype(BF16)
    s_c = _dot(kc, qbd) + cb_ref[...]
    m_c = jnp.max(s_c, axis=0, keepdims=True)
    p_c = jnp.exp(s_c - m_c)
    p_c = p_c * (1.0 / jnp.sum(p_c, axis=0, keepdims=True))
    o_cmp = _dot(p_c.T.astype(BF16), vc.astype(BF16))
    p_ref[...] = p_c
    ratio = SEL_BLOCK // CMP_BLOCK
    imp = p_ref[pl.ds(0, nsb, stride=ratio), :]
    for k in range(1, ratio):
        imp = imp + p_ref[pl.ds(k, nsb, stride=ratio), :]
    hi, mid, lo = _split3(imp)
    gs = gsum_ref[...]
    imp = _dot(hi, gs) + _dot(mid, gs) + _dot(lo, gs)
    jb = lax.broadcasted_iota(jnp.int32, (nsb, LANES), 0)
    imp = jnp.where((jb == 0) | (jb == nsb - 1), FORCE_SCORE, imp)
    selb = jnp.full((nsb, LANES), NEG, F32)
    for _ in range(min(TOP_N, nsb + 1) - 1):
        mx = jnp.max(imp, axis=0, keepdims=True)
        idx = jnp.min(jnp.where(imp == mx, jb, nsb), axis=0, keepdims=True)
        hit = jb == idx
        selb = jnp.where(hit, 0.0, selb)
        imp = jnp.where(hit, -3.0e38, imp)
    rows = qn_ref.shape[1]
    sel_nat = selb.T[0:rows, :]

    qn = qn_ref[0]
    qnb = qn.astype(BF16)
    kvn = kvn_ref[0]
    b0 = b0_ref[:, 0:1]
    m_s = jnp.sum(qn * kvn[:, 2 * kvw:3 * kvw], axis=1, keepdims=True) + b0
    l_s = jnp.ones_like(m_s)
    acc = jnp.broadcast_to(kvn[:, 3 * kvw:4 * kvw], (rows, kvw))
    nks = e_ref.shape[1]
    bps = nks // SEL_BLOCK
    per_chunk = chunk * page // nks
    for ch in range(n_chunks):
        wait_chunk(n_chunks + ch)
        for sub in range(per_chunk):
            c = ch * per_chunk + sub
            keys = slice(sub * nks, (sub + 1) * nks)
            rolled = sel_nat if c == 0 else pltpu.roll(sel_nat, nsb - c * bps, 1)
            mask = _dot(rolled[:, 0:e_ref.shape[0]].astype(BF16), e_ref[...])
            s = (_dot(qnb, ringb_ref[ch % 2, 0:kvw, keys].astype(BF16)) + mask
                 + sb_ref[:, c * nks:(c + 1) * nks])
            m_new = jnp.maximum(m_s, jnp.max(s, axis=1, keepdims=True))
            alpha = jnp.exp(m_s - m_new)
            p = jnp.exp(s - m_new)
            l_s = alpha * l_s + jnp.sum(p, axis=1, keepdims=True)
            acc = alpha * acc + _dot_nt(p.astype(BF16), ringb_ref[ch % 2, kvw:2 * kvw, keys].astype(BF16))
            m_s = m_new
        request_ahead(n_chunks + ch)
    o_sel = acc * (1.0 / l_s)

    winn = winn_ref[0]
    s_w = _dot(qnb, cwin_ref[0, 0:kvw, :].astype(BF16)) + wb_ref[...]
    s_wn = jnp.sum(qn * winn[:, 0:kvw], axis=1, keepdims=True) + b0
    m_w = jnp.maximum(jnp.max(s_w, axis=1, keepdims=True), s_wn)
    p_w = jnp.exp(s_w - m_w)
    p_wn = jnp.exp(s_wn - m_w)
    l_w = jnp.sum(p_w, axis=1, keepdims=True) + p_wn
    o_win = (_dot_nt(p_w.astype(BF16), cwin_ref[0, kvw:2 * kvw, :].astype(BF16))
             + p_wn * winn[:, kvw:2 * kvw]) * (1.0 / l_w)

    g = gates_ref[0]
    out = g[0] * o_cmp[0:rows, :] + g[1] * o_sel + g[2] * o_win
    o_ref[0] = out[0:8, :]


def _nsa_sample(q, gates, kv_new, win_new, cache_kv, cache_win, page_table, lp, rel_bias,
                groups, hpg, hd):
    r = q.shape[0]
    heads = groups * hpg
    kvw = groups * hd
    n_phys, page = cache_kv.shape[0], cache_kv.shape[1]
    n_pages = page_table.shape[1]
    past = n_pages * page
    wbuf = cache_win.shape[1]
    ncb, nsb = past // CMP_BLOCK, past // SEL_BLOCK
    rows = 16
    chunk = min(32, n_pages // 2)
    nks = min(NKS_SAMPLE, chunk * page)
    assert n_pages % (2 * chunk) == 0 and chunk % 8 == 0 and (chunk * page) % nks == 0 and heads <= rows
    q3 = q.reshape(r, heads, hd)
    gmask = (jnp.arange(groups)[:, None] == (jnp.arange(heads) // hpg)[None, :]).astype(F32)
    qbd = jnp.einsum('rhd,gh->rgdh', q3, gmask).reshape(r, kvw, heads)
    qbd = jnp.pad(qbd, ((0, 0), (0, 0), (0, LANES - heads))).astype(BF16)
    qn = jnp.einsum('rhd,gh->rhgd', q3, gmask).reshape(r, heads, kvw)
    qn = jnp.pad(qn, ((0, 0), (0, rows - heads), (0, 0)))
    gts = gates[:, :3 * heads].reshape(r, heads, 3).transpose(0, 2, 1)
    gts = jnp.broadcast_to(gts[:, :, :, None], (r, 3, heads, kvw))
    gts = jnp.pad(gts, ((0, 0), (0, 0), (0, rows - heads), (0, 0)))
    wk = _compress_weight(lp['cmp_w_k'], groups, hd).reshape(CMP_BLOCK, kvw, kvw)
    wv = _compress_weight(lp['cmp_w_v'], groups, hd).reshape(CMP_BLOCK, kvw, kvw)
    zero = jnp.zeros_like(wk)
    wkv = jnp.concatenate([jnp.concatenate([wk, zero], axis=2),
                           jnp.concatenate([zero, wv], axis=2)], axis=1).astype(BF16)
    pe_t = jnp.concatenate([jnp.tile(lp['pe_k'].T, (groups, 1)), jnp.tile(lp['pe_v'].T, (groups, 1))], axis=0)
    pe_t = jnp.tile(pe_t, (1, 2 * page // CMP_BLOCK))
    bpp = 2 * page // CMP_BLOCK
    dst = np.arange(2 * page)
    src = (dst % bpp) * CMP_BLOCK + dst // bpp
    perm = jnp.asarray(src[:, None] == np.arange(2 * page)[None, :], BF16)
    kg = jnp.tile(lp['k_gain'][0], groups)[None, :]
    lane_g = np.where(np.arange(LANES) < heads, np.arange(LANES) // hpg, -1)
    gsum = jnp.asarray((lane_g[:, None] == lane_g[None, :]) & (lane_g[:, None] >= 0), BF16)
    nbk = rel_bias.shape[0]
    tbl = rel_bias.astype(F32)
    d_c = (past - (np.arange(ncb) * CMP_BLOCK + CMP_BLOCK - 1))[:, None]
    vals_t = jnp.concatenate([tbl, jnp.broadcast_to(tbl[:, :1], (nbk, LANES - heads))], axis=1)
    cb = _bias_table(np.broadcast_to(_bucket_np(d_c, nbk), (ncb, LANES)), vals_t)
    vals_n = jnp.concatenate([jnp.repeat(tbl, LANES, axis=1), jnp.full((1, heads * LANES), NEG, F32)], axis=0)

    def head_rows(ids):
        out = _bias_table(ids, vals_n).reshape(ids.shape[0], heads, LANES)
        out = jnp.transpose(out, (1, 0, 2)).reshape(heads, ids.shape[0] * LANES)
        return jnp.pad(out, ((0, rows - heads), (0, 0)))

    sb = head_rows(_bucket_np(past - np.arange(past), nbk).reshape(past // LANES, LANES))
    d_w = wbuf - np.arange(wbuf)
    wb = head_rows(np.where(d_w < WINDOW, _bucket_np(d_w, nbk), nbk).reshape(wbuf // LANES, LANES))
    b0 = jnp.pad(jnp.broadcast_to(tbl[0][:, None], (heads, LANES)), ((0, rows - heads), (0, 0)))
    blk = np.arange(min(nsb, max(LANES, nks // SEL_BLOCK)))[:, None]
    spread = jnp.asarray((blk == np.arange(nks)[None, :] // SEL_BLOCK), BF16)

    cache_t = jnp.transpose(cache_kv, (0, 2, 3, 4, 1)).reshape(n_phys, 4 * kvw, page)
    cwin_t = jnp.transpose(cache_win, (0, 2, 3, 4, 1)).reshape(r, 2 * kvw, wbuf)
    kvn = kv_new.reshape(r, 1, 4 * kvw)
    winn = win_new.reshape(r, 1, 2 * kvw)
    per_b = lambda s: pl.BlockSpec((1,) + s, lambda b, pt: (b,) + (0,) * len(s))
    const = lambda a: pl.BlockSpec(a.shape, lambda b, pt: (0,) * a.ndim, pipeline_mode=pl.Buffered(1))
    consts = (wkv, pe_t, perm, kg, gsum, cb, sb, wb, b0, spread)
    grid_spec = pltpu.PrefetchScalarGridSpec(
        num_scalar_prefetch=1, grid=(r,),
        in_specs=[pl.BlockSpec(memory_space=pl.ANY), per_b((kvw, LANES)), per_b((rows, kvw)),
                  per_b((3, rows, kvw)), per_b((1, 4 * kvw)), per_b((1, 2 * kvw)),
                  per_b((2 * kvw, wbuf))]
                 + [const(a) for a in consts],
        out_specs=per_b((8, kvw)),
        scratch_shapes=[pltpu.VMEM((2, chunk, 2 * kvw, page), F32),
                        pltpu.VMEM((2, 2 * kvw, chunk * page), F32),
                        pltpu.VMEM((CMP_BLOCK, ncb, 2 * kvw), F32),
                        pltpu.SemaphoreType.DMA((2,)), pltpu.SemaphoreType.DMA((2,)),
                        pltpu.VMEM((ncb, LANES), F32)])
    o8 = pl.pallas_call(
        functools.partial(_nsa_sample_kernel, hd=hd, n_pages=n_pages, page=page, chunk=chunk),
        grid_spec=grid_spec,
        out_shape=jax.ShapeDtypeStruct((r, 8, kvw), F32),
        compiler_params=_cparams(), name="nsa_sample",
    )(page_table, cache_t, qbd, qn, gts, kvn, winn, cwin_t, *consts)
    o4 = o8[:, :heads, :].reshape(r, heads, groups, hd)
    y = jnp.take_along_axis(o4, (jnp.arange(heads) // hpg)[None, :, None, None], axis=2)
    return y.reshape(r, heads * hd).astype(BF16)


def _mlp_ple(h, p_ref, gm_ref, wu_ref, wd_ref, gp_ref, wg_ref, wp_ref):
    up = _dot(_rms_rows(h, gm_ref[...]).astype(BF16), wu_ref[...])
    act = jnp.square(jnp.maximum(up, 0.0)).astype(BF16)
    h = h + _dot(act, wd_ref[...])
    gate = jax.nn.sigmoid(_dot(_rms_rows(h, gp_ref[...]).astype(BF16), wg_ref[...]))
    return h + gate * _dot(p_ref[...].astype(BF16), wp_ref[...])


def _even_tail_kernel(h_ref, ya_ref, yb_ref, p_ref, woa_ref, wob_ref,
                      gm_ref, wu_ref, wd_ref, gp_ref, wg_ref, wp_ref, o_ref):
    h = h_ref[...] + _dot(ya_ref[...], woa_ref[...]) + _dot(yb_ref[...], wob_ref[...])
    o_ref[...] = _mlp_ple(h, p_ref, gm_ref, wu_ref, wd_ref, gp_ref, wg_ref, wp_ref)


def _tail_weights(g_mlp, w_up, w_down, g_ple, w_gate, w_proj):
    return (g_mlp[None, :], w_up.astype(BF16), w_down.astype(BF16), g_ple[None, :],
            w_gate.astype(BF16), w_proj.astype(BF16))


def _ple_spec(p_all, layer, tm):
    if p_all.shape[1] == 1:
        return pl.BlockSpec((None, None, tm, p_all.shape[3]), lambda i: (layer, 0, i, 0))
    return pl.BlockSpec((None, tm, None, p_all.shape[3]), lambda i: (layer, i, 0, 0))


def _even_tail(h, ya, yb, p_all, layer, w_out, tail_w, tm=256):
    t, d = h.shape
    tm = min(tm, t)
    lw = ya.shape[1]
    woa, wob = w_out[:lw].astype(BF16), w_out[lw:].astype(BF16)
    row = lambda w: pl.BlockSpec((tm, w), lambda i: (i, 0))
    weights = (woa, wob) + tuple(tail_w)
    return pl.pallas_call(
        _even_tail_kernel, grid=(t // tm,),
        in_specs=[row(d), row(lw), row(yb.shape[1]), _ple_spec(p_all, layer, tm)]
                 + [_const_spec(w.shape) for w in weights],
        out_specs=row(d), out_shape=jax.ShapeDtypeStruct((t, d), F32),
        compiler_params=_cparams(), name="even_tail",
    )(h, ya, yb, p_all, *weights)


def _odd_kernel(h_ref, p_ref, gx_ref, wi_ref, lg_ref, lb_ref, sw_ref, sb_ref, wo_ref,
                gm_ref, wu_ref, wd_ref, gp_ref, wg_ref, wp_ref, o_ref, v_ref, *, sw_groups, single):
    h = h_ref[...]
    tm = h.shape[0]
    z = _gelu(_dot(_rms_rows(h, gx_ref[...]).astype(BF16), wi_ref[...]))
    width = z.shape[1] // 2
    u, v = z[:, :width], z[:, width:]
    mu = jnp.mean(v, axis=-1, keepdims=True)
    vc = v - mu
    vn = vc * lax.rsqrt(jnp.mean(vc * vc, axis=-1, keepdims=True) + EPS) * lg_ref[...] + lb_ref[...]
    v_ref[...] = vn
    gw = width // sw_groups
    if single:
        s = vn * sw_ref[...] + sb_ref[...]
    else:
        vb = vn.astype(BF16)
        parts = []
        for c in range(tm // CHUNK):
            cols = [_dot(sw_ref[g], vb[c * CHUNK:(c + 1) * CHUNK, g * gw:(g + 1) * gw])
                    for g in range(sw_groups)]
            parts.append(jnp.concatenate(cols, axis=1) + sb_ref[...])
        s = jnp.concatenate(parts, axis=0)
    h = h + _dot((u * s).astype(BF16), wo_ref[...])
    o_ref[...] = _mlp_ple(h, p_ref, gm_ref, wu_ref, wd_ref, gp_ref, wg_ref, wp_ref)


def _odd_layer(h, p_all, layer, g_mix, w_in, ln_g, ln_b, sgu_w, sgu_b, w_out, tail_w, single, tm=256):
    t, d = h.shape
    tm = min(tm, t)
    ng, ch, _ = sgu_w.shape
    width = w_in.shape[1] // 2
    gw = width // ng
    if single:
        sw = jnp.repeat(sgu_w[:, 0, 0], gw)[None, :]
        sb = jnp.repeat(sgu_b[:, 0], gw)[None, :]
    else:
        sw = (sgu_w * jnp.tril(jnp.ones((ch, ch), sgu_w.dtype))).astype(BF16)
        sb = jnp.repeat(sgu_b.T, gw, axis=1)
    weights = (g_mix[None, :], w_in.astype(BF16), ln_g[None, :], ln_b[None, :], sw, sb,
               w_out.astype(BF16)) + tuple(tail_w)
    row = lambda w: pl.BlockSpec((tm, w), lambda i: (i, 0))
    return pl.pallas_call(
        functools.partial(_odd_kernel, sw_groups=ng, single=single), grid=(t // tm,),
        in_specs=[row(d), _ple_spec(p_all, layer, tm)] + [_const_spec(w.shape) for w in weights],
        out_specs=[row(d), row(width)],
        out_shape=[jax.ShapeDtypeStruct((t, d), F32), jax.ShapeDtypeStruct((t, width), F32)],
        compiler_params=_cparams(), name="odd_single" if single else "odd_prompt",
    )(h, p_all, *weights)


def kernel(x_prompt, x_sample, cache_kv, cache_win, state_conv, state_h, page_table, p_prompt, p_sample,
           rel_bias, g_mix, g_mlp, w_up, w_down, g_ple, w_ple_gate, w_ple_proj,
           w_in_even, w_out_even, conv_w, conv_b, rg_w_a, rg_b_a, rg_w_x, rg_b_x, rg_lambda,
           q_gain, k_gain, cmp_w_k, cmp_w_v, cmp_pe_k, cmp_pe_v,
           w_in_odd, ln_v_g, ln_v_b, sgu_w, sgu_b, w_out_odd):
    bsz, t, d = x_prompt.shape
    r = x_sample.shape[0]
    assert bsz == 1 and x_sample.shape[1] == 1
    depth = g_mix.shape[0]
    lw = conv_w.shape[-1]
    hd = q_gain.shape[-1]
    groups = cache_kv.shape[4]
    heads = rel_bias.shape[1]
    hpg = heads // groups
    qw, kvw = heads * hd, groups * hd
    dims = (lw, qw, kvw, hd, heads)
    wbuf = cache_win.shape[2]

    hp = x_prompt.reshape(t, d)
    hs = x_sample.reshape(r, d)
    kvp, kvs, wnp, wns, cvp, cvs, hhp, hhs, vvs = [], [], [], [], [], [], [], [], []
    for i in range(depth):
        tail_w = _tail_weights(g_mlp[i], w_up[i], w_down[i], g_ple[i], w_ple_gate[i], w_ple_proj[i])
        if i % 2 == 0:
            e = i // 2
            lp = {'cmp_w_k': cmp_w_k[e], 'cmp_w_v': cmp_w_v[e], 'pe_k': cmp_pe_k[e], 'pe_v': cmp_pe_v[e],
                  'k_gain': k_gain[e]}
            rg_w = _rglru_weights(conv_w[e], conv_b[e], rg_w_a[e], rg_b_a[e], rg_w_x[e], rg_b_x[e],
                                  rg_lambda[e])
            (xa, ga, kv_rows, win_rows, kcmp, vcmp, ks, kw, qT, vsT, vwT, gT) = _even_in_prompt(
                hp, g_mix[i], w_in_even[e], q_gain[e], k_gain[e], dims)
            ya, h_last = _rglru_prompt(xa, ga, rg_w)
            kc, vcT = _compress_prompt(kcmp, vcmp, cmp_w_k[e], cmp_w_v[e], cmp_pe_k[e], cmp_pe_v[e],
                                       k_gain[e, 0], groups, hd)
            yb = _nsa_prompt(qT, gT, kc, vcT, ks, vsT, kw, vwT, rel_bias, groups, hpg, hd)
            hp = _even_tail(hp, ya, yb, p_prompt, i, w_out_even[e], tail_w)
            kvp.append(kv_rows.reshape(1, t, 4, groups, hd))
            wnp.append(win_rows[t - wbuf:].reshape(1, wbuf, 2, groups, hd))
            cvp.append(xa[t - (conv_w.shape[1] - 1):][None])
            hhp.append(h_last)
            xa_s, ga_s, q_s, kv_s, win_s, gate_s = _even_in_sample(
                hs, g_mix[i], w_in_even[e], q_gain[e], k_gain[e], dims)
            ya_s, h_s = _rglru_sample(xa_s, ga_s, state_conv[e], state_h[e], rg_w)
            yb_s = _nsa_sample(q_s, gate_s, kv_s, win_s, cache_kv[e], cache_win[e], page_table, lp,
                               rel_bias, groups, hpg, hd)
            hs = _even_tail(hs, ya_s, yb_s, p_sample, i, w_out_even[e], tail_w)
            kvs.append(kv_s.reshape(r, 1, 4, groups, hd))
            wns.append(jnp.concatenate([cache_win[e][:, 1:], win_s.reshape(r, 1, 2, groups, hd)], axis=1))
            cvs.append(jnp.concatenate([state_conv[e][:, 1:], xa_s[:, None, :]], axis=1))
            hhs.append(h_s)
        else:
            o = i // 2
            hp, _ = _odd_layer(hp, p_prompt, i, g_mix[i], w_in_odd[o], ln_v_g[o], ln_v_b[o],
                               sgu_w[o], sgu_b[o], w_out_odd[o], tail_w, single=False)
            hs, v_new = _odd_layer(hs, p_sample, i, g_mix[i], w_in_odd[o], ln_v_g[o], ln_v_b[o],
                                   sgu_w[o], sgu_b[o], w_out_odd[o], tail_w, single=True)
            vvs.append(v_new[:, None, :])
    v_sample = jnp.stack(vvs) if vvs else jnp.zeros((0, r, 1, w_in_odd.shape[-1] // 2), F32)
    return (hp[None], hs[:, None, :], jnp.stack(kvp), jnp.stack(kvs), jnp.stack(wnp), jnp.stack(wns),
            jnp.stack(cvp), jnp.stack(cvs), jnp.stack(hhp), jnp.stack(hhs), v_sample)
```

```python
import functools
import math

import numpy as np
import jax
import jax.numpy as jnp
from jax import lax
from jax.experimental import pallas as pl
from jax.experimental.pallas import tpu as pltpu

F32 = jnp.float32
BF16 = jnp.bfloat16

EPS = 1e-6
NEG = -1e30
RG_C = 8.0
CMP_BLOCK = 32
SEL_BLOCK = 64
TOP_N = 16
WINDOW = 512
Q_BLOCK = 128
FORCE_SCORE = 1.0e4
MAX_DIST = 128
CHUNK = 128

LANES = 128
VMEM_LIMIT = 56 * 1024 * 1024

LOG2E = math.log2(math.e)
KEY_GROUP = 512
ACC_ROWS = 80
NKS_SAMPLE = 4096


def _cparams(n_axes=1):
    return pltpu.CompilerParams(dimension_semantics=("arbitrary",) * n_axes,
                                vmem_limit_bytes=VMEM_LIMIT)


def _const_spec(shape):
    nd = len(shape)
    return pl.BlockSpec(shape, lambda *_: (0,) * nd, pipeline_mode=pl.Buffered(1))


def _dot(a, b):
    return jnp.dot(a, b, preferred_element_type=F32)


def _dot_nt(a, b):
    return lax.dot_general(a, b, (((1,), (1,)), ((), ())), preferred_element_type=F32)


def _gelu(x):
    return 0.5 * x * (1.0 + jnp.tanh(math.sqrt(2.0 / math.pi) * (x + 0.044715 * (x * x * x))))


def _softplus(x):
    return jnp.maximum(x, 0.0) + jnp.log1p(jnp.exp(-jnp.abs(x)))


def _rms_rows(x, g):
    return x * lax.rsqrt(jnp.mean(x * x, axis=-1, keepdims=True) + EPS) * g


def _group_rmsnorm(x, gain, width):
    n = x.shape[-1] // width
    lane = lax.broadcasted_iota(jnp.int32, x.shape, 1)
    x2 = x * x
    ms = jnp.zeros_like(x)
    for i in range(n):
        inside = (lane >= i * width) & (lane < (i + 1) * width)
        s = jnp.sum(jnp.where(inside, x2, 0.0), axis=-1, keepdims=True)
        ms = jnp.where(inside, s, ms)
    return x * lax.rsqrt(ms * (1.0 / width) + EPS) * gain


def _bucket_np(dist, n_buckets):
    n = np.maximum(dist, 0)
    exact = n_buckets // 2
    nf = np.maximum(n, 1).astype(np.float32)
    large = exact + (np.log(nf / np.float32(exact)) / np.float32(math.log(MAX_DIST / exact))
                     * np.float32(n_buckets - exact)).astype(np.int32)
    return np.where(n < exact, n, np.minimum(large, n_buckets - 1)).astype(np.int32)


def _even_in_prompt_kernel(x_ref, g_ref, wn_ref, wt_ref, qg_ref, kg_ref,
                           xa_ref, ga_ref, kv_ref, win_ref, kc_ref, vc_ref,
                           ks_ref, kw_ref, qT_ref, vsT_ref, vwT_ref, gT_ref, *, lw, qw, kvw, hd):
    tm = x_ref.shape[0]
    hn = _rms_rows(x_ref[...], g_ref[...]).astype(BF16)
    zn = _dot(hn, wn_ref[...])
    xa_ref[...] = zn[:, :lw]
    ga_ref[...] = zn[:, lw:2 * lw]
    kv = [zn[:, 2 * lw + i * kvw: 2 * lw + (i + 1) * kvw] for i in range(6)]
    k_sel = _group_rmsnorm(kv[2], kg_ref[1:2, :], hd)
    k_win = _group_rmsnorm(kv[4], kg_ref[2:3, :], hd)
    kv_ref[:, 0 * kvw:1 * kvw] = kv[0]
    kv_ref[:, 1 * kvw:2 * kvw] = kv[1]
    kv_ref[:, 2 * kvw:3 * kvw] = k_sel
    kv_ref[:, 3 * kvw:4 * kvw] = kv[3]
    win_ref[:, :kvw] = k_win
    win_ref[:, kvw:] = kv[5]
    kc_ref[...] = kv[0]
    vc_ref[...] = kv[1]
    blk = lax.broadcasted_iota(jnp.int32, (tm, kvw), 0) // SEL_BLOCK
    col = lax.broadcasted_iota(jnp.int32, (tm, kvw), 1)
    ks_ref[:, :kvw] = k_sel.astype(BF16)
    ks_ref[:, kvw:] = jnp.where(col == blk, 1.0, 0.0).astype(BF16)
    kw_ref[...] = k_win.astype(BF16)

    zt = _dot_nt(wt_ref[...], hn)
    for h in range(qw // hd):
        blk = zt[h * hd:(h + 1) * hd, :]
        r = lax.rsqrt(jnp.mean(blk * blk, axis=0, keepdims=True) + EPS)
        qn = blk * r
        for j in range(0, tm, LANES):
            qT_ref[h * hd:(h + 1) * hd, j:j + LANES] = (
                qn[:, j:j + LANES] * qg_ref[h * hd:(h + 1) * hd, :]).astype(BF16)
    pad = ACC_ROWS - hd
    ones_rows = jnp.where(lax.broadcasted_iota(jnp.int32, (pad, tm), 0) == 0, 1.0, 0.0).astype(BF16)
    for out_ref, base in ((vsT_ref, qw), (vwT_ref, qw + kvw)):
        for g in range(kvw // hd):
            out_ref[g * ACC_ROWS:g * ACC_ROWS + hd, :] = zt[base + g * hd:base + (g + 1) * hd, :].astype(BF16)
            out_ref[g * ACC_ROWS + hd:(g + 1) * ACC_ROWS, :] = ones_rows
    gT_ref[...] = jax.nn.sigmoid(zt[qw + 2 * kvw:, :])


def _even_in_prompt(x, g_mix, w_in, q_gain, k_gain, dims, tm=512):
    t, d = x.shape
    lw, qw, kvw, hd, heads = dims
    tm = min(tm, t)
    c0, c1, c2 = 2 * lw, 2 * lw + qw, 2 * lw + qw + 6 * kvw
    wn = jnp.concatenate([w_in[:, :c0], w_in[:, c1:c2]], axis=1).astype(BF16)
    wg = w_in[:, c2:c2 + 3 * heads].reshape(d, heads, 3).transpose(0, 2, 1).reshape(d, 3 * heads)
    wt = jnp.concatenate([w_in[:, c0:c1], w_in[:, c1 + 3 * kvw:c1 + 4 * kvw],
                          w_in[:, c1 + 5 * kvw:c1 + 6 * kvw], wg,
                          jnp.zeros((d, 32 - 3 * heads), F32)], axis=1).T.astype(BF16)
    assert tm % KEY_GROUP == 0 and ACC_ROWS > hd and ACC_ROWS % 16 == 0
    qg = jnp.broadcast_to((jnp.tile(q_gain, heads) * (hd ** -0.5 * LOG2E))[:, None], (qw, LANES)).astype(F32)
    vrows = (kvw // hd) * ACC_ROWS
    kg = jnp.tile(k_gain, (1, kvw // hd)).astype(F32)
    kg = jnp.concatenate([kg, jnp.zeros((5, kvw), F32)], axis=0)
    nt = wt.shape[0]
    row = lambda w: pl.BlockSpec((tm, w), lambda i: (i, 0))
    col = lambda r: pl.BlockSpec((r, tm), lambda i: (0, i))
    outs = pl.pallas_call(
        functools.partial(_even_in_prompt_kernel, lw=lw, qw=qw, kvw=kvw, hd=hd),
        grid=(t // tm,),
        in_specs=[row(d), _const_spec((1, d)), _const_spec(wn.shape), _const_spec(wt.shape),
                  _const_spec(qg.shape), _const_spec(kg.shape)],
        out_specs=[row(lw), row(lw), row(4 * kvw), row(2 * kvw), row(kvw), row(kvw),
                   row(2 * kvw), row(kvw), col(qw), col(vrows), col(vrows), col(32)],
        out_shape=[jax.ShapeDtypeStruct((t, lw), F32), jax.ShapeDtypeStruct((t, lw), F32),
                   jax.ShapeDtypeStruct((t, 4 * kvw), F32), jax.ShapeDtypeStruct((t, 2 * kvw), F32),
                   jax.ShapeDtypeStruct((t, kvw), F32), jax.ShapeDtypeStruct((t, kvw), F32),
                   jax.ShapeDtypeStruct((t, 2 * kvw), BF16), jax.ShapeDtypeStruct((t, kvw), BF16),
                   jax.ShapeDtypeStruct((qw, t), BF16), jax.ShapeDtypeStruct((vrows, t), BF16),
                   jax.ShapeDtypeStruct((vrows, t), BF16), jax.ShapeDtypeStruct((32, t), F32)],
        compiler_params=_cparams(), name="even_in_prompt",
    )(x, g_mix[None, :], wn, wt, qg, kg)
    return outs


def _even_in_sample_kernel(x_ref, g_ref, w_ref, qg_ref, kg_ref,
                           xa_ref, ga_ref, q_ref, kv_ref, win_ref, gate_ref, *, lw, qw, kvw, hd):
    hn = _rms_rows(x_ref[...], g_ref[...]).astype(BF16)
    z = _dot(hn, w_ref[...])
    xa_ref[...] = z[:, :lw]
    ga_ref[...] = z[:, lw:2 * lw]
    q_ref[...] = _group_rmsnorm(z[:, 2 * lw:2 * lw + qw], qg_ref[...], hd)
    o = 2 * lw + qw
    kv = [z[:, o + i * kvw: o + (i + 1) * kvw] for i in range(6)]
    kv_ref[:, 0 * kvw:1 * kvw] = kv[0]
    kv_ref[:, 1 * kvw:2 * kvw] = kv[1]
    kv_ref[:, 2 * kvw:3 * kvw] = _group_rmsnorm(kv[2], kg_ref[1:2, :], hd)
    kv_ref[:, 3 * kvw:4 * kvw] = kv[3]
    win_ref[:, :kvw] = _group_rmsnorm(kv[4], kg_ref[2:3, :], hd)
    win_ref[:, kvw:] = kv[5]
    gate_ref[...] = jax.nn.sigmoid(z[:, o + 6 * kvw:])


def _even_in_sample(x, g_mix, w_in, q_gain, k_gain, dims):
    r, d = x.shape
    lw, qw, kvw, hd, heads = dims
    n_in = w_in.shape[1]
    pad = (-n_in) % LANES
    w = jnp.pad(w_in, ((0, 0), (0, pad))).astype(BF16)
    gw = n_in + pad - (2 * lw + qw + 6 * kvw)
    qg = (jnp.tile(q_gain, heads) * hd ** -0.5)[None, :].astype(F32)
    kg = jnp.tile(k_gain, (1, kvw // hd)).astype(F32)
    kg = jnp.concatenate([kg, jnp.zeros((5, kvw), F32)], axis=0)
    full = lambda s: pl.BlockSpec(s, lambda i: (0,) * len(s))
    return pl.pallas_call(
        functools.partial(_even_in_sample_kernel, lw=lw, qw=qw, kvw=kvw, hd=hd),
        grid=(1,),
        in_specs=[full((r, d)), full((1, d)), full(w.shape), full(qg.shape), full(kg.shape)],
        out_specs=[full((r, lw)), full((r, lw)), full((r, qw)), full((r, 4 * kvw)),
                   full((r, 2 * kvw)), full((r, gw))],
        out_shape=[jax.ShapeDtypeStruct((r, lw), F32), jax.ShapeDtypeStruct((r, lw), F32),
                   jax.ShapeDtypeStruct((r, qw), F32), jax.ShapeDtypeStruct((r, 4 * kvw), F32),
                   jax.ShapeDtypeStruct((r, 2 * kvw), F32), jax.ShapeDtypeStruct((r, gw), F32)],
        compiler_params=_cparams(), name="even_in_sample",
    )(x, g_mix[None, :], w, qg, kg)


def _rglru_gates(conv, wa_ref, ba_ref, wx_ref, bx_ref, lam_ref):
    cb = conv.astype(BF16)
    r = jax.nn.sigmoid(_dot(cb, wa_ref[...]) + ba_ref[...])
    i = jax.nn.sigmoid(_dot(cb, wx_ref[...]) + bx_ref[...])
    log_a = -RG_C * r * _softplus(-lam_ref[...])
    a = jnp.exp(log_a)
    th = jnp.tanh(log_a)
    b = jnp.sqrt(-2.0 * th / (1.0 - th)) * (i * conv)
    return a, b


def _rglru_prompt_kernel(xa_ref, ga_ref, cw_ref, cb_ref, wa_ref, ba_ref, wx_ref, bx_ref, lam_ref,
                         ya_ref, hl_ref, xext_ref, h_ref):
    tr = xa_ref.shape[0]

    @pl.when(pl.program_id(0) == 0)
    def _():
        xext_ref[0:8, :] = jnp.zeros((8, xext_ref.shape[1]), F32)
        h_ref[...] = jnp.zeros(h_ref.shape, F32)

    x = xa_ref[...]
    xext_ref[8:8 + tr, :] = x
    conv = (cb_ref[...] + cw_ref[3:4, :] * x + cw_ref[2:3, :] * xext_ref[7:7 + tr, :]
            + cw_ref[1:2, :] * xext_ref[6:6 + tr, :] + cw_ref[0:1, :] * xext_ref[5:5 + tr, :])
    xext_ref[0:8, :] = xext_ref[tr:tr + 8, :]
    a, b = _rglru_gates(conv, wa_ref, ba_ref, wx_ref, bx_ref, lam_ref)
    sub = 8
    lw = a.shape[1]
    a = a.reshape(tr // sub, sub, lw)
    b = b.reshape(tr // sub, sub, lw)
    pos = lax.broadcasted_iota(jnp.int32, a.shape, 1)
    s = 1
    while s < sub:
        keep = pos >= s
        a_sh = jnp.where(keep, pltpu.roll(a, s, 1), 1.0)
        b_sh = jnp.where(keep, pltpu.roll(b, s, 1), 0.0)
        b = a * b_sh + b
        a = a * a_sh
        s *= 2
    h_in = h_ref[0:1, :]
    tiles = []
    for i in range(tr // sub):
        h_tile = a[i] * h_in + b[i]
        tiles.append(h_tile)
        h_in = h_tile[sub - 1:sub, :]
    hs = jnp.concatenate(tiles, axis=0)
    h_last = h_in
    h_ref[...] = jnp.broadcast_to(h_last, h_ref.shape)
    hl_ref[...] = h_last
    ya_ref[...] = (hs * _gelu(ga_ref[...])).astype(BF16)


def _block_diag(w):
    n, a, b = w.shape
    eye = jnp.eye(n, dtype=w.dtype)
    return (eye[:, None, :, None] * w[:, :, None, :]).reshape(n * a, n * b)


def _rglru_weights(conv_w, conv_b, w_a, b_a, w_x, b_x, lam):
    return (conv_w.astype(F32), conv_b[None, :], _block_diag(w_a).astype(BF16), b_a[None, :],
            _block_diag(w_x).astype(BF16), b_x[None, :], lam[None, :])


def _rglru_prompt(xa, ga, weights, tr=256):
    t, lw = xa.shape
    tr = min(tr, t)
    row = pl.BlockSpec((tr, lw), lambda i: (i, 0))
    return pl.pallas_call(
        _rglru_prompt_kernel, grid=(t // tr,),
        in_specs=[row, row] + [_const_spec(w.shape) for w in weights],
        out_specs=[row, pl.BlockSpec((1, lw), lambda i: (0, 0))],
        out_shape=[jax.ShapeDtypeStruct((t, lw), BF16), jax.ShapeDtypeStruct((1, lw), F32)],
        scratch_shapes=[pltpu.VMEM((tr + 8, lw), F32), pltpu.VMEM((8, lw), F32)],
        compiler_params=_cparams(), name="rglru_prompt",
    )(xa, ga, *weights)


def _rglru_sample_kernel(xa_ref, ga_ref, sc_ref, h0_ref, cw_ref, cb_ref, wa_ref, ba_ref, wx_ref,
                         bx_ref, lam_ref, ya_ref, h_ref):
    conv = (cb_ref[...] + cw_ref[3:4, :] * xa_ref[...] + cw_ref[2:3, :] * sc_ref[2]
            + cw_ref[1:2, :] * sc_ref[1] + cw_ref[0:1, :] * sc_ref[0])
    a, b = _rglru_gates(conv, wa_ref, ba_ref, wx_ref, bx_ref, lam_ref)
    h = a * h0_ref[...] + b
    h_ref[...] = h
    ya_ref[...] = (h * _gelu(ga_ref[...])).astype(BF16)


def _rglru_sample(xa, ga, state_conv, state_h, weights):
    r, lw = xa.shape
    sc = jnp.transpose(state_conv, (1, 0, 2))
    args = (xa, ga, sc, state_h) + tuple(weights)
    full = lambda s: pl.BlockSpec(s, lambda i: (0,) * len(s))
    return pl.pallas_call(
        _rglru_sample_kernel, grid=(1,),
        in_specs=[full(a.shape) for a in args],
        out_specs=[full((r, lw)), full((r, lw))],
        out_shape=[jax.ShapeDtypeStruct((r, lw), BF16), jax.ShapeDtypeStruct((r, lw), F32)],
        compiler_params=_cparams(), name="rglru_sample",
    )(*args)


def _compress_prompt_kernel(xk_ref, xv_ref, pek_ref, pev_ref, wk_ref, wvT_ref, kg_ref,
                            kc_ref, vcT_ref, *, hd):
    xk = (xk_ref[...] + pek_ref[...]).astype(BF16)
    kc = _dot(xk, wk_ref[...])
    kc_ref[...] = _group_rmsnorm(kc, kg_ref[...], hd).astype(BF16)
    xv = (xv_ref[...] + pev_ref[...]).astype(BF16)
    vcT_ref[...] = _dot_nt(wvT_ref[...], xv).astype(BF16)


def _compress_weight(w, groups, hd):
    w3 = w.reshape(CMP_BLOCK, hd, hd)
    eye = jnp.eye(groups, dtype=w.dtype)
    full = eye[None, :, None, :, None] * w3[:, None, :, None, :]
    return full.reshape(CMP_BLOCK * groups * hd, groups * hd)


def _compress_prompt(kcmp, vcmp, w_k, w_v, pe_k, pe_v, k_gain_cmp, groups, hd):
    t, kvw = kcmp.shape
    ncb = t // CMP_BLOCK
    kdim = CMP_BLOCK * kvw
    xk = kcmp.reshape(ncb, kdim)
    xv = vcmp.reshape(ncb, kdim)
    wk = _compress_weight(w_k, groups, hd).astype(BF16)
    wvT = _compress_weight(w_v, groups, hd).T.astype(BF16)
    pek = jnp.tile(pe_k, (1, groups)).reshape(1, kdim)
    pev = jnp.tile(pe_v, (1, groups)).reshape(1, kdim)
    kg = jnp.tile(k_gain_cmp, groups)[None, :]
    nb = min(LANES, ncb)
    return pl.pallas_call(
        functools.partial(_compress_prompt_kernel, hd=hd), grid=(ncb // nb,),
        in_specs=[pl.BlockSpec((nb, kdim), lambda i: (i, 0)), pl.BlockSpec((nb, kdim), lambda i: (i, 0)),
                  _const_spec((1, kdim)), _const_spec((1, kdim)), _const_spec(wk.shape),
                  _const_spec(wvT.shape), _const_spec((1, kvw))],
        out_specs=[pl.BlockSpec((nb, kvw), lambda i: (i, 0)), pl.BlockSpec((kvw, nb), lambda i: (0, i))],
        out_shape=[jax.ShapeDtypeStruct((ncb, kvw), BF16), jax.ShapeDtypeStruct((kvw, ncb), BF16)],
        compiler_params=_cparams(), name="compress_prompt",
    )(xk, xv, pek, pev, wk, wvT, kg)


def _tile_lanes(x, n):
    return jnp.concatenate([x] * n, axis=1)


def _attend(s, shift, vT, slot, g, m_ref, acc_ref, m_blk=None):
    m_old = m_ref[slot, g]
    if m_blk is None:
        m_blk = jnp.max(s, axis=0, keepdims=True)
    if shift is not None:
        m_blk = m_blk + shift
    m_new = jnp.maximum(m_old, m_blk)
    p = jnp.exp2(s - (m_new if shift is None else m_new - shift)).astype(BF16)
    acc_ref[slot, g] = jnp.exp2(m_old - m_new) * acc_ref[slot, g] + _dot(vT, p)
    m_ref[slot, g] = m_new


def _bias_table_kernel(id_ref, val_ref, o_ref, *, reps):
    ids = _tile_lanes(id_ref[...], reps)
    out = jnp.zeros(ids.shape, F32)
    for k in range(val_ref.shape[0]):
        out = jnp.where(ids == k, val_ref[k:k + 1, :], out)
    o_ref[...] = out


def _bias_table(ids, vals):
    r = ids.shape[0]
    reps = vals.shape[1] // LANES
    full = lambda s: pl.BlockSpec(s, lambda i: (0,) * len(s))
    return pl.pallas_call(
        functools.partial(_bias_table_kernel, reps=reps), grid=(1,),
        in_specs=[full(ids.shape), full(vals.shape)], out_specs=full((r, vals.shape[1])),
        out_shape=jax.ShapeDtypeStruct((r, vals.shape[1]), F32),
        compiler_params=_cparams(), name="bias_table",
    )(jnp.asarray(ids, jnp.int32), vals)


def _nsa_prompt_kernel(qT_ref, gT_ref, kc_ref, vcT_ref, ks_ref, vsT_ref, kw_ref, vwT_ref,
                       b31_ref, tab_ref,
                       y_ref,
                       qaug_ref, sa_ref, sb_ref, sc_ref, p4_ref, selb_ref, ocmp_ref, m_ref, acc_ref,
                       *, groups, hpg, hd):
    qb = pl.program_id(0)
    t0 = qb * Q_BLOCK
    ncb = kc_ref.shape[0]
    nsb = ncb * CMP_BLOCK // SEL_BLOCK
    gl = hpg * Q_BLOCK
    n_top = min(TOP_N, nsb)
    kvw = groups * hd
    bpg = KEY_GROUP // SEL_BLOCK

    rows = []
    for g in range(groups):
        top = jnp.concatenate([qT_ref[(g * hpg + hh) * hd:(g * hpg + hh + 1) * hd, :]
                               for hh in range(hpg)], axis=1)
        z = jnp.zeros_like(top)
        rows.append(jnp.concatenate([top if gg == g else z for gg in range(groups)], axis=1))
    qbd = jnp.concatenate(rows, axis=0)
    qaug_ref[0:kvw, :] = qbd
    qaug_ref[kvw:2 * kvw, :] = jnp.zeros((kvw, groups * gl), BF16)
    b31 = b31_ref[...]

    def compress_and_select(rc):
        nsv = rc * CMP_BLOCK // SEL_BLOCK
        jrow = lax.broadcasted_iota(jnp.int32, (rc, Q_BLOCK), 0)
        qcol = lax.broadcasted_iota(jnp.int32, (rc, Q_BLOCK), 1)
        vis = _tile_lanes(jrow * CMP_BLOCK + (CMP_BLOCK - 1) <= t0 + qcol, groups * hpg)
        s_c = _dot(kc_ref[0:rc, :], qbd) + b31
        sc_ref[0:rc, :] = jnp.where(vis, s_c, NEG)
        odd = (qb % 2) == 1
        v_a = jnp.where(qb == 0, 0, jnp.where(odd, 1, 2))
        w_a = pl.multiple_of(jnp.where(qb == 0, 0, jnp.where(odd, 4 * qb - 4, 4 * qb - 8)), 8)
        v_b = jnp.where((qb > 0) & jnp.logical_not(odd), 0, 3)
        w_b = pl.multiple_of(jnp.where((qb > 0) & jnp.logical_not(odd), 4 * qb, 0), 8)
        sc_ref[pl.ds(w_a, 8), :] = sc_ref[pl.ds(w_a, 8), :] + tab_ref[pl.ds(pl.multiple_of(v_a * 8, 8), 8), :]
        sc_ref[pl.ds(w_b, 8), :] = sc_ref[pl.ds(w_b, 8), :] + tab_ref[pl.ds(pl.multiple_of(v_b * 8, 8), 8), :]
        s_c = sc_ref[0:rc, :]
        m_c = jnp.max(s_c, axis=0, keepdims=True)
        p_c = jnp.where(vis, jnp.exp2(s_c - m_c), 0.0)
        l_c = jnp.sum(p_c, axis=0, keepdims=True)
        p_c = p_c * jnp.where(l_c > 0.0, 1.0 / l_c, 0.0)
        for g in range(groups):
            pg = p_c[:, g * gl:(g + 1) * gl]
            ocmp_ref[g] = _dot(vcT_ref[g * hd:(g + 1) * hd, 0:rc], pg.astype(BF16))
            p4 = pg[:, 0:Q_BLOCK]
            for hh in range(1, hpg):
                p4 = p4 + pg[:, hh * Q_BLOCK:(hh + 1) * Q_BLOCK]
            p4_ref[g, 0:rc, :] = p4

        ratio = SEL_BLOCK // CMP_BLOCK
        jb = lax.broadcasted_iota(jnp.int32, (nsv, Q_BLOCK), 0)
        qc = lax.broadcasted_iota(jnp.int32, (nsv, Q_BLOCK), 1)
        cur = (t0 + qc) // SEL_BLOCK
        forced = (jb == 0) | (jb == cur) | (jb == cur - 1)
        for g in range(groups):
            imp = p4_ref[g, pl.ds(0, nsv, stride=ratio), :]
            for k in range(1, ratio):
                imp = imp + p4_ref[g, pl.ds(k, nsv, stride=ratio), :]
            imp = jnp.where(forced, FORCE_SCORE, imp)
            imp = jnp.where(jb <= cur, imp, -1.0)
            selb = jnp.full((nsv, Q_BLOCK), NEG, F32)
            for _ in range(n_top):
                mx = jnp.max(imp, axis=0, keepdims=True)
                idx = jnp.min(jnp.where(imp == mx, jb, nsv), axis=0, keepdims=True)
                hit = jb == idx
                selb = jnp.where(hit, 0.0, selb)
                imp = jnp.where(hit, -3.0e38, imp)
            selb_ref[g, 0:nsv, :] = selb
            if nsv < nsb:
                selb_ref[g, nsv:nsb, :] = jnp.full((nsb - nsv, Q_BLOCK), NEG, F32)

    m_ref[...] = jnp.full(m_ref.shape, NEG, F32)
    acc_ref[...] = jnp.zeros(acc_ref.shape, F32)
    n_wk = WINDOW + Q_BLOCK

    def window_block():
        k0 = pl.multiple_of(t0 - WINDOW, Q_BLOCK)
        s = _dot(kw_ref[pl.ds(k0, n_wk), :], qaug_ref[0:kvw, :]) + tab_ref[_TAB_WIN:_TAB_WIN + n_wk, :]
        for g in range(groups):
            _attend(s[:, g * gl:(g + 1) * gl], None,
                    vwT_ref[g * ACC_ROWS:(g + 1) * ACC_ROWS, pl.ds(k0, n_wk)], 1, g, m_ref, acc_ref)

    def put_mask_rows(g, rows8):
        tile = jnp.concatenate([rows8, jnp.zeros_like(rows8)], axis=0).astype(BF16)
        for hh in range(hpg):
            qaug_ref[kvw:kvw + 2 * bpg, g * gl + hh * Q_BLOCK:g * gl + (hh + 1) * Q_BLOCK] = tile

    def far_scores(c, dst_ref):
        for g in range(groups):
            put_mask_rows(g, selb_ref[g, pl.ds(pl.multiple_of(c * bpg, bpg), bpg), :])
        k0 = pl.multiple_of(c * KEY_GROUP, KEY_GROUP)
        s = _dot(ks_ref[pl.ds(k0, KEY_GROUP), :], qaug_ref[...])
        dst_ref[0:KEY_GROUP, :] = s
        dst_ref[KEY_GROUP:KEY_GROUP + 8, :] = jnp.broadcast_to(jnp.max(s, axis=0, keepdims=True),
                                                               (8, groups * gl))

    tiers = list(range(LANES, ncb + 1, LANES)) if ncb % LANES == 0 else [ncb]
    n_vis = (t0 + Q_BLOCK) // CMP_BLOCK
    for i, rc in enumerate(tiers):
        lo = tiers[i - 1] if i else 0
        in_tier = (n_vis > lo) & (n_vis <= rc)

        @pl.when(in_tier & (t0 >= WINDOW))
        def _(rc=rc):
            compress_and_select(rc)
            window_block()
            far_scores(0, sa_ref)

        if lo * CMP_BLOCK < WINDOW + Q_BLOCK:
            @pl.when(in_tier & (t0 < WINDOW))
            def _(rc=rc):
                compress_and_select(rc)
                far_scores(0, sa_ref)


    def far_attend(c, src_ref, live_rows=None):
        k0 = pl.multiple_of(c * KEY_GROUP, KEY_GROUP)
        drop = None
        if live_rows is not None:
            drop = lax.broadcasted_iota(jnp.int32, (KEY_GROUP, Q_BLOCK), 0) >= live_rows
        for g in range(groups):
            sg = src_ref[0:KEY_GROUP, g * gl:(g + 1) * gl]
            m_blk = src_ref[KEY_GROUP:KEY_GROUP + 1, g * gl:(g + 1) * gl]
            if drop is not None:
                sg = jnp.where(_tile_lanes(drop, hpg), NEG, sg)
                m_blk = None
            _attend(sg, b31[:, g * gl:(g + 1) * gl],
                    vsT_ref[g * ACC_ROWS:(g + 1) * ACC_ROWS, pl.ds(k0, KEY_GROUP)], 0, g, m_ref, acc_ref,
                    m_blk=m_blk)

    far_len = jnp.maximum(t0 - Q_BLOCK, 0)
    n_far = far_len // KEY_GROUP
    live_tail = far_len - n_far * KEY_GROUP

    def far_pair(j, carry):
        far_scores(2 * j + 1, sb_ref)
        far_attend(2 * j, sa_ref)
        far_scores(2 * j + 2, sa_ref)
        far_attend(2 * j + 1, sb_ref)
        return carry

    lax.fori_loop(0, n_far // 2, far_pair, 0)

    near_tab = tab_ref[_TAB_NEAR:_TAB_NEAR + 2 * Q_BLOCK, :]

    def near_piece():
        b0 = (t0 - Q_BLOCK) // SEL_BLOCK
        base = (b0 // bpg) * bpg
        nxt = jnp.minimum(base + bpg, nsb - bpg)
        jrow8 = lax.broadcasted_iota(jnp.int32, (bpg, Q_BLOCK), 0)
        for g in range(groups):
            lo = selb_ref[g, pl.ds(pl.multiple_of(base, bpg), bpg), :]
            hi = selb_ref[g, pl.ds(pl.multiple_of(nxt, bpg), bpg), :]
            put_mask_rows(g, jnp.where(jrow8 >= b0 - base, lo, hi))
        k0 = pl.multiple_of(t0 - Q_BLOCK, Q_BLOCK)
        s = _dot(ks_ref[pl.ds(k0, 2 * Q_BLOCK), :], qaug_ref[...]) + near_tab
        for g in range(groups):
            _attend(s[:, g * gl:(g + 1) * gl], None,
                    vsT_ref[g * ACC_ROWS:(g + 1) * ACC_ROWS, pl.ds(k0, 2 * Q_BLOCK)], 0, g, m_ref, acc_ref)

    @pl.when(qb == 0)
    def _():
        for g in range(groups):
            put_mask_rows(g, selb_ref[g, 0:bpg, :])
        s = _dot(ks_ref[0:Q_BLOCK, :], qaug_ref[...]) + near_tab[Q_BLOCK:2 * Q_BLOCK, :]
        for g in range(groups):
            _attend(s[:, g * gl:(g + 1) * gl], None,
                    vsT_ref[g * ACC_ROWS:(g + 1) * ACC_ROWS, 0:Q_BLOCK], 0, g, m_ref, acc_ref)

    odd_far = n_far % 2 == 1
    has_tail = live_tail > 0

    @pl.when(odd_far & has_tail)
    def _():
        far_scores(n_far, sb_ref)
        far_attend(n_far - 1, sa_ref)
        far_attend(n_far, sb_ref, live_tail)
        near_piece()

    @pl.when(odd_far & jnp.logical_not(has_tail))
    def _():
        far_attend(n_far - 1, sa_ref)
        near_piece()

    @pl.when(jnp.logical_not(odd_far) & has_tail)
    def _():
        far_attend(n_far, sa_ref, live_tail)
        near_piece()

    @pl.when(jnp.logical_not(odd_far) & jnp.logical_not(has_tail) & (qb >= 1))
    def _():
        near_piece()

    for i in range(WINDOW // Q_BLOCK):
        k0 = t0 - i * Q_BLOCK

        @pl.when((t0 < WINDOW) & (k0 >= 0))
        def _(i=i, k0=k0):
            k0a = pl.multiple_of(k0, Q_BLOCK)
            r0 = _TAB_WIN + WINDOW - i * Q_BLOCK
            s = _dot(kw_ref[pl.ds(k0a, Q_BLOCK), :], qaug_ref[0:kvw, :]) + tab_ref[r0:r0 + Q_BLOCK, :]
            for g in range(groups):
                _attend(s[:, g * gl:(g + 1) * gl], None,
                        vwT_ref[g * ACC_ROWS:(g + 1) * ACC_ROWS, pl.ds(k0a, Q_BLOCK)], 1, g, m_ref, acc_ref)

    heads = groups * hpg
    pieces = []
    for g in range(groups):
        gate = [jnp.concatenate([gT_ref[j * heads + g * hpg + hh: j * heads + g * hpg + hh + 1, :]
                                 for hh in range(hpg)], axis=1) for j in range(3)]
        o = gate[0] * ocmp_ref[g]
        for slot in range(2):
            acc = acc_ref[slot, g]
            o = o + gate[1 + slot] * (acc[0:hd, :] * (1.0 / acc[hd:hd + 1, :]))
        for hh in range(hpg):
            pieces.append(o[:, hh * Q_BLOCK:(hh + 1) * Q_BLOCK])
    y_ref[...] = jnp.concatenate(pieces, axis=0).T.astype(BF16)


_TAB_NEAR = 32
_TAB_WIN = _TAB_NEAR + 2 * Q_BLOCK


def _nsa_prompt_tables(rel_bias, n_lanes):
    nb, heads = rel_bias.shape
    tbl = rel_bias.astype(F32) * LOG2E
    lanes = lambda x: jnp.repeat(x, n_lanes // heads, axis=1)
    vals = jnp.concatenate([lanes(tbl), jnp.full((1, n_lanes), NEG, F32), jnp.zeros((1, n_lanes), F32),
                            lanes(tbl - tbl[nb - 1:nb])], axis=0)
    c = np.arange(Q_BLOCK)[None, :]
    i8 = np.arange(8)[:, None]
    ids = []
    for base in (-(CMP_BLOCK - 1), 3 * CMP_BLOCK + 1, 7 * CMP_BLOCK + 1):
        d = base + c - CMP_BLOCK * i8
        ids.append(np.where(d >= 0, nb + 2 + _bucket_np(d, nb), nb + 1))
    ids.append(np.full((8, Q_BLOCK), nb + 1))
    d = Q_BLOCK + c - np.arange(2 * Q_BLOCK)[:, None]
    ids.append(np.where(d >= 0, _bucket_np(d, nb), nb))
    d = WINDOW + c - np.arange(WINDOW + Q_BLOCK)[:, None]
    ids.append(np.where((d >= 0) & (d < WINDOW), _bucket_np(d, nb), nb))
    return vals[nb - 1:nb], _bias_table(np.concatenate(ids, axis=0), vals)


def _nsa_prompt(qT, gT, kc, vcT, ks, vsT, kw, vwT, rel_bias, groups, hpg, hd):
    qw, t = qT.shape
    ncb = kc.shape[0]
    nsb = t // SEL_BLOCK
    gl = hpg * Q_BLOCK
    kvw = groups * hd
    assert t % KEY_GROUP == 0 and Q_BLOCK == 4 * CMP_BLOCK and MAX_DIST <= Q_BLOCK
    tables = _nsa_prompt_tables(rel_bias, groups * gl)
    resident = (kc, vcT, ks, vsT, kw, vwT) + tuple(tables)
    return pl.pallas_call(
        functools.partial(_nsa_prompt_kernel, groups=groups, hpg=hpg, hd=hd),
        grid=(t // Q_BLOCK,),
        in_specs=[pl.BlockSpec((qw, Q_BLOCK), lambda i: (0, i)),
                  pl.BlockSpec((gT.shape[0], Q_BLOCK), lambda i: (0, i))]
                 + [_const_spec(a.shape) for a in resident],
        out_specs=pl.BlockSpec((Q_BLOCK, qw), lambda i: (i, 0)),
        out_shape=jax.ShapeDtypeStruct((t, qw), BF16),
        scratch_shapes=[pltpu.VMEM((2 * kvw, groups * gl), BF16),
                        pltpu.VMEM((KEY_GROUP + 8, groups * gl), F32),
                        pltpu.VMEM((KEY_GROUP + 8, groups * gl), F32),
                        pltpu.VMEM((ncb, groups * gl), F32),
                        pltpu.VMEM((groups, ncb, Q_BLOCK), F32),
                        pltpu.VMEM((groups, nsb, Q_BLOCK), F32),
                        pltpu.VMEM((groups, hd, gl), F32),
                        pltpu.VMEM((2, groups, 1, gl), F32),
                        pltpu.VMEM((2, groups, ACC_ROWS, gl), F32)],
        compiler_params=_cparams(), name="nsa_prompt",
    )(qT, gT, *resident)


def _split3(x):
    hi = x.astype(BF16)
    r1 = x - hi.astype(F32)
    mid = r1.astype(BF16)
    lo = (r1 - mid.astype(F32)).astype(BF16)
    return hi, mid, lo


def _nsa_sample_kernel(pt_ref, cache_ref, qbd_ref, qn_ref, gates_ref, kvn_ref, winn_ref, cwin_ref,
                       wkv_ref, pe_ref, perm_ref, kg_ref, gsum_ref,
                       cb_ref, sb_ref, wb_ref, b0_ref, e_ref,
                       o_ref,
                       ringa_ref, ringb_ref, bufkv_ref, sema, semb, p_ref,
                       *, hd, n_pages, page, chunk):
    b = pl.program_id(0)
    nb = pl.num_programs(0)
    kvw = kg_ref.shape[1]
    past = n_pages * page
    ncb = past // CMP_BLOCK
    nsb = past // SEL_BLOCK
    n_chunks = n_pages // chunk
    bpp = 2 * page // CMP_BLOCK

    def chunk_copy(bb, q, p):
        if q < n_chunks:
            return pltpu.make_async_copy(cache_ref.at[pt_ref[bb, q * chunk + p], 0:2 * kvw, :],
                                         ringa_ref.at[q % 2, p], sema.at[q % 2])
        ch = q - n_chunks
        return pltpu.make_async_copy(cache_ref.at[pt_ref[bb, ch * chunk + p], 2 * kvw:4 * kvw, :],
                                     ringb_ref.at[ch % 2, p], semb.at[ch % 2])

    def for_pages(n, fn, unroll=1):
        def body(p, c):
            fn(p)
            return c
        lax.fori_loop(0, n, body, 0, unroll=unroll)

    def start_chunk(bb, q):
        for_pages(chunk, lambda p: chunk_copy(bb, q, p).start(), unroll=8)

    def wait_chunk(q):
        for_pages(chunk, lambda p: chunk_copy(b, q, p).wait(), unroll=chunk)

    def request_ahead(q):
        if q + 2 < 2 * n_chunks:
            start_chunk(b, q + 2)
        else:
            @pl.when(b + 1 < nb)
            def _():
                start_chunk(b + 1, q + 2 - 2 * n_chunks)

    @pl.when(b == 0)
    def _():
        start_chunk(0, 0)
        start_chunk(0, 1)

    qbd = qbd_ref[0]

    for ch in range(n_chunks):
        wait_chunk(ch)

        def regroup(q, ch=ch):
            xt = jnp.concatenate([ringa_ref[ch % 2, 2 * q], ringa_ref[ch % 2, 2 * q + 1]], axis=1)
            y = _dot_nt(perm_ref[...], (xt + pe_ref[...]).astype(BF16))
            blocks = pl.ds(pl.multiple_of((ch * (chunk // 2) + q) * bpp, bpp), bpp)
            for r in range(CMP_BLOCK):
                bufkv_ref[r, blocks, :] = y[r * bpp:(r + 1) * bpp, :]
        for_pages(chunk // 2, regroup, unroll=chunk // 2)
        request_ahead(ch)

    half = ncb // 2
    zs = [jnp.zeros((half, 2 * kvw), F32), jnp.zeros((ncb - half, 2 * kvw), F32)]
    for r in range(CMP_BLOCK):
        zs[0] = zs[0] + _dot(bufkv_ref[r, 0:half, :].astype(BF16), wkv_ref[r])
        zs[1] = zs[1] + _dot(bufkv_ref[r, half:ncb, :].astype(BF16), wkv_ref[r])
    z = jnp.concatenate(zs, axis=0)
    vc = z[:, kvw:2 * kvw]
    kc = _group_rmsnorm(z[:, 0:kvw], kg_ref[...], hd).astype(BF16)
    s_c = _dot(kc, qbd) + cb_ref[...]
    m_c = jnp.max(s_c, axis=0, keepdims=True)
    p_c = jnp.exp(s_c - m_c)
    p_c = p_c * (1.0 / jnp.sum(p_c, axis=0, keepdims=True))
    o_cmp = _dot(p_c.T.astype(BF16), vc.astype(BF16))
    p_ref[...] = p_c
    ratio = SEL_BLOCK // CMP_BLOCK
    imp = p_ref[pl.ds(0, nsb, stride=ratio), :]
    for k in range(1, ratio):
        imp = imp + p_ref[pl.ds(k, nsb, stride=ratio), :]
    hi, mid, lo = _split3(imp)
    gs = gsum_ref[...]
    imp = _dot(hi, gs) + _dot(mid, gs) + _dot(lo, gs)
    jb = lax.broadcasted_iota(jnp.int32, (nsb, LANES), 0)
    imp = jnp.where((jb == 0) | (jb == nsb - 1), FORCE_SCORE, imp)
    selb = jnp.full((nsb, LANES), NEG, F32)
    for _ in range(min(TOP_N, nsb + 1) - 1):
        mx = jnp.max(imp, axis=0, keepdims=True)
        idx = jnp.min(jnp.where(imp == mx, jb, nsb), axis=0, keepdims=True)
        hit = jb == idx
        selb = jnp.where(hit, 0.0, selb)
        imp = jnp.where(hit, -3.0e38, imp)
    rows = qn_ref.shape[1]
    sel_nat = selb.T[0:rows, :]

    qn = qn_ref[0]
    qnb = qn.astype(BF16)
    kvn = kvn_ref[0]
    b0 = b0_ref[:, 0:1]
    m_s = jnp.sum(qn * kvn[:, 2 * kvw:3 * kvw], axis=1, keepdims=True) + b0
    l_s = jnp.ones_like(m_s)
    acc = jnp.broadcast_to(kvn[:, 3 * kvw:4 * kvw], (rows, kvw))
    nks = e_ref.shape[1]
    bps = nks // SEL_BLOCK
    per_chunk = chunk * page // nks
    for ch in range(n_chunks):
        wait_chunk(n_chunks + ch)
        for sub in range(per_chunk):
            c = ch * per_chunk + sub
            pages = range(sub * nks // page, (sub + 1) * nks // page)
            k_t = jnp.concatenate([ringb_ref[ch % 2, p, 0:kvw, :] for p in pages], axis=1)
            v_t = jnp.concatenate([ringb_ref[ch % 2, p, kvw:2 * kvw, :] for p in pages], axis=1)
            rolled = sel_nat if c == 0 else pltpu.roll(sel_nat, nsb - c * bps, 1)
            mask = _dot(rolled[:, 0:e_ref.shape[0]].astype(BF16), e_ref[...])
            s = _dot(qnb, k_t.astype(BF16)) + mask + sb_ref[:, c * nks:(c + 1) * nks]
            m_new = jnp.maximum(m_s, jnp.max(s, axis=1, keepdims=True))
            alpha = jnp.exp(m_s - m_new)
            p = jnp.exp(s - m_new)
            l_s = alpha * l_s + jnp.sum(p, axis=1, keepdims=True)
            acc = alpha * acc + _dot_nt(p.astype(BF16), v_t.astype(BF16))
            m_s = m_new
        request_ahead(n_chunks + ch)
    o_sel = acc * (1.0 / l_s)

    winn = winn_ref[0]
    s_w = _dot(qnb, cwin_ref[0, 0:kvw, :].astype(BF16)) + wb_ref[...]
    s_wn = jnp.sum(qn * winn[:, 0:kvw], axis=1, keepdims=True) + b0
    m_w = jnp.maximum(jnp.max(s_w, axis=1, keepdims=True), s_wn)
    p_w = jnp.exp(s_w - m_w)
    p_wn = jnp.exp(s_wn - m_w)
    l_w = jnp.sum(p_w, axis=1, keepdims=True) + p_wn
    o_win = (_dot_nt(p_w.astype(BF16), cwin_ref[0, kvw:2 * kvw, :].astype(BF16))
             + p_wn * winn[:, kvw:2 * kvw]) * (1.0 / l_w)

    g = gates_ref[0]
    out = g[0] * o_cmp[0:rows, :] + g[1] * o_sel + g[2] * o_win
    o_ref[0] = out[0:8, :]


def _nsa_sample(q, gates, kv_new, win_new, cache_kv, cache_win, page_table, lp, rel_bias,
                groups, hpg, hd):
    r = q.shape[0]
    heads = groups * hpg
    kvw = groups * hd
    n_phys, page = cache_kv.shape[0], cache_kv.shape[1]
    n_pages = page_table.shape[1]
    past = n_pages * page
    wbuf = cache_win.shape[1]
    ncb, nsb = past // CMP_BLOCK, past // SEL_BLOCK
    rows = 16
    chunk = min(32, n_pages // 2)
    nks = min(NKS_SAMPLE, chunk * page)
    assert n_pages % (2 * chunk) == 0 and chunk % 8 == 0 and (chunk * page) % nks == 0 and heads <= rows
    q3 = q.reshape(r, heads, hd)
    gmask = (jnp.arange(groups)[:, None] == (jnp.arange(heads) // hpg)[None, :]).astype(F32)
    qbd = jnp.einsum('rhd,gh->rgdh', q3, gmask).reshape(r, kvw, heads)
    qbd = jnp.pad(qbd, ((0, 0), (0, 0), (0, LANES - heads))).astype(BF16)
    qn = jnp.einsum('rhd,gh->rhgd', q3, gmask).reshape(r, heads, kvw)
    qn = jnp.pad(qn, ((0, 0), (0, rows - heads), (0, 0)))
    gts = gates[:, :3 * heads].reshape(r, heads, 3).transpose(0, 2, 1)
    gts = jnp.broadcast_to(gts[:, :, :, None], (r, 3, heads, kvw))
    gts = jnp.pad(gts, ((0, 0), (0, 0), (0, rows - heads), (0, 0)))
    wk = _compress_weight(lp['cmp_w_k'], groups, hd).reshape(CMP_BLOCK, kvw, kvw)
    wv = _compress_weight(lp['cmp_w_v'], groups, hd).reshape(CMP_BLOCK, kvw, kvw)
    zero = jnp.zeros_like(wk)
    wkv = jnp.concatenate([jnp.concatenate([wk, zero], axis=2),
                           jnp.concatenate([zero, wv], axis=2)], axis=1).astype(BF16)
    pe_t = jnp.concatenate([jnp.tile(lp['pe_k'].T, (groups, 1)), jnp.tile(lp['pe_v'].T, (groups, 1))], axis=0)
    pe_t = jnp.tile(pe_t, (1, 2 * page // CMP_BLOCK))
    bpp = 2 * page // CMP_BLOCK
    dst = np.arange(2 * page)
    src = (dst % bpp) * CMP_BLOCK + dst // bpp
    perm = jnp.asarray(src[:, None] == np.arange(2 * page)[None, :], BF16)
    kg = jnp.tile(lp['k_gain'][0], groups)[None, :]
    lane_g = np.where(np.arange(LANES) < heads, np.arange(LANES) // hpg, -1)
    gsum = jnp.asarray((lane_g[:, None] == lane_g[None, :]) & (lane_g[:, None] >= 0), BF16)
    nbk = rel_bias.shape[0]
    tbl = rel_bias.astype(F32)
    d_c = (past - (np.arange(ncb) * CMP_BLOCK + CMP_BLOCK - 1))[:, None]
    vals_t = jnp.concatenate([tbl, jnp.broadcast_to(tbl[:, :1], (nbk, LANES - heads))], axis=1)
    cb = _bias_table(np.broadcast_to(_bucket_np(d_c, nbk), (ncb, LANES)), vals_t)
    vals_n = jnp.concatenate([jnp.repeat(tbl, LANES, axis=1), jnp.full((1, heads * LANES), NEG, F32)], axis=0)

    def head_rows(ids):
        out = _bias_table(ids, vals_n).reshape(ids.shape[0], heads, LANES)
        out = jnp.transpose(out, (1, 0, 2)).reshape(heads, ids.shape[0] * LANES)
        return jnp.pad(out, ((0, rows - heads), (0, 0)))

    sb = head_rows(_bucket_np(past - np.arange(past), nbk).reshape(past // LANES, LANES))
    d_w = wbuf - np.arange(wbuf)
    wb = head_rows(np.where(d_w < WINDOW, _bucket_np(d_w, nbk), nbk).reshape(wbuf // LANES, LANES))
    b0 = jnp.pad(jnp.broadcast_to(tbl[0][:, None], (heads, LANES)), ((0, rows - heads), (0, 0)))
    blk = np.arange(min(nsb, max(LANES, nks // SEL_BLOCK)))[:, None]
    spread = jnp.asarray((blk == np.arange(nks)[None, :] // SEL_BLOCK), BF16)

    cache_t = jnp.transpose(cache_kv, (0, 2, 3, 4, 1)).reshape(n_phys, 4 * kvw, page)
    cwin_t = jnp.transpose(cache_win, (0, 2, 3, 4, 1)).reshape(r, 2 * kvw, wbuf)
    kvn = kv_new.reshape(r, 1, 4 * kvw)
    winn = win_new.reshape(r, 1, 2 * kvw)
    per_b = lambda s: pl.BlockSpec((1,) + s, lambda b, pt: (b,) + (0,) * len(s))
    const = lambda a: pl.BlockSpec(a.shape, lambda b, pt: (0,) * a.ndim, pipeline_mode=pl.Buffered(1))
    consts = (wkv, pe_t, perm, kg, gsum, cb, sb, wb, b0, spread)
    grid_spec = pltpu.PrefetchScalarGridSpec(
        num_scalar_prefetch=1, grid=(r,),
        in_specs=[pl.BlockSpec(memory_space=pl.ANY), per_b((kvw, LANES)), per_b((rows, kvw)),
                  per_b((3, rows, kvw)), per_b((1, 4 * kvw)), per_b((1, 2 * kvw)),
                  per_b((2 * kvw, wbuf))]
                 + [const(a) for a in consts],
        out_specs=per_b((8, kvw)),
        scratch_shapes=[pltpu.VMEM((2, chunk, 2 * kvw, page), F32),
                        pltpu.VMEM((2, chunk, 2 * kvw, page), F32),
                        pltpu.VMEM((CMP_BLOCK, ncb, 2 * kvw), F32),
                        pltpu.SemaphoreType.DMA((2,)), pltpu.SemaphoreType.DMA((2,)),
                        pltpu.VMEM((ncb, LANES), F32)])
    o8 = pl.pallas_call(
        functools.partial(_nsa_sample_kernel, hd=hd, n_pages=n_pages, page=page, chunk=chunk),
        grid_spec=grid_spec,
        out_shape=jax.ShapeDtypeStruct((r, 8, kvw), F32),
        compiler_params=_cparams(), name="nsa_sample",
    )(page_table, cache_t, qbd, qn, gts, kvn, winn, cwin_t, *consts)
    o4 = o8[:, :heads, :].reshape(r, heads, groups, hd)
    y = jnp.take_along_axis(o4, (jnp.arange(heads) // hpg)[None, :, None, None], axis=2)
    return y.reshape(r, heads * hd).astype(BF16)


def _mlp_ple(h, p_ref, gm_ref, wu_ref, wd_ref, gp_ref, wg_ref, wp_ref):
    up = _dot(_rms_rows(h, gm_ref[...]).astype(BF16), wu_ref[...])
    act = jnp.square(jnp.maximum(up, 0.0)).astype(BF16)
    h = h + _dot(act, wd_ref[...])
    gate = jax.nn.sigmoid(_dot(_rms_rows(h, gp_ref[...]).astype(BF16), wg_ref[...]))
    return h + gate * _dot(p_ref[...].astype(BF16), wp_ref[...])


def _even_tail_kernel(h_ref, ya_ref, yb_ref, p_ref, woa_ref, wob_ref,
                      gm_ref, wu_ref, wd_ref, gp_ref, wg_ref, wp_ref, o_ref):
    h = h_ref[...] + _dot(ya_ref[...], woa_ref[...]) + _dot(yb_ref[...], wob_ref[...])
    o_ref[...] = _mlp_ple(h, p_ref, gm_ref, wu_ref, wd_ref, gp_ref, wg_ref, wp_ref)


def _tail_weights(g_mlp, w_up, w_down, g_ple, w_gate, w_proj):
    return (g_mlp[None, :], w_up.astype(BF16), w_down.astype(BF16), g_ple[None, :],
            w_gate.astype(BF16), w_proj.astype(BF16))


def _ple_spec(p_all, layer, tm):
    if p_all.shape[1] == 1:
        return pl.BlockSpec((None, None, tm, p_all.shape[3]), lambda i: (layer, 0, i, 0))
    return pl.BlockSpec((None, tm, None, p_all.shape[3]), lambda i: (layer, i, 0, 0))


def _even_tail(h, ya, yb, p_all, layer, w_out, tail_w, tm=256):
    t, d = h.shape
    tm = min(tm, t)
    lw = ya.shape[1]
    woa, wob = w_out[:lw].astype(BF16), w_out[lw:].astype(BF16)
    row = lambda w: pl.BlockSpec((tm, w), lambda i: (i, 0))
    weights = (woa, wob) + tuple(tail_w)
    return pl.pallas_call(
        _even_tail_kernel, grid=(t // tm,),
        in_specs=[row(d), row(lw), row(yb.shape[1]), _ple_spec(p_all, layer, tm)]
                 + [_const_spec(w.shape) for w in weights],
        out_specs=row(d), out_shape=jax.ShapeDtypeStruct((t, d), F32),
        compiler_params=_cparams(), name="even_tail",
    )(h, ya, yb, p_all, *weights)


def _odd_kernel(h_ref, p_ref, gx_ref, wi_ref, lg_ref, lb_ref, sw_ref, sb_ref, wo_ref,
                gm_ref, wu_ref, wd_ref, gp_ref, wg_ref, wp_ref, o_ref, v_ref, *, sw_groups, single):
    h = h_ref[...]
    tm = h.shape[0]
    z = _gelu(_dot(_rms_rows(h, gx_ref[...]).astype(BF16), wi_ref[...]))
    width = z.shape[1] // 2
    u, v = z[:, :width], z[:, width:]
    mu = jnp.mean(v, axis=-1, keepdims=True)
    vc = v - mu
    vn = vc * lax.rsqrt(jnp.mean(vc * vc, axis=-1, keepdims=True) + EPS) * lg_ref[...] + lb_ref[...]
    v_ref[...] = vn
    gw = width // sw_groups
    if single:
        s = vn * sw_ref[...] + sb_ref[...]
    else:
        vb = vn.astype(BF16)
        parts = []
        for c in range(tm // CHUNK):
            cols = [_dot(sw_ref[g], vb[c * CHUNK:(c + 1) * CHUNK, g * gw:(g + 1) * gw])
                    for g in range(sw_groups)]
            parts.append(jnp.concatenate(cols, axis=1) + sb_ref[...])
        s = jnp.concatenate(parts, axis=0)
    h = h + _dot((u * s).astype(BF16), wo_ref[...])
    o_ref[...] = _mlp_ple(h, p_ref, gm_ref, wu_ref, wd_ref, gp_ref, wg_ref, wp_ref)


def _odd_layer(h, p_all, layer, g_mix, w_in, ln_g, ln_b, sgu_w, sgu_b, w_out, tail_w, single, tm=256):
    t, d = h.shape
    tm = min(tm, t)
    ng, ch, _ = sgu_w.shape
    width = w_in.shape[1] // 2
    gw = width // ng
    if single:
        sw = jnp.repeat(sgu_w[:, 0, 0], gw)[None, :]
        sb = jnp.repeat(sgu_b[:, 0], gw)[None, :]
    else:
        sw = (sgu_w * jnp.tril(jnp.ones((ch, ch), sgu_w.dtype))).astype(BF16)
        sb = jnp.repeat(sgu_b.T, gw, axis=1)
    weights = (g_mix[None, :], w_in.astype(BF16), ln_g[None, :], ln_b[None, :], sw, sb,
               w_out.astype(BF16)) + tuple(tail_w)
    row = lambda w: pl.BlockSpec((tm, w), lambda i: (i, 0))
    return pl.pallas_call(
        functools.partial(_odd_kernel, sw_groups=ng, single=single), grid=(t // tm,),
        in_specs=[row(d), _ple_spec(p_all, layer, tm)] + [_const_spec(w.shape) for w in weights],
        out_specs=[row(d), row(width)],
        out_shape=[jax.ShapeDtypeStruct((t, d), F32), jax.ShapeDtypeStruct((t, width), F32)],
        compiler_params=_cparams(), name="odd_single" if single else "odd_prompt",
    )(h, p_all, *weights)


def kernel(x_prompt, x_sample, cache_kv, cache_win, state_conv, state_h, page_table, p_prompt, p_sample,
           rel_bias, g_mix, g_mlp, w_up, w_down, g_ple, w_ple_gate, w_ple_proj,
           w_in_even, w_out_even, conv_w, conv_b, rg_w_a, rg_b_a, rg_w_x, rg_b_x, rg_lambda,
           q_gain, k_gain, cmp_w_k, cmp_w_v, cmp_pe_k, cmp_pe_v,
           w_in_odd, ln_v_g, ln_v_b, sgu_w, sgu_b, w_out_odd):
    bsz, t, d = x_prompt.shape
    r = x_sample.shape[0]
    assert bsz == 1 and x_sample.shape[1] == 1
    depth = g_mix.shape[0]
    lw = conv_w.shape[-1]
    hd = q_gain.shape[-1]
    groups = cache_kv.shape[4]
    heads = rel_bias.shape[1]
    hpg = heads // groups
    qw, kvw = heads * hd, groups * hd
    dims = (lw, qw, kvw, hd, heads)
    wbuf = cache_win.shape[2]

    hp = x_prompt.reshape(t, d)
    hs = x_sample.reshape(r, d)
    kvp, kvs, wnp, wns, cvp, cvs, hhp, hhs, vvs = [], [], [], [], [], [], [], [], []
    for i in range(depth):
        tail_w = _tail_weights(g_mlp[i], w_up[i], w_down[i], g_ple[i], w_ple_gate[i], w_ple_proj[i])
        if i % 2 == 0:
            e = i // 2
            lp = {'cmp_w_k': cmp_w_k[e], 'cmp_w_v': cmp_w_v[e], 'pe_k': cmp_pe_k[e], 'pe_v': cmp_pe_v[e],
                  'k_gain': k_gain[e]}
            rg_w = _rglru_weights(conv_w[e], conv_b[e], rg_w_a[e], rg_b_a[e], rg_w_x[e], rg_b_x[e],
                                  rg_lambda[e])
            (xa, ga, kv_rows, win_rows, kcmp, vcmp, ks, kw, qT, vsT, vwT, gT) = _even_in_prompt(
                hp, g_mix[i], w_in_even[e], q_gain[e], k_gain[e], dims)
            ya, h_last = _rglru_prompt(xa, ga, rg_w)
            kc, vcT = _compress_prompt(kcmp, vcmp, cmp_w_k[e], cmp_w_v[e], cmp_pe_k[e], cmp_pe_v[e],
                                       k_gain[e, 0], groups, hd)
            yb = _nsa_prompt(qT, gT, kc, vcT, ks, vsT, kw, vwT, rel_bias, groups, hpg, hd)
            hp = _even_tail(hp, ya, yb, p_prompt, i, w_out_even[e], tail_w)
            kvp.append(kv_rows.reshape(1, t, 4, groups, hd))
            wnp.append(win_rows[t - wbuf:].reshape(1, wbuf, 2, groups, hd))
            cvp.append(xa[t - (conv_w.shape[1] - 1):][None])
            hhp.append(h_last)
            xa_s, ga_s, q_s, kv_s, win_s, gate_s = _even_in_sample(
                hs, g_mix[i], w_in_even[e], q_gain[e], k_gain[e], dims)
            ya_s, h_s = _rglru_sample(xa_s, ga_s, state_conv[e], state_h[e], rg_w)
            yb_s = _nsa_sample(q_s, gate_s, kv_s, win_s, cache_kv[e], cache_win[e], page_table, lp,
                               rel_bias, groups, hpg, hd)
            hs = _even_tail(hs, ya_s, yb_s, p_sample, i, w_out_even[e], tail_w)
            kvs.append(kv_s.reshape(r, 1, 4, groups, hd))
            wns.append(jnp.concatenate([cache_win[e][:, 1:], win_s.reshape(r, 1, 2, groups, hd)], axis=1))
            cvs.append(jnp.concatenate([state_conv[e][:, 1:], xa_s[:, None, :]], axis=1))
            hhs.append(h_s)
        else:
            o = i // 2
            hp, _ = _odd_layer(hp, p_prompt, i, g_mix[i], w_in_odd[o], ln_v_g[o], ln_v_b[o],
                               sgu_w[o], sgu_b[o], w_out_odd[o], tail_w, single=False)
            hs, v_new = _odd_layer(hs, p_sample, i, g_mix[i], w_in_odd[o], ln_v_g[o], ln_v_b[o],
                                   sgu_w[o], sgu_b[o], w_out_odd[o], tail_w, single=True)
            vvs.append(v_new[:, None, :])
    v_sample = jnp.stack(vvs) if vvs else jnp.zeros((0, r, 1, w_in_odd.shape[-1] // 2), F32)
    return (hp[None], hs[:, None, :], jnp.stack(kvp), jnp.stack(kvs), jnp.stack(wnp), jnp.stack(wns),
            jnp.stack(cvp), jnp.stack(cvs), jnp.stack(hhp), jnp.stack(hhs), v_sample)
```

```python
import functools
import math

import numpy as np
import jax
import jax.numpy as jnp
from jax import lax
from jax.experimental import pallas as pl
from jax.experimental.pallas import tpu as pltpu

F32 = jnp.float32
BF16 = jnp.bfloat16

EPS = 1e-6
NEG = -1e30
RG_C = 8.0
CMP_BLOCK = 32
SEL_BLOCK = 64
TOP_N = 16
WINDOW = 512
Q_BLOCK = 128
FORCE_SCORE = 1.0e4
MAX_DIST = 128
CHUNK = 128

LANES = 128
VMEM_LIMIT = 56 * 1024 * 1024

LOG2E = math.log2(math.e)
KEY_GROUP = 512
ACC_ROWS = 80
NKS_SAMPLE = 4096


def _cparams(n_axes=1):
    return pltpu.CompilerParams(dimension_semantics=("arbitrary",) * n_axes,
                                vmem_limit_bytes=VMEM_LIMIT)


def _const_spec(shape):
    nd = len(shape)
    return pl.BlockSpec(shape, lambda *_: (0,) * nd, pipeline_mode=pl.Buffered(1))


def _dot(a, b):
    return jnp.dot(a, b, preferred_element_type=F32)


def _dot_nt(a, b):
    return lax.dot_general(a, b, (((1,), (1,)), ((), ())), preferred_element_type=F32)


def _gelu(x):
    return 0.5 * x * (1.0 + jnp.tanh(math.sqrt(2.0 / math.pi) * (x + 0.044715 * (x * x * x))))


def _softplus(x):
    return jnp.maximum(x, 0.0) + jnp.log1p(jnp.exp(-jnp.abs(x)))


def _rms_rows(x, g):
    return x * lax.rsqrt(jnp.mean(x * x, axis=-1, keepdims=True) + EPS) * g


def _group_rmsnorm(x, gain, width):
    n = x.shape[-1] // width
    lane = lax.broadcasted_iota(jnp.int32, x.shape, 1)
    x2 = x * x
    ms = jnp.zeros_like(x)
    for i in range(n):
        inside = (lane >= i * width) & (lane < (i + 1) * width)
        s = jnp.sum(jnp.where(inside, x2, 0.0), axis=-1, keepdims=True)
        ms = jnp.where(inside, s, ms)
    return x * lax.rsqrt(ms * (1.0 / width) + EPS) * gain


def _bucket_np(dist, n_buckets):
    n = np.maximum(dist, 0)
    exact = n_buckets // 2
    nf = np.maximum(n, 1).astype(np.float32)
    large = exact + (np.log(nf / np.float32(exact)) / np.float32(math.log(MAX_DIST / exact))
                     * np.float32(n_buckets - exact)).astype(np.int32)
    return np.where(n < exact, n, np.minimum(large, n_buckets - 1)).astype(np.int32)


def _even_in_prompt_kernel(x_ref, g_ref, wn_ref, wt_ref, qg_ref, kg_ref,
                           cw_ref, cb_ref, wa_ref, ba_ref, wx_ref, bx_ref, lam_ref,
                           ya_ref, hl_ref, xt_ref, kv_ref, win_ref, kc_ref, vc_ref,
                           ks_ref, kw_ref, qT_ref, vsT_ref, vwT_ref, gT_ref,
                           xext_ref, h_ref, *, lw, qw, kvw, hd):
    tm = x_ref.shape[0]
    hn = _rms_rows(x_ref[...], g_ref[...]).astype(BF16)
    zn = _dot(hn, wn_ref[...])
    _rglru_tile(zn[:, :lw], zn[:, lw:2 * lw], cw_ref, cb_ref, wa_ref, ba_ref, wx_ref, bx_ref, lam_ref,
                ya_ref, hl_ref, xext_ref, h_ref)
    xt_ref[...] = zn[tm - 8:tm, :lw]
    kv = [zn[:, 2 * lw + i * kvw: 2 * lw + (i + 1) * kvw] for i in range(6)]
    k_sel = _group_rmsnorm(kv[2], kg_ref[1:2, :], hd)
    k_win = _group_rmsnorm(kv[4], kg_ref[2:3, :], hd)
    kv_ref[:, 0 * kvw:1 * kvw] = kv[0]
    kv_ref[:, 1 * kvw:2 * kvw] = kv[1]
    kv_ref[:, 2 * kvw:3 * kvw] = k_sel
    kv_ref[:, 3 * kvw:4 * kvw] = kv[3]
    win_ref[:, :kvw] = k_win
    win_ref[:, kvw:] = kv[5]
    kc_ref[...] = kv[0]
    vc_ref[...] = kv[1]
    blk = lax.broadcasted_iota(jnp.int32, (tm, kvw), 0) // SEL_BLOCK
    col = lax.broadcasted_iota(jnp.int32, (tm, kvw), 1)
    ks_ref[:, :kvw] = k_sel.astype(BF16)
    ks_ref[:, kvw:] = jnp.where(col == blk, 1.0, 0.0).astype(BF16)
    kw_ref[...] = k_win.astype(BF16)

    zt = _dot_nt(wt_ref[...], hn)
    for h in range(qw // hd):
        blk = zt[h * hd:(h + 1) * hd, :]
        r = lax.rsqrt(jnp.mean(blk * blk, axis=0, keepdims=True) + EPS)
        qn = blk * r
        for j in range(0, tm, LANES):
            qT_ref[h * hd:(h + 1) * hd, j:j + LANES] = (
                qn[:, j:j + LANES] * qg_ref[h * hd:(h + 1) * hd, :]).astype(BF16)
    pad = ACC_ROWS - hd
    ones_rows = jnp.where(lax.broadcasted_iota(jnp.int32, (pad, tm), 0) == 0, 1.0, 0.0).astype(BF16)
    for out_ref, base in ((vsT_ref, qw), (vwT_ref, qw + kvw)):
        for g in range(kvw // hd):
            out_ref[g * ACC_ROWS:g * ACC_ROWS + hd, :] = zt[base + g * hd:base + (g + 1) * hd, :].astype(BF16)
            out_ref[g * ACC_ROWS + hd:(g + 1) * ACC_ROWS, :] = ones_rows
    gT_ref[...] = jax.nn.sigmoid(zt[qw + 2 * kvw:, :])


def _even_in_prompt(x, g_mix, w_in, q_gain, k_gain, dims, rg_w, tm=512):
    t, d = x.shape
    lw, qw, kvw, hd, heads = dims
    tm = min(tm, t)
    c0, c1, c2 = 2 * lw, 2 * lw + qw, 2 * lw + qw + 6 * kvw
    wn = jnp.concatenate([w_in[:, :c0], w_in[:, c1:c2]], axis=1).astype(BF16)
    wg = w_in[:, c2:c2 + 3 * heads].reshape(d, heads, 3).transpose(0, 2, 1).reshape(d, 3 * heads)
    wt = jnp.concatenate([w_in[:, c0:c1], w_in[:, c1 + 3 * kvw:c1 + 4 * kvw],
                          w_in[:, c1 + 5 * kvw:c1 + 6 * kvw], wg,
                          jnp.zeros((d, 32 - 3 * heads), F32)], axis=1).T.astype(BF16)
    assert tm % KEY_GROUP == 0
    qg = jnp.broadcast_to((jnp.tile(q_gain, heads) * (hd ** -0.5 * LOG2E))[:, None], (qw, LANES)).astype(F32)
    vrows = (kvw // hd) * ACC_ROWS
    kg = jnp.tile(k_gain, (1, kvw // hd)).astype(F32)
    kg = jnp.concatenate([kg, jnp.zeros((5, kvw), F32)], axis=0)
    nt = wt.shape[0]
    row = lambda w: pl.BlockSpec((tm, w), lambda i: (i, 0))
    col = lambda r: pl.BlockSpec((r, tm), lambda i: (0, i))
    outs = pl.pallas_call(
        functools.partial(_even_in_prompt_kernel, lw=lw, qw=qw, kvw=kvw, hd=hd),
        grid=(t // tm,),
        in_specs=[row(d), _const_spec((1, d)), _const_spec(wn.shape), _const_spec(wt.shape),
                  _const_spec(qg.shape), _const_spec(kg.shape)] + [_const_spec(w.shape) for w in rg_w],
        out_specs=[row(lw), pl.BlockSpec((1, lw), lambda i: (0, 0)), pl.BlockSpec((8, lw), lambda i: (0, 0)),
                   row(4 * kvw), row(2 * kvw), row(kvw), row(kvw),
                   row(2 * kvw), row(kvw), col(qw), col(vrows), col(vrows), col(32)],
        out_shape=[jax.ShapeDtypeStruct((t, lw), BF16), jax.ShapeDtypeStruct((1, lw), F32),
                   jax.ShapeDtypeStruct((8, lw), F32),
                   jax.ShapeDtypeStruct((t, 4 * kvw), F32), jax.ShapeDtypeStruct((t, 2 * kvw), F32),
                   jax.ShapeDtypeStruct((t, kvw), F32), jax.ShapeDtypeStruct((t, kvw), F32),
                   jax.ShapeDtypeStruct((t, 2 * kvw), BF16), jax.ShapeDtypeStruct((t, kvw), BF16),
                   jax.ShapeDtypeStruct((qw, t), BF16), jax.ShapeDtypeStruct((vrows, t), BF16),
                   jax.ShapeDtypeStruct((vrows, t), BF16), jax.ShapeDtypeStruct((32, t), F32)],
        scratch_shapes=[pltpu.VMEM((tm + 8, lw), F32), pltpu.VMEM((8, lw), F32)],
        compiler_params=_cparams(), name="even_in_prompt",
    )(x, g_mix[None, :], wn, wt, qg, kg, *rg_w)
    return outs


def _even_in_sample_kernel(x_ref, g_ref, w_ref, qg_ref, kg_ref,
                           xa_ref, ga_ref, q_ref, kv_ref, win_ref, gate_ref, *, lw, qw, kvw, hd):
    hn = _rms_rows(x_ref[...], g_ref[...]).astype(BF16)
    z = _dot(hn, w_ref[...])
    xa_ref[...] = z[:, :lw]
    ga_ref[...] = z[:, lw:2 * lw]
    q_ref[...] = _group_rmsnorm(z[:, 2 * lw:2 * lw + qw], qg_ref[...], hd)
    o = 2 * lw + qw
    kv = [z[:, o + i * kvw: o + (i + 1) * kvw] for i in range(6)]
    kv_ref[:, 0 * kvw:1 * kvw] = kv[0]
    kv_ref[:, 1 * kvw:2 * kvw] = kv[1]
    kv_ref[:, 2 * kvw:3 * kvw] = _group_rmsnorm(kv[2], kg_ref[1:2, :], hd)
    kv_ref[:, 3 * kvw:4 * kvw] = kv[3]
    win_ref[:, :kvw] = _group_rmsnorm(kv[4], kg_ref[2:3, :], hd)
    win_ref[:, kvw:] = kv[5]
    gate_ref[...] = jax.nn.sigmoid(z[:, o + 6 * kvw:])


def _even_in_sample(x, g_mix, w_in, q_gain, k_gain, dims):
    r, d = x.shape
    lw, qw, kvw, hd, heads = dims
    n_in = w_in.shape[1]
    pad = (-n_in) % LANES
    w = jnp.pad(w_in, ((0, 0), (0, pad))).astype(BF16)
    gw = n_in + pad - (2 * lw + qw + 6 * kvw)
    qg = (jnp.tile(q_gain, heads) * hd ** -0.5)[None, :].astype(F32)
    kg = jnp.tile(k_gain, (1, kvw // hd)).astype(F32)
    kg = jnp.concatenate([kg, jnp.zeros((5, kvw), F32)], axis=0)
    full = lambda s: pl.BlockSpec(s, lambda i: (0,) * len(s))
    return pl.pallas_call(
        functools.partial(_even_in_sample_kernel, lw=lw, qw=qw, kvw=kvw, hd=hd),
        grid=(1,),
        in_specs=[full((r, d)), full((1, d)), full(w.shape), full(qg.shape), full(kg.shape)],
        out_specs=[full((r, lw)), full((r, lw)), full((r, qw)), full((r, 4 * kvw)),
                   full((r, 2 * kvw)), full((r, gw))],
        out_shape=[jax.ShapeDtypeStruct((r, lw), F32), jax.ShapeDtypeStruct((r, lw), F32),
                   jax.ShapeDtypeStruct((r, qw), F32), jax.ShapeDtypeStruct((r, 4 * kvw), F32),
                   jax.ShapeDtypeStruct((r, 2 * kvw), F32), jax.ShapeDtypeStruct((r, gw), F32)],
        compiler_params=_cparams(), name="even_in_sample",
    )(x, g_mix[None, :], w, qg, kg)


def _rglru_gates(conv, wa_ref, ba_ref, wx_ref, bx_ref, lam_ref):
    cb = conv.astype(BF16)
    r = jax.nn.sigmoid(_dot(cb, wa_ref[...]) + ba_ref[...])
    i = jax.nn.sigmoid(_dot(cb, wx_ref[...]) + bx_ref[...])
    log_a = -RG_C * r * _softplus(-lam_ref[...])
    a = jnp.exp(log_a)
    th = jnp.tanh(log_a)
    b = jnp.sqrt(-2.0 * th / (1.0 - th)) * (i * conv)
    return a, b


def _rglru_tile(x, ga, cw_ref, cb_ref, wa_ref, ba_ref, wx_ref, bx_ref, lam_ref,
                ya_ref, hl_ref, xext_ref, h_ref):
    tr = x.shape[0]

    @pl.when(pl.program_id(0) == 0)
    def _():
        xext_ref[0:8, :] = jnp.zeros((8, xext_ref.shape[1]), F32)
        h_ref[...] = jnp.zeros(h_ref.shape, F32)

    xext_ref[8:8 + tr, :] = x
    conv = (cb_ref[...] + cw_ref[3:4, :] * x + cw_ref[2:3, :] * xext_ref[7:7 + tr, :]
            + cw_ref[1:2, :] * xext_ref[6:6 + tr, :] + cw_ref[0:1, :] * xext_ref[5:5 + tr, :])
    xext_ref[0:8, :] = xext_ref[tr:tr + 8, :]
    a, b = _rglru_gates(conv, wa_ref, ba_ref, wx_ref, bx_ref, lam_ref)
    sub = 8
    lw = a.shape[1]
    a = a.reshape(tr // sub, sub, lw)
    b = b.reshape(tr // sub, sub, lw)
    pos = lax.broadcasted_iota(jnp.int32, a.shape, 1)
    s = 1
    while s < sub:
        keep = pos >= s
        a_sh = jnp.where(keep, pltpu.roll(a, s, 1), 1.0)
        b_sh = jnp.where(keep, pltpu.roll(b, s, 1), 0.0)
        b = a * b_sh + b
        a = a * a_sh
        s *= 2
    h_in = h_ref[0:1, :]
    tiles = []
    for i in range(tr // sub):
        h_tile = a[i] * h_in + b[i]
        tiles.append(h_tile)
        h_in = h_tile[sub - 1:sub, :]
    hs = jnp.concatenate(tiles, axis=0)
    h_last = h_in
    h_ref[...] = jnp.broadcast_to(h_last, h_ref.shape)
    hl_ref[...] = h_last
    ya_ref[...] = (hs * _gelu(ga)).astype(BF16)


def _block_diag(w):
    n, a, b = w.shape
    eye = jnp.eye(n, dtype=w.dtype)
    return (eye[:, None, :, None] * w[:, :, None, :]).reshape(n * a, n * b)


def _rglru_weights(conv_w, conv_b, w_a, b_a, w_x, b_x, lam):
    return (conv_w.astype(F32), conv_b[None, :], _block_diag(w_a).astype(BF16), b_a[None, :],
            _block_diag(w_x).astype(BF16), b_x[None, :], lam[None, :])


def _rglru_sample_kernel(xa_ref, ga_ref, sc_ref, h0_ref, cw_ref, cb_ref, wa_ref, ba_ref, wx_ref,
                         bx_ref, lam_ref, ya_ref, h_ref):
    conv = (cb_ref[...] + cw_ref[3:4, :] * xa_ref[...] + cw_ref[2:3, :] * sc_ref[2]
            + cw_ref[1:2, :] * sc_ref[1] + cw_ref[0:1, :] * sc_ref[0])
    a, b = _rglru_gates(conv, wa_ref, ba_ref, wx_ref, bx_ref, lam_ref)
    h = a * h0_ref[...] + b
    h_ref[...] = h
    ya_ref[...] = (h * _gelu(ga_ref[...])).astype(BF16)


def _rglru_sample(xa, ga, state_conv, state_h, weights):
    r, lw = xa.shape
    sc = jnp.transpose(state_conv, (1, 0, 2))
    args = (xa, ga, sc, state_h) + tuple(weights)
    full = lambda s: pl.BlockSpec(s, lambda i: (0,) * len(s))
    return pl.pallas_call(
        _rglru_sample_kernel, grid=(1,),
        in_specs=[full(a.shape) for a in args],
        out_specs=[full((r, lw)), full((r, lw))],
        out_shape=[jax.ShapeDtypeStruct((r, lw), BF16), jax.ShapeDtypeStruct((r, lw), F32)],
        compiler_params=_cparams(), name="rglru_sample",
    )(*args)


def _compress_prompt_kernel(xk_ref, xv_ref, pek_ref, pev_ref, wk_ref, wvT_ref, kg_ref,
                            kc_ref, vcT_ref, *, hd):
    xk = (xk_ref[...] + pek_ref[...]).astype(BF16)
    kc = _dot(xk, wk_ref[...])
    kc_ref[...] = _group_rmsnorm(kc, kg_ref[...], hd).astype(BF16)
    xv = (xv_ref[...] + pev_ref[...]).astype(BF16)
    vcT_ref[...] = _dot_nt(wvT_ref[...], xv).astype(BF16)


def _compress_weight(w, groups, hd):
    w3 = w.reshape(CMP_BLOCK, hd, hd)
    eye = jnp.eye(groups, dtype=w.dtype)
    full = eye[None, :, None, :, None] * w3[:, None, :, None, :]
    return full.reshape(CMP_BLOCK * groups * hd, groups * hd)


def _compress_prompt(kcmp, vcmp, w_k, w_v, pe_k, pe_v, k_gain_cmp, groups, hd):
    t, kvw = kcmp.shape
    ncb = t // CMP_BLOCK
    kdim = CMP_BLOCK * kvw
    xk = kcmp.reshape(ncb, kdim)
    xv = vcmp.reshape(ncb, kdim)
    wk = _compress_weight(w_k, groups, hd).astype(BF16)
    wvT = _compress_weight(w_v, groups, hd).T.astype(BF16)
    pek = jnp.tile(pe_k, (1, groups)).reshape(1, kdim)
    pev = jnp.tile(pe_v, (1, groups)).reshape(1, kdim)
    kg = jnp.tile(k_gain_cmp, groups)[None, :]
    nb = min(LANES, ncb)
    return pl.pallas_call(
        functools.partial(_compress_prompt_kernel, hd=hd), grid=(ncb // nb,),
        in_specs=[pl.BlockSpec((nb, kdim), lambda i: (i, 0)), pl.BlockSpec((nb, kdim), lambda i: (i, 0)),
                  _const_spec((1, kdim)), _const_spec((1, kdim)), _const_spec(wk.shape),
                  _const_spec(wvT.shape), _const_spec((1, kvw))],
        out_specs=[pl.BlockSpec((nb, kvw), lambda i: (i, 0)), pl.BlockSpec((kvw, nb), lambda i: (0, i))],
        out_shape=[jax.ShapeDtypeStruct((ncb, kvw), BF16), jax.ShapeDtypeStruct((kvw, ncb), BF16)],
        compiler_params=_cparams(), name="compress_prompt",
    )(xk, xv, pek, pev, wk, wvT, kg)


def _tile_lanes(x, n):
    return jnp.concatenate([x] * n, axis=1)


def _attend(s, shift, vT, slot, g, m_ref, acc_ref, m_blk=None):
    m_old = m_ref[slot, g]
    if m_blk is None:
        m_blk = jnp.max(s, axis=0, keepdims=True)
    if shift is not None:
        m_blk = m_blk + shift
    m_new = jnp.maximum(m_old, m_blk)
    p = jnp.exp2(s - (m_new if shift is None else m_new - shift)).astype(BF16)
    acc_ref[slot, g] = jnp.exp2(m_old - m_new) * acc_ref[slot, g] + _dot(vT, p)
    m_ref[slot, g] = m_new


def _bias_table_kernel(id_ref, val_ref, o_ref, *, reps):
    ids = _tile_lanes(id_ref[...], reps)
    out = jnp.zeros(ids.shape, F32)
    for k in range(val_ref.shape[0]):
        out = jnp.where(ids == k, val_ref[k:k + 1, :], out)
    o_ref[...] = out


def _bias_table(ids, vals):
    r = ids.shape[0]
    reps = vals.shape[1] // LANES
    full = lambda s: pl.BlockSpec(s, lambda i: (0,) * len(s))
    return pl.pallas_call(
        functools.partial(_bias_table_kernel, reps=reps), grid=(1,),
        in_specs=[full(ids.shape), full(vals.shape)], out_specs=full((r, vals.shape[1])),
        out_shape=jax.ShapeDtypeStruct((r, vals.shape[1]), F32),
        compiler_params=_cparams(), name="bias_table",
    )(jnp.asarray(ids, jnp.int32), vals)


def _nsa_prompt_kernel(qT_ref, gT_ref, kc_ref, vcT_ref, ks_ref, vsT_ref, kw_ref, vwT_ref,
                       b31_ref, tab_ref,
                       y_ref,
                       qaug_ref, sa_ref, sb_ref, sc_ref, p4_ref, selb_ref, ocmp_ref, m_ref, acc_ref,
                       *, groups, hpg, hd):
    qb = pl.program_id(0)
    t0 = qb * Q_BLOCK
    ncb = kc_ref.shape[0]
    nsb = ncb * CMP_BLOCK // SEL_BLOCK
    gl = hpg * Q_BLOCK
    n_top = min(TOP_N, nsb)
    kvw = groups * hd
    bpg = KEY_GROUP // SEL_BLOCK

    rows = []
    for g in range(groups):
        top = jnp.concatenate([qT_ref[(g * hpg + hh) * hd:(g * hpg + hh + 1) * hd, :]
                               for hh in range(hpg)], axis=1)
        z = jnp.zeros_like(top)
        rows.append(jnp.concatenate([top if gg == g else z for gg in range(groups)], axis=1))
    qbd = jnp.concatenate(rows, axis=0)
    qaug_ref[0:kvw, :] = qbd
    qaug_ref[kvw:2 * kvw, :] = jnp.zeros((kvw, groups * gl), BF16)
    b31 = b31_ref[...]

    def compress_and_select(rc):
        nsv = rc * CMP_BLOCK // SEL_BLOCK
        jrow = lax.broadcasted_iota(jnp.int32, (rc, Q_BLOCK), 0)
        qcol = lax.broadcasted_iota(jnp.int32, (rc, Q_BLOCK), 1)
        vis = _tile_lanes(jrow * CMP_BLOCK + (CMP_BLOCK - 1) <= t0 + qcol, groups * hpg)
        s_c = _dot(kc_ref[0:rc, :], qbd) + b31
        sc_ref[0:rc, :] = jnp.where(vis, s_c, NEG)
        odd = (qb % 2) == 1
        v_a = jnp.where(qb == 0, 0, jnp.where(odd, 1, 2))
        w_a = pl.multiple_of(jnp.where(qb == 0, 0, jnp.where(odd, 4 * qb - 4, 4 * qb - 8)), 8)
        v_b = jnp.where((qb > 0) & jnp.logical_not(odd), 0, 3)
        w_b = pl.multiple_of(jnp.where((qb > 0) & jnp.logical_not(odd), 4 * qb, 0), 8)
        sc_ref[pl.ds(w_a, 8), :] = sc_ref[pl.ds(w_a, 8), :] + tab_ref[pl.ds(pl.multiple_of(v_a * 8, 8), 8), :]
        sc_ref[pl.ds(w_b, 8), :] = sc_ref[pl.ds(w_b, 8), :] + tab_ref[pl.ds(pl.multiple_of(v_b * 8, 8), 8), :]
        s_c = sc_ref[0:rc, :]
        m_c = jnp.max(s_c, axis=0, keepdims=True)
        p_c = jnp.where(vis, jnp.exp2(s_c - m_c), 0.0)
        l_c = jnp.sum(p_c, axis=0, keepdims=True)
        p_c = p_c * jnp.where(l_c > 0.0, 1.0 / l_c, 0.0)
        for g in range(groups):
            pg = p_c[:, g * gl:(g + 1) * gl]
            ocmp_ref[g] = _dot(vcT_ref[g * hd:(g + 1) * hd, 0:rc], pg.astype(BF16))
            p4 = pg[:, 0:Q_BLOCK]
            for hh in range(1, hpg):
                p4 = p4 + pg[:, hh * Q_BLOCK:(hh + 1) * Q_BLOCK]
            p4_ref[g, 0:rc, :] = p4

        ratio = SEL_BLOCK // CMP_BLOCK
        jb = lax.broadcasted_iota(jnp.int32, (nsv, Q_BLOCK), 0)
        qc = lax.broadcasted_iota(jnp.int32, (nsv, Q_BLOCK), 1)
        cur = (t0 + qc) // SEL_BLOCK
        forced = (jb == 0) | (jb == cur) | (jb == cur - 1)
        for g in range(groups):
            imp = p4_ref[g, pl.ds(0, nsv, stride=ratio), :]
            for k in range(1, ratio):
                imp = imp + p4_ref[g, pl.ds(k, nsv, stride=ratio), :]
            imp = jnp.where(forced, FORCE_SCORE, imp)
            imp = jnp.where(jb <= cur, imp, -1.0)
            selb = jnp.full((nsv, Q_BLOCK), NEG, F32)
            for _ in range(n_top):
                mx = jnp.max(imp, axis=0, keepdims=True)
                idx = jnp.min(jnp.where(imp == mx, jb, nsv), axis=0, keepdims=True)
                hit = jb == idx
                selb = jnp.where(hit, 0.0, selb)
                imp = jnp.where(hit, -3.0e38, imp)
            selb_ref[g, 0:nsv, :] = selb
            if nsv < nsb:
                selb_ref[g, nsv:nsb, :] = jnp.full((nsb - nsv, Q_BLOCK), NEG, F32)

    m_ref[...] = jnp.full(m_ref.shape, NEG, F32)
    acc_ref[...] = jnp.zeros(acc_ref.shape, F32)
    n_wk = WINDOW + Q_BLOCK

    def window_block():
        k0 = pl.multiple_of(t0 - WINDOW, Q_BLOCK)
        s = _dot(kw_ref[pl.ds(k0, n_wk), :], qaug_ref[0:kvw, :]) + tab_ref[_TAB_WIN:_TAB_WIN + n_wk, :]
        for g in range(groups):
            _attend(s[:, g * gl:(g + 1) * gl], None,
                    vwT_ref[g * ACC_ROWS:(g + 1) * ACC_ROWS, pl.ds(k0, n_wk)], 1, g, m_ref, acc_ref)

    def put_mask_rows(g, rows8):
        tile = jnp.concatenate([rows8, jnp.zeros_like(rows8)], axis=0).astype(BF16)
        for hh in range(hpg):
            qaug_ref[kvw:kvw + 2 * bpg, g * gl + hh * Q_BLOCK:g * gl + (hh + 1) * Q_BLOCK] = tile

    def far_scores(c, dst_ref):
        for g in range(groups):
            put_mask_rows(g, selb_ref[g, pl.ds(pl.multiple_of(c * bpg, bpg), bpg), :])
        k0 = pl.multiple_of(c * KEY_GROUP, KEY_GROUP)
        s = _dot(ks_ref[pl.ds(k0, KEY_GROUP), :], qaug_ref[...])
        dst_ref[0:KEY_GROUP, :] = s
        dst_ref[KEY_GROUP:KEY_GROUP + 8, :] = jnp.broadcast_to(jnp.max(s, axis=0, keepdims=True),
                                                               (8, groups * gl))

    tiers = list(range(LANES, ncb + 1, LANES)) if ncb % LANES == 0 else [ncb]
    n_vis = (t0 + Q_BLOCK) // CMP_BLOCK
    for i, rc in enumerate(tiers):
        lo = tiers[i - 1] if i else 0
        in_tier = (n_vis > lo) & (n_vis <= rc)

        @pl.when(in_tier & (t0 >= WINDOW))
        def _(rc=rc):
            compress_and_select(rc)
            window_block()
            far_scores(0, sa_ref)

        if lo * CMP_BLOCK < WINDOW + Q_BLOCK:
            @pl.when(in_tier & (t0 < WINDOW))
            def _(rc=rc):
                compress_and_select(rc)
                far_scores(0, sa_ref)


    def far_attend(c, src_ref, live_rows=None):
        k0 = pl.multiple_of(c * KEY_GROUP, KEY_GROUP)
        drop = None
        if live_rows is not None:
            drop = lax.broadcasted_iota(jnp.int32, (KEY_GROUP, Q_BLOCK), 0) >= live_rows
        for g in range(groups):
            sg = src_ref[0:KEY_GROUP, g * gl:(g + 1) * gl]
            m_blk = src_ref[KEY_GROUP:KEY_GROUP + 1, g * gl:(g + 1) * gl]
            if drop is not None:
                sg = jnp.where(_tile_lanes(drop, hpg), NEG, sg)
                m_blk = None
            _attend(sg, b31[:, g * gl:(g + 1) * gl],
                    vsT_ref[g * ACC_ROWS:(g + 1) * ACC_ROWS, pl.ds(k0, KEY_GROUP)], 0, g, m_ref, acc_ref,
                    m_blk=m_blk)

    far_len = jnp.maximum(t0 - Q_BLOCK, 0)
    n_far = far_len // KEY_GROUP
    live_tail = far_len - n_far * KEY_GROUP

    def far_pair(j, carry):
        far_scores(2 * j + 1, sb_ref)
        far_attend(2 * j, sa_ref)
        far_scores(2 * j + 2, sa_ref)
        far_attend(2 * j + 1, sb_ref)
        return carry

    lax.fori_loop(0, n_far // 2, far_pair, 0)

    near_tab = tab_ref[_TAB_NEAR:_TAB_NEAR + 2 * Q_BLOCK, :]

    def near_piece():
        b0 = (t0 - Q_BLOCK) // SEL_BLOCK
        base = (b0 // bpg) * bpg
        nxt = jnp.minimum(base + bpg, nsb - bpg)
        jrow8 = lax.broadcasted_iota(jnp.int32, (bpg, Q_BLOCK), 0)
        for g in range(groups):
            lo = selb_ref[g, pl.ds(pl.multiple_of(base, bpg), bpg), :]
            hi = selb_ref[g, pl.ds(pl.multiple_of(nxt, bpg), bpg), :]
            put_mask_rows(g, jnp.where(jrow8 >= b0 - base, lo, hi))
        k0 = pl.multiple_of(t0 - Q_BLOCK, Q_BLOCK)
        s = _dot(ks_ref[pl.ds(k0, 2 * Q_BLOCK), :], qaug_ref[...]) + near_tab
        for g in range(groups):
            _attend(s[:, g * gl:(g + 1) * gl], None,
                    vsT_ref[g * ACC_ROWS:(g + 1) * ACC_ROWS, pl.ds(k0, 2 * Q_BLOCK)], 0, g, m_ref, acc_ref)

    @pl.when(qb == 0)
    def _():
        for g in range(groups):
            put_mask_rows(g, selb_ref[g, 0:bpg, :])
        s = _dot(ks_ref[0:Q_BLOCK, :], qaug_ref[...]) + near_tab[Q_BLOCK:2 * Q_BLOCK, :]
        for g in range(groups):
            _attend(s[:, g * gl:(g + 1) * gl], None,
                    vsT_ref[g * ACC_ROWS:(g + 1) * ACC_ROWS, 0:Q_BLOCK], 0, g, m_ref, acc_ref)

    odd_far = n_far % 2 == 1
    has_tail = live_tail > 0

    @pl.when(odd_far & has_tail)
    def _():
        far_scores(n_far, sb_ref)
        far_attend(n_far - 1, sa_ref)
        far_attend(n_far, sb_ref, live_tail)
        near_piece()

    @pl.when(odd_far & jnp.logical_not(has_tail))
    def _():
        far_attend(n_far - 1, sa_ref)
        near_piece()

    @pl.when(jnp.logical_not(odd_far) & has_tail)
    def _():
        far_attend(n_far, sa_ref, live_tail)
        near_piece()

    @pl.when(jnp.logical_not(odd_far) & jnp.logical_not(has_tail) & (qb >= 1))
    def _():
        near_piece()

    for i in range(WINDOW // Q_BLOCK):
        k0 = t0 - i * Q_BLOCK

        @pl.when((t0 < WINDOW) & (k0 >= 0))
        def _(i=i, k0=k0):
            k0a = pl.multiple_of(k0, Q_BLOCK)
            r0 = _TAB_WIN + WINDOW - i * Q_BLOCK
            s = _dot(kw_ref[pl.ds(k0a, Q_BLOCK), :], qaug_ref[0:kvw, :]) + tab_ref[r0:r0 + Q_BLOCK, :]
            for g in range(groups):
                _attend(s[:, g * gl:(g + 1) * gl], None,
                        vwT_ref[g * ACC_ROWS:(g + 1) * ACC_ROWS, pl.ds(k0a, Q_BLOCK)], 1, g, m_ref, acc_ref)

    heads = groups * hpg
    pieces = []
    for g in range(groups):
        gate = [jnp.concatenate([gT_ref[j * heads + g * hpg + hh: j * heads + g * hpg + hh + 1, :]
                                 for hh in range(hpg)], axis=1) for j in range(3)]
        o = gate[0] * ocmp_ref[g]
        for slot in range(2):
            acc = acc_ref[slot, g]
            o = o + gate[1 + slot] * (acc[0:hd, :] * (1.0 / acc[hd:hd + 1, :]))
        for hh in range(hpg):
            pieces.append(o[:, hh * Q_BLOCK:(hh + 1) * Q_BLOCK])
    y_ref[...] = jnp.concatenate(pieces, axis=0).T.astype(BF16)


_TAB_NEAR = 32
_TAB_WIN = _TAB_NEAR + 2 * Q_BLOCK


def _nsa_prompt_tables(rel_bias, n_lanes):
    nb, heads = rel_bias.shape
    tbl = rel_bias.astype(F32) * LOG2E
    lanes = lambda x: jnp.repeat(x, n_lanes // heads, axis=1)
    vals = jnp.concatenate([lanes(tbl), jnp.full((1, n_lanes), NEG, F32), jnp.zeros((1, n_lanes), F32),
                            lanes(tbl - tbl[nb - 1:nb])], axis=0)
    c = np.arange(Q_BLOCK)[None, :]
    i8 = np.arange(8)[:, None]
    ids = []
    for base in (-(CMP_BLOCK - 1), 3 * CMP_BLOCK + 1, 7 * CMP_BLOCK + 1):
        d = base + c - CMP_BLOCK * i8
        ids.append(np.where(d >= 0, nb + 2 + _bucket_np(d, nb), nb + 1))
    ids.append(np.full((8, Q_BLOCK), nb + 1))
    d = Q_BLOCK + c - np.arange(2 * Q_BLOCK)[:, None]
    ids.append(np.where(d >= 0, _bucket_np(d, nb), nb))
    d = WINDOW + c - np.arange(WINDOW + Q_BLOCK)[:, None]
    ids.append(np.where((d >= 0) & (d < WINDOW), _bucket_np(d, nb), nb))
    return vals[nb - 1:nb], _bias_table(np.concatenate(ids, axis=0), vals)


def _nsa_prompt(qT, gT, kc, vcT, ks, vsT, kw, vwT, rel_bias, groups, hpg, hd):
    qw, t = qT.shape
    ncb = kc.shape[0]
    nsb = t // SEL_BLOCK
    gl = hpg * Q_BLOCK
    kvw = groups * hd
    assert t % KEY_GROUP == 0 and Q_BLOCK == 4 * CMP_BLOCK and MAX_DIST <= Q_BLOCK
    tables = _nsa_prompt_tables(rel_bias, groups * gl)
    resident = (kc, vcT, ks, vsT, kw, vwT) + tuple(tables)
    return pl.pallas_call(
        functools.partial(_nsa_prompt_kernel, groups=groups, hpg=hpg, hd=hd),
        grid=(t // Q_BLOCK,),
        in_specs=[pl.BlockSpec((qw, Q_BLOCK), lambda i: (0, i)),
                  pl.BlockSpec((gT.shape[0], Q_BLOCK), lambda i: (0, i))]
                 + [_const_spec(a.shape) for a in resident],
        out_specs=pl.BlockSpec((Q_BLOCK, qw), lambda i: (i, 0)),
        out_shape=jax.ShapeDtypeStruct((t, qw), BF16),
        scratch_shapes=[pltpu.VMEM((2 * kvw, groups * gl), BF16),
                        pltpu.VMEM((KEY_GROUP + 8, groups * gl), F32),
                        pltpu.VMEM((KEY_GROUP + 8, groups * gl), F32),
                        pltpu.VMEM((ncb, groups * gl), F32),
                        pltpu.VMEM((groups, ncb, Q_BLOCK), F32),
                        pltpu.VMEM((groups, nsb, Q_BLOCK), F32),
                        pltpu.VMEM((groups, hd, gl), F32),
                        pltpu.VMEM((2, groups, 1, gl), F32),
                        pltpu.VMEM((2, groups, ACC_ROWS, gl), F32)],
        compiler_params=_cparams(), name="nsa_prompt",
    )(qT, gT, *resident)


def _split3(x):
    hi = x.astype(BF16)
    r1 = x - hi.astype(F32)
    mid = r1.astype(BF16)
    lo = (r1 - mid.astype(F32)).astype(BF16)
    return hi, mid, lo


def _nsa_sample_kernel(pt_ref, cache_ref, qbd_ref, qn_ref, gates_ref, kvn_ref, winn_ref, cwin_ref,
                       wkv_ref, pe_ref, perm_ref, kg_ref, gsum_ref,
                       cb_ref, sb_ref, wb_ref, b0_ref, e_ref,
                       o_ref,
                       ringa_ref, ringb_ref, bufkv_ref, sema, semb, p_ref,
                       *, hd, n_pages, page, chunk):
    b = pl.program_id(0)
    nb = pl.num_programs(0)
    kvw = kg_ref.shape[1]
    past = n_pages * page
    ncb = past // CMP_BLOCK
    nsb = past // SEL_BLOCK
    n_chunks = n_pages // chunk
    bpp = 2 * page // CMP_BLOCK

    def chunk_copy(bb, q, p):
        if q < n_chunks:
            return pltpu.make_async_copy(cache_ref.at[pt_ref[bb, q * chunk + p], 0:2 * kvw, :],
                                         ringa_ref.at[q % 2, p], sema.at[q % 2])
        ch = q - n_chunks
        return pltpu.make_async_copy(cache_ref.at[pt_ref[bb, ch * chunk + p], 2 * kvw:4 * kvw, :],
                                     ringb_ref.at[ch % 2, :, pl.ds(p * page, page)], semb.at[ch % 2])

    def for_pages(n, fn, unroll=1):
        def body(p, c):
            fn(p)
            return c
        lax.fori_loop(0, n, body, 0, unroll=unroll)

    def start_chunk(bb, q):
        for_pages(chunk, lambda p: chunk_copy(bb, q, p).start(), unroll=8)

    def wait_chunk(q):
        for_pages(chunk, lambda p: chunk_copy(b, q, p).wait(), unroll=chunk)

    def request_ahead(q):
        if q + 2 < 2 * n_chunks:
            start_chunk(b, q + 2)
        else:
            @pl.when(b + 1 < nb)
            def _():
                start_chunk(b + 1, q + 2 - 2 * n_chunks)

    @pl.when(b == 0)
    def _():
        start_chunk(0, 0)
        start_chunk(0, 1)

    qbd = qbd_ref[0]

    for ch in range(n_chunks):
        wait_chunk(ch)

        def regroup(q, ch=ch):
            xt = jnp.concatenate([ringa_ref[ch % 2, 2 * q], ringa_ref[ch % 2, 2 * q + 1]], axis=1)
            y = _dot_nt(perm_ref[...], (xt + pe_ref[...]).astype(BF16))
            blocks = pl.ds(pl.multiple_of((ch * (chunk // 2) + q) * bpp, bpp), bpp)
            for r in range(CMP_BLOCK):
                bufkv_ref[r, blocks, :] = y[r * bpp:(r + 1) * bpp, :]
        for_pages(chunk // 2, regroup, unroll=chunk // 2)
        request_ahead(ch)

    half = ncb // 2
    zs = [jnp.zeros((half, 2 * kvw), F32), jnp.zeros((ncb - half, 2 * kvw), F32)]
    for r in range(CMP_BLOCK):
        zs[0] = zs[0] + _dot(bufkv_ref[r, 0:half, :].astype(BF16), wkv_ref[r])
        zs[1] = zs[1] + _dot(bufkv_ref[r, half:ncb, :].astype(BF16), wkv_ref[r])
    z = jnp.concatenate(zs, axis=0)
    vc = z[:, kvw:2 * kvw]
    kc = _group_rmsnorm(z[:, 0:kvw], kg_ref[...], hd).astype(BF16)
    s_c = _dot(kc, qbd) + cb_ref[...]
    m_c = jnp.max(s_c, axis=0, keepdims=True)
    p_c = jnp.exp(s_c - m_c)
    p_c = p_c * (1.0 / jnp.sum(p_c, axis=0, keepdims=True))
    o_cmp = _dot(p_c.T.astype(BF16), vc.astype(BF16))
    p_ref[...] = p_c
    ratio = SEL_BLOCK // CMP_BLOCK
    imp = p_ref[pl.ds(0, nsb, stride=ratio), :]
    for k in range(1, ratio):
        imp = imp + p_ref[pl.ds(k, nsb, stride=ratio), :]
    hi, mid, lo = _split3(imp)
    gs = gsum_ref[...]
    imp = _dot(hi, gs) + _dot(mid, gs) + _dot(lo, gs)
    jb = lax.broadcasted_iota(jnp.int32, (nsb, LANES), 0)
    imp = jnp.where((jb == 0) | (jb == nsb - 1), FORCE_SCORE, imp)
    selb = jnp.full((nsb, LANES), NEG, F32)
    for _ in range(min(TOP_N, nsb + 1) - 1):
        mx = jnp.max(imp, axis=0, keepdims=True)
        idx = jnp.min(jnp.where(imp == mx, jb, nsb), axis=0, keepdims=True)
        hit = jb == idx
        selb = jnp.where(hit, 0.0, selb)
        imp = jnp.where(hit, -3.0e38, imp)
    rows = qn_ref.shape[1]
    sel_nat = selb.T[0:rows, :]

    qn = qn_ref[0]
    qnb = qn.astype(BF16)
    kvn = kvn_ref[0]
    b0 = b0_ref[:, 0:1]
    m_s = jnp.sum(qn * kvn[:, 2 * kvw:3 * kvw], axis=1, keepdims=True) + b0
    l_s = jnp.ones_like(m_s)
    acc = jnp.broadcast_to(kvn[:, 3 * kvw:4 * kvw], (rows, kvw))
    nks = e_ref.shape[1]
    bps = nks // SEL_BLOCK
    per_chunk = chunk * page // nks
    for ch in range(n_chunks):
        wait_chunk(n_chunks + ch)
        for sub in range(per_chunk):
            c = ch * per_chunk + sub
            keys = slice(sub * nks, (sub + 1) * nks)
            rolled = sel_nat if c == 0 else pltpu.roll(sel_nat, nsb - c * bps, 1)
            mask = _dot(rolled.astype(BF16), e_ref[...])
            s = (_dot(qnb, ringb_ref[ch % 2, 0:kvw, keys].astype(BF16)) + mask
                 + sb_ref[:, c * nks:(c + 1) * nks])
            m_new = jnp.maximum(m_s, jnp.max(s, axis=1, keepdims=True))
            alpha = jnp.exp(m_s - m_new)
            p = jnp.exp(s - m_new)
            l_s = alpha * l_s + jnp.sum(p, axis=1, keepdims=True)
            acc = alpha * acc + _dot_nt(p.astype(BF16), ringb_ref[ch % 2, kvw:2 * kvw, keys].astype(BF16))
            m_s = m_new
        request_ahead(n_chunks + ch)
    o_sel = acc * (1.0 / l_s)

    winn = winn_ref[0]
    s_w = _dot(qnb, cwin_ref[0, 0:kvw, :].astype(BF16)) + wb_ref[...]
    s_wn = jnp.sum(qn * winn[:, 0:kvw], axis=1, keepdims=True) + b0
    m_w = jnp.maximum(jnp.max(s_w, axis=1, keepdims=True), s_wn)
    p_w = jnp.exp(s_w - m_w)
    p_wn = jnp.exp(s_wn - m_w)
    l_w = jnp.sum(p_w, axis=1, keepdims=True) + p_wn
    o_win = (_dot_nt(p_w.astype(BF16), cwin_ref[0, kvw:2 * kvw, :].astype(BF16))
             + p_wn * winn[:, kvw:2 * kvw]) * (1.0 / l_w)

    g = gates_ref[0]
    out = g[0] * o_cmp[0:rows, :] + g[1] * o_sel + g[2] * o_win
    o_ref[0] = out[0:8, :]


def _nsa_sample(q, gates, kv_new, win_new, cache_kv, cache_win, page_table, lp, rel_bias,
                groups, hpg, hd):
    r = q.shape[0]
    heads = groups * hpg
    kvw = groups * hd
    n_phys, page = cache_kv.shape[0], cache_kv.shape[1]
    n_pages = page_table.shape[1]
    past = n_pages * page
    wbuf = cache_win.shape[1]
    ncb, nsb = past // CMP_BLOCK, past // SEL_BLOCK
    rows = 16
    chunk = min(32, n_pages // 2)
    nks = min(NKS_SAMPLE, chunk * page)
    assert n_pages % (2 * chunk) == 0 and chunk % 8 == 0 and (chunk * page) % nks == 0 and heads <= rows
    q3 = q.reshape(r, heads, hd)
    gmask = (jnp.arange(groups)[:, None] == (jnp.arange(heads) // hpg)[None, :]).astype(F32)
    qbd = jnp.einsum('rhd,gh->rgdh', q3, gmask).reshape(r, kvw, heads)
    qbd = jnp.pad(qbd, ((0, 0), (0, 0), (0, LANES - heads))).astype(BF16)
    qn = jnp.einsum('rhd,gh->rhgd', q3, gmask).reshape(r, heads, kvw)
    qn = jnp.pad(qn, ((0, 0), (0, rows - heads), (0, 0)))
    gts = gates[:, :3 * heads].reshape(r, heads, 3).transpose(0, 2, 1)
    gts = jnp.broadcast_to(gts[:, :, :, None], (r, 3, heads, kvw))
    gts = jnp.pad(gts, ((0, 0), (0, 0), (0, rows - heads), (0, 0)))
    wk = _compress_weight(lp['cmp_w_k'], groups, hd).reshape(CMP_BLOCK, kvw, kvw)
    wv = _compress_weight(lp['cmp_w_v'], groups, hd).reshape(CMP_BLOCK, kvw, kvw)
    zero = jnp.zeros_like(wk)
    wkv = jnp.concatenate([jnp.concatenate([wk, zero], axis=2),
                           jnp.concatenate([zero, wv], axis=2)], axis=1).astype(BF16)
    pe_t = jnp.concatenate([jnp.tile(lp['pe_k'].T, (groups, 1)), jnp.tile(lp['pe_v'].T, (groups, 1))], axis=0)
    pe_t = jnp.tile(pe_t, (1, 2 * page // CMP_BLOCK))
    bpp = 2 * page // CMP_BLOCK
    dst = np.arange(2 * page)
    src = (dst % bpp) * CMP_BLOCK + dst // bpp
    perm = jnp.asarray(src[:, None] == np.arange(2 * page)[None, :], BF16)
    kg = jnp.tile(lp['k_gain'][0], groups)[None, :]
    lane_g = np.where(np.arange(LANES) < heads, np.arange(LANES) // hpg, -1)
    gsum = jnp.asarray((lane_g[:, None] == lane_g[None, :]) & (lane_g[:, None] >= 0), BF16)
    nbk = rel_bias.shape[0]
    tbl = rel_bias.astype(F32)
    d_c = (past - (np.arange(ncb) * CMP_BLOCK + CMP_BLOCK - 1))[:, None]
    vals_t = jnp.concatenate([tbl, jnp.broadcast_to(tbl[:, :1], (nbk, LANES - heads))], axis=1)
    cb = _bias_table(np.broadcast_to(_bucket_np(d_c, nbk), (ncb, LANES)), vals_t)
    vals_n = jnp.concatenate([jnp.repeat(tbl, LANES, axis=1), jnp.full((1, heads * LANES), NEG, F32)], axis=0)

    def head_rows(ids):
        out = _bias_table(ids, vals_n).reshape(ids.shape[0], heads, LANES)
        out = jnp.transpose(out, (1, 0, 2)).reshape(heads, ids.shape[0] * LANES)
        return jnp.pad(out, ((0, rows - heads), (0, 0)))

    sb = head_rows(_bucket_np(past - np.arange(past), nbk).reshape(past // LANES, LANES))
    d_w = wbuf - np.arange(wbuf)
    wb = head_rows(np.where(d_w < WINDOW, _bucket_np(d_w, nbk), nbk).reshape(wbuf // LANES, LANES))
    b0 = jnp.pad(jnp.broadcast_to(tbl[0][:, None], (heads, LANES)), ((0, rows - heads), (0, 0)))
    blk = np.arange(nsb)[:, None]
    spread = jnp.asarray((blk == np.arange(nks)[None, :] // SEL_BLOCK), BF16)

    cache_t = jnp.transpose(cache_kv, (0, 2, 3, 4, 1)).reshape(n_phys, 4 * kvw, page)
    cwin_t = jnp.transpose(cache_win, (0, 2, 3, 4, 1)).reshape(r, 2 * kvw, wbuf)
    kvn = kv_new.reshape(r, 1, 4 * kvw)
    winn = win_new.reshape(r, 1, 2 * kvw)
    per_b = lambda s: pl.BlockSpec((1,) + s, lambda b, pt: (b,) + (0,) * len(s))
    const = lambda a: pl.BlockSpec(a.shape, lambda b, pt: (0,) * a.ndim, pipeline_mode=pl.Buffered(1))
    consts = (wkv, pe_t, perm, kg, gsum, cb, sb, wb, b0, spread)
    grid_spec = pltpu.PrefetchScalarGridSpec(
        num_scalar_prefetch=1, grid=(r,),
        in_specs=[pl.BlockSpec(memory_space=pl.ANY), per_b((kvw, LANES)), per_b((rows, kvw)),
                  per_b((3, rows, kvw)), per_b((1, 4 * kvw)), per_b((1, 2 * kvw)),
                  per_b((2 * kvw, wbuf))]
                 + [const(a) for a in consts],
        out_specs=per_b((8, kvw)),
        scratch_shapes=[pltpu.VMEM((2, chunk, 2 * kvw, page), F32),
                        pltpu.VMEM((2, 2 * kvw, chunk * page), F32),
                        pltpu.VMEM((CMP_BLOCK, ncb, 2 * kvw), F32),
                        pltpu.SemaphoreType.DMA((2,)), pltpu.SemaphoreType.DMA((2,)),
                        pltpu.VMEM((ncb, LANES), F32)])
    o8 = pl.pallas_call(
        functools.partial(_nsa_sample_kernel, hd=hd, n_pages=n_pages, page=page, chunk=chunk),
        grid_spec=grid_spec,
        out_shape=jax.ShapeDtypeStruct((r, 8, kvw), F32),
        compiler_params=_cparams(), name="nsa_sample",
    )(page_table, cache_t, qbd, qn, gts, kvn, winn, cwin_t, *consts)
    o4 = o8[:, :heads, :].reshape(r, heads, groups, hd)
    y = jnp.take_along_axis(o4, (jnp.arange(heads) // hpg)[None, :, None, None], axis=2)
    return y.reshape(r, heads * hd).astype(BF16)


def _mlp_ple(h, p_ref, gm_ref, wu_ref, wd_ref, gp_ref, wg_ref, wp_ref):
    up = _dot(_rms_rows(h, gm_ref[...]).astype(BF16), wu_ref[...])
    act = jnp.square(jnp.maximum(up, 0.0)).astype(BF16)
    h = h + _dot(act, wd_ref[...])
    gate = jax.nn.sigmoid(_dot(_rms_rows(h, gp_ref[...]).astype(BF16), wg_ref[...]))
    return h + gate * _dot(p_ref[...].astype(BF16), wp_ref[...])


def _even_tail_kernel(h_ref, ya_ref, yb_ref, p_ref, woa_ref, wob_ref,
                      gm_ref, wu_ref, wd_ref, gp_ref, wg_ref, wp_ref, o_ref):
    h = h_ref[...] + _dot(ya_ref[...], woa_ref[...]) + _dot(yb_ref[...], wob_ref[...])
    o_ref[...] = _mlp_ple(h, p_ref, gm_ref, wu_ref, wd_ref, gp_ref, wg_ref, wp_ref)


def _tail_weights(g_mlp, w_up, w_down, g_ple, w_gate, w_proj):
    return (g_mlp[None, :], w_up.astype(BF16), w_down.astype(BF16), g_ple[None, :],
            w_gate.astype(BF16), w_proj.astype(BF16))


def _ple_spec(p_all, layer, tm):
    if p_all.shape[1] == 1:
        return pl.BlockSpec((None, None, tm, p_all.shape[3]), lambda i: (layer, 0, i, 0))
    return pl.BlockSpec((None, tm, None, p_all.shape[3]), lambda i: (layer, i, 0, 0))


def _even_tail(h, ya, yb, p_all, layer, w_out, tail_w, tm=256):
    t, d = h.shape
    tm = min(tm, t)
    lw = ya.shape[1]
    woa, wob = w_out[:lw].astype(BF16), w_out[lw:].astype(BF16)
    row = lambda w: pl.BlockSpec((tm, w), lambda i: (i, 0))
    weights = (woa, wob) + tuple(tail_w)
    return pl.pallas_call(
        _even_tail_kernel, grid=(t // tm,),
        in_specs=[row(d), row(lw), row(yb.shape[1]), _ple_spec(p_all, layer, tm)]
                 + [_const_spec(w.shape) for w in weights],
        out_specs=row(d), out_shape=jax.ShapeDtypeStruct((t, d), F32),
        compiler_params=_cparams(), name="even_tail",
    )(h, ya, yb, p_all, *weights)


def _odd_kernel(h_ref, p_ref, gx_ref, wi_ref, lg_ref, lb_ref, sw_ref, sb_ref, wo_ref,
                gm_ref, wu_ref, wd_ref, gp_ref, wg_ref, wp_ref, o_ref, v_ref, *, sw_groups, single):
    h = h_ref[...]
    tm = h.shape[0]
    z = _gelu(_dot(_rms_rows(h, gx_ref[...]).astype(BF16), wi_ref[...]))
    width = z.shape[1] // 2
    u, v = z[:, :width], z[:, width:]
    mu = jnp.mean(v, axis=-1, keepdims=True)
    vc = v - mu
    vn = vc * lax.rsqrt(jnp.mean(vc * vc, axis=-1, keepdims=True) + EPS) * lg_ref[...] + lb_ref[...]
    v_ref[...] = vn
    gw = width // sw_groups
    if single:
        s = vn * sw_ref[...] + sb_ref[...]
    else:
        vb = vn.astype(BF16)
        parts = []
        for c in range(tm // CHUNK):
            cols = [_dot(sw_ref[g], vb[c * CHUNK:(c + 1) * CHUNK, g * gw:(g + 1) * gw])
                    for g in range(sw_groups)]
            parts.append(jnp.concatenate(cols, axis=1) + sb_ref[...])
        s = jnp.concatenate(parts, axis=0)
    h = h + _dot((u * s).astype(BF16), wo_ref[...])
    o_ref[...] = _mlp_ple(h, p_ref, gm_ref, wu_ref, wd_ref, gp_ref, wg_ref, wp_ref)


def _odd_layer(h, p_all, layer, g_mix, w_in, ln_g, ln_b, sgu_w, sgu_b, w_out, tail_w, single, tm=256):
    t, d = h.shape
    tm = min(tm, t)
    ng, ch, _ = sgu_w.shape
    width = w_in.shape[1] // 2
    gw = width // ng
    if single:
        sw = jnp.repeat(sgu_w[:, 0, 0], gw)[None, :]
        sb = jnp.repeat(sgu_b[:, 0], gw)[None, :]
    else:
        sw = (sgu_w * jnp.tril(jnp.ones((ch, ch), sgu_w.dtype))).astype(BF16)
        sb = jnp.repeat(sgu_b.T, gw, axis=1)
    weights = (g_mix[None, :], w_in.astype(BF16), ln_g[None, :], ln_b[None, :], sw, sb,
               w_out.astype(BF16)) + tuple(tail_w)
    row = lambda w: pl.BlockSpec((tm, w), lambda i: (i, 0))
    return pl.pallas_call(
        functools.partial(_odd_kernel, sw_groups=ng, single=single), grid=(t // tm,),
        in_specs=[row(d), _ple_spec(p_all, layer, tm)] + [_const_spec(w.shape) for w in weights],
        out_specs=[row(d), row(width)],
        out_shape=[jax.ShapeDtypeStruct((t, d), F32), jax.ShapeDtypeStruct((t, width), F32)],
        compiler_params=_cparams(), name="odd_single" if single else "odd_prompt",
    )(h, p_all, *weights)


def kernel(x_prompt, x_sample, cache_kv, cache_win, state_conv, state_h, page_table, p_prompt, p_sample,
           rel_bias, g_mix, g_mlp, w_up, w_down, g_ple, w_ple_gate, w_ple_proj,
           w_in_even, w_out_even, conv_w, conv_b, rg_w_a, rg_b_a, rg_w_x, rg_b_x, rg_lambda,
           q_gain, k_gain, cmp_w_k, cmp_w_v, cmp_pe_k, cmp_pe_v,
           w_in_odd, ln_v_g, ln_v_b, sgu_w, sgu_b, w_out_odd):
    bsz, t, d = x_prompt.shape
    r = x_sample.shape[0]
    assert bsz == 1 and x_sample.shape[1] == 1
    depth = g_mix.shape[0]
    lw = conv_w.shape[-1]
    hd = q_gain.shape[-1]
    groups = cache_kv.shape[4]
    heads = rel_bias.shape[1]
    hpg = heads // groups
    qw, kvw = heads * hd, groups * hd
    dims = (lw, qw, kvw, hd, heads)
    wbuf = cache_win.shape[2]

    hp = x_prompt.reshape(t, d)
    hs = x_sample.reshape(r, d)
    kvp, kvs, wnp, wns, cvp, cvs, hhp, hhs, vvs = [], [], [], [], [], [], [], [], []
    for i in range(depth):
        tail_w = _tail_weights(g_mlp[i], w_up[i], w_down[i], g_ple[i], w_ple_gate[i], w_ple_proj[i])
        if i % 2 == 0:
            e = i // 2
            lp = {'cmp_w_k': cmp_w_k[e], 'cmp_w_v': cmp_w_v[e], 'pe_k': cmp_pe_k[e], 'pe_v': cmp_pe_v[e],
                  'k_gain': k_gain[e]}
            rg_w = _rglru_weights(conv_w[e], conv_b[e], rg_w_a[e], rg_b_a[e], rg_w_x[e], rg_b_x[e],
                                  rg_lambda[e])
            (ya, h_last, xa_tail, kv_rows, win_rows, kcmp, vcmp, ks, kw, qT, vsT, vwT, gT) = _even_in_prompt(
                hp, g_mix[i], w_in_even[e], q_gain[e], k_gain[e], dims, rg_w)
            kc, vcT = _compress_prompt(kcmp, vcmp, cmp_w_k[e], cmp_w_v[e], cmp_pe_k[e], cmp_pe_v[e],
                                       k_gain[e, 0], groups, hd)
            yb = _nsa_prompt(qT, gT, kc, vcT, ks, vsT, kw, vwT, rel_bias, groups, hpg, hd)
            hp = _even_tail(hp, ya, yb, p_prompt, i, w_out_even[e], tail_w)
            kvp.append(kv_rows.reshape(1, t, 4, groups, hd))
            wnp.append(win_rows[t - wbuf:].reshape(1, wbuf, 2, groups, hd))
            cvp.append(xa_tail[8 - (conv_w.shape[1] - 1):][None])
            hhp.append(h_last)
            xa_s, ga_s, q_s, kv_s, win_s, gate_s = _even_in_sample(
                hs, g_mix[i], w_in_even[e], q_gain[e], k_gain[e], dims)
            ya_s, h_s = _rglru_sample(xa_s, ga_s, state_conv[e], state_h[e], rg_w)
            yb_s = _nsa_sample(q_s, gate_s, kv_s, win_s, cache_kv[e], cache_win[e], page_table, lp,
                               rel_bias, groups, hpg, hd)
            hs = _even_tail(hs, ya_s, yb_s, p_sample, i, w_out_even[e], tail_w)
            kvs.append(kv_s.reshape(r, 1, 4, groups, hd))
            wns.append(jnp.concatenate([cache_win[e][:, 1:], win_s.reshape(r, 1, 2, groups, hd)], axis=1))
            cvs.append(jnp.concatenate([state_conv[e][:, 1:], xa_s[:, None, :]], axis=1))
            hhs.append(h_s)
        else:
            o = i // 2
            hp, _ = _odd_layer(hp, p_prompt, i, g_mix[i], w_in_odd[o], ln_v_g[o], ln_v_b[o],
                               sgu_w[o], sgu_b[o], w_out_odd[o], tail_w, single=False)
            hs, v_new = _odd_layer(hs, p_sample, i, g_mix[i], w_in_odd[o], ln_v_g[o], ln_v_b[o],
                                   sgu_w[o], sgu_b[o], w_out_odd[o], tail_w, single=True)
            vvs.append(v_new[:, None, :])
    v_sample = jnp.stack(vvs) if vvs else jnp.zeros((0, r, 1, w_in_odd.shape[-1] // 2), F32)
    return (hp[None], hs[:, None, :], jnp.stack(kvp), jnp.stack(kvs), jnp.stack(wnp), jnp.stack(wns),
            jnp.stack(cvp), jnp.stack(cvs), jnp.stack(hhp), jnp.stack(hhs), v_sample)
```
